```python
import math
import jax
import jax.numpy as jnp
from jax import lax
import numpy as np

D_MODEL = 1024
BATCH = 32
SEQ = 256
DEPTH = 1
DEC_BATCH = 8
DEC_SEQ = 4096
PAST_LEN = 512

GRID_W = 64
D_MIX = D_MODEL
MIX_A = D_MIX // 2
MIX_B = D_MIX - MIX_A
HEAD_DIM = 128
N_HEADS_A = MIX_A // HEAD_DIM
SHORT_CONV = 5
CHUNK = 64
CONV_W = 31
N_GROUPS = 4
EXPERTS_PER_GROUP = 8
N_EXPERTS = N_GROUPS * EXPERTS_PER_GROUP
TOP_K = 2
D_EXPERT = D_MODEL // 2
ROUTE_BLOCK = 128
N_IN = 4 * MIX_A + 4 * N_HEADS_A + 2 * MIX_B
EPS = 1e-6
F32 = jnp.float32

kernel_name = 'hybrid_deltanet_conformer_hmoe_dit_step'


def rmsnorm(x, g):
    xf = x.astype(F32)
    y = xf * lax.rsqrt(jnp.mean(xf * xf, axis=-1, keepdims=True) + EPS)
    return (y * g.astype(F32)).astype(x.dtype)


def layernorm(x, g, b):
    xf = x.astype(F32)
    xc = xf - jnp.mean(xf, axis=-1, keepdims=True)
    y = xc * lax.rsqrt(jnp.mean(xc * xc, axis=-1, keepdims=True) + EPS)
    return (y * g.astype(F32) + b.astype(F32)).astype(x.dtype)


def l2norm(x):
    return x * lax.rsqrt(jnp.sum(x * x, axis=-1, keepdims=True) + EPS)


def dwconv_centered(x, w):
    width = w.shape[0]
    pad = width // 2
    n = x.shape[1]
    xp = jnp.pad(x, ((0, 0), (pad, pad), (0, 0)))
    out = xp[:, 0:n] * w[0]
    for i in range(1, width):
        out = out + xp[:, i:i + n] * w[i]
    return out


def grid_pos_embed(n_tokens):
    rows = n_tokens // GRID_W
    r = jnp.repeat(jnp.arange(rows, dtype=F32), GRID_W)
    col = jnp.tile(jnp.arange(GRID_W, dtype=F32), rows)
    n_freq = D_MODEL // 4
    freq = jnp.exp(jnp.arange(n_freq, dtype=F32) * (-math.log(10000.0) / n_freq))

    def enc(p):
        ang = p[:, None] * freq[None, :]
        return jnp.concatenate([jnp.sin(ang), jnp.cos(ang)], axis=-1)

    return jnp.concatenate([enc(r), enc(col)], axis=-1)


def chunk_gated_delta(q, k, v, g, beta, s0):
    b, n_tok, h, dk = q.shape
    dv = v.shape[-1]
    n_chunks = n_tok // CHUNK

    def to_chunks(t):
        return jnp.moveaxis(t.reshape(b, n_chunks, CHUNK, *t.shape[2:]), 3, 1)

    q = to_chunks(q) * (dk ** -0.5)
    k = to_chunks(k)
    v = to_chunks(v)
    beta = to_chunks(beta)
    gc = jnp.cumsum(to_chunks(g), axis=-1)
    causal = jnp.tril(jnp.ones((CHUNK, CHUNK), dtype=bool))
    strict = jnp.tril(jnp.ones((CHUNK, CHUNK), dtype=bool), -1)
    diff = gc[..., :, None] - gc[..., None, :]
    decay = jnp.where(causal, jnp.exp(jnp.where(causal, diff, 0.0)), 0.0)
    kb = k * beta[..., None]
    m = jnp.where(strict, jnp.einsum('bhncd,bhnsd->bhncs', kb, k) * decay, 0.0)
    eye = jnp.eye(CHUNK, dtype=F32)
    rhs = jnp.concatenate([v * beta[..., None], kb * jnp.exp(gc)[..., None]], axis=-1)
    sol = lax.linalg.triangular_solve(m + eye, rhs, left_side=True, lower=True, unit_diagonal=True)
    u = sol[..., :dv]
    w = sol[..., dv:]
    qk = jnp.einsum('bhncd,bhnsd->bhncs', q, k) * decay
    g_last = gc[..., -1]
    q_dec = q * jnp.exp(gc)[..., None]
    k_dec = k * jnp.exp(g_last[..., None] - gc)[..., None]

    def step(s, xs):
        q_i, k_i, u_i, w_i, qk_i, gl_i = xs
        v_new = u_i - jnp.einsum('bhck,bhkv->bhcv', w_i, s)
        o_i = jnp.einsum('bhck,bhkv->bhcv', q_i, s) + jnp.einsum('bhcs,bhsv->bhcv', qk_i, v_new)
        s = s * jnp.exp(gl_i)[..., None, None] + jnp.einsum('bhck,bhcv->bhkv', k_i, v_new)
        return s, o_i

    xs = tuple(jnp.moveaxis(t, 2, 0) for t in (q_dec, k_dec, u, w, qk, g_last))
    s_final, o = lax.scan(step, s0.astype(F32), xs)
    o = jnp.transpose(o, (1, 0, 3, 2, 4)).reshape(b, n_tok, h, dv)
    return o, s_final


def token_mixer(h, s0, w_in, conv_qkv_w, A_log, dt_bias, onorm_g, dw_w, dw_b, cln_g, cln_b, w_out):
    b, n, _ = h.shape
    hh = N_HEADS_A
    proj = h @ w_in
    qkv = proj[..., :3 * MIX_A]
    z = proj[..., 3 * MIX_A:4 * MIX_A]
    b_raw = proj[..., 4 * MIX_A:4 * MIX_A + 2 * hh]
    a_raw = proj[..., 4 * MIX_A + 2 * hh:4 * MIX_A + 4 * hh]
    glu = proj[..., 4 * MIX_A + 4 * hh:]
    qkv = jax.nn.silu(dwconv_centered(qkv, conv_qkv_w)).astype(F32).reshape(b, n, 3, hh, HEAD_DIM)
    q = l2norm(qkv[:, :, 0])
    k = l2norm(qkv[:, :, 1])
    v = qkv[:, :, 2]
    beta = jax.nn.sigmoid(b_raw.astype(F32)).reshape(b, n, 2, hh)
    g = -jnp.exp(A_log.astype(F32)) * jax.nn.softplus(a_raw.astype(F32).reshape(b, n, 2, hh) + dt_bias.astype(F32))
    o_f, s_f = chunk_gated_delta(q, k, v, g[:, :, 0], beta[:, :, 0], s0[:, 0])
    o_b, s_b = chunk_gated_delta(jnp.flip(q, 1), jnp.flip(k, 1), jnp.flip(v, 1),
                                 jnp.flip(g[:, :, 1], 1), jnp.flip(beta[:, :, 1], 1), s0[:, 1])
    o = o_f + jnp.flip(o_b, 1)
    o = rmsnorm(o, onorm_g) * jax.nn.silu(z.astype(F32).reshape(b, n, hh, HEAD_DIM))
    u = glu[..., :MIX_B] * jax.nn.sigmoid(glu[..., MIX_B:])
    u = dwconv_centered(u, dw_w) + dw_b
    u = jax.nn.silu(layernorm(u, cln_g, cln_b))
    mix = jnp.concatenate([o.reshape(b, n, MIX_A).astype(h.dtype), u.astype(h.dtype)], axis=-1) @ w_out
    return mix, jnp.stack([s_f, s_b], axis=1)


def hier_moe(h, w_group, b_group, w_expert, b_expert, w_e_gate, w_e_up, w_e_down):
    n_tok, d = h.shape
    rows = jnp.arange(n_tok)
    g_logits = (h @ w_group).astype(F32) + b_group.astype(F32)
    grp = jnp.argmax(g_logits, axis=-1)
    p_grp = jax.nn.softmax(g_logits, axis=-1)[rows, grp]
    e_logits = ((h @ w_expert).astype(F32) + b_expert.astype(F32)).reshape(n_tok, N_GROUPS, EXPERTS_PER_GROUP)
    top_v, top_i = lax.top_k(e_logits[rows, grp], TOP_K)
    weights = jax.nn.softmax(top_v, axis=-1) * p_grp[:, None]
    expert_id = (grp[:, None] * EXPERTS_PER_GROUP + top_i).astype(jnp.int32)
    n_assign = n_tok * TOP_K
    flat_e = expert_id.reshape(-1)
    flat_t = jnp.repeat(jnp.arange(n_tok, dtype=jnp.int32), TOP_K)
    flat_w = weights.reshape(-1)
    order = jnp.argsort(flat_e)
    se, st, sw = flat_e[order], flat_t[order], flat_w[order]
    counts = jnp.zeros((N_EXPERTS,), jnp.int32).at[flat_e].add(1)
    padded = (counts + ROUTE_BLOCK - 1) // ROUTE_BLOCK * ROUTE_BLOCK
    pad_end = jnp.cumsum(padded)
    pad_start = pad_end - padded
    raw_start = jnp.cumsum(counts) - counts
    dest = pad_start[se] + jnp.arange(n_assign, dtype=jnp.int32) - raw_start[se]
    n_blocks = -(-(n_assign + N_EXPERTS * (ROUTE_BLOCK - 1)) // ROUTE_BLOCK)
    cap = n_blocks * ROUTE_BLOCK
    slot_tok = jnp.full((cap,), n_tok, jnp.int32).at[dest].set(st)
    slot_w = jnp.zeros((cap,), F32).at[dest].set(sw)
    block_start = jnp.arange(n_blocks, dtype=jnp.int32) * ROUTE_BLOCK
    block_expert = jnp.minimum(jnp.searchsorted(pad_end, block_start, side='right'), N_EXPERTS - 1)
    h_pad = jnp.concatenate([h, jnp.zeros((1, d), h.dtype)], axis=0)
    xb = h_pad[slot_tok].reshape(n_blocks, ROUTE_BLOCK, d)

    def expert_block(args):
        x_blk, e = args
        return (jax.nn.silu(x_blk @ w_e_gate[e]) * (x_blk @ w_e_up[e])) @ w_e_down[e]

    yb = lax.map(expert_block, (xb, block_expert)).reshape(cap, d)
    out = jnp.zeros((n_tok + 1, d), F32).at[slot_tok].add(yb.astype(F32) * slot_w[:, None])
    return out[:n_tok].astype(h.dtype)


def trunk_layer(x, cond, s0, norm1_g, w_mod, b_mod, w_in, conv_qkv_w, A_log, dt_bias, onorm_g,
                dw_w, dw_b, cln_g, cln_b, w_out, norm2_g, w_group, b_group, w_expert, b_expert,
                w_e_gate, w_e_up, w_e_down):
    mod = jax.nn.silu(cond) @ w_mod + b_mod
    sh1, sc1, g1, sh2, sc2, g2 = jnp.split(mod[:, None, :], 6, axis=-1)
    h = rmsnorm(x, norm1_g) * (1 + sc1) + sh1
    mix, s_new = token_mixer(h, s0, w_in, conv_qkv_w, A_log, dt_bias, onorm_g, dw_w, dw_b, cln_g, cln_b, w_out)
    x = x + g1 * mix
    h = rmsnorm(x, norm2_g) * (1 + sc2) + sh2
    b, n, d = x.shape
    ff = hier_moe(h.reshape(b * n, d), w_group, b_group, w_expert, b_expert, w_e_gate, w_e_up, w_e_down)
    x = x + g2 * ff.reshape(b, n, d)
    return x, s_new


def setup_inputs(seed: int = 0) -> dict:
    key = jax.random.key(seed)
    ks = jax.random.split(key, 28)
    nrm = jax.random.normal

    def gain(k, shape):
        return 1.0 + 0.02 * nrm(k, shape, F32)

    dt = jnp.exp(jax.random.uniform(ks[10], (DEPTH, 2, N_HEADS_A), F32, math.log(1e-3), math.log(1e-1)))
    return {
        'x_prompt': nrm(ks[0], (BATCH, SEQ, D_MODEL), F32),
        'x_sample': nrm(ks[1], (DEC_BATCH, DEC_SEQ, D_MODEL), F32),
        'state_delta': nrm(ks[2], (DEC_BATCH, DEPTH, 2, N_HEADS_A, HEAD_DIM, HEAD_DIM), F32) * HEAD_DIM ** -0.5,
        'c': nrm(ks[3], (DEC_BATCH, D_MODEL), F32),
        'c_ctx': nrm(ks[4], (D_MODEL,), F32),
        'norm1_g': gain(ks[5], (DEPTH, D_MODEL)),
        'w_mod': nrm(ks[6], (DEPTH, D_MODEL, 6 * D_MODEL), F32) * D_MODEL ** -0.5,
        'b_mod': 0.01 * nrm(ks[7], (DEPTH, 6 * D_MODEL), F32),
        'w_in': nrm(ks[8], (DEPTH, D_MODEL, N_IN), F32) * D_MODEL ** -0.5,
        'conv_qkv_w': nrm(ks[9], (DEPTH, SHORT_CONV, 3 * MIX_A), F32) * SHORT_CONV ** -0.5,
        'A_log': jnp.log(jax.random.uniform(ks[11], (DEPTH, 2, N_HEADS_A), F32, 1.0, 16.0)),
        'dt_bias': dt + jnp.log(-jnp.expm1(-dt)),
        'onorm_g': gain(ks[12], (DEPTH, HEAD_DIM)),
        'dw_w': nrm(ks[13], (DEPTH, CONV_W, MIX_B), F32) * CONV_W ** -0.5,
        'dw_b': 0.01 * nrm(ks[14], (DEPTH, MIX_B), F32),
        'cln_g': gain(ks[15], (DEPTH, MIX_B)),
        'cln_b': 0.01 * nrm(ks[16], (DEPTH, MIX_B), F32),
        'w_out': nrm(ks[17], (DEPTH, D_MIX, D_MODEL), F32) * D_MIX ** -0.5,
        'norm2_g': gain(ks[18], (DEPTH, D_MODEL)),
        'w_group': nrm(ks[19], (DEPTH, D_MODEL, N_GROUPS), F32) * D_MODEL ** -0.5,
        'b_group': 0.01 * nrm(ks[20], (DEPTH, N_GROUPS), F32),
        'w_expert': nrm(ks[21], (DEPTH, D_MODEL, N_EXPERTS), F32) * D_MODEL ** -0.5,
        'b_expert': 0.01 * nrm(ks[22], (DEPTH, N_EXPERTS), F32),
        'w_e_gate': nrm(ks[23], (DEPTH, N_EXPERTS, D_MODEL, D_EXPERT), F32) * D_MODEL ** -0.5,
        'w_e_up': nrm(ks[24], (DEPTH, N_EXPERTS, D_MODEL, D_EXPERT), F32) * D_MODEL ** -0.5,
        'w_e_down': nrm(ks[25], (DEPTH, N_EXPERTS, D_EXPERT, D_MODEL), F32) * D_EXPERT ** -0.5,
        'final_g': gain(ks[26], (D_MODEL,)),
    }


def reference(x_prompt, x_sample, state_delta, c, c_ctx, norm1_g, w_mod, b_mod, w_in, conv_qkv_w,
              A_log, dt_bias, onorm_g, dw_w, dw_b, cln_g, cln_b, w_out, norm2_g, w_group, b_group,
              w_expert, b_expert, w_e_gate, w_e_up, w_e_down, final_g):
    xp = x_prompt
    s_zero = jnp.zeros((x_prompt.shape[0], 2, N_HEADS_A, HEAD_DIM, HEAD_DIM), F32)
    xs = x_sample + grid_pos_embed(x_sample.shape[1]).astype(x_sample.dtype)
    ctx_states = []
    for l in range(DEPTH):
        p = dict(norm1_g=norm1_g[l], w_mod=w_mod[l], b_mod=b_mod[l], w_in=w_in[l],
                 conv_qkv_w=conv_qkv_w[l], A_log=A_log[l], dt_bias=dt_bias[l], onorm_g=onorm_g[l],
                 dw_w=dw_w[l], dw_b=dw_b[l], cln_g=cln_g[l], cln_b=cln_b[l], w_out=w_out[l],
                 norm2_g=norm2_g[l], w_group=w_group[l], b_group=b_group[l], w_expert=w_expert[l],
                 b_expert=b_expert[l], w_e_gate=w_e_gate[l], w_e_up=w_e_up[l], w_e_down=w_e_down[l])
        xp, s_ctx = trunk_layer(xp, c_ctx[None, :], s_zero, **p)
        ctx_states.append(s_ctx)
        xs, _ = trunk_layer(xs, c, state_delta[:, l], **p)
    y_prompt = rmsnorm(xp, final_g)
    y_sample = rmsnorm(xs, final_g)
    new_state_delta = jnp.stack(ctx_states, axis=1)
    return (y_prompt, y_sample, new_state_delta)
```

```python
import functools
import math

import jax
import jax.numpy as jnp
from jax import lax
from jax.experimental import pallas as pl
from jax.experimental.pallas import tpu as pltpu

F32 = jnp.float32
BF16 = jnp.bfloat16
I32 = jnp.int32

D_MODEL = 1024
MIX_A = 512
MIX_B = 512
HEAD_DIM = 128
N_HEADS = 4
SHORT_CONV = 5
CONV_W = 31
N_GROUPS = 4
EXPERTS_PER_GROUP = 8
N_EXPERTS = 32
D_EXPERT = 512
GRID_W = 64
EPS = 1e-6

LANES = 128
TOK_TILE = 256
CHUNK = 64
DELTA_BLK = 512
ROUTE_BM = 256
Q_HALO = 8
U_HALO = 16
VMEM_LIMIT = 56 * 1024 * 1024

HIGHEST = lax.Precision.HIGHEST


def _cparams(sem):
    return pltpu.CompilerParams(dimension_semantics=sem, vmem_limit_bytes=VMEM_LIMIT)


def _sigmoid(x):
    return 1.0 / (1.0 + jnp.exp(-x))


def _silu(x):
    return x * _sigmoid(x)


def _softplus(x):
    return jnp.maximum(x, 0.0) + jnp.log1p(jnp.exp(-jnp.abs(x)))


def _rms(x):
    return x * lax.rsqrt(jnp.mean(x * x, axis=-1, keepdims=True) + EPS)


def _dot(a, b):
    return jnp.dot(a, b, preferred_element_type=F32)


def _dot_nt(a, b):
    return lax.dot_general(a, b, (((1,), (1,)), ((), ())), preferred_element_type=F32)


def _dot_tn(a, b):
    return lax.dot_general(a, b, (((0,), (0,)), ((), ())), preferred_element_type=F32)


def _mod_kernel(cond_ref, w_ref, b_ref, o_ref):
    s = _silu(cond_ref[...])
    o_ref[...] = jnp.dot(s, w_ref[...], preferred_element_type=F32, precision=HIGHEST) + b_ref[...]


def _modulation(cond, w_mod, b_mod):
    n = cond.shape[0]
    tn = D_MODEL
    return pl.pallas_call(
        _mod_kernel,
        grid=(6 * D_MODEL // tn,),
        in_specs=[pl.BlockSpec((n, D_MODEL), lambda j: (0, 0)),
                  pl.BlockSpec((D_MODEL, tn), lambda j: (0, j)),
                  pl.BlockSpec((1, tn), lambda j: (0, j))],
        out_specs=pl.BlockSpec((n, tn), lambda j: (0, j)),
        out_shape=jax.ShapeDtypeStruct((n, 6 * D_MODEL), F32),
        compiler_params=_cparams(("parallel",)),
        name="mod",
    )(cond, w_mod, b_mod.reshape(1, -1))


class _Layout:
    def __init__(self, n_p, l_p, n_s, l_s):
        self.n_p, self.l_p, self.n_s, self.l_s = n_p, l_p, n_s, l_s
        self.t_p = n_p * l_p
        self.t_s = n_s * l_s
        self.t = self.t_p + self.t_s
        assert l_p % TOK_TILE == 0 and l_s % TOK_TILE == 0 and l_s % DELTA_BLK == 0
        self.tiles_p = self.t_p // TOK_TILE
        self.tiles_s = self.t_s // TOK_TILE
        self.tiles = self.tiles_p + self.tiles_s
        self.tps_p = l_p // TOK_TILE
        self.tps_s = l_s // TOK_TILE

    def is_sample(self, i):
        return i >= self.tiles_p

    def mod_row(self, i):
        return jnp.where(i < self.tiles_p, 0, 1 + (i - self.tiles_p) // self.tps_s)

    def pos_blk(self, i):
        return jnp.where(i < self.tiles_p, 0, (i - self.tiles_p) % self.tps_s)

    def xp_blk(self, i):
        return jnp.minimum(i, self.tiles_p - 1)

    def xs_blk(self, i):
        return jnp.maximum(i - self.tiles_p, 0)

    def seq_pos(self, i):
        in_s = i >= self.tiles_p
        pos = jnp.where(in_s, (i - self.tiles_p) % self.tps_s, i % self.tps_p)
        n = jnp.where(in_s, self.tps_s, self.tps_p)
        return pos, n


def _load_x(lay, i, xp_ref, xs_ref, pos_ref):
    return jnp.where(lay.is_sample(i), xs_ref[...] + pos_ref[...], xp_ref[...])


def _x_specs(lay):
    return [pl.BlockSpec((TOK_TILE, D_MODEL), lambda i: (lay.xp_blk(i), 0)),
            pl.BlockSpec((TOK_TILE, D_MODEL), lambda i: (lay.xs_blk(i), 0)),
            pl.BlockSpec((TOK_TILE, D_MODEL), lambda i: (lay.pos_blk(i), 0))]


def _mod_spec(lay):
    return pl.BlockSpec((1, 6, D_MODEL), lambda i: (lay.mod_row(i), 0, 0))


def _row_spec(width):
    return pl.BlockSpec((TOK_TILE, width), lambda i: (i, 0))


def _full_spec(shape):
    nd = len(shape)
    return pl.BlockSpec(shape, lambda i: (0,) * nd)


def _inproj_kernel(lay, xp_ref, xs_ref, pos_ref, mod_ref, g_ref, wm_ref, ws_ref,
                   qkv_ref, z_ref, ug_ref, ba_ref):
    i = pl.program_id(0)
    x = _load_x(lay, i, xp_ref, xs_ref, pos_ref)
    m = mod_ref[0]
    h = _rms(x) * g_ref[...] * (1.0 + m[1:2]) + m[0:1]
    hb = h.astype(BF16)
    qkv_ref[...] = _dot(hb, wm_ref[:, 0:3 * MIX_A])
    z_ref[...] = _dot(hb, wm_ref[:, 3 * MIX_A:4 * MIX_A])
    glu = _dot(hb, wm_ref[:, 4 * MIX_A:4 * MIX_A + 2 * MIX_B])
    ug_ref[...] = glu[:, :MIX_B] * _sigmoid(glu[:, MIX_B:])
    ba_ref[...] = _dot(hb, ws_ref[...])


def _inproj(lay, xp, xs, pos, mod, norm_g, w_main, w_small):
    t = lay.t
    return pl.pallas_call(
        functools.partial(_inproj_kernel, lay),
        grid=(lay.tiles,),
        in_specs=_x_specs(lay) + [_mod_spec(lay), _full_spec((1, D_MODEL)),
                                  _full_spec(w_main.shape), _full_spec(w_small.shape)],
        out_specs=[_row_spec(3 * MIX_A), _row_spec(MIX_A), _row_spec(MIX_B), _row_spec(LANES)],
        out_shape=[jax.ShapeDtypeStruct((t, 3 * MIX_A), F32),
                   jax.ShapeDtypeStruct((t, MIX_A), F32),
                   jax.ShapeDtypeStruct((t, MIX_B), F32),
                   jax.ShapeDtypeStruct((t, LANES), F32)],
        compiler_params=_cparams(("parallel",)),
        name="inproj",
    )(xp, xs, pos, mod, norm_g, w_main, w_small)


CONV_ROWS = 64


def _conv_kernel(lay, qc_ref, qp_ref, qn_ref, uc_ref, up_ref, un_ref, ba_ref, cw_ref, dw_ref,
                 dwb_ref, lng_ref, lnb_ref, alog_ref, dtb_ref,
                 qkv_ref, uo_ref, gate_ref, eq_scr, eu_scr):
    i = pl.program_id(0)
    pos, n = lay.seq_pos(i)
    has_prev = pos != 0
    has_next = pos != n - 1
    eq_scr[0:Q_HALO] = jnp.where(has_prev, qp_ref[...], 0.0)
    eq_scr[Q_HALO:Q_HALO + TOK_TILE] = qc_ref[...]
    eq_scr[Q_HALO + TOK_TILE:] = jnp.where(has_next, qn_ref[...], 0.0)
    eu_scr[0:U_HALO] = jnp.where(has_prev, up_ref[...], 0.0)
    eu_scr[U_HALO:U_HALO + TOK_TILE] = uc_ref[...]
    eu_scr[U_HALO + TOK_TILE:] = jnp.where(has_next, un_ref[...], 0.0)

    q_off = Q_HALO - SHORT_CONV // 2
    u_off = U_HALO - CONV_W // 2
    for rc in range(TOK_TILE // CONV_ROWS):
        r0 = rc * CONV_ROWS
        for p in range(3):
            c0 = p * MIX_A
            acc = eq_scr[q_off + r0:q_off + r0 + CONV_ROWS, c0:c0 + MIX_A] * cw_ref[0:1, c0:c0 + MIX_A]
            for tap in range(1, SHORT_CONV):
                acc = acc + (eq_scr[q_off + tap + r0:q_off + tap + r0 + CONV_ROWS, c0:c0 + MIX_A]
                             * cw_ref[tap:tap + 1, c0:c0 + MIX_A])
            y = _silu(acc)
            for hd in range(N_HEADS):
                yh = y[:, hd * HEAD_DIM:(hd + 1) * HEAD_DIM]
                if p < 2:
                    yh = yh * lax.rsqrt(jnp.sum(yh * yh, axis=-1, keepdims=True) + EPS)
                qkv_ref[r0:r0 + CONV_ROWS, c0 + hd * HEAD_DIM:c0 + (hd + 1) * HEAD_DIM] = yh
        acc = eu_scr[u_off + r0:u_off + r0 + CONV_ROWS, :] * dw_ref[0:1, :]
        for tap in range(1, CONV_W):
            acc = acc + eu_scr[u_off + tap + r0:u_off + tap + r0 + CONV_ROWS, :] * dw_ref[tap:tap + 1, :]
        u = acc + dwb_ref[...]
        uc = u - jnp.mean(u, axis=-1, keepdims=True)
        un = uc * lax.rsqrt(jnp.mean(uc * uc, axis=-1, keepdims=True) + EPS)
        uo_ref[r0:r0 + CONV_ROWS, :] = _silu(un * lng_ref[...] + lnb_ref[...]).astype(BF16)

    x = ba_ref[...]
    lane = lax.broadcasted_iota(I32, x.shape, 1)
    g = -jnp.exp(alog_ref[...]) * _softplus(x + dtb_ref[...])
    gate_ref[...] = jnp.where(lane < 2 * N_HEADS, _sigmoid(x), g)


def _conv(lay, qkv_raw, ug, ba, cw, dw, dwb, lng, lnb, alog, dtb):
    t = lay.t
    qh = TOK_TILE // Q_HALO
    uh = TOK_TILE // U_HALO
    n_qh = t // Q_HALO
    n_uh = t // U_HALO
    in_specs = [
        _row_spec(3 * MIX_A),
        pl.BlockSpec((Q_HALO, 3 * MIX_A), lambda i: (jnp.maximum(i * qh - 1, 0), 0)),
        pl.BlockSpec((Q_HALO, 3 * MIX_A), lambda i: (jnp.minimum((i + 1) * qh, n_qh - 1), 0)),
        _row_spec(MIX_B),
        pl.BlockSpec((U_HALO, MIX_B), lambda i: (jnp.maximum(i * uh - 1, 0), 0)),
        pl.BlockSpec((U_HALO, MIX_B), lambda i: (jnp.minimum((i + 1) * uh, n_uh - 1), 0)),
        _row_spec(LANES),
        _full_spec(cw.shape), _full_spec(dw.shape), _full_spec(dwb.shape),
        _full_spec(lng.shape), _full_spec(lnb.shape), _full_spec(alog.shape), _full_spec(dtb.shape),
    ]
    return pl.pallas_call(
        functools.partial(_conv_kernel, lay),
        grid=(lay.tiles,),
        in_specs=in_specs,
        out_specs=[_row_spec(3 * MIX_A), _row_spec(MIX_B), _row_spec(LANES)],
        out_shape=[jax.ShapeDtypeStruct((t, 3 * MIX_A), F32),
                   jax.ShapeDtypeStruct((t, MIX_B), BF16),
                   jax.ShapeDtypeStruct((t, LANES), F32)],
        scratch_shapes=[pltpu.VMEM((TOK_TILE + 2 * Q_HALO, 3 * MIX_A), F32),
                        pltpu.VMEM((TOK_TILE + 2 * U_HALO, MIX_B), F32)],
        compiler_params=_cparams(("parallel",)),
        name="conv",
    )(qkv_raw, qkv_raw, qkv_raw, ug, ug, ug, ba, cw, dw, dwb, lng, lnb, alog, dtb)


INV_BASE = 8


def _tri_inverse_minus_eye(nmat, rows, cols):
    c = nmat.shape[0]
    shift = int(math.log2(INV_BASE))
    same = (rows >> shift) == (cols >> shift)
    n1 = jnp.where(same, nmat, 0.0)
    n1b = n1.astype(BF16)
    n2 = _dot(n1b, n1b)
    r = _dot(jnp.concatenate([n1, n2], axis=0).astype(BF16), n2.astype(BF16))
    q = n1 + n2 + r[:c]
    n4 = r[c:]
    q = q + n4 + _dot(q.astype(BF16), n4.astype(BF16))
    while (1 << shift) < c:
        off = ((rows >> (shift + 1)) == (cols >> (shift + 1))) & ((rows >> shift) != (cols >> shift))
        a = jnp.where(off, -nmat, 0.0)
        x = a + _dot(a.astype(BF16), q.astype(BF16))
        q = q - x - _dot(q.astype(BF16), x.astype(BF16))
        shift += 1
    return q


def _delta_unit(d, hd, qkv_ref, r0, gate, gcol, grow, s_scr, o_ref):
    c = CHUNK
    rows = lax.broadcasted_iota(I32, (c, c), 0)
    cols = lax.broadcasted_iota(I32, (c, c), 1)
    incl = (rows >= cols) if d == 0 else (rows <= cols)
    strict = (rows > cols) if d == 0 else (rows < cols)
    lane = 2 * N_HEADS + N_HEADS * d + hd
    beta = gate[:, N_HEADS * d + hd:N_HEADS * d + hd + 1]
    gc = gcol[:, lane:lane + 1]
    gr = grow[lane:lane + 1, :]
    last = c - 1 if d == 0 else 0
    gtot = gcol[last:last + 1, lane:lane + 1]
    dec = jnp.where(incl, jnp.exp(jnp.where(incl, gc - gr, 0.0)), 0.0)

    q = qkv_ref[pl.ds(r0, c), hd * HEAD_DIM:(hd + 1) * HEAD_DIM]
    k = qkv_ref[pl.ds(r0, c), MIX_A + hd * HEAD_DIM:MIX_A + (hd + 1) * HEAD_DIM]
    v = qkv_ref[pl.ds(r0, c), 2 * MIX_A + hd * HEAD_DIM:2 * MIX_A + (hd + 1) * HEAD_DIM]
    qs = q * (HEAD_DIM ** -0.5)
    kbeta = k * beta
    kb16 = k.astype(BF16)
    ak = _dot_nt(jnp.concatenate([kbeta, qs], axis=0).astype(BF16), kb16)
    nmat = jnp.where(strict, -ak[:c] * dec, 0.0)
    qk = ak[c:] * dec
    qinv = _tri_inverse_minus_eye(nmat, rows, cols)

    egc = jnp.exp(gc)
    rhs = jnp.concatenate([v * beta, kbeta * egc], axis=1)
    uw = rhs + _dot(qinv.astype(BF16), rhs.astype(BF16))
    u = uw[:, :HEAD_DIM]
    w = uw[:, HEAD_DIM:]
    qdec = qs * egc
    kdec = k * jnp.exp(gtot - gc)

    idx = N_HEADS * d + hd
    s = s_scr[idx]
    sw = _dot(jnp.concatenate([w, qdec], axis=0).astype(BF16), s.astype(BF16))
    vnew = u - sw[:c]
    vnb = vnew.astype(BF16)
    o = sw[c:] + _dot(qk.astype(BF16), vnb)
    s_scr[idx] = s * jnp.exp(gtot) + _dot_tn(kdec.astype(BF16), vnb)
    o_ref[pl.ds(r0, c), hd * HEAD_DIM:(hd + 1) * HEAD_DIM] = o


def _delta_kernel(has_s0, emit_final, n_chunk, *refs):
    refs = list(refs)
    qkvf_ref, qkvb_ref, gatef_ref, gateb_ref, gtf_ref, gtb_ref = refs[:6]
    refs = refs[6:]
    s0_ref = refs.pop(0) if has_s0 else None
    of_ref, ob_ref = refs[:2]
    refs = refs[2:]
    sfin_ref = refs.pop(0) if emit_final else None
    s_scr = refs[0]

    j = pl.program_id(1)

    @pl.when(j == 0)
    def _():
        if has_s0:
            s_scr[...] = s0_ref[0]
        else:
            s_scr[...] = jnp.zeros(s_scr.shape, F32)

    c = CHUNK
    rows = lax.broadcasted_iota(I32, (c, c), 0)
    cols = lax.broadcasted_iota(I32, (c, c), 1)
    tri_lo = (rows >= cols).astype(F32)
    tri_up = (rows <= cols).astype(F32)

    def body(ci, carry):
        for d in range(2):
            cidx = ci if d == 0 else n_chunk - 1 - ci
            r0 = pl.multiple_of(cidx * c, c)
            qkv_ref = qkvf_ref if d == 0 else qkvb_ref
            gate = (gatef_ref if d == 0 else gateb_ref)[pl.ds(r0, c), :]
            gt = (gtf_ref if d == 0 else gtb_ref)[cidx]
            tri = tri_lo if d == 0 else tri_up
            tri_t = tri_up if d == 0 else tri_lo
            gcol = jnp.dot(tri, gate, preferred_element_type=F32, precision=HIGHEST)
            grow = jnp.dot(gt, tri_t, preferred_element_type=F32, precision=HIGHEST)
            for hd in range(N_HEADS):
                _delta_unit(d, hd, qkv_ref, r0, gate, gcol, grow, s_scr, of_ref if d == 0 else ob_ref)
        return carry

    lax.fori_loop(0, n_chunk, body, 0)

    if emit_final:
        @pl.when(j == pl.num_programs(1) - 1)
        def _():
            sfin_ref[0] = s_scr[...]


def _delta(qkv, gate, gate_t, n_seq, seq_len, row0, blk, s0):
    t = qkv.shape[0]
    nblk = seq_len // blk
    n_chunk = blk // CHUNK
    b0 = row0 // blk
    has_s0 = s0 is not None
    emit_final = not has_s0
    n_state = 2 * N_HEADS

    def fwd(b, j):
        return b0 + b * nblk + j

    def bwd(b, j):
        return b0 + b * nblk + (nblk - 1 - j)

    in_specs = [
        pl.BlockSpec((blk, 3 * MIX_A), lambda b, j: (fwd(b, j), 0)),
        pl.BlockSpec((blk, 3 * MIX_A), lambda b, j: (bwd(b, j), 0)),
        pl.BlockSpec((blk, LANES), lambda b, j: (fwd(b, j), 0)),
        pl.BlockSpec((blk, LANES), lambda b, j: (bwd(b, j), 0)),
        pl.BlockSpec((n_chunk, 16, CHUNK), lambda b, j: (fwd(b, j), 0, 0)),
        pl.BlockSpec((n_chunk, 16, CHUNK), lambda b, j: (bwd(b, j), 0, 0)),
    ]
    args = [qkv, qkv, gate, gate, gate_t, gate_t]
    if has_s0:
        in_specs.append(pl.BlockSpec((1, n_state, HEAD_DIM, HEAD_DIM), lambda b, j: (b, 0, 0, 0)))
        args.append(s0)
    n_rows = n_seq * seq_len
    out_specs = [pl.BlockSpec((blk, MIX_A), lambda b, j: (b * nblk + j, 0)),
                 pl.BlockSpec((blk, MIX_A), lambda b, j: (b * nblk + (nblk - 1 - j), 0))]
    out_shape = [jax.ShapeDtypeStruct((n_rows, MIX_A), F32), jax.ShapeDtypeStruct((n_rows, MIX_A), F32)]
    if emit_final:
        out_specs.append(pl.BlockSpec((1, n_state, HEAD_DIM, HEAD_DIM), lambda b, j: (b, 0, 0, 0)))
        out_shape.append(jax.ShapeDtypeStruct((n_seq, n_state, HEAD_DIM, HEAD_DIM), F32))
    del t
    return pl.pallas_call(
        functools.partial(_delta_kernel, has_s0, emit_final, n_chunk),
        grid=(n_seq, nblk),
        in_specs=in_specs,
        out_specs=out_specs,
        out_shape=out_shape,
        scratch_shapes=[pltpu.VMEM((n_state, HEAD_DIM, HEAD_DIM), F32)],
        compiler_params=_cparams(("parallel", "arbitrary")),
        name="delta_latent" if has_s0 else "delta_prompt",
    )(*args)


def _outproj_kernel(lay, xp_ref, xs_ref, pos_ref, mod_ref, of_ref, ob_ref, z_ref, u_ref, og_ref, n2_ref,
                    wo_ref, wr_ref, br_ref, x1_ref, h2_ref, ids_ref, wts_ref):
    i = pl.program_id(0)
    x0 = _load_x(lay, i, xp_ref, xs_ref, pos_ref)
    m = mod_ref[0]
    o = of_ref[...] + ob_ref[...]
    z = z_ref[...]
    mix = _dot(u_ref[...], wo_ref[MIX_A:, :])
    for hd in range(N_HEADS):
        sl = slice(hd * HEAD_DIM, (hd + 1) * HEAD_DIM)
        oh = _rms(o[:, sl]) * og_ref[...] * _silu(z[:, sl])
        mix = mix + _dot(oh.astype(BF16), wo_ref[sl, :])
    x1 = x0 + m[2:3] * mix
    x1_ref[...] = x1
    h2 = _rms(x1) * n2_ref[...] * (1.0 + m[4:5]) + m[3:4]
    h2b = h2.astype(BF16)
    half = D_MODEL // 2
    hi = pltpu.bitcast(h2b[:, :half].astype(F32), jnp.uint32)
    lo = pltpu.bitcast(h2b[:, half:].astype(F32), jnp.uint32)
    h2_ref[...] = (hi & jnp.uint32(0xFFFF0000)) | (lo >> 16)

    logits = _dot(h2b, wr_ref[...]) + br_ref[...]
    lane = lax.broadcasted_iota(I32, logits.shape, 1)
    neg = jnp.float32(-jnp.inf)
    big = jnp.int32(LANES)
    gl = jnp.where(lane < N_GROUPS, logits, neg)
    gmax = jnp.max(gl, axis=-1, keepdims=True)
    grp = jnp.min(jnp.where(gl == gmax, lane, big), axis=-1, keepdims=True)
    p_grp = 1.0 / jnp.sum(jnp.where(lane < N_GROUPS, jnp.exp(gl - gmax), 0.0), axis=-1, keepdims=True)
    e_lane = lane - N_GROUPS
    in_grp = (e_lane >= grp * EXPERTS_PER_GROUP) & (e_lane < (grp + 1) * EXPERTS_PER_GROUP)
    el = jnp.where(in_grp, logits, neg)
    m1 = jnp.max(el, axis=-1, keepdims=True)
    i1 = jnp.min(jnp.where(el == m1, lane, big), axis=-1, keepdims=True)
    el2 = jnp.where(lane == i1, neg, el)
    m2 = jnp.max(el2, axis=-1, keepdims=True)
    i2 = jnp.min(jnp.where(el2 == m2, lane, big), axis=-1, keepdims=True)
    e2 = jnp.exp(m2 - m1)
    w1 = p_grp / (1.0 + e2)
    w2 = p_grp * e2 / (1.0 + e2)
    ids_ref[...] = jnp.where(lane == 0, i1 - N_GROUPS, jnp.where(lane == 1, i2 - N_GROUPS, 0))
    wts_ref[...] = jnp.where(lane == 0, w1, jnp.where(lane == 1, w2, 0.0))


def _outproj(lay, xp, xs, pos, mod, o_f, o_b, z, u, onorm_g, norm2_g, w_out, w_route, b_route):
    t = lay.t
    return pl.pallas_call(
        functools.partial(_outproj_kernel, lay),
        grid=(lay.tiles,),
        in_specs=_x_specs(lay) + [_mod_spec(lay), _row_spec(MIX_A), _row_spec(MIX_A), _row_spec(MIX_A),
                                  _row_spec(MIX_B), _full_spec((1, HEAD_DIM)), _full_spec((1, D_MODEL)),
                                  _full_spec(w_out.shape), _full_spec(w_route.shape),
                                  _full_spec(b_route.shape)],
        out_specs=[_row_spec(D_MODEL), _row_spec(D_MODEL // 2), _row_spec(LANES), _row_spec(LANES)],
        out_shape=[jax.ShapeDtypeStruct((t, D_MODEL), F32),
                   jax.ShapeDtypeStruct((t, D_MODEL // 2), jnp.uint32),
                   jax.ShapeDtypeStruct((t, LANES), I32),
                   jax.ShapeDtypeStruct((t, LANES), F32)],
        compiler_params=_cparams(("parallel",)),
        name="outproj",
    )(xp, xs, pos, mod, o_f, o_b, z, u, onorm_g, norm2_g, w_out, w_route, b_route)


def _gather_rows(src_hbm, idx_ref, dst, sem, n_rows, start):
    def body(r, carry):
        row = idx_ref[0, 0, r] if start else 0
        cp = pltpu.make_async_copy(src_hbm.at[pl.ds(row, 1)], dst.at[pl.ds(r, 1)], sem)
        if start:
            cp.start()
        else:
            cp.wait()
        return carry
    lax.fori_loop(0, n_rows, body, 0)


def _expert_kernel(blk_e_ref, nused_ref, tokc_ref, tokn_ref, h2_hbm, wg_ref, wu_ref, wd_ref, yb_ref,
                   xbuf, sem, wg_s, wu_s, wd_s):
    i = pl.program_id(0)
    nused = nused_ref[0]
    slot = i % 2

    @pl.when((i == 0) & (nused > 0))
    def _():
        _gather_rows(h2_hbm, tokc_ref, xbuf.at[0], sem.at[0], ROUTE_BM, True)

    @pl.when(i + 1 < nused)
    def _():
        _gather_rows(h2_hbm, tokn_ref, xbuf.at[1 - slot], sem.at[1 - slot], ROUTE_BM, True)

    @pl.when(i < nused)
    def _():
        _gather_rows(h2_hbm, tokc_ref, xbuf.at[slot], sem.at[slot], ROUTE_BM, False)
        changed = (i == 0) | (blk_e_ref[i] != blk_e_ref[jnp.maximum(i - 1, 0)])

        @pl.when(changed)
        def _():
            wg_s[...] = wg_ref[0].astype(BF16)
            wu_s[...] = wu_ref[0].astype(BF16)
            wd_s[...] = wd_ref[0].astype(BF16)

        x = xbuf[slot]
        half = D_MODEL // 2
        xa = pltpu.bitcast(x & jnp.uint32(0xFFFF0000), F32).astype(BF16)
        xb = pltpu.bitcast(x << 16, F32).astype(BF16)
        g = _dot(xa, wg_s[:half, :]) + _dot(xb, wg_s[half:, :])
        u = _dot(xa, wu_s[:half, :]) + _dot(xb, wu_s[half:, :])
        hmid = (_silu(g) * u).astype(BF16)
        yb_ref[...] = _dot(hmid, wd_s[...])

    @pl.when(i >= nused)
    def _():
        yb_ref[...] = jnp.zeros(yb_ref.shape, F32)


def _experts(h2p, slot_tok, blk_e, nused, w_gate, w_up, w_down):
    nb = blk_e.shape[0]
    tok3 = slot_tok.reshape(nb, 1, ROUTE_BM)
    grid_spec = pltpu.PrefetchScalarGridSpec(
        num_scalar_prefetch=2,
        grid=(nb,),
        in_specs=[
            pl.BlockSpec((1, 1, ROUTE_BM), lambda i, be, nu: (i, 0, 0), memory_space=pltpu.SMEM),
            pl.BlockSpec((1, 1, ROUTE_BM), lambda i, be, nu: (jnp.minimum(i + 1, nb - 1), 0, 0),
                         memory_space=pltpu.SMEM),
            pl.BlockSpec(memory_space=pl.ANY),
            pl.BlockSpec((1, D_MODEL, D_EXPERT), lambda i, be, nu: (be[i], 0, 0)),
            pl.BlockSpec((1, D_MODEL, D_EXPERT), lambda i, be, nu: (be[i], 0, 0)),
            pl.BlockSpec((1, D_EXPERT, D_MODEL), lambda i, be, nu: (be[i], 0, 0)),
        ],
        out_specs=pl.BlockSpec((ROUTE_BM, D_MODEL), lambda i, be, nu: (i, 0)),
        scratch_shapes=[pltpu.VMEM((2, ROUTE_BM, D_MODEL // 2), jnp.uint32),
                        pltpu.SemaphoreType.DMA((2,)),
                        pltpu.VMEM((D_MODEL, D_EXPERT), BF16),
                        pltpu.VMEM((D_MODEL, D_EXPERT), BF16),
                        pltpu.VMEM((D_EXPERT, D_MODEL), BF16)],
    )
    return pl.pallas_call(
        _expert_kernel,
        grid_spec=grid_spec,
        out_shape=jax.ShapeDtypeStruct((nb * ROUTE_BM, D_MODEL), F32),
        compiler_params=_cparams(("arbitrary",)),
        name="expert",
    )(blk_e, nused, tok3, tok3, h2p, w_gate, w_up, w_down)


def _combine_kernel(lay, p0c_ref, p1c_ref, p0n_ref, p1n_ref, yb_hbm, x1_ref, wts_ref, mod_ref, fg_ref,
                    yp_ref, ys_ref, ybuf, sem):
    i = pl.program_id(0)
    n = pl.num_programs(0)
    slot = i % 2

    def gather(p0_ref, p1_ref, s, start):
        _gather_rows(yb_hbm, p0_ref, ybuf.at[s, 0], sem.at[s], TOK_TILE, start)
        _gather_rows(yb_hbm, p1_ref, ybuf.at[s, 1], sem.at[s], TOK_TILE, start)

    @pl.when(i == 0)
    def _():
        gather(p0c_ref, p1c_ref, 0, True)

    @pl.when(i + 1 < n)
    def _():
        gather(p0n_ref, p1n_ref, 1 - slot, True)

    gather(p0c_ref, p1c_ref, slot, False)
    m = mod_ref[0]
    w = wts_ref[...]
    ff = ybuf[slot, 0] * w[:, 0:1] + ybuf[slot, 1] * w[:, 1:2]
    y = _rms(x1_ref[...] + m[5:6] * ff) * fg_ref[...]

    @pl.when(i < lay.tiles_p)
    def _():
        yp_ref[...] = y

    @pl.when(i >= lay.tiles_p)
    def _():
        ys_ref[...] = y


def _combine(lay, pos0, pos1, yb, x1, wts, mod, final_g):
    nt = lay.tiles
    p0 = pos0.reshape(nt, 1, TOK_TILE)
    p1 = pos1.reshape(nt, 1, TOK_TILE)

    def smem_cur(i):
        return (i, 0, 0)

    def smem_next(i):
        return (jnp.minimum(i + 1, nt - 1), 0, 0)

    return pl.pallas_call(
        functools.partial(_combine_kernel, lay),
        grid=(nt,),
        in_specs=[pl.BlockSpec((1, 1, TOK_TILE), smem_cur, memory_space=pltpu.SMEM),
                  pl.BlockSpec((1, 1, TOK_TILE), smem_cur, memory_space=pltpu.SMEM),
                  pl.BlockSpec((1, 1, TOK_TILE), smem_next, memory_space=pltpu.SMEM),
                  pl.BlockSpec((1, 1, TOK_TILE), smem_next, memory_space=pltpu.SMEM),
                  pl.BlockSpec(memory_space=pl.ANY),
                  _row_spec(D_MODEL), _row_spec(LANES), _mod_spec(lay), _full_spec((1, D_MODEL))],
        out_specs=[pl.BlockSpec((TOK_TILE, D_MODEL), lambda i: (lay.xp_blk(i), 0)),
                   pl.BlockSpec((TOK_TILE, D_MODEL), lambda i: (lay.xs_blk(i), 0))],
        out_shape=[jax.ShapeDtypeStruct((lay.t_p, D_MODEL), F32),
                   jax.ShapeDtypeStruct((lay.t_s, D_MODEL), F32)],
        scratch_shapes=[pltpu.VMEM((2, 2, TOK_TILE, D_MODEL), F32), pltpu.SemaphoreType.DMA((2,))],
        compiler_params=_cparams(("arbitrary",)),
        name="combine",
    )(p0, p1, p0, p1, yb, x1, wts, mod, final_g)


def _dispatch_plan(ids, n_tok):
    flat_e = ids[:, :2].reshape(-1)
    n_assign = flat_e.shape[0]
    onehot = (flat_e[:, None] == jnp.arange(N_EXPERTS, dtype=I32)[None, :]).astype(I32)
    csum = jnp.cumsum(onehot, axis=0)
    rank = jnp.sum(csum * onehot, axis=1) - 1
    counts = csum[-1]
    padded = (counts + ROUTE_BM - 1) // ROUTE_BM * ROUTE_BM
    pad_end = jnp.cumsum(padded)
    pad_start = pad_end - padded
    dest = (jnp.sum(onehot * pad_start[None, :], axis=1) + rank).astype(I32)
    nb = -(-(n_assign + N_EXPERTS * (ROUTE_BM - 1)) // ROUTE_BM)
    cap = nb * ROUTE_BM
    slot_tok = jnp.zeros((cap,), I32).at[dest].set(jnp.arange(n_assign, dtype=I32) // 2)
    block_start = jnp.arange(nb, dtype=I32) * ROUTE_BM
    blk_e = jnp.minimum(jnp.searchsorted(pad_end, block_start, side='right'), N_EXPERTS - 1).astype(I32)
    nused = (pad_end[-1:] // ROUTE_BM).astype(I32)
    del n_tok
    return slot_tok, blk_e, nused, dest[0::2], dest[1::2]


def _grid_pos_embed(n_tokens):
    rows = n_tokens // GRID_W
    r = jnp.repeat(jnp.arange(rows, dtype=F32), GRID_W)
    col = jnp.tile(jnp.arange(GRID_W, dtype=F32), rows)
    n_freq = D_MODEL // 4
    freq = jnp.exp(jnp.arange(n_freq, dtype=F32) * (-math.log(10000.0) / n_freq))

    def enc(p):
        ang = p[:, None] * freq[None, :]
        return jnp.concatenate([jnp.sin(ang), jnp.cos(ang)], axis=-1)

    return jnp.concatenate([enc(r), enc(col)], axis=-1)


def _lane_pad(v, offset):
    return jnp.zeros((1, LANES), F32).at[0, offset:offset + v.shape[0]].set(v.astype(F32))


def kernel(x_prompt, x_sample, state_delta, c, c_ctx, norm1_g, w_mod, b_mod, w_in, conv_qkv_w, A_log, dt_bias, onorm_g, dw_w, dw_b, cln_g, cln_b, w_out, norm2_g, w_group, b_group, w_expert, b_expert, w_e_gate, w_e_up, w_e_down, final_g):
    n_p, l_p, _ = x_prompt.shape
    n_s, l_s, _ = x_sample.shape
    lay = _Layout(n_p, l_p, n_s, l_s)
    depth = w_in.shape[0]
    assert depth == 1
    xp = x_prompt.reshape(lay.t_p, D_MODEL)
    xs = x_sample.reshape(lay.t_s, D_MODEL)
    pos = _grid_pos_embed(l_s)

    cond = jnp.concatenate([c_ctx[None, :], c], axis=0)
    cond = jnp.pad(cond, ((0, (-cond.shape[0]) % 8), (0, 0)))
    mod = _modulation(cond, w_mod[0], b_mod[0]).reshape(cond.shape[0], 6, D_MODEL)

    n_gate = 4 * N_HEADS
    wi = w_in[0]
    w_main = jnp.concatenate([wi[:, :4 * MIX_A], wi[:, 4 * MIX_A + n_gate:]], axis=1).astype(BF16)
    w_small = jnp.pad(wi[:, 4 * MIX_A:4 * MIX_A + n_gate], ((0, 0), (0, LANES - n_gate))).astype(BF16)
    qkv_raw, z, ug, ba = _inproj(lay, xp, xs, pos, mod, norm1_g[0][None, :], w_main, w_small)

    cw = jnp.pad(conv_qkv_w[0], ((0, 8 - SHORT_CONV), (0, 0)))
    dw = jnp.pad(dw_w[0], ((0, 32 - CONV_W), (0, 0)))
    alog = _lane_pad(A_log[0].reshape(-1), 2 * N_HEADS)
    dtb = _lane_pad(dt_bias[0].reshape(-1), 2 * N_HEADS)
    qkv, u_conf, gate = _conv(lay, qkv_raw, ug, ba, cw, dw, dw_b[0][None, :], cln_g[0][None, :],
                              cln_b[0][None, :], alog, dtb)

    gate_t = gate[:, :16].reshape(lay.t // CHUNK, CHUNK, 16).transpose(0, 2, 1)
    o_pf, o_pb, s_fin = _delta(qkv, gate, gate_t, n_p, l_p, 0, l_p, None)
    s0 = state_delta[:, 0].reshape(n_s, 2 * N_HEADS, HEAD_DIM, HEAD_DIM)
    o_sf, o_sb = _delta(qkv, gate, gate_t, n_s, l_s, lay.t_p, DELTA_BLK, s0)
    o_f = jnp.concatenate([o_pf, o_sf], axis=0)
    o_b = jnp.concatenate([o_pb, o_sb], axis=0)

    w_route = jnp.pad(jnp.concatenate([w_group[0], w_expert[0]], axis=1),
                      ((0, 0), (0, LANES - N_GROUPS - N_EXPERTS))).astype(BF16)
    b_route = _lane_pad(jnp.concatenate([b_group[0], b_expert[0]]), 0)
    x1, h2p, ids, wts = _outproj(lay, xp, xs, pos, mod, o_f, o_b, z, u_conf, onorm_g[0][None, :],
                                 norm2_g[0][None, :], w_out[0].astype(BF16), w_route, b_route)

    slot_tok, blk_e, nused, pos0, pos1 = _dispatch_plan(ids, lay.t)
    yb = _experts(h2p, slot_tok, blk_e, nused, w_e_gate[0], w_e_up[0], w_e_down[0])
    y_p, y_s = _combine(lay, pos0, pos1, yb, x1, wts, mod, final_g[None, :])

    new_state = s_fin.reshape(n_p, 1, 2, N_HEADS, HEAD_DIM, HEAD_DIM)
    return (y_p.reshape(x_prompt.shape), y_s.reshape(x_sample.shape), new_state)
```

```python
import functools
import math

import jax
import jax.numpy as jnp
from jax import lax
from jax.experimental import pallas as pl
from jax.experimental.pallas import tpu as pltpu

F32 = jnp.float32
BF16 = jnp.bfloat16
I32 = jnp.int32

D_MODEL = 1024
MIX_A = 512
MIX_B = 512
HEAD_DIM = 128
N_HEADS = 4
SHORT_CONV = 5
CONV_W = 31
N_GROUPS = 4
EXPERTS_PER_GROUP = 8
N_EXPERTS = 32
D_EXPERT = 512
GRID_W = 64
EPS = 1e-6

LANES = 128
TOK_TILE = 256
CHUNK = 64
DELTA_BLK = 512
ROUTE_BM = 256
Q_HALO = 8
U_HALO = 16
VMEM_LIMIT = 56 * 1024 * 1024

HIGHEST = lax.Precision.HIGHEST


def _cparams(sem):
    return pltpu.CompilerParams(dimension_semantics=sem, vmem_limit_bytes=VMEM_LIMIT)


def _sigmoid(x):
    return 1.0 / (1.0 + jnp.exp(-x))


def _silu(x):
    return x * _sigmoid(x)


def _softplus(x):
    return jnp.maximum(x, 0.0) + jnp.log1p(jnp.exp(-jnp.abs(x)))


def _rms(x):
    return x * lax.rsqrt(jnp.mean(x * x, axis=-1, keepdims=True) + EPS)


def _dot(a, b):
    return jnp.dot(a, b, preferred_element_type=F32)


def _dot_nt(a, b):
    return lax.dot_general(a, b, (((1,), (1,)), ((), ())), preferred_element_type=F32)


def _dot_tn(a, b):
    return lax.dot_general(a, b, (((0,), (0,)), ((), ())), preferred_element_type=F32)


def _mod_kernel(cond_ref, w_ref, b_ref, o_ref):
    s = _silu(cond_ref[...])
    o_ref[...] = jnp.dot(s, w_ref[...], preferred_element_type=F32, precision=HIGHEST) + b_ref[...]


def _modulation(cond, w_mod, b_mod):
    n = cond.shape[0]
    tn = D_MODEL
    return pl.pallas_call(
        _mod_kernel,
        grid=(6 * D_MODEL // tn,),
        in_specs=[pl.BlockSpec((n, D_MODEL), lambda j: (0, 0)),
                  pl.BlockSpec((D_MODEL, tn), lambda j: (0, j)),
                  pl.BlockSpec((1, tn), lambda j: (0, j))],
        out_specs=pl.BlockSpec((n, tn), lambda j: (0, j)),
        out_shape=jax.ShapeDtypeStruct((n, 6 * D_MODEL), F32),
        compiler_params=_cparams(("parallel",)),
        name="mod",
    )(cond, w_mod, b_mod.reshape(1, -1))


class _Layout:
    def __init__(self, n_p, l_p, n_s, l_s):
        self.n_p, self.l_p, self.n_s, self.l_s = n_p, l_p, n_s, l_s
        self.t_p = n_p * l_p
        self.t_s = n_s * l_s
        self.t = self.t_p + self.t_s
        assert l_p % TOK_TILE == 0 and l_s % TOK_TILE == 0 and l_s % DELTA_BLK == 0
        self.tiles_p = self.t_p // TOK_TILE
        self.tiles_s = self.t_s // TOK_TILE
        self.tiles = self.tiles_p + self.tiles_s
        self.tps_p = l_p // TOK_TILE
        self.tps_s = l_s // TOK_TILE

    def is_sample(self, i):
        return i >= self.tiles_p

    def mod_row(self, i):
        return jnp.where(i < self.tiles_p, 0, 1 + (i - self.tiles_p) // self.tps_s)

    def pos_blk(self, i):
        return jnp.where(i < self.tiles_p, 0, (i - self.tiles_p) % self.tps_s)

    def xp_blk(self, i):
        return jnp.minimum(i, self.tiles_p - 1)

    def xs_blk(self, i):
        return jnp.maximum(i - self.tiles_p, 0)

    def seq_pos(self, i):
        in_s = i >= self.tiles_p
        pos = jnp.where(in_s, (i - self.tiles_p) % self.tps_s, i % self.tps_p)
        n = jnp.where(in_s, self.tps_s, self.tps_p)
        return pos, n


def _load_x(lay, i, xp_ref, xs_ref, pos_ref):
    return jnp.where(lay.is_sample(i), xs_ref[...] + pos_ref[...], xp_ref[...])


def _x_specs(lay):
    return [pl.BlockSpec((TOK_TILE, D_MODEL), lambda i: (lay.xp_blk(i), 0)),
            pl.BlockSpec((TOK_TILE, D_MODEL), lambda i: (lay.xs_blk(i), 0)),
            pl.BlockSpec((TOK_TILE, D_MODEL), lambda i: (lay.pos_blk(i), 0))]


def _mod_spec(lay):
    return pl.BlockSpec((1, 6, D_MODEL), lambda i: (lay.mod_row(i), 0, 0))


def _row_spec(width):
    return pl.BlockSpec((TOK_TILE, width), lambda i: (i, 0))


def _full_spec(shape):
    nd = len(shape)
    return pl.BlockSpec(shape, lambda i: (0,) * nd)


def _inproj_kernel(lay, xp_ref, xs_ref, pos_ref, mod_ref, g_ref, wm_ref, ws_ref,
                   qkv_ref, z_ref, ug_ref, ba_ref):
    i = pl.program_id(0)
    x = _load_x(lay, i, xp_ref, xs_ref, pos_ref)
    m = mod_ref[0]
    h = _rms(x) * g_ref[...] * (1.0 + m[1:2]) + m[0:1]
    hb = h.astype(BF16)
    qkv_ref[...] = _dot(hb, wm_ref[:, 0:3 * MIX_A])
    z_ref[...] = _dot(hb, wm_ref[:, 3 * MIX_A:4 * MIX_A])
    glu = _dot(hb, wm_ref[:, 4 * MIX_A:4 * MIX_A + 2 * MIX_B])
    ug_ref[...] = glu[:, :MIX_B] * _sigmoid(glu[:, MIX_B:])
    ba_ref[...] = _dot(hb, ws_ref[...])


def _inproj(lay, xp, xs, pos, mod, norm_g, w_main, w_small):
    t = lay.t
    return pl.pallas_call(
        functools.partial(_inproj_kernel, lay),
        grid=(lay.tiles,),
        in_specs=_x_specs(lay) + [_mod_spec(lay), _full_spec((1, D_MODEL)),
                                  _full_spec(w_main.shape), _full_spec(w_small.shape)],
        out_specs=[_row_spec(3 * MIX_A), _row_spec(MIX_A), _row_spec(MIX_B), _row_spec(LANES)],
        out_shape=[jax.ShapeDtypeStruct((t, 3 * MIX_A), F32),
                   jax.ShapeDtypeStruct((t, MIX_A), F32),
                   jax.ShapeDtypeStruct((t, MIX_B), F32),
                   jax.ShapeDtypeStruct((t, LANES), F32)],
        compiler_params=_cparams(("parallel",)),
        name="inproj",
    )(xp, xs, pos, mod, norm_g, w_main, w_small)


CONV_ROWS = 64


def _conv_kernel(lay, qc_ref, qp_ref, qn_ref, uc_ref, up_ref, un_ref, ba_ref, cw_ref, dw_ref,
                 dwb_ref, lng_ref, lnb_ref, alog_ref, dtb_ref,
                 qkv_ref, uo_ref, gate_ref, eq_scr, eu_scr):
    i = pl.program_id(0)
    pos, n = lay.seq_pos(i)
    has_prev = pos != 0
    has_next = pos != n - 1
    eq_scr[0:Q_HALO] = jnp.where(has_prev, qp_ref[...], 0.0)
    eq_scr[Q_HALO:Q_HALO + TOK_TILE] = qc_ref[...]
    eq_scr[Q_HALO + TOK_TILE:] = jnp.where(has_next, qn_ref[...], 0.0)
    eu_scr[0:U_HALO] = jnp.where(has_prev, up_ref[...], 0.0)
    eu_scr[U_HALO:U_HALO + TOK_TILE] = uc_ref[...]
    eu_scr[U_HALO + TOK_TILE:] = jnp.where(has_next, un_ref[...], 0.0)

    q_off = Q_HALO - SHORT_CONV // 2
    u_off = U_HALO - CONV_W // 2
    for rc in range(TOK_TILE // CONV_ROWS):
        r0 = rc * CONV_ROWS
        for p in range(3):
            c0 = p * MIX_A
            acc = eq_scr[q_off + r0:q_off + r0 + CONV_ROWS, c0:c0 + MIX_A] * cw_ref[0:1, c0:c0 + MIX_A]
            for tap in range(1, SHORT_CONV):
                acc = acc + (eq_scr[q_off + tap + r0:q_off + tap + r0 + CONV_ROWS, c0:c0 + MIX_A]
                             * cw_ref[tap:tap + 1, c0:c0 + MIX_A])
            y = _silu(acc)
            for hd in range(N_HEADS):
                yh = y[:, hd * HEAD_DIM:(hd + 1) * HEAD_DIM]
                if p < 2:
                    yh = yh * lax.rsqrt(jnp.sum(yh * yh, axis=-1, keepdims=True) + EPS)
                qkv_ref[r0:r0 + CONV_ROWS, c0 + hd * HEAD_DIM:c0 + (hd + 1) * HEAD_DIM] = yh
        acc = eu_scr[u_off + r0:u_off + r0 + CONV_ROWS, :] * dw_ref[0:1, :]
        for tap in range(1, CONV_W):
            acc = acc + eu_scr[u_off + tap + r0:u_off + tap + r0 + CONV_ROWS, :] * dw_ref[tap:tap + 1, :]
        u = acc + dwb_ref[...]
        uc = u - jnp.mean(u, axis=-1, keepdims=True)
        un = uc * lax.rsqrt(jnp.mean(uc * uc, axis=-1, keepdims=True) + EPS)
        uo_ref[r0:r0 + CONV_ROWS, :] = _silu(un * lng_ref[...] + lnb_ref[...]).astype(BF16)

    x = ba_ref[...]
    lane = lax.broadcasted_iota(I32, x.shape, 1)
    g = -jnp.exp(alog_ref[...]) * _softplus(x + dtb_ref[...])
    gate_ref[...] = jnp.where(lane < 2 * N_HEADS, _sigmoid(x), g)


def _conv(lay, qkv_raw, ug, ba, cw, dw, dwb, lng, lnb, alog, dtb):
    t = lay.t
    qh = TOK_TILE // Q_HALO
    uh = TOK_TILE // U_HALO
    n_qh = t // Q_HALO
    n_uh = t // U_HALO
    in_specs = [
        _row_spec(3 * MIX_A),
        pl.BlockSpec((Q_HALO, 3 * MIX_A), lambda i: (jnp.maximum(i * qh - 1, 0), 0)),
        pl.BlockSpec((Q_HALO, 3 * MIX_A), lambda i: (jnp.minimum((i + 1) * qh, n_qh - 1), 0)),
        _row_spec(MIX_B),
        pl.BlockSpec((U_HALO, MIX_B), lambda i: (jnp.maximum(i * uh - 1, 0), 0)),
        pl.BlockSpec((U_HALO, MIX_B), lambda i: (jnp.minimum((i + 1) * uh, n_uh - 1), 0)),
        _row_spec(LANES),
        _full_spec(cw.shape), _full_spec(dw.shape), _full_spec(dwb.shape),
        _full_spec(lng.shape), _full_spec(lnb.shape), _full_spec(alog.shape), _full_spec(dtb.shape),
    ]
    return pl.pallas_call(
        functools.partial(_conv_kernel, lay),
        grid=(lay.tiles,),
        in_specs=in_specs,
        out_specs=[_row_spec(3 * MIX_A), _row_spec(MIX_B), _row_spec(LANES)],
        out_shape=[jax.ShapeDtypeStruct((t, 3 * MIX_A), F32),
                   jax.ShapeDtypeStruct((t, MIX_B), BF16),
                   jax.ShapeDtypeStruct((t, LANES), F32)],
        scratch_shapes=[pltpu.VMEM((TOK_TILE + 2 * Q_HALO, 3 * MIX_A), F32),
                        pltpu.VMEM((TOK_TILE + 2 * U_HALO, MIX_B), F32)],
        compiler_params=_cparams(("parallel",)),
        name="conv",
    )(qkv_raw, qkv_raw, qkv_raw, ug, ug, ug, ba, cw, dw, dwb, lng, lnb, alog, dtb)


INV_BASE = 8


def _b16(xs):
    return [x.astype(BF16) for x in xs]


def _tri_inverse_minus_eye(nmats, rows, cols):
    assert INV_BASE == 8
    c = nmats[0].shape[0]
    shift = int(math.log2(INV_BASE))
    same = (rows >> shift) == (cols >> shift)
    n1 = [jnp.where(same, n, 0.0) for n in nmats]
    n1b = _b16(n1)
    n2 = [_dot(x, x) for x in n1b]
    n2b = _b16(n2)
    r = [_dot(jnp.concatenate([a, b], axis=0), b) for a, b in zip(n1b, n2b)]
    q = [a + b + x[:c] for a, b, x in zip(n1, n2, r)]
    n4 = [x[c:] for x in r]
    qn4 = [_dot(a, b) for a, b in zip(_b16(q), _b16(n4))]
    q = [a + b + x for a, b, x in zip(q, n4, qn4)]
    while (1 << shift) < c:
        off = ((rows >> (shift + 1)) == (cols >> (shift + 1))) & ((rows >> shift) != (cols >> shift))
        a = [jnp.where(off, -n, 0.0) for n in nmats]
        x = [ai + _dot(ab, qb) for ai, ab, qb in zip(a, _b16(a), _b16(q))]
        qx = [_dot(qb, xb) for qb, xb in zip(_b16(q), _b16(x))]
        q = [qi - xi - qxi for qi, xi, qxi in zip(q, x, qx)]
        shift += 1
    return q


def _delta_units(units):
    c = CHUNK
    rows = lax.broadcasted_iota(I32, (c, c), 0)
    cols = lax.broadcasted_iota(I32, (c, c), 1)
    pre = []
    for d, hd, qkv_ref, r0, gate, gcol, grow, s in units:
        incl = (rows >= cols) if d == 0 else (rows <= cols)
        lane = 2 * N_HEADS + N_HEADS * d + hd
        beta = gate[:, N_HEADS * d + hd:N_HEADS * d + hd + 1]
        gc = gcol[:, lane:lane + 1]
        gr = grow[lane:lane + 1, :]
        last = c - 1 if d == 0 else 0
        gtot = gcol[last:last + 1, lane:lane + 1]
        dec = jnp.where(incl, jnp.exp(jnp.where(incl, gc - gr, 0.0)), 0.0)
        q = qkv_ref[pl.ds(r0, c), hd * HEAD_DIM:(hd + 1) * HEAD_DIM]
        k = qkv_ref[pl.ds(r0, c), MIX_A + hd * HEAD_DIM:MIX_A + (hd + 1) * HEAD_DIM]
        v = qkv_ref[pl.ds(r0, c), 2 * MIX_A + hd * HEAD_DIM:2 * MIX_A + (hd + 1) * HEAD_DIM]
        qs = q * (HEAD_DIM ** -0.5)
        kbeta = k * beta
        egc = jnp.exp(gc)
        pre.append(dict(d=d, beta=beta, gc=gc, gtot=gtot, dec=dec, k=k, v=v, qs=qs, kbeta=kbeta, egc=egc, s=s))

    ak = [_dot_nt(jnp.concatenate([p['kbeta'], p['qs']], axis=0).astype(BF16), p['k'].astype(BF16))
          for p in pre]
    nmats, qks = [], []
    for p, a in zip(pre, ak):
        strict = (rows > cols) if p['d'] == 0 else (rows < cols)
        nmats.append(jnp.where(strict, -a[:c] * p['dec'], 0.0))
        qks.append(a[c:] * p['dec'])
    qinv = _tri_inverse_minus_eye(nmats, rows, cols)

    rhs = [jnp.concatenate([p['v'] * p['beta'], p['kbeta'] * p['egc']], axis=1) for p in pre]
    uw = [r + _dot(qi, rb) for r, qi, rb in zip(rhs, _b16(qinv), _b16(rhs))]
    sw = [_dot(jnp.concatenate([x[:, HEAD_DIM:], p['qs'] * p['egc']], axis=0).astype(BF16), p['s'].astype(BF16))
          for x, p in zip(uw, pre)]
    vnb = [(x[:, :HEAD_DIM] - y[:c]).astype(BF16) for x, y in zip(uw, sw)]
    o = [y[c:] + _dot(qk, vb) for y, qk, vb in zip(sw, _b16(qks), vnb)]
    kdec = [(p['k'] * jnp.exp(p['gtot'] - p['gc'])).astype(BF16) for p in pre]
    s_new = [p['s'] * jnp.exp(p['gtot']) + _dot_tn(kd, vb) for p, kd, vb in zip(pre, kdec, vnb)]
    return list(zip(o, s_new))


def _delta_kernel(has_s0, emit_final, n_chunk, *refs):
    refs = list(refs)
    qkvf_ref, qkvb_ref, gatef_ref, gateb_ref, gtf_ref, gtb_ref = refs[:6]
    refs = refs[6:]
    s0_ref = refs.pop(0) if has_s0 else None
    of_ref, ob_ref = refs[:2]
    refs = refs[2:]
    sfin_ref = refs.pop(0) if emit_final else None
    s_scr = refs[0]

    j = pl.program_id(1)

    @pl.when(j == 0)
    def _():
        if has_s0:
            s_scr[...] = s0_ref[0]
        else:
            s_scr[...] = jnp.zeros(s_scr.shape, F32)

    c = CHUNK
    rows = lax.broadcasted_iota(I32, (c, c), 0)
    cols = lax.broadcasted_iota(I32, (c, c), 1)
    tri_lo = (rows >= cols).astype(F32)
    tri_up = (rows <= cols).astype(F32)

    def body(ci, carry):
        units = []
        for d in range(2):
            cidx = ci if d == 0 else n_chunk - 1 - ci
            r0 = pl.multiple_of(cidx * c, c)
            qkv_ref = qkvf_ref if d == 0 else qkvb_ref
            gate = (gatef_ref if d == 0 else gateb_ref)[pl.ds(r0, c), :]
            gt = (gtf_ref if d == 0 else gtb_ref)[cidx]
            tri = tri_lo if d == 0 else tri_up
            tri_t = tri_up if d == 0 else tri_lo
            gcol = jnp.dot(tri, gate, preferred_element_type=F32, precision=HIGHEST)
            grow = jnp.dot(gt, tri_t, preferred_element_type=F32, precision=HIGHEST)
            for hd in range(N_HEADS):
                units.append((d, hd, qkv_ref, r0, gate, gcol, grow, s_scr[N_HEADS * d + hd]))
        for (d, hd, _, r0, _, _, _, _), (o, s_new) in zip(units, _delta_units(units)):
            s_scr[N_HEADS * d + hd] = s_new
            (of_ref if d == 0 else ob_ref)[pl.ds(r0, c), hd * HEAD_DIM:(hd + 1) * HEAD_DIM] = o
        return carry

    lax.fori_loop(0, n_chunk, body, 0)

    if emit_final:
        @pl.when(j == pl.num_programs(1) - 1)
        def _():
            sfin_ref[0] = s_scr[...]


def _delta(qkv, gate, gate_t, n_seq, seq_len, row0, blk, s0):
    t = qkv.shape[0]
    nblk = seq_len // blk
    n_chunk = blk // CHUNK
    b0 = row0 // blk
    has_s0 = s0 is not None
    emit_final = not has_s0
    n_state = 2 * N_HEADS

    def fwd(b, j):
        return b0 + b * nblk + j

    def bwd(b, j):
        return b0 + b * nblk + (nblk - 1 - j)

    in_specs = [
        pl.BlockSpec((blk, 3 * MIX_A), lambda b, j: (fwd(b, j), 0)),
        pl.BlockSpec((blk, 3 * MIX_A), lambda b, j: (bwd(b, j), 0)),
        pl.BlockSpec((blk, LANES), lambda b, j: (fwd(b, j), 0)),
        pl.BlockSpec((blk, LANES), lambda b, j: (bwd(b, j), 0)),
        pl.BlockSpec((n_chunk, 16, CHUNK), lambda b, j: (fwd(b, j), 0, 0)),
        pl.BlockSpec((n_chunk, 16, CHUNK), lambda b, j: (bwd(b, j), 0, 0)),
    ]
    args = [qkv, qkv, gate, gate, gate_t, gate_t]
    if has_s0:
        in_specs.append(pl.BlockSpec((1, n_state, HEAD_DIM, HEAD_DIM), lambda b, j: (b, 0, 0, 0)))
        args.append(s0)
    n_rows = n_seq * seq_len
    out_specs = [pl.BlockSpec((blk, MIX_A), lambda b, j: (b * nblk + j, 0)),
                 pl.BlockSpec((blk, MIX_A), lambda b, j: (b * nblk + (nblk - 1 - j), 0))]
    out_shape = [jax.ShapeDtypeStruct((n_rows, MIX_A), F32), jax.ShapeDtypeStruct((n_rows, MIX_A), F32)]
    if emit_final:
        out_specs.append(pl.BlockSpec((1, n_state, HEAD_DIM, HEAD_DIM), lambda b, j: (b, 0, 0, 0)))
        out_shape.append(jax.ShapeDtypeStruct((n_seq, n_state, HEAD_DIM, HEAD_DIM), F32))
    del t
    return pl.pallas_call(
        functools.partial(_delta_kernel, has_s0, emit_final, n_chunk),
        grid=(n_seq, nblk),
        in_specs=in_specs,
        out_specs=out_specs,
        out_shape=out_shape,
        scratch_shapes=[pltpu.VMEM((n_state, HEAD_DIM, HEAD_DIM), F32)],
        compiler_params=_cparams(("parallel", "arbitrary")),
        name="delta_latent" if has_s0 else "delta_prompt",
    )(*args)


def _outproj_kernel(lay, xp_ref, xs_ref, pos_ref, mod_ref, opf_ref, opb_ref, osf_ref, osb_ref, z_ref, u_ref,
                    og_ref, n2_ref, wo_ref, wr_ref, br_ref, x1_ref, h2_ref, ids_ref, wts_ref):
    i = pl.program_id(0)
    x0 = _load_x(lay, i, xp_ref, xs_ref, pos_ref)
    m = mod_ref[0]
    o = jnp.where(lay.is_sample(i), osf_ref[...] + osb_ref[...], opf_ref[...] + opb_ref[...])
    z = z_ref[...]
    mix = _dot(u_ref[...], wo_ref[MIX_A:, :])
    for hd in range(N_HEADS):
        sl = slice(hd * HEAD_DIM, (hd + 1) * HEAD_DIM)
        oh = _rms(o[:, sl]) * og_ref[...] * _silu(z[:, sl])
        mix = mix + _dot(oh.astype(BF16), wo_ref[sl, :])
    x1 = x0 + m[2:3] * mix
    x1_ref[...] = x1
    h2 = _rms(x1) * n2_ref[...] * (1.0 + m[4:5]) + m[3:4]
    h2b = h2.astype(BF16)
    half = D_MODEL // 2
    hi = pltpu.bitcast(h2b[:, :half].astype(F32), jnp.uint32)
    lo = pltpu.bitcast(h2b[:, half:].astype(F32), jnp.uint32)
    h2_ref[...] = (hi & jnp.uint32(0xFFFF0000)) | (lo >> 16)

    logits = _dot(h2b, wr_ref[...]) + br_ref[...]
    lane = lax.broadcasted_iota(I32, logits.shape, 1)
    neg = jnp.float32(-jnp.inf)
    big = jnp.int32(LANES)
    gl = jnp.where(lane < N_GROUPS, logits, neg)
    gmax = jnp.max(gl, axis=-1, keepdims=True)
    grp = jnp.min(jnp.where(gl == gmax, lane, big), axis=-1, keepdims=True)
    p_grp = 1.0 / jnp.sum(jnp.where(lane < N_GROUPS, jnp.exp(gl - gmax), 0.0), axis=-1, keepdims=True)
    e_lane = lane - N_GROUPS
    in_grp = (e_lane >= grp * EXPERTS_PER_GROUP) & (e_lane < (grp + 1) * EXPERTS_PER_GROUP)
    el = jnp.where(in_grp, logits, neg)
    m1 = jnp.max(el, axis=-1, keepdims=True)
    i1 = jnp.min(jnp.where(el == m1, lane, big), axis=-1, keepdims=True)
    el2 = jnp.where(lane == i1, neg, el)
    m2 = jnp.max(el2, axis=-1, keepdims=True)
    i2 = jnp.min(jnp.where(el2 == m2, lane, big), axis=-1, keepdims=True)
    e2 = jnp.exp(m2 - m1)
    w1 = p_grp / (1.0 + e2)
    w2 = p_grp * e2 / (1.0 + e2)
    ids_ref[...] = jnp.where(lane == 0, i1 - N_GROUPS, jnp.where(lane == 1, i2 - N_GROUPS, 0))
    wts_ref[...] = jnp.where(lane == 0, w1, jnp.where(lane == 1, w2, 0.0))


def _outproj(lay, xp, xs, pos, mod, o_pf, o_pb, o_sf, o_sb, z, u, onorm_g, norm2_g, w_out, w_route, b_route):
    t = lay.t
    p_spec = pl.BlockSpec((TOK_TILE, MIX_A), lambda i: (lay.xp_blk(i), 0))
    s_spec = pl.BlockSpec((TOK_TILE, MIX_A), lambda i: (lay.xs_blk(i), 0))
    return pl.pallas_call(
        functools.partial(_outproj_kernel, lay),
        grid=(lay.tiles,),
        in_specs=_x_specs(lay) + [_mod_spec(lay), p_spec, p_spec, s_spec, s_spec, _row_spec(MIX_A),
                                  _row_spec(MIX_B), _full_spec((1, HEAD_DIM)), _full_spec((1, D_MODEL)),
                                  _full_spec(w_out.shape), _full_spec(w_route.shape),
                                  _full_spec(b_route.shape)],
        out_specs=[_row_spec(D_MODEL), _row_spec(D_MODEL // 2), _row_spec(LANES), _row_spec(LANES)],
        out_shape=[jax.ShapeDtypeStruct((t, D_MODEL), F32),
                   jax.ShapeDtypeStruct((t, D_MODEL // 2), jnp.uint32),
                   jax.ShapeDtypeStruct((t, LANES), I32),
                   jax.ShapeDtypeStruct((t, LANES), F32)],
        compiler_params=_cparams(("parallel",)),
        name="outproj",
    )(xp, xs, pos, mod, o_pf, o_pb, o_sf, o_sb, z, u, onorm_g, norm2_g, w_out, w_route, b_route)


def _gather_rows(src_hbm, idx_ref, dst, sem, n_rows, start):
    def body(r, carry):
        row = idx_ref[0, 0, r] if start else 0
        cp = pltpu.make_async_copy(src_hbm.at[pl.ds(row, 1)], dst.at[pl.ds(r, 1)], sem)
        if start:
            cp.start()
        else:
            cp.wait()
        return carry
    lax.fori_loop(0, n_rows, body, 0)


def _expert_kernel(blk_e_ref, nused_ref, tokc_ref, tokn_ref, h2_hbm, wg_ref, wu_ref, wd_ref, yb_ref,
                   xbuf, sem, wg_s, wu_s, wd_s):
    i = pl.program_id(0)
    nused = nused_ref[0]
    slot = i % 2

    @pl.when((i == 0) & (nused > 0))
    def _():
        _gather_rows(h2_hbm, tokc_ref, xbuf.at[0], sem.at[0], ROUTE_BM, True)

    @pl.when(i + 1 < nused)
    def _():
        _gather_rows(h2_hbm, tokn_ref, xbuf.at[1 - slot], sem.at[1 - slot], ROUTE_BM, True)

    @pl.when(i < nused)
    def _():
        _gather_rows(h2_hbm, tokc_ref, xbuf.at[slot], sem.at[slot], ROUTE_BM, False)
        changed = (i == 0) | (blk_e_ref[i] != blk_e_ref[jnp.maximum(i - 1, 0)])

        @pl.when(changed)
        def _():
            wg_s[...] = wg_ref[0].astype(BF16)
            wu_s[...] = wu_ref[0].astype(BF16)
            wd_s[...] = wd_ref[0].astype(BF16)

        x = xbuf[slot]
        half = D_MODEL // 2
        xa = pltpu.bitcast(x & jnp.uint32(0xFFFF0000), F32).astype(BF16)
        xb = pltpu.bitcast(x << 16, F32).astype(BF16)
        g = _dot(xa, wg_s[:half, :]) + _dot(xb, wg_s[half:, :])
        u = _dot(xa, wu_s[:half, :]) + _dot(xb, wu_s[half:, :])
        hmid = (_silu(g) * u).astype(BF16)
        yb_ref[...] = _dot(hmid, wd_s[...])

    @pl.when(i >= nused)
    def _():
        yb_ref[...] = jnp.zeros(yb_ref.shape, F32)


def _experts(h2p, slot_tok, blk_e, nused, w_gate, w_up, w_down):
    nb = blk_e.shape[0]
    tok3 = slot_tok.reshape(nb, 1, ROUTE_BM)
    grid_spec = pltpu.PrefetchScalarGridSpec(
        num_scalar_prefetch=2,
        grid=(nb,),
        in_specs=[
            pl.BlockSpec((1, 1, ROUTE_BM), lambda i, be, nu: (i, 0, 0), memory_space=pltpu.SMEM),
            pl.BlockSpec((1, 1, ROUTE_BM), lambda i, be, nu: (jnp.minimum(i + 1, nb - 1), 0, 0),
                         memory_space=pltpu.SMEM),
            pl.BlockSpec(memory_space=pl.ANY),
            pl.BlockSpec((1, D_MODEL, D_EXPERT), lambda i, be, nu: (be[i], 0, 0)),
            pl.BlockSpec((1, D_MODEL, D_EXPERT), lambda i, be, nu: (be[i], 0, 0)),
            pl.BlockSpec((1, D_EXPERT, D_MODEL), lambda i, be, nu: (be[i], 0, 0)),
        ],
        out_specs=pl.BlockSpec((ROUTE_BM, D_MODEL), lambda i, be, nu: (i, 0)),
        scratch_shapes=[pltpu.VMEM((2, ROUTE_BM, D_MODEL // 2), jnp.uint32),
                        pltpu.SemaphoreType.DMA((2,)),
                        pltpu.VMEM((D_MODEL, D_EXPERT), BF16),
                        pltpu.VMEM((D_MODEL, D_EXPERT), BF16),
                        pltpu.VMEM((D_EXPERT, D_MODEL), BF16)],
    )
    return pl.pallas_call(
        _expert_kernel,
        grid_spec=grid_spec,
        out_shape=jax.ShapeDtypeStruct((nb * ROUTE_BM, D_MODEL), F32),
        compiler_params=_cparams(("arbitrary",)),
        name="expert",
    )(blk_e, nused, tok3, tok3, h2p, w_gate, w_up, w_down)


def _combine_kernel(lay, p0c_ref, p1c_ref, p0n_ref, p1n_ref, yb_hbm, x1_ref, wts_ref, mod_ref, fg_ref,
                    yp_ref, ys_ref, ybuf, sem):
    i = pl.program_id(0)
    n = pl.num_programs(0)
    slot = i % 2

    def gather(p0_ref, p1_ref, s, start):
        _gather_rows(yb_hbm, p0_ref, ybuf.at[s, 0], sem.at[s], TOK_TILE, start)
        _gather_rows(yb_hbm, p1_ref, ybuf.at[s, 1], sem.at[s], TOK_TILE, start)

    @pl.when(i == 0)
    def _():
        gather(p0c_ref, p1c_ref, 0, True)

    @pl.when(i + 1 < n)
    def _():
        gather(p0n_ref, p1n_ref, 1 - slot, True)

    gather(p0c_ref, p1c_ref, slot, False)
    m = mod_ref[0]
    w = wts_ref[...]
    ff = ybuf[slot, 0] * w[:, 0:1] + ybuf[slot, 1] * w[:, 1:2]
    y = _rms(x1_ref[...] + m[5:6] * ff) * fg_ref[...]

    @pl.when(i < lay.tiles_p)
    def _():
        yp_ref[...] = y

    @pl.when(i >= lay.tiles_p)
    def _():
        ys_ref[...] = y


def _combine(lay, pos0, pos1, yb, x1, wts, mod, final_g):
    nt = lay.tiles
    p0 = pos0.reshape(nt, 1, TOK_TILE)
    p1 = pos1.reshape(nt, 1, TOK_TILE)

    def smem_cur(i):
        return (i, 0, 0)

    def smem_next(i):
        return (jnp.minimum(i + 1, nt - 1), 0, 0)

    return pl.pallas_call(
        functools.partial(_combine_kernel, lay),
        grid=(nt,),
        in_specs=[pl.BlockSpec((1, 1, TOK_TILE), smem_cur, memory_space=pltpu.SMEM),
                  pl.BlockSpec((1, 1, TOK_TILE), smem_cur, memory_space=pltpu.SMEM),
                  pl.BlockSpec((1, 1, TOK_TILE), smem_next, memory_space=pltpu.SMEM),
                  pl.BlockSpec((1, 1, TOK_TILE), smem_next, memory_space=pltpu.SMEM),
                  pl.BlockSpec(memory_space=pl.ANY),
                  _row_spec(D_MODEL), _row_spec(LANES), _mod_spec(lay), _full_spec((1, D_MODEL))],
        out_specs=[pl.BlockSpec((TOK_TILE, D_MODEL), lambda i: (lay.xp_blk(i), 0)),
                   pl.BlockSpec((TOK_TILE, D_MODEL), lambda i: (lay.xs_blk(i), 0))],
        out_shape=[jax.ShapeDtypeStruct((lay.t_p, D_MODEL), F32),
                   jax.ShapeDtypeStruct((lay.t_s, D_MODEL), F32)],
        scratch_shapes=[pltpu.VMEM((2, 2, TOK_TILE, D_MODEL), F32), pltpu.SemaphoreType.DMA((2,))],
        compiler_params=_cparams(("arbitrary",)),
        name="combine",
    )(p0, p1, p0, p1, yb, x1, wts, mod, final_g)


def _dispatch_plan(ids, n_tok):
    flat_e = ids[:, :2].reshape(-1)
    n_assign = flat_e.shape[0]
    onehot = (flat_e[:, None] == jnp.arange(N_EXPERTS, dtype=I32)[None, :]).astype(I32)
    csum = jnp.cumsum(onehot, axis=0)
    rank = jnp.sum(csum * onehot, axis=1) - 1
    counts = csum[-1]
    padded = (counts + ROUTE_BM - 1) // ROUTE_BM * ROUTE_BM
    pad_end = jnp.cumsum(padded)
    pad_start = pad_end - padded
    dest = (jnp.sum(onehot * pad_start[None, :], axis=1) + rank).astype(I32)
    nb = -(-(n_assign + N_EXPERTS * (ROUTE_BM - 1)) // ROUTE_BM)
    cap = nb * ROUTE_BM
    slot_tok = jnp.zeros((cap,), I32).at[dest].set(jnp.arange(n_assign, dtype=I32) // 2)
    block_start = jnp.arange(nb, dtype=I32) * ROUTE_BM
    blk_e = jnp.minimum(jnp.searchsorted(pad_end, block_start, side='right'), N_EXPERTS - 1).astype(I32)
    nused = (pad_end[-1:] // ROUTE_BM).astype(I32)
    del n_tok
    return slot_tok, blk_e, nused, dest[0::2], dest[1::2]


def _grid_pos_embed(n_tokens):
    rows = n_tokens // GRID_W
    r = jnp.repeat(jnp.arange(rows, dtype=F32), GRID_W)
    col = jnp.tile(jnp.arange(GRID_W, dtype=F32), rows)
    n_freq = D_MODEL // 4
    freq = jnp.exp(jnp.arange(n_freq, dtype=F32) * (-math.log(10000.0) / n_freq))

    def enc(p):
        ang = p[:, None] * freq[None, :]
        return jnp.concatenate([jnp.sin(ang), jnp.cos(ang)], axis=-1)

    return jnp.concatenate([enc(r), enc(col)], axis=-1)


def _lane_pad(v, offset):
    return jnp.zeros((1, LANES), F32).at[0, offset:offset + v.shape[0]].set(v.astype(F32))


def kernel(x_prompt, x_sample, state_delta, c, c_ctx, norm1_g, w_mod, b_mod, w_in, conv_qkv_w, A_log, dt_bias, onorm_g, dw_w, dw_b, cln_g, cln_b, w_out, norm2_g, w_group, b_group, w_expert, b_expert, w_e_gate, w_e_up, w_e_down, final_g):
    n_p, l_p, _ = x_prompt.shape
    n_s, l_s, _ = x_sample.shape
    lay = _Layout(n_p, l_p, n_s, l_s)
    depth = w_in.shape[0]
    assert depth == 1
    xp = x_prompt.reshape(lay.t_p, D_MODEL)
    xs = x_sample.reshape(lay.t_s, D_MODEL)
    pos = _grid_pos_embed(l_s)

    cond = jnp.concatenate([c_ctx[None, :], c], axis=0)
    cond = jnp.pad(cond, ((0, (-cond.shape[0]) % 8), (0, 0)))
    mod = _modulation(cond, w_mod[0], b_mod[0]).reshape(cond.shape[0], 6, D_MODEL)

    n_gate = 4 * N_HEADS
    wi = w_in[0]
    w_main = jnp.concatenate([wi[:, :4 * MIX_A], wi[:, 4 * MIX_A + n_gate:]], axis=1).astype(BF16)
    w_small = jnp.pad(wi[:, 4 * MIX_A:4 * MIX_A + n_gate], ((0, 0), (0, LANES - n_gate))).astype(BF16)
    qkv_raw, z, ug, ba = _inproj(lay, xp, xs, pos, mod, norm1_g[0][None, :], w_main, w_small)

    cw = jnp.pad(conv_qkv_w[0], ((0, 8 - SHORT_CONV), (0, 0)))
    dw = jnp.pad(dw_w[0], ((0, 32 - CONV_W), (0, 0)))
    alog = _lane_pad(A_log[0].reshape(-1), 2 * N_HEADS)
    dtb = _lane_pad(dt_bias[0].reshape(-1), 2 * N_HEADS)
    qkv, u_conf, gate = _conv(lay, qkv_raw, ug, ba, cw, dw, dw_b[0][None, :], cln_g[0][None, :],
                              cln_b[0][None, :], alog, dtb)

    gate_t = gate[:, :16].reshape(lay.t // CHUNK, CHUNK, 16).transpose(0, 2, 1)
    o_pf, o_pb, s_fin = _delta(qkv, gate, gate_t, n_p, l_p, 0, l_p, None)
    s0 = state_delta[:, 0].reshape(n_s, 2 * N_HEADS, HEAD_DIM, HEAD_DIM)
    o_sf, o_sb = _delta(qkv, gate, gate_t, n_s, l_s, lay.t_p, DELTA_BLK, s0)

    w_route = jnp.pad(jnp.concatenate([w_group[0], w_expert[0]], axis=1),
                      ((0, 0), (0, LANES - N_GROUPS - N_EXPERTS))).astype(BF16)
    b_route = _lane_pad(jnp.concatenate([b_group[0], b_expert[0]]), 0)
    x1, h2p, ids, wts = _outproj(lay, xp, xs, pos, mod, o_pf, o_pb, o_sf, o_sb, z, u_conf, onorm_g[0][None, :],
                                 norm2_g[0][None, :], w_out[0].astype(BF16), w_route, b_route)

    slot_tok, blk_e, nused, pos0, pos1 = _dispatch_plan(ids, lay.t)
    yb = _experts(h2p, slot_tok, blk_e, nused, w_e_gate[0], w_e_up[0], w_e_down[0])
    y_p, y_s = _combine(lay, pos0, pos1, yb, x1, wts, mod, final_g[None, :])

    new_state = s_fin.reshape(n_p, 1, 2, N_HEADS, HEAD_DIM, HEAD_DIM)
    return (y_p.reshape(x_prompt.shape), y_s.reshape(x_sample.shape), new_state)
```

```python
import functools
import math

import jax
import jax.numpy as jnp
from jax import lax
from jax.experimental import pallas as pl
from jax.experimental.pallas import tpu as pltpu
from jax.experimental.pallas import tpu_sc as plsc

F32 = jnp.float32
BF16 = jnp.bfloat16
I32 = jnp.int32

D_MODEL = 1024
MIX_A = 512
MIX_B = 512
HEAD_DIM = 128
N_HEADS = 4
SHORT_CONV = 5
CONV_W = 31
N_GROUPS = 4
EXPERTS_PER_GROUP = 8
N_EXPERTS = 32
D_EXPERT = 512
GRID_W = 64
EPS = 1e-6

LANES = 128
TOK_TILE = 256
CHUNK = 64
DELTA_BLK = 512
ROUTE_BM = 256
DISPATCH_CHUNK = 64
COMBINE_CHUNK = 32
Q_HALO = 8
U_HALO = 16
VMEM_LIMIT = 56 * 1024 * 1024

HIGHEST = lax.Precision.HIGHEST


def _cparams(sem):
    return pltpu.CompilerParams(dimension_semantics=sem, vmem_limit_bytes=VMEM_LIMIT)


def _sigmoid(x):
    return 1.0 / (1.0 + jnp.exp(-x))


def _silu(x):
    return x * _sigmoid(x)


def _softplus(x):
    return jnp.maximum(x, 0.0) + jnp.log1p(jnp.exp(-jnp.abs(x)))


def _rms(x):
    return x * lax.rsqrt(jnp.mean(x * x, axis=-1, keepdims=True) + EPS)


def _dot(a, b):
    return jnp.dot(a, b, preferred_element_type=F32)


def _dot_nt(a, b):
    return lax.dot_general(a, b, (((1,), (1,)), ((), ())), preferred_element_type=F32)


def _dot_tn(a, b):
    return lax.dot_general(a, b, (((0,), (0,)), ((), ())), preferred_element_type=F32)


def _mod_kernel(cond_ref, w_ref, b_ref, o_ref):
    s = _silu(cond_ref[...])
    o_ref[...] = jnp.dot(s, w_ref[...], preferred_element_type=F32, precision=HIGHEST) + b_ref[...]


def _modulation(cond, w_mod, b_mod):
    n = cond.shape[0]
    tn = D_MODEL
    return pl.pallas_call(
        _mod_kernel,
        grid=(6 * D_MODEL // tn,),
        in_specs=[pl.BlockSpec((n, D_MODEL), lambda j: (0, 0)),
                  pl.BlockSpec((D_MODEL, tn), lambda j: (0, j)),
                  pl.BlockSpec((1, tn), lambda j: (0, j))],
        out_specs=pl.BlockSpec((n, tn), lambda j: (0, j)),
        out_shape=jax.ShapeDtypeStruct((n, 6 * D_MODEL), F32),
        compiler_params=_cparams(("parallel",)),
        name="mod",
    )(cond, w_mod, b_mod.reshape(1, -1))


class _Layout:
    def __init__(self, n_p, l_p, n_s, l_s):
        self.n_p, self.l_p, self.n_s, self.l_s = n_p, l_p, n_s, l_s
        self.t_p = n_p * l_p
        self.t_s = n_s * l_s
        self.t = self.t_p + self.t_s
        assert l_p % TOK_TILE == 0 and l_s % TOK_TILE == 0 and l_s % DELTA_BLK == 0
        self.tiles_p = self.t_p // TOK_TILE
        self.tiles_s = self.t_s // TOK_TILE
        self.tiles = self.tiles_p + self.tiles_s
        self.tps_p = l_p // TOK_TILE
        self.tps_s = l_s // TOK_TILE

    def is_sample(self, i):
        return i >= self.tiles_p

    def mod_row(self, i):
        return jnp.where(i < self.tiles_p, 0, 1 + (i - self.tiles_p) // self.tps_s)

    def pos_blk(self, i):
        return jnp.where(i < self.tiles_p, 0, (i - self.tiles_p) % self.tps_s)

    def xp_blk(self, i):
        return jnp.minimum(i, self.tiles_p - 1)

    def xs_blk(self, i):
        return jnp.maximum(i - self.tiles_p, 0)

    def seq_pos(self, i):
        in_s = i >= self.tiles_p
        pos = jnp.where(in_s, (i - self.tiles_p) % self.tps_s, i % self.tps_p)
        n = jnp.where(in_s, self.tps_s, self.tps_p)
        return pos, n


def _load_x(lay, i, xp_ref, xs_ref, pos_ref):
    return jnp.where(lay.is_sample(i), xs_ref[...] + pos_ref[...], xp_ref[...])


def _x_specs(lay):
    return [pl.BlockSpec((TOK_TILE, D_MODEL), lambda i: (lay.xp_blk(i), 0)),
            pl.BlockSpec((TOK_TILE, D_MODEL), lambda i: (lay.xs_blk(i), 0)),
            pl.BlockSpec((TOK_TILE, D_MODEL), lambda i: (lay.pos_blk(i), 0))]


def _mod_spec(lay):
    return pl.BlockSpec((1, 6, D_MODEL), lambda i: (lay.mod_row(i), 0, 0))


def _row_spec(width):
    return pl.BlockSpec((TOK_TILE, width), lambda i: (i, 0))


def _full_spec(shape):
    nd = len(shape)
    return pl.BlockSpec(shape, lambda i: (0,) * nd)


def _inproj_kernel(lay, xp_ref, xs_ref, pos_ref, mod_ref, g_ref, wm_ref, ws_ref,
                   qkv_ref, z_ref, ug_ref, ba_ref):
    i = pl.program_id(0)
    x = _load_x(lay, i, xp_ref, xs_ref, pos_ref)
    m = mod_ref[0]
    h = _rms(x) * g_ref[...] * (1.0 + m[1:2]) + m[0:1]
    hb = h.astype(BF16)
    qkv_ref[...] = _dot(hb, wm_ref[:, 0:3 * MIX_A])
    z_ref[...] = _dot(hb, wm_ref[:, 3 * MIX_A:4 * MIX_A])
    glu = _dot(hb, wm_ref[:, 4 * MIX_A:4 * MIX_A + 2 * MIX_B])
    ug_ref[...] = glu[:, :MIX_B] * _sigmoid(glu[:, MIX_B:])
    ba_ref[...] = _dot(hb, ws_ref[...])


def _inproj(lay, xp, xs, pos, mod, norm_g, w_main, w_small):
    t = lay.t
    return pl.pallas_call(
        functools.partial(_inproj_kernel, lay),
        grid=(lay.tiles,),
        in_specs=_x_specs(lay) + [_mod_spec(lay), _full_spec((1, D_MODEL)),
                                  _full_spec(w_main.shape), _full_spec(w_small.shape)],
        out_specs=[_row_spec(3 * MIX_A), _row_spec(MIX_A), _row_spec(MIX_B), _row_spec(LANES)],
        out_shape=[jax.ShapeDtypeStruct((t, 3 * MIX_A), F32),
                   jax.ShapeDtypeStruct((t, MIX_A), F32),
                   jax.ShapeDtypeStruct((t, MIX_B), F32),
                   jax.ShapeDtypeStruct((t, LANES), F32)],
        compiler_params=_cparams(("parallel",)),
        name="inproj",
    )(xp, xs, pos, mod, norm_g, w_main, w_small)


CONV_ROWS = 64


def _conv_kernel(lay, qc_ref, qp_ref, qn_ref, uc_ref, up_ref, un_ref, ba_ref, cw_ref, dw_ref,
                 dwb_ref, lng_ref, lnb_ref, alog_ref, dtb_ref,
                 qkv_ref, uo_ref, gate_ref, eq_scr, eu_scr):
    i = pl.program_id(0)
    pos, n = lay.seq_pos(i)
    has_prev = pos != 0
    has_next = pos != n - 1
    eq_scr[0:Q_HALO] = jnp.where(has_prev, qp_ref[...], 0.0)
    eq_scr[Q_HALO:Q_HALO + TOK_TILE] = qc_ref[...]
    eq_scr[Q_HALO + TOK_TILE:] = jnp.where(has_next, qn_ref[...], 0.0)
    eu_scr[0:U_HALO] = jnp.where(has_prev, up_ref[...], 0.0)
    eu_scr[U_HALO:U_HALO + TOK_TILE] = uc_ref[...]
    eu_scr[U_HALO + TOK_TILE:] = jnp.where(has_next, un_ref[...], 0.0)

    q_off = Q_HALO - SHORT_CONV // 2
    u_off = U_HALO - CONV_W // 2
    for rc in range(TOK_TILE // CONV_ROWS):
        r0 = rc * CONV_ROWS
        for p in range(3):
            c0 = p * MIX_A
            acc = eq_scr[q_off + r0:q_off + r0 + CONV_ROWS, c0:c0 + MIX_A] * cw_ref[0:1, c0:c0 + MIX_A]
            for tap in range(1, SHORT_CONV):
                acc = acc + (eq_scr[q_off + tap + r0:q_off + tap + r0 + CONV_ROWS, c0:c0 + MIX_A]
                             * cw_ref[tap:tap + 1, c0:c0 + MIX_A])
            y = _silu(acc)
            for hd in range(N_HEADS):
                yh = y[:, hd * HEAD_DIM:(hd + 1) * HEAD_DIM]
                if p < 2:
                    yh = yh * lax.rsqrt(jnp.sum(yh * yh, axis=-1, keepdims=True) + EPS)
                qkv_ref[r0:r0 + CONV_ROWS, c0 + hd * HEAD_DIM:c0 + (hd + 1) * HEAD_DIM] = yh
        acc = eu_scr[u_off + r0:u_off + r0 + CONV_ROWS, :] * dw_ref[0:1, :]
        for tap in range(1, CONV_W):
            acc = acc + eu_scr[u_off + tap + r0:u_off + tap + r0 + CONV_ROWS, :] * dw_ref[tap:tap + 1, :]
        u = acc + dwb_ref[...]
        uc = u - jnp.mean(u, axis=-1, keepdims=True)
        un = uc * lax.rsqrt(jnp.mean(uc * uc, axis=-1, keepdims=True) + EPS)
        uo_ref[r0:r0 + CONV_ROWS, :] = _silu(un * lng_ref[...] + lnb_ref[...]).astype(BF16)

    x = ba_ref[...]
    lane = lax.broadcasted_iota(I32, x.shape, 1)
    g = -jnp.exp(alog_ref[...]) * _softplus(x + dtb_ref[...])
    gate_ref[...] = jnp.where(lane < 2 * N_HEADS, _sigmoid(x), g)


def _conv(lay, qkv_raw, ug, ba, cw, dw, dwb, lng, lnb, alog, dtb):
    t = lay.t
    qh = TOK_TILE // Q_HALO
    uh = TOK_TILE // U_HALO
    n_qh = t // Q_HALO
    n_uh = t // U_HALO
    in_specs = [
        _row_spec(3 * MIX_A),
        pl.BlockSpec((Q_HALO, 3 * MIX_A), lambda i: (jnp.maximum(i * qh - 1, 0), 0)),
        pl.BlockSpec((Q_HALO, 3 * MIX_A), lambda i: (jnp.minimum((i + 1) * qh, n_qh - 1), 0)),
        _row_spec(MIX_B),
        pl.BlockSpec((U_HALO, MIX_B), lambda i: (jnp.maximum(i * uh - 1, 0), 0)),
        pl.BlockSpec((U_HALO, MIX_B), lambda i: (jnp.minimum((i + 1) * uh, n_uh - 1), 0)),
        _row_spec(LANES),
        _full_spec(cw.shape), _full_spec(dw.shape), _full_spec(dwb.shape),
        _full_spec(lng.shape), _full_spec(lnb.shape), _full_spec(alog.shape), _full_spec(dtb.shape),
    ]
    return pl.pallas_call(
        functools.partial(_conv_kernel, lay),
        grid=(lay.tiles,),
        in_specs=in_specs,
        out_specs=[_row_spec(3 * MIX_A), _row_spec(MIX_B), _row_spec(LANES)],
        out_shape=[jax.ShapeDtypeStruct((t, 3 * MIX_A), F32),
                   jax.ShapeDtypeStruct((t, MIX_B), BF16),
                   jax.ShapeDtypeStruct((t, LANES), F32)],
        scratch_shapes=[pltpu.VMEM((TOK_TILE + 2 * Q_HALO, 3 * MIX_A), F32),
                        pltpu.VMEM((TOK_TILE + 2 * U_HALO, MIX_B), F32)],
        compiler_params=_cparams(("parallel",)),
        name="conv",
    )(qkv_raw, qkv_raw, qkv_raw, ug, ug, ug, ba, cw, dw, dwb, lng, lnb, alog, dtb)


INV_BASE = 8


def _b16(xs):
    return [x.astype(BF16) for x in xs]


def _tri_inverse_minus_eye(nmats, rows, cols):
    assert INV_BASE == 8
    c = nmats[0].shape[0]
    shift = int(math.log2(INV_BASE))
    same = (rows >> shift) == (cols >> shift)
    n1 = [jnp.where(same, n, 0.0) for n in nmats]
    n1b = _b16(n1)
    n2 = [_dot(x, x) for x in n1b]
    n2b = _b16(n2)
    r = [_dot(jnp.concatenate([a, b], axis=0), b) for a, b in zip(n1b, n2b)]
    q = [a + b + x[:c] for a, b, x in zip(n1, n2, r)]
    n4 = [x[c:] for x in r]
    qn4 = [_dot(a, b) for a, b in zip(_b16(q), _b16(n4))]
    q = [a + b + x for a, b, x in zip(q, n4, qn4)]
    while (1 << shift) < c:
        off = ((rows >> (shift + 1)) == (cols >> (shift + 1))) & ((rows >> shift) != (cols >> shift))
        a = [jnp.where(off, -n, 0.0) for n in nmats]
        x = [ai + _dot(ab, qb) for ai, ab, qb in zip(a, _b16(a), _b16(q))]
        qx = [_dot(qb, xb) for qb, xb in zip(_b16(q), _b16(x))]
        q = [qi - xi - qxi for qi, xi, qxi in zip(q, x, qx)]
        shift += 1
    return q


def _delta_units(units):
    c = CHUNK
    rows = lax.broadcasted_iota(I32, (c, c), 0)
    cols = lax.broadcasted_iota(I32, (c, c), 1)
    pre = []
    for d, hd, qkv_ref, r0, gate, gcol, grow, s in units:
        incl = (rows >= cols) if d == 0 else (rows <= cols)
        lane = 2 * N_HEADS + N_HEADS * d + hd
        beta = gate[:, N_HEADS * d + hd:N_HEADS * d + hd + 1]
        gc = gcol[:, lane:lane + 1]
        gr = grow[lane:lane + 1, :]
        last = c - 1 if d == 0 else 0
        gtot = gcol[last:last + 1, lane:lane + 1]
        dec = jnp.where(incl, jnp.exp(jnp.where(incl, gc - gr, 0.0)), 0.0)
        q = qkv_ref[pl.ds(r0, c), hd * HEAD_DIM:(hd + 1) * HEAD_DIM]
        k = qkv_ref[pl.ds(r0, c), MIX_A + hd * HEAD_DIM:MIX_A + (hd + 1) * HEAD_DIM]
        v = qkv_ref[pl.ds(r0, c), 2 * MIX_A + hd * HEAD_DIM:2 * MIX_A + (hd + 1) * HEAD_DIM]
        qs = q * (HEAD_DIM ** -0.5)
        kbeta = k * beta
        egc = jnp.exp(gc)
        pre.append(dict(d=d, beta=beta, gc=gc, gtot=gtot, dec=dec, k=k, v=v, qs=qs, kbeta=kbeta, egc=egc, s=s))

    ak = [_dot_nt(jnp.concatenate([p['kbeta'], p['qs']], axis=0).astype(BF16), p['k'].astype(BF16))
          for p in pre]
    nmats, qks = [], []
    for p, a in zip(pre, ak):
        strict = (rows > cols) if p['d'] == 0 else (rows < cols)
        nmats.append(jnp.where(strict, -a[:c] * p['dec'], 0.0))
        qks.append(a[c:] * p['dec'])
    qinv = _tri_inverse_minus_eye(nmats, rows, cols)

    rhs = [jnp.concatenate([p['v'] * p['beta'], p['kbeta'] * p['egc']], axis=1) for p in pre]
    uw = [r + _dot(qi, rb) for r, qi, rb in zip(rhs, _b16(qinv), _b16(rhs))]
    sw = [_dot(jnp.concatenate([x[:, HEAD_DIM:], p['qs'] * p['egc']], axis=0).astype(BF16), p['s'].astype(BF16))
          for x, p in zip(uw, pre)]
    vnb = [(x[:, :HEAD_DIM] - y[:c]).astype(BF16) for x, y in zip(uw, sw)]
    o = [y[c:] + _dot(qk, vb) for y, qk, vb in zip(sw, _b16(qks), vnb)]
    kdec = [(p['k'] * jnp.exp(p['gtot'] - p['gc'])).astype(BF16) for p in pre]
    s_new = [p['s'] * jnp.exp(p['gtot']) + _dot_tn(kd, vb) for p, kd, vb in zip(pre, kdec, vnb)]
    return list(zip(o, s_new))


def _delta_kernel(has_s0, emit_final, n_chunk, *refs):
    refs = list(refs)
    qkvf_ref, qkvb_ref, gatef_ref, gateb_ref, gtf_ref, gtb_ref = refs[:6]
    refs = refs[6:]
    s0_ref = refs.pop(0) if has_s0 else None
    of_ref, ob_ref = refs[:2]
    refs = refs[2:]
    sfin_ref = refs.pop(0) if emit_final else None
    s_scr = refs[0]

    j = pl.program_id(1)

    @pl.when(j == 0)
    def _():
        if has_s0:
            s_scr[...] = s0_ref[0]
        else:
            s_scr[...] = jnp.zeros(s_scr.shape, F32)

    c = CHUNK
    rows = lax.broadcasted_iota(I32, (c, c), 0)
    cols = lax.broadcasted_iota(I32, (c, c), 1)
    tri_lo = (rows >= cols).astype(F32)
    tri_up = (rows <= cols).astype(F32)

    def body(ci, carry):
        units = []
        for d in range(2):
            cidx = ci if d == 0 else n_chunk - 1 - ci
            r0 = pl.multiple_of(cidx * c, c)
            qkv_ref = qkvf_ref if d == 0 else qkvb_ref
            gate = (gatef_ref if d == 0 else gateb_ref)[pl.ds(r0, c), :]
            gt = (gtf_ref if d == 0 else gtb_ref)[cidx]
            tri = tri_lo if d == 0 else tri_up
            tri_t = tri_up if d == 0 else tri_lo
            gcol = jnp.dot(tri, gate, preferred_element_type=F32, precision=HIGHEST)
            grow = jnp.dot(gt, tri_t, preferred_element_type=F32, precision=HIGHEST)
            for hd in range(N_HEADS):
                units.append((d, hd, qkv_ref, r0, gate, gcol, grow, s_scr[N_HEADS * d + hd]))
        for (d, hd, _, r0, _, _, _, _), (o, s_new) in zip(units, _delta_units(units)):
            s_scr[N_HEADS * d + hd] = s_new
            (of_ref if d == 0 else ob_ref)[pl.ds(r0, c), hd * HEAD_DIM:(hd + 1) * HEAD_DIM] = o
        return carry

    lax.fori_loop(0, n_chunk, body, 0)

    if emit_final:
        @pl.when(j == pl.num_programs(1) - 1)
        def _():
            sfin_ref[0] = s_scr[...]


def _delta(qkv, gate, gate_t, n_seq, seq_len, row0, blk, s0):
    t = qkv.shape[0]
    nblk = seq_len // blk
    n_chunk = blk // CHUNK
    b0 = row0 // blk
    has_s0 = s0 is not None
    emit_final = not has_s0
    n_state = 2 * N_HEADS

    def fwd(b, j):
        return b0 + b * nblk + j

    def bwd(b, j):
        return b0 + b * nblk + (nblk - 1 - j)

    in_specs = [
        pl.BlockSpec((blk, 3 * MIX_A), lambda b, j: (fwd(b, j), 0)),
        pl.BlockSpec((blk, 3 * MIX_A), lambda b, j: (bwd(b, j), 0)),
        pl.BlockSpec((blk, LANES), lambda b, j: (fwd(b, j), 0)),
        pl.BlockSpec((blk, LANES), lambda b, j: (bwd(b, j), 0)),
        pl.BlockSpec((n_chunk, 16, CHUNK), lambda b, j: (fwd(b, j), 0, 0)),
        pl.BlockSpec((n_chunk, 16, CHUNK), lambda b, j: (bwd(b, j), 0, 0)),
    ]
    args = [qkv, qkv, gate, gate, gate_t, gate_t]
    if has_s0:
        in_specs.append(pl.BlockSpec((1, n_state, HEAD_DIM, HEAD_DIM), lambda b, j: (b, 0, 0, 0)))
        args.append(s0)
    n_rows = n_seq * seq_len
    out_specs = [pl.BlockSpec((blk, MIX_A), lambda b, j: (b * nblk + j, 0)),
                 pl.BlockSpec((blk, MIX_A), lambda b, j: (b * nblk + (nblk - 1 - j), 0))]
    out_shape = [jax.ShapeDtypeStruct((n_rows, MIX_A), F32), jax.ShapeDtypeStruct((n_rows, MIX_A), F32)]
    if emit_final:
        out_specs.append(pl.BlockSpec((1, n_state, HEAD_DIM, HEAD_DIM), lambda b, j: (b, 0, 0, 0)))
        out_shape.append(jax.ShapeDtypeStruct((n_seq, n_state, HEAD_DIM, HEAD_DIM), F32))
    del t
    return pl.pallas_call(
        functools.partial(_delta_kernel, has_s0, emit_final, n_chunk),
        grid=(n_seq, nblk),
        in_specs=in_specs,
        out_specs=out_specs,
        out_shape=out_shape,
        scratch_shapes=[pltpu.VMEM((n_state, HEAD_DIM, HEAD_DIM), F32)],
        compiler_params=_cparams(("parallel", "arbitrary")),
        name="delta_latent" if has_s0 else "delta_prompt",
    )(*args)


def _outproj_kernel(lay, xp_ref, xs_ref, pos_ref, mod_ref, opf_ref, opb_ref, osf_ref, osb_ref, z_ref, u_ref,
                    og_ref, n2_ref, wo_ref, wr_ref, br_ref, x1_ref, h2_ref, ids_ref, wts_ref):
    i = pl.program_id(0)
    x0 = _load_x(lay, i, xp_ref, xs_ref, pos_ref)
    m = mod_ref[0]
    o = jnp.where(lay.is_sample(i), osf_ref[...] + osb_ref[...], opf_ref[...] + opb_ref[...])
    z = z_ref[...]
    mix = _dot(u_ref[...], wo_ref[MIX_A:, :])
    for hd in range(N_HEADS):
        sl = slice(hd * HEAD_DIM, (hd + 1) * HEAD_DIM)
        oh = _rms(o[:, sl]) * og_ref[...] * _silu(z[:, sl])
        mix = mix + _dot(oh.astype(BF16), wo_ref[sl, :])
    x1 = x0 + m[2:3] * mix
    x1_ref[...] = x1
    h2 = _rms(x1) * n2_ref[...] * (1.0 + m[4:5]) + m[3:4]
    h2b = h2.astype(BF16)
    half = D_MODEL // 2
    hi = pltpu.bitcast(h2b[:, :half].astype(F32), jnp.uint32)
    lo = pltpu.bitcast(h2b[:, half:].astype(F32), jnp.uint32)
    h2_ref[...] = pltpu.bitcast((hi & jnp.uint32(0xFFFF0000)) | (lo >> 16), I32)

    logits = _dot(h2b, wr_ref[...]) + br_ref[...]
    lane = lax.broadcasted_iota(I32, logits.shape, 1)
    neg = jnp.float32(-jnp.inf)
    big = jnp.int32(LANES)
    gl = jnp.where(lane < N_GROUPS, logits, neg)
    gmax = jnp.max(gl, axis=-1, keepdims=True)
    grp = jnp.min(jnp.where(gl == gmax, lane, big), axis=-1, keepdims=True)
    p_grp = 1.0 / jnp.sum(jnp.where(lane < N_GROUPS, jnp.exp(gl - gmax), 0.0), axis=-1, keepdims=True)
    e_lane = lane - N_GROUPS
    in_grp = (e_lane >= grp * EXPERTS_PER_GROUP) & (e_lane < (grp + 1) * EXPERTS_PER_GROUP)
    el = jnp.where(in_grp, logits, neg)
    m1 = jnp.max(el, axis=-1, keepdims=True)
    i1 = jnp.min(jnp.where(el == m1, lane, big), axis=-1, keepdims=True)
    el2 = jnp.where(lane == i1, neg, el)
    m2 = jnp.max(el2, axis=-1, keepdims=True)
    i2 = jnp.min(jnp.where(el2 == m2, lane, big), axis=-1, keepdims=True)
    e2 = jnp.exp(m2 - m1)
    w1 = p_grp / (1.0 + e2)
    w2 = p_grp * e2 / (1.0 + e2)
    ids_ref[...] = jnp.where(lane == 0, i1 - N_GROUPS, jnp.where(lane == 1, i2 - N_GROUPS, 0))
    wts_ref[...] = jnp.where(lane == 0, w1, jnp.where(lane == 1, w2, 0.0))


def _outproj(lay, xp, xs, pos, mod, o_pf, o_pb, o_sf, o_sb, z, u, onorm_g, norm2_g, w_out, w_route, b_route):
    t = lay.t
    p_spec = pl.BlockSpec((TOK_TILE, MIX_A), lambda i: (lay.xp_blk(i), 0))
    s_spec = pl.BlockSpec((TOK_TILE, MIX_A), lambda i: (lay.xs_blk(i), 0))
    return pl.pallas_call(
        functools.partial(_outproj_kernel, lay),
        grid=(lay.tiles,),
        in_specs=_x_specs(lay) + [_mod_spec(lay), p_spec, p_spec, s_spec, s_spec, _row_spec(MIX_A),
                                  _row_spec(MIX_B), _full_spec((1, HEAD_DIM)), _full_spec((1, D_MODEL)),
                                  _full_spec(w_out.shape), _full_spec(w_route.shape),
                                  _full_spec(b_route.shape)],
        out_specs=[_row_spec(D_MODEL), _row_spec(D_MODEL // 2), _row_spec(LANES), _row_spec(LANES)],
        out_shape=[jax.ShapeDtypeStruct((t, D_MODEL), F32),
                   jax.ShapeDtypeStruct((t, D_MODEL // 2), I32),
                   jax.ShapeDtypeStruct((t, LANES), I32),
                   jax.ShapeDtypeStruct((t, LANES), F32)],
        compiler_params=_cparams(("parallel",)),
        name="outproj",
    )(xp, xs, pos, mod, o_pf, o_pb, o_sf, o_sb, z, u, onorm_g, norm2_g, w_out, w_route, b_route)


def _sc_gather(table, idx, chunk, name):
    n_rows, width = idx.shape[0], table.shape[1]
    mesh = plsc.VectorSubcoreMesh(core_axis_name="c", subcore_axis_name="s")
    n_workers = mesh.num_cores * mesh.num_subcores
    per_worker = n_rows // n_workers
    n_chunks = per_worker // chunk
    assert n_rows == n_workers * n_chunks * chunk and n_chunks % 2 == 0 and chunk % 8 == 0 and chunk <= LANES

    def body(table_hbm, idx_hbm, out_hbm, idx_v, rows_v, sem):
        base = (lax.axis_index("s") * mesh.num_cores + lax.axis_index("c")) * per_worker

        def gather(slot):
            return pltpu.make_async_copy(table_hbm.at[idx_v.at[slot]], rows_v.at[slot], sem.at[slot])

        def fetch(g, slot):
            off = pl.multiple_of(base + g * chunk, 8)
            pltpu.sync_copy(idx_hbm.at[pl.ds(off, chunk)], idx_v.at[slot])
            gather(slot).start()

        for slot in range(2):
            fetch(slot, slot)

        @pl.loop(0, n_chunks, step=2)
        def _(g):
            for slot in range(2):
                off = pl.multiple_of(base + (g + slot) * chunk, 8)
                gather(slot).wait()
                pltpu.sync_copy(rows_v.at[slot], out_hbm.at[pl.ds(off, chunk)])

                @pl.when(g + slot + 2 < n_chunks)
                def _():
                    fetch(g + slot + 2, slot)

    return pl.kernel(
        body,
        out_type=jax.ShapeDtypeStruct((n_rows, width), table.dtype),
        mesh=mesh,
        scratch_types=[pltpu.VMEM((2, chunk), I32), pltpu.VMEM((2, chunk, width), table.dtype),
                       pltpu.SemaphoreType.DMA((2,))],
        name=name,
    )(table, idx)


def _expert_kernel(blk_e_ref, nused_ref, xb_ref, wg_ref, wu_ref, wd_ref, yb_ref, wg_s, wu_s, wd_s):
    i = pl.program_id(0)
    nused = nused_ref[0]

    @pl.when(i < nused)
    def _():
        changed = (i == 0) | (blk_e_ref[i] != blk_e_ref[jnp.maximum(i - 1, 0)])

        @pl.when(changed)
        def _():
            wg_s[...] = wg_ref[0].astype(BF16)
            wu_s[...] = wu_ref[0].astype(BF16)
            wd_s[...] = wd_ref[0].astype(BF16)

        x = pltpu.bitcast(xb_ref[...], jnp.uint32)
        half = D_MODEL // 2
        xa = pltpu.bitcast(x & jnp.uint32(0xFFFF0000), F32).astype(BF16)
        xb = pltpu.bitcast(x << 16, F32).astype(BF16)
        g = _dot(xa, wg_s[:half, :]) + _dot(xb, wg_s[half:, :])
        u = _dot(xa, wu_s[:half, :]) + _dot(xb, wu_s[half:, :])
        hmid = (_silu(g) * u).astype(BF16)
        yb_ref[...] = _dot(hmid, wd_s[...])

    @pl.when(i >= nused)
    def _():
        yb_ref[...] = jnp.zeros(yb_ref.shape, F32)


def _experts(xb, blk_e, nused, w_gate, w_up, w_down):
    nb = blk_e.shape[0]
    grid_spec = pltpu.PrefetchScalarGridSpec(
        num_scalar_prefetch=2,
        grid=(nb,),
        in_specs=[
            pl.BlockSpec((ROUTE_BM, D_MODEL // 2), lambda i, be, nu: (i, 0)),
            pl.BlockSpec((1, D_MODEL, D_EXPERT), lambda i, be, nu: (be[i], 0, 0)),
            pl.BlockSpec((1, D_MODEL, D_EXPERT), lambda i, be, nu: (be[i], 0, 0)),
            pl.BlockSpec((1, D_EXPERT, D_MODEL), lambda i, be, nu: (be[i], 0, 0)),
        ],
        out_specs=pl.BlockSpec((ROUTE_BM, D_MODEL), lambda i, be, nu: (i, 0)),
        scratch_shapes=[pltpu.VMEM((D_MODEL, D_EXPERT), BF16),
                        pltpu.VMEM((D_MODEL, D_EXPERT), BF16),
                        pltpu.VMEM((D_EXPERT, D_MODEL), BF16)],
    )
    return pl.pallas_call(
        _expert_kernel,
        grid_spec=grid_spec,
        out_shape=jax.ShapeDtypeStruct((nb * ROUTE_BM, D_MODEL), F32),
        compiler_params=_cparams(("arbitrary",)),
        name="expert",
    )(blk_e, nused, xb, w_gate, w_up, w_down)


def _combine_kernel(lay, yg_ref, x1_ref, wts_ref, mod_ref, fg_ref, yp_ref, ys_ref):
    i = pl.program_id(0)
    m = mod_ref[0]
    w = wts_ref[...]
    ff = yg_ref[:, :D_MODEL] * w[:, 0:1] + yg_ref[:, D_MODEL:] * w[:, 1:2]
    y = _rms(x1_ref[...] + m[5:6] * ff) * fg_ref[...]

    @pl.when(i < lay.tiles_p)
    def _():
        yp_ref[...] = y

    @pl.when(i >= lay.tiles_p)
    def _():
        ys_ref[...] = y


def _combine(lay, yg, x1, wts, mod, final_g):
    return pl.pallas_call(
        functools.partial(_combine_kernel, lay),
        grid=(lay.tiles,),
        in_specs=[_row_spec(2 * D_MODEL), _row_spec(D_MODEL), _row_spec(LANES), _mod_spec(lay),
                  _full_spec((1, D_MODEL))],
        out_specs=[pl.BlockSpec((TOK_TILE, D_MODEL), lambda i: (lay.xp_blk(i), 0)),
                   pl.BlockSpec((TOK_TILE, D_MODEL), lambda i: (lay.xs_blk(i), 0))],
        out_shape=[jax.ShapeDtypeStruct((lay.t_p, D_MODEL), F32),
                   jax.ShapeDtypeStruct((lay.t_s, D_MODEL), F32)],
        compiler_params=_cparams(("arbitrary",)),
        name="combine",
    )(yg, x1, wts, mod, final_g)


def _dispatch_plan(ids, n_tok):
    flat_e = ids[:, :2].reshape(-1)
    n_assign = flat_e.shape[0]
    onehot = (flat_e[:, None] == jnp.arange(N_EXPERTS, dtype=I32)[None, :]).astype(I32)
    csum = jnp.cumsum(onehot, axis=0)
    rank = jnp.sum(csum * onehot, axis=1) - 1
    counts = csum[-1]
    padded = (counts + ROUTE_BM - 1) // ROUTE_BM * ROUTE_BM
    pad_end = jnp.cumsum(padded)
    pad_start = pad_end - padded
    dest = (jnp.sum(onehot * pad_start[None, :], axis=1) + rank).astype(I32)
    nb = -(-(n_assign + N_EXPERTS * (ROUTE_BM - 1)) // ROUTE_BM)
    cap = nb * ROUTE_BM
    slot_tok = jnp.zeros((cap,), I32).at[dest].set(jnp.arange(n_assign, dtype=I32) // 2)
    block_start = jnp.arange(nb, dtype=I32) * ROUTE_BM
    blk_e = jnp.minimum(jnp.searchsorted(pad_end, block_start, side='right'), N_EXPERTS - 1).astype(I32)
    nused = (pad_end[-1:] // ROUTE_BM).astype(I32)
    del n_tok
    return slot_tok, blk_e, nused, dest


def _grid_pos_embed(n_tokens):
    rows = n_tokens // GRID_W
    r = jnp.repeat(jnp.arange(rows, dtype=F32), GRID_W)
    col = jnp.tile(jnp.arange(GRID_W, dtype=F32), rows)
    n_freq = D_MODEL // 4
    freq = jnp.exp(jnp.arange(n_freq, dtype=F32) * (-math.log(10000.0) / n_freq))

    def enc(p):
        ang = p[:, None] * freq[None, :]
        return jnp.concatenate([jnp.sin(ang), jnp.cos(ang)], axis=-1)

    return jnp.concatenate([enc(r), enc(col)], axis=-1)


def _lane_pad(v, offset):
    return jnp.zeros((1, LANES), F32).at[0, offset:offset + v.shape[0]].set(v.astype(F32))


def kernel(x_prompt, x_sample, state_delta, c, c_ctx, norm1_g, w_mod, b_mod, w_in, conv_qkv_w, A_log, dt_bias, onorm_g, dw_w, dw_b, cln_g, cln_b, w_out, norm2_g, w_group, b_group, w_expert, b_expert, w_e_gate, w_e_up, w_e_down, final_g):
    n_p, l_p, _ = x_prompt.shape
    n_s, l_s, _ = x_sample.shape
    lay = _Layout(n_p, l_p, n_s, l_s)
    depth = w_in.shape[0]
    assert depth == 1
    xp = x_prompt.reshape(lay.t_p, D_MODEL)
    xs = x_sample.reshape(lay.t_s, D_MODEL)
    pos = _grid_pos_embed(l_s)

    cond = jnp.concatenate([c_ctx[None, :], c], axis=0)
    cond = jnp.pad(cond, ((0, (-cond.shape[0]) % 8), (0, 0)))
    mod = _modulation(cond, w_mod[0], b_mod[0]).reshape(cond.shape[0], 6, D_MODEL)

    n_gate = 4 * N_HEADS
    wi = w_in[0]
    w_main = jnp.concatenate([wi[:, :4 * MIX_A], wi[:, 4 * MIX_A + n_gate:]], axis=1).astype(BF16)
    w_small = jnp.pad(wi[:, 4 * MIX_A:4 * MIX_A + n_gate], ((0, 0), (0, LANES - n_gate))).astype(BF16)
    qkv_raw, z, ug, ba = _inproj(lay, xp, xs, pos, mod, norm1_g[0][None, :], w_main, w_small)

    cw = jnp.pad(conv_qkv_w[0], ((0, 8 - SHORT_CONV), (0, 0)))
    dw = jnp.pad(dw_w[0], ((0, 32 - CONV_W), (0, 0)))
    alog = _lane_pad(A_log[0].reshape(-1), 2 * N_HEADS)
    dtb = _lane_pad(dt_bias[0].reshape(-1), 2 * N_HEADS)
    qkv, u_conf, gate = _conv(lay, qkv_raw, ug, ba, cw, dw, dw_b[0][None, :], cln_g[0][None, :],
                              cln_b[0][None, :], alog, dtb)

    gate_t = gate[:, :16].reshape(lay.t // CHUNK, CHUNK, 16).transpose(0, 2, 1)
    o_pf, o_pb, s_fin = _delta(qkv, gate, gate_t, n_p, l_p, 0, l_p, None)
    s0 = state_delta[:, 0].reshape(n_s, 2 * N_HEADS, HEAD_DIM, HEAD_DIM)
    o_sf, o_sb = _delta(qkv, gate, gate_t, n_s, l_s, lay.t_p, DELTA_BLK, s0)

    w_route = jnp.pad(jnp.concatenate([w_group[0], w_expert[0]], axis=1),
                      ((0, 0), (0, LANES - N_GROUPS - N_EXPERTS))).astype(BF16)
    b_route = _lane_pad(jnp.concatenate([b_group[0], b_expert[0]]), 0)
    x1, h2p, ids, wts = _outproj(lay, xp, xs, pos, mod, o_pf, o_pb, o_sf, o_sb, z, u_conf, onorm_g[0][None, :],
                                 norm2_g[0][None, :], w_out[0].astype(BF16), w_route, b_route)

    slot_tok, blk_e, nused, dest = _dispatch_plan(ids, lay.t)
    xb = _sc_gather(h2p, slot_tok, DISPATCH_CHUNK, "dispatch_gather")
    yb = _experts(xb, blk_e, nused, w_e_gate[0], w_e_up[0], w_e_down[0])
    yg = _sc_gather(yb, dest, COMBINE_CHUNK, "combine_gather").reshape(lay.t, 2 * D_MODEL)
    y_p, y_s = _combine(lay, yg, x1, wts, mod, final_g[None, :])

    new_state = s_fin.reshape(n_p, 1, 2, N_HEADS, HEAD_DIM, HEAD_DIM)
    return (y_p.reshape(x_prompt.shape), y_s.reshape(x_sample.shape), new_state)
```

```python
import functools
import math

import jax
import jax.numpy as jnp
from jax import lax
from jax.experimental import pallas as pl
from jax.experimental.pallas import tpu as pltpu
from jax.experimental.pallas import tpu_sc as plsc

F32 = jnp.float32
BF16 = jnp.bfloat16
I32 = jnp.int32

D_MODEL = 1024
MIX_A = 512
MIX_B = 512
HEAD_DIM = 128
N_HEADS = 4
SHORT_CONV = 5
CONV_W = 31
N_GROUPS = 4
EXPERTS_PER_GROUP = 8
N_EXPERTS = 32
D_EXPERT = 512
GRID_W = 64
EPS = 1e-6

LANES = 128
TOK_TILE = 256
CHUNK = 64
DELTA_BLK = 512
ROUTE_BM = 256
DISPATCH_CHUNK = 64
COMBINE_CHUNK = 32
Q_HALO = 8
U_HALO = 16
VMEM_LIMIT = 56 * 1024 * 1024

HIGHEST = lax.Precision.HIGHEST


def _cparams(sem):
    return pltpu.CompilerParams(dimension_semantics=sem, vmem_limit_bytes=VMEM_LIMIT)


def _sigmoid(x):
    return 1.0 / (1.0 + jnp.exp(-x))


def _silu(x):
    return x * _sigmoid(x)


def _softplus(x):
    return jnp.maximum(x, 0.0) + jnp.log1p(jnp.exp(-jnp.abs(x)))


def _rms(x):
    return x * lax.rsqrt(jnp.mean(x * x, axis=-1, keepdims=True) + EPS)


def _dot(a, b):
    return jnp.dot(a, b, preferred_element_type=F32)


def _dot_nt(a, b):
    return lax.dot_general(a, b, (((1,), (1,)), ((), ())), preferred_element_type=F32)


def _dot_tn(a, b):
    return lax.dot_general(a, b, (((0,), (0,)), ((), ())), preferred_element_type=F32)


def _mod_kernel(cond_ref, w_ref, b_ref, o_ref):
    s = _silu(cond_ref[...])
    o_ref[...] = jnp.dot(s, w_ref[...], preferred_element_type=F32, precision=HIGHEST) + b_ref[...]


def _modulation(cond, w_mod, b_mod):
    n = cond.shape[0]
    tn = D_MODEL
    return pl.pallas_call(
        _mod_kernel,
        grid=(6 * D_MODEL // tn,),
        in_specs=[pl.BlockSpec((n, D_MODEL), lambda j: (0, 0)),
                  pl.BlockSpec((D_MODEL, tn), lambda j: (0, j)),
                  pl.BlockSpec((1, tn), lambda j: (0, j))],
        out_specs=pl.BlockSpec((n, tn), lambda j: (0, j)),
        out_shape=jax.ShapeDtypeStruct((n, 6 * D_MODEL), F32),
        compiler_params=_cparams(("parallel",)),
        name="mod",
    )(cond, w_mod, b_mod.reshape(1, -1))


class _Layout:
    def __init__(self, n_p, l_p, n_s, l_s):
        self.n_p, self.l_p, self.n_s, self.l_s = n_p, l_p, n_s, l_s
        self.t_p = n_p * l_p
        self.t_s = n_s * l_s
        self.t = self.t_p + self.t_s
        assert l_p % TOK_TILE == 0 and l_s % TOK_TILE == 0 and l_s % DELTA_BLK == 0
        self.tiles_p = self.t_p // TOK_TILE
        self.tiles_s = self.t_s // TOK_TILE
        self.tiles = self.tiles_p + self.tiles_s
        self.tps_p = l_p // TOK_TILE
        self.tps_s = l_s // TOK_TILE

    def is_sample(self, i):
        return i >= self.tiles_p

    def mod_row(self, i):
        return jnp.where(i < self.tiles_p, 0, 1 + (i - self.tiles_p) // self.tps_s)

    def pos_blk(self, i):
        return jnp.where(i < self.tiles_p, 0, (i - self.tiles_p) % self.tps_s)

    def xp_blk(self, i):
        return jnp.minimum(i, self.tiles_p - 1)

    def xs_blk(self, i):
        return jnp.maximum(i - self.tiles_p, 0)

    def seq_pos(self, i):
        in_s = i >= self.tiles_p
        pos = jnp.where(in_s, (i - self.tiles_p) % self.tps_s, i % self.tps_p)
        n = jnp.where(in_s, self.tps_s, self.tps_p)
        return pos, n


def _load_x(lay, i, xp_ref, xs_ref, pos_ref):
    return jnp.where(lay.is_sample(i), xs_ref[...] + pos_ref[...], xp_ref[...])


def _x_specs(lay):
    return [pl.BlockSpec((TOK_TILE, D_MODEL), lambda i: (lay.xp_blk(i), 0)),
            pl.BlockSpec((TOK_TILE, D_MODEL), lambda i: (lay.xs_blk(i), 0)),
            pl.BlockSpec((TOK_TILE, D_MODEL), lambda i: (lay.pos_blk(i), 0))]


def _mod_spec(lay):
    return pl.BlockSpec((1, 6, D_MODEL), lambda i: (lay.mod_row(i), 0, 0))


def _row_spec(width):
    return pl.BlockSpec((TOK_TILE, width), lambda i: (i, 0))


def _full_spec(shape):
    nd = len(shape)
    return pl.BlockSpec(shape, lambda i: (0,) * nd)


def _inproj_kernel(lay, xp_ref, xs_ref, pos_ref, mod_ref, g_ref, wm_ref, ws_ref,
                   qkv_ref, z_ref, ug_ref, ba_ref):
    i = pl.program_id(0)
    x = _load_x(lay, i, xp_ref, xs_ref, pos_ref)
    m = mod_ref[0]
    h = _rms(x) * g_ref[...] * (1.0 + m[1:2]) + m[0:1]
    hb = h.astype(BF16)
    qkv_ref[...] = _dot(hb, wm_ref[:, 0:3 * MIX_A])
    z_ref[...] = _dot(hb, wm_ref[:, 3 * MIX_A:4 * MIX_A])
    glu = _dot(hb, wm_ref[:, 4 * MIX_A:4 * MIX_A + 2 * MIX_B])
    ug_ref[...] = glu[:, :MIX_B] * _sigmoid(glu[:, MIX_B:])
    ba_ref[...] = _dot(hb, ws_ref[...])


def _inproj(lay, xp, xs, pos, mod, norm_g, w_main, w_small):
    t = lay.t
    return pl.pallas_call(
        functools.partial(_inproj_kernel, lay),
        grid=(lay.tiles,),
        in_specs=_x_specs(lay) + [_mod_spec(lay), _full_spec((1, D_MODEL)),
                                  _full_spec(w_main.shape), _full_spec(w_small.shape)],
        out_specs=[_row_spec(3 * MIX_A), _row_spec(MIX_A), _row_spec(MIX_B), _row_spec(LANES)],
        out_shape=[jax.ShapeDtypeStruct((t, 3 * MIX_A), F32),
                   jax.ShapeDtypeStruct((t, MIX_A), F32),
                   jax.ShapeDtypeStruct((t, MIX_B), F32),
                   jax.ShapeDtypeStruct((t, LANES), F32)],
        compiler_params=_cparams(("parallel",)),
        name="inproj",
    )(xp, xs, pos, mod, norm_g, w_main, w_small)


CONV_ROWS = 64


def _conv_kernel(lay, qc_ref, qp_ref, qn_ref, uc_ref, up_ref, un_ref, ba_ref, cw_ref, dw_ref,
                 dwb_ref, lng_ref, lnb_ref, alog_ref, dtb_ref,
                 qkv_ref, uo_ref, gate_ref, eq_scr, eu_scr):
    i = pl.program_id(0)
    pos, n = lay.seq_pos(i)
    has_prev = pos != 0
    has_next = pos != n - 1
    eq_scr[0:Q_HALO] = jnp.where(has_prev, qp_ref[...], 0.0)
    eq_scr[Q_HALO:Q_HALO + TOK_TILE] = qc_ref[...]
    eq_scr[Q_HALO + TOK_TILE:] = jnp.where(has_next, qn_ref[...], 0.0)
    eu_scr[0:U_HALO] = jnp.where(has_prev, up_ref[...], 0.0)
    eu_scr[U_HALO:U_HALO + TOK_TILE] = uc_ref[...]
    eu_scr[U_HALO + TOK_TILE:] = jnp.where(has_next, un_ref[...], 0.0)

    q_off = Q_HALO - SHORT_CONV // 2
    u_off = U_HALO - CONV_W // 2
    for rc in range(TOK_TILE // CONV_ROWS):
        r0 = rc * CONV_ROWS
        for p in range(3):
            c0 = p * MIX_A
            acc = eq_scr[q_off + r0:q_off + r0 + CONV_ROWS, c0:c0 + MIX_A] * cw_ref[0:1, c0:c0 + MIX_A]
            for tap in range(1, SHORT_CONV):
                acc = acc + (eq_scr[q_off + tap + r0:q_off + tap + r0 + CONV_ROWS, c0:c0 + MIX_A]
                             * cw_ref[tap:tap + 1, c0:c0 + MIX_A])
            y = _silu(acc)
            for hd in range(N_HEADS):
                yh = y[:, hd * HEAD_DIM:(hd + 1) * HEAD_DIM]
                if p < 2:
                    yh = yh * lax.rsqrt(jnp.sum(yh * yh, axis=-1, keepdims=True) + EPS)
                qkv_ref[r0:r0 + CONV_ROWS, c0 + hd * HEAD_DIM:c0 + (hd + 1) * HEAD_DIM] = yh
        acc = eu_scr[u_off + r0:u_off + r0 + CONV_ROWS, :] * dw_ref[0:1, :]
        for tap in range(1, CONV_W):
            acc = acc + eu_scr[u_off + tap + r0:u_off + tap + r0 + CONV_ROWS, :] * dw_ref[tap:tap + 1, :]
        u = acc + dwb_ref[...]
        uc = u - jnp.mean(u, axis=-1, keepdims=True)
        un = uc * lax.rsqrt(jnp.mean(uc * uc, axis=-1, keepdims=True) + EPS)
        uo_ref[r0:r0 + CONV_ROWS, :] = _silu(un * lng_ref[...] + lnb_ref[...]).astype(BF16)

    x = ba_ref[...]
    lane = lax.broadcasted_iota(I32, x.shape, 1)
    g = -jnp.exp(alog_ref[...]) * _softplus(x + dtb_ref[...])
    gate_ref[...] = jnp.where(lane < 2 * N_HEADS, _sigmoid(x), g)


def _conv(lay, qkv_raw, ug, ba, cw, dw, dwb, lng, lnb, alog, dtb):
    t = lay.t
    qh = TOK_TILE // Q_HALO
    uh = TOK_TILE // U_HALO
    n_qh = t // Q_HALO
    n_uh = t // U_HALO
    in_specs = [
        _row_spec(3 * MIX_A),
        pl.BlockSpec((Q_HALO, 3 * MIX_A), lambda i: (jnp.maximum(i * qh - 1, 0), 0)),
        pl.BlockSpec((Q_HALO, 3 * MIX_A), lambda i: (jnp.minimum((i + 1) * qh, n_qh - 1), 0)),
        _row_spec(MIX_B),
        pl.BlockSpec((U_HALO, MIX_B), lambda i: (jnp.maximum(i * uh - 1, 0), 0)),
        pl.BlockSpec((U_HALO, MIX_B), lambda i: (jnp.minimum((i + 1) * uh, n_uh - 1), 0)),
        _row_spec(LANES),
        _full_spec(cw.shape), _full_spec(dw.shape), _full_spec(dwb.shape),
        _full_spec(lng.shape), _full_spec(lnb.shape), _full_spec(alog.shape), _full_spec(dtb.shape),
    ]
    return pl.pallas_call(
        functools.partial(_conv_kernel, lay),
        grid=(lay.tiles,),
        in_specs=in_specs,
        out_specs=[_row_spec(3 * MIX_A), _row_spec(MIX_B), _row_spec(LANES)],
        out_shape=[jax.ShapeDtypeStruct((t, 3 * MIX_A), F32),
                   jax.ShapeDtypeStruct((t, MIX_B), BF16),
                   jax.ShapeDtypeStruct((t, LANES), F32)],
        scratch_shapes=[pltpu.VMEM((TOK_TILE + 2 * Q_HALO, 3 * MIX_A), F32),
                        pltpu.VMEM((TOK_TILE + 2 * U_HALO, MIX_B), F32)],
        compiler_params=_cparams(("parallel",)),
        name="conv",
    )(qkv_raw, qkv_raw, qkv_raw, ug, ug, ug, ba, cw, dw, dwb, lng, lnb, alog, dtb)


INV_BASE = 8


def _b16(xs):
    return [x.astype(BF16) for x in xs]


def _tri_inverse_minus_eye(nmats, rows, cols):
    assert INV_BASE == 8
    c = nmats[0].shape[0]
    shift = int(math.log2(INV_BASE))
    same = (rows >> shift) == (cols >> shift)
    n1 = [jnp.where(same, n, 0.0) for n in nmats]
    n1b = _b16(n1)
    n2 = [_dot(x, x) for x in n1b]
    n2b = _b16(n2)
    r = [_dot(jnp.concatenate([a, b], axis=0), b) for a, b in zip(n1b, n2b)]
    q = [a + b + x[:c] for a, b, x in zip(n1, n2, r)]
    n4 = [x[c:] for x in r]
    qn4 = [_dot(a, b) for a, b in zip(_b16(q), _b16(n4))]
    q = [a + b + x for a, b, x in zip(q, n4, qn4)]
    while (1 << shift) < c:
        off = ((rows >> (shift + 1)) == (cols >> (shift + 1))) & ((rows >> shift) != (cols >> shift))
        a = [jnp.where(off, -n, 0.0) for n in nmats]
        x = [ai + _dot(ab, qb) for ai, ab, qb in zip(a, _b16(a), _b16(q))]
        qx = [_dot(qb, xb) for qb, xb in zip(_b16(q), _b16(x))]
        q = [qi - xi - qxi for qi, xi, qxi in zip(q, x, qx)]
        shift += 1
    return q


def _delta_units(units):
    c = CHUNK
    rows = lax.broadcasted_iota(I32, (c, c), 0)
    cols = lax.broadcasted_iota(I32, (c, c), 1)
    pre = []
    for d, hd, qkv_ref, r0, gate, gcol, grow, s in units:
        incl = (rows >= cols) if d == 0 else (rows <= cols)
        lane = 2 * N_HEADS + N_HEADS * d + hd
        beta = gate[:, N_HEADS * d + hd:N_HEADS * d + hd + 1]
        gc = gcol[:, lane:lane + 1]
        gr = grow[lane:lane + 1, :]
        last = c - 1 if d == 0 else 0
        gtot = gcol[last:last + 1, lane:lane + 1]
        dec = jnp.where(incl, jnp.exp(jnp.where(incl, gc - gr, 0.0)), 0.0)
        q = qkv_ref[pl.ds(r0, c), hd * HEAD_DIM:(hd + 1) * HEAD_DIM]
        k = qkv_ref[pl.ds(r0, c), MIX_A + hd * HEAD_DIM:MIX_A + (hd + 1) * HEAD_DIM]
        v = qkv_ref[pl.ds(r0, c), 2 * MIX_A + hd * HEAD_DIM:2 * MIX_A + (hd + 1) * HEAD_DIM]
        qs = q * (HEAD_DIM ** -0.5)
        kbeta = k * beta
        egc = jnp.exp(gc)
        pre.append(dict(d=d, beta=beta, gc=gc, gtot=gtot, dec=dec, k=k, v=v, qs=qs, kbeta=kbeta, egc=egc, s=s))

    ak = [_dot_nt(jnp.concatenate([p['kbeta'], p['qs']], axis=0).astype(BF16), p['k'].astype(BF16))
          for p in pre]
    nmats, qks = [], []
    for p, a in zip(pre, ak):
        strict = (rows > cols) if p['d'] == 0 else (rows < cols)
        nmats.append(jnp.where(strict, -a[:c] * p['dec'], 0.0))
        qks.append(a[c:] * p['dec'])
    qinv = _tri_inverse_minus_eye(nmats, rows, cols)

    rhs = [jnp.concatenate([p['v'] * p['beta'], p['kbeta'] * p['egc']], axis=1) for p in pre]
    uw = [r + _dot(qi, rb) for r, qi, rb in zip(rhs, _b16(qinv), _b16(rhs))]
    sw = [_dot(jnp.concatenate([x[:, HEAD_DIM:], p['qs'] * p['egc']], axis=0).astype(BF16), p['s'].astype(BF16))
          for x, p in zip(uw, pre)]
    vnb = [(x[:, :HEAD_DIM] - y[:c]).astype(BF16) for x, y in zip(uw, sw)]
    o = [y[c:] + _dot(qk, vb) for y, qk, vb in zip(sw, _b16(qks), vnb)]
    kdec = [(p['k'] * jnp.exp(p['gtot'] - p['gc'])).astype(BF16) for p in pre]
    s_new = [p['s'] * jnp.exp(p['gtot']) + _dot_tn(kd, vb) for p, kd, vb in zip(pre, kdec, vnb)]
    return list(zip(o, s_new))


def _delta_kernel(has_s0, emit_final, n_chunk, *refs):
    refs = list(refs)
    qkvf_ref, qkvb_ref, gatef_ref, gateb_ref, gtf_ref, gtb_ref = refs[:6]
    refs = refs[6:]
    s0_ref = refs.pop(0) if has_s0 else None
    of_ref, ob_ref = refs[:2]
    refs = refs[2:]
    sfin_ref = refs.pop(0) if emit_final else None
    s_scr = refs[0]

    j = pl.program_id(1)

    @pl.when(j == 0)
    def _():
        if has_s0:
            s_scr[...] = s0_ref[0]
        else:
            s_scr[...] = jnp.zeros(s_scr.shape, F32)

    c = CHUNK
    rows = lax.broadcasted_iota(I32, (c, c), 0)
    cols = lax.broadcasted_iota(I32, (c, c), 1)
    tri_lo = (rows >= cols).astype(F32)
    tri_up = (rows <= cols).astype(F32)

    def body(ci, carry):
        units = []
        for d in range(2):
            cidx = ci if d == 0 else n_chunk - 1 - ci
            r0 = pl.multiple_of(cidx * c, c)
            qkv_ref = qkvf_ref if d == 0 else qkvb_ref
            gate = (gatef_ref if d == 0 else gateb_ref)[pl.ds(r0, c), :]
            gt = (gtf_ref if d == 0 else gtb_ref)[cidx]
            tri = tri_lo if d == 0 else tri_up
            tri_t = tri_up if d == 0 else tri_lo
            gcol = jnp.dot(tri, gate, preferred_element_type=F32, precision=HIGHEST)
            grow = jnp.dot(gt, tri_t, preferred_element_type=F32, precision=HIGHEST)
            for hd in range(N_HEADS):
                units.append((d, hd, qkv_ref, r0, gate, gcol, grow, s_scr[N_HEADS * d + hd]))
        for (d, hd, _, r0, _, _, _, _), (o, s_new) in zip(units, _delta_units(units)):
            s_scr[N_HEADS * d + hd] = s_new
            (of_ref if d == 0 else ob_ref)[pl.ds(r0, c), hd * HEAD_DIM:(hd + 1) * HEAD_DIM] = o
        return carry

    lax.fori_loop(0, n_chunk, body, 0)

    if emit_final:
        @pl.when(j == pl.num_programs(1) - 1)
        def _():
            sfin_ref[0] = s_scr[...]


def _delta(qkv, gate, gate_t, n_seq, seq_len, row0, blk, s0):
    t = qkv.shape[0]
    nblk = seq_len // blk
    n_chunk = blk // CHUNK
    b0 = row0 // blk
    has_s0 = s0 is not None
    emit_final = not has_s0
    n_state = 2 * N_HEADS

    def fwd(b, j):
        return b0 + b * nblk + j

    def bwd(b, j):
        return b0 + b * nblk + (nblk - 1 - j)

    in_specs = [
        pl.BlockSpec((blk, 3 * MIX_A), lambda b, j: (fwd(b, j), 0)),
        pl.BlockSpec((blk, 3 * MIX_A), lambda b, j: (bwd(b, j), 0)),
        pl.BlockSpec((blk, LANES), lambda b, j: (fwd(b, j), 0)),
        pl.BlockSpec((blk, LANES), lambda b, j: (bwd(b, j), 0)),
        pl.BlockSpec((n_chunk, 16, CHUNK), lambda b, j: (fwd(b, j), 0, 0)),
        pl.BlockSpec((n_chunk, 16, CHUNK), lambda b, j: (bwd(b, j), 0, 0)),
    ]
    args = [qkv, qkv, gate, gate, gate_t, gate_t]
    if has_s0:
        in_specs.append(pl.BlockSpec((1, n_state, HEAD_DIM, HEAD_DIM), lambda b, j: (b, 0, 0, 0)))
        args.append(s0)
    n_rows = n_seq * seq_len
    out_specs = [pl.BlockSpec((blk, MIX_A), lambda b, j: (b * nblk + j, 0)),
                 pl.BlockSpec((blk, MIX_A), lambda b, j: (b * nblk + (nblk - 1 - j), 0))]
    out_shape = [jax.ShapeDtypeStruct((n_rows, MIX_A), F32), jax.ShapeDtypeStruct((n_rows, MIX_A), F32)]
    if emit_final:
        out_specs.append(pl.BlockSpec((1, n_state, HEAD_DIM, HEAD_DIM), lambda b, j: (b, 0, 0, 0)))
        out_shape.append(jax.ShapeDtypeStruct((n_seq, n_state, HEAD_DIM, HEAD_DIM), F32))
    del t
    return pl.pallas_call(
        functools.partial(_delta_kernel, has_s0, emit_final, n_chunk),
        grid=(n_seq, nblk),
        in_specs=in_specs,
        out_specs=out_specs,
        out_shape=out_shape,
        scratch_shapes=[pltpu.VMEM((n_state, HEAD_DIM, HEAD_DIM), F32)],
        compiler_params=_cparams(("parallel", "arbitrary")),
        name="delta_latent" if has_s0 else "delta_prompt",
    )(*args)


def _outproj_kernel(lay, xp_ref, xs_ref, pos_ref, mod_ref, opf_ref, opb_ref, osf_ref, osb_ref, z_ref, u_ref,
                    og_ref, n2_ref, wo_ref, wr_ref, br_ref, x1_ref, h2_ref, ids_ref, wts_ref, cnt_ref, cnt_scr):
    i = pl.program_id(0)
    x0 = _load_x(lay, i, xp_ref, xs_ref, pos_ref)
    m = mod_ref[0]
    o = jnp.where(lay.is_sample(i), osf_ref[...] + osb_ref[...], opf_ref[...] + opb_ref[...])
    z = z_ref[...]
    mix = _dot(u_ref[...], wo_ref[MIX_A:, :])
    for hd in range(N_HEADS):
        sl = slice(hd * HEAD_DIM, (hd + 1) * HEAD_DIM)
        oh = _rms(o[:, sl]) * og_ref[...] * _silu(z[:, sl])
        mix = mix + _dot(oh.astype(BF16), wo_ref[sl, :])
    x1 = x0 + m[2:3] * mix
    x1_ref[...] = x1
    h2 = _rms(x1) * n2_ref[...] * (1.0 + m[4:5]) + m[3:4]
    h2b = h2.astype(BF16)
    half = D_MODEL // 2
    hi = pltpu.bitcast(h2b[:, :half].astype(F32), jnp.uint32)
    lo = pltpu.bitcast(h2b[:, half:].astype(F32), jnp.uint32)
    h2_ref[...] = pltpu.bitcast((hi & jnp.uint32(0xFFFF0000)) | (lo >> 16), I32)

    logits = _dot(h2b, wr_ref[...]) + br_ref[...]
    lane = lax.broadcasted_iota(I32, logits.shape, 1)
    neg = jnp.float32(-jnp.inf)
    big = jnp.int32(LANES)
    gl = jnp.where(lane < N_GROUPS, logits, neg)
    gmax = jnp.max(gl, axis=-1, keepdims=True)
    grp = jnp.min(jnp.where(gl == gmax, lane, big), axis=-1, keepdims=True)
    p_grp = 1.0 / jnp.sum(jnp.where(lane < N_GROUPS, jnp.exp(gl - gmax), 0.0), axis=-1, keepdims=True)
    e_lane = lane - N_GROUPS
    in_grp = (e_lane >= grp * EXPERTS_PER_GROUP) & (e_lane < (grp + 1) * EXPERTS_PER_GROUP)
    el = jnp.where(in_grp, logits, neg)
    m1 = jnp.max(el, axis=-1, keepdims=True)
    i1 = jnp.min(jnp.where(el == m1, lane, big), axis=-1, keepdims=True)
    el2 = jnp.where(lane == i1, neg, el)
    m2 = jnp.max(el2, axis=-1, keepdims=True)
    i2 = jnp.min(jnp.where(el2 == m2, lane, big), axis=-1, keepdims=True)
    e2 = jnp.exp(m2 - m1)
    w1 = p_grp / (1.0 + e2)
    w2 = p_grp * e2 / (1.0 + e2)
    wts_ref[...] = jnp.where(lane == 0, w1, jnp.where(lane == 1, w2, 0.0))

    @pl.when(i == 0)
    def _():
        cnt_scr[...] = jnp.zeros(cnt_scr.shape, F32)

    oh1 = lane == i1
    oh2 = lane == i2
    oh = jnp.where(oh1, 1.0, jnp.where(oh2, 1.0, 0.0))
    tm = logits.shape[0]
    rows = lax.broadcasted_iota(I32, (tm, tm), 0)
    cols = lax.broadcasted_iota(I32, (tm, tm), 1)
    before = jnp.where(rows > cols, 1.0, 0.0).astype(BF16)
    seen = cnt_scr[...] + _dot(before, oh.astype(BF16))
    r1 = jnp.sum(jnp.where(oh1, seen, 0.0), axis=-1, keepdims=True).astype(I32)
    r2 = jnp.sum(jnp.where(oh2, seen, 0.0), axis=-1, keepdims=True).astype(I32)
    cnt_scr[...] = cnt_scr[...] + jnp.sum(oh, axis=0, keepdims=True)
    cnt_ref[...] = cnt_scr[...]
    ids_ref[...] = jnp.where(lane == 0, i1 - N_GROUPS,
                             jnp.where(lane == 1, i2 - N_GROUPS,
                                       jnp.where(lane == 2, r1, jnp.where(lane == 3, r2, 0))))


def _outproj(lay, xp, xs, pos, mod, o_pf, o_pb, o_sf, o_sb, z, u, onorm_g, norm2_g, w_out, w_route, b_route):
    t = lay.t
    p_spec = pl.BlockSpec((TOK_TILE, MIX_A), lambda i: (lay.xp_blk(i), 0))
    s_spec = pl.BlockSpec((TOK_TILE, MIX_A), lambda i: (lay.xs_blk(i), 0))
    return pl.pallas_call(
        functools.partial(_outproj_kernel, lay),
        grid=(lay.tiles,),
        in_specs=_x_specs(lay) + [_mod_spec(lay), p_spec, p_spec, s_spec, s_spec, _row_spec(MIX_A),
                                  _row_spec(MIX_B), _full_spec((1, HEAD_DIM)), _full_spec((1, D_MODEL)),
                                  _full_spec(w_out.shape), _full_spec(w_route.shape),
                                  _full_spec(b_route.shape)],
        out_specs=[_row_spec(D_MODEL), _row_spec(D_MODEL // 2), _row_spec(LANES), _row_spec(LANES),
                   _full_spec((1, LANES))],
        out_shape=[jax.ShapeDtypeStruct((t, D_MODEL), F32),
                   jax.ShapeDtypeStruct((t, D_MODEL // 2), I32),
                   jax.ShapeDtypeStruct((t, LANES), I32),
                   jax.ShapeDtypeStruct((t, LANES), F32),
                   jax.ShapeDtypeStruct((1, LANES), F32)],
        scratch_shapes=[pltpu.VMEM((1, LANES), F32)],
        compiler_params=_cparams(("arbitrary",)),
        name="outproj",
    )(xp, xs, pos, mod, o_pf, o_pb, o_sf, o_sb, z, u, onorm_g, norm2_g, w_out, w_route, b_route)


def _sc_gather(table, idx, chunk, name):
    n_rows, width = idx.shape[0], table.shape[1]
    mesh = plsc.VectorSubcoreMesh(core_axis_name="c", subcore_axis_name="s")
    n_workers = mesh.num_cores * mesh.num_subcores
    per_worker = n_rows // n_workers
    n_chunks = per_worker // chunk
    assert n_rows == n_workers * n_chunks * chunk and n_chunks % 2 == 0 and chunk % 8 == 0 and chunk <= LANES

    def body(table_hbm, idx_hbm, out_hbm, idx_v, rows_v, sem):
        base = (lax.axis_index("s") * mesh.num_cores + lax.axis_index("c")) * per_worker

        def gather(slot):
            return pltpu.make_async_copy(table_hbm.at[idx_v.at[slot]], rows_v.at[slot], sem.at[slot])

        def fetch(g, slot):
            off = pl.multiple_of(base + g * chunk, 8)
            pltpu.sync_copy(idx_hbm.at[pl.ds(off, chunk)], idx_v.at[slot])
            gather(slot).start()

        for slot in range(2):
            fetch(slot, slot)

        @pl.loop(0, n_chunks, step=2)
        def _(g):
            for slot in range(2):
                off = pl.multiple_of(base + (g + slot) * chunk, 8)
                gather(slot).wait()
                pltpu.sync_copy(rows_v.at[slot], out_hbm.at[pl.ds(off, chunk)])

                @pl.when(g + slot + 2 < n_chunks)
                def _():
                    fetch(g + slot + 2, slot)

    return pl.kernel(
        body,
        out_type=jax.ShapeDtypeStruct((n_rows, width), table.dtype),
        mesh=mesh,
        scratch_types=[pltpu.VMEM((2, chunk), I32), pltpu.VMEM((2, chunk, width), table.dtype),
                       pltpu.SemaphoreType.DMA((2,))],
        name=name,
    )(table, idx)


def _sc_scatter2(src, idx0, idx1, n_out, chunk, name):
    n_rows, width = src.shape
    mesh = plsc.VectorSubcoreMesh(core_axis_name="c", subcore_axis_name="s")
    n_workers = mesh.num_cores * mesh.num_subcores
    per_worker = n_rows // n_workers
    n_chunks = per_worker // chunk
    assert n_rows == n_workers * n_chunks * chunk and n_chunks % 2 == 0 and chunk % 8 == 0 and chunk <= LANES

    def body(src_hbm, i0_hbm, i1_hbm, out_hbm, i0_v, i1_v, rows_v, sem_in, sem_out):
        base = (lax.axis_index("s") * mesh.num_cores + lax.axis_index("c")) * per_worker

        def rows_in(g, slot):
            off = pl.multiple_of(base + g * chunk, 8)
            return pltpu.make_async_copy(src_hbm.at[pl.ds(off, chunk)], rows_v.at[slot], sem_in.at[slot])

        def fetch(g, slot):
            off = pl.multiple_of(base + g * chunk, 8)
            pltpu.sync_copy(i0_hbm.at[pl.ds(off, chunk)], i0_v.at[slot])
            pltpu.sync_copy(i1_hbm.at[pl.ds(off, chunk)], i1_v.at[slot])
            rows_in(g, slot).start()

        for slot in range(2):
            fetch(slot, slot)

        @pl.loop(0, n_chunks, step=2)
        def _(g):
            for slot in range(2):
                rows_in(g + slot, slot).wait()
                puts = [pltpu.make_async_copy(rows_v.at[slot], out_hbm.at[iv.at[slot]], sem_out.at[slot])
                        for iv in (i0_v, i1_v)]
                for put in puts:
                    put.start()
                for put in puts:
                    put.wait()

                @pl.when(g + slot + 2 < n_chunks)
                def _():
                    fetch(g + slot + 2, slot)

    return pl.kernel(
        body,
        out_type=jax.ShapeDtypeStruct((n_out, width), src.dtype),
        mesh=mesh,
        scratch_types=[pltpu.VMEM((2, chunk), I32), pltpu.VMEM((2, chunk), I32),
                       pltpu.VMEM((2, chunk, width), src.dtype),
                       pltpu.SemaphoreType.DMA((2,)), pltpu.SemaphoreType.DMA((2,))],
        name=name,
    )(src, idx0, idx1)


def _expert_kernel(blk_e_ref, nused_ref, xb_ref, wg_ref, wu_ref, wd_ref, yb_ref, wg_s, wu_s, wd_s):
    i = pl.program_id(0)
    nused = nused_ref[0]

    @pl.when(i < nused)
    def _():
        changed = (i == 0) | (blk_e_ref[i] != blk_e_ref[jnp.maximum(i - 1, 0)])

        @pl.when(changed)
        def _():
            wg_s[...] = wg_ref[0].astype(BF16)
            wu_s[...] = wu_ref[0].astype(BF16)
            wd_s[...] = wd_ref[0].astype(BF16)

        x = pltpu.bitcast(xb_ref[...], jnp.uint32)
        half = D_MODEL // 2
        xa = pltpu.bitcast(x & jnp.uint32(0xFFFF0000), F32).astype(BF16)
        xb = pltpu.bitcast(x << 16, F32).astype(BF16)
        g = _dot(xa, wg_s[:half, :]) + _dot(xb, wg_s[half:, :])
        u = _dot(xa, wu_s[:half, :]) + _dot(xb, wu_s[half:, :])
        hmid = (_silu(g) * u).astype(BF16)
        yb_ref[...] = _dot(hmid, wd_s[...])

    @pl.when(i >= nused)
    def _():
        yb_ref[...] = jnp.zeros(yb_ref.shape, F32)


def _experts(xb, blk_e, nused, w_gate, w_up, w_down):
    nb = blk_e.shape[0]
    grid_spec = pltpu.PrefetchScalarGridSpec(
        num_scalar_prefetch=2,
        grid=(nb,),
        in_specs=[
            pl.BlockSpec((ROUTE_BM, D_MODEL // 2), lambda i, be, nu: (i, 0)),
            pl.BlockSpec((1, D_MODEL, D_EXPERT), lambda i, be, nu: (be[i], 0, 0)),
            pl.BlockSpec((1, D_MODEL, D_EXPERT), lambda i, be, nu: (be[i], 0, 0)),
            pl.BlockSpec((1, D_EXPERT, D_MODEL), lambda i, be, nu: (be[i], 0, 0)),
        ],
        out_specs=pl.BlockSpec((ROUTE_BM, D_MODEL), lambda i, be, nu: (i, 0)),
        scratch_shapes=[pltpu.VMEM((D_MODEL, D_EXPERT), BF16),
                        pltpu.VMEM((D_MODEL, D_EXPERT), BF16),
                        pltpu.VMEM((D_EXPERT, D_MODEL), BF16)],
    )
    return pl.pallas_call(
        _expert_kernel,
        grid_spec=grid_spec,
        out_shape=jax.ShapeDtypeStruct((nb * ROUTE_BM, D_MODEL), F32),
        compiler_params=_cparams(("arbitrary",)),
        name="expert",
    )(blk_e, nused, xb, w_gate, w_up, w_down)


def _combine_kernel(lay, y0_ref, y1_ref, x1_ref, wts_ref, mod_ref, fg_ref, yp_ref, ys_ref):
    i = pl.program_id(0)
    m = mod_ref[0]
    w = wts_ref[...]
    ff = y0_ref[...] * w[:, 0:1] + y1_ref[...] * w[:, 1:2]
    y = _rms(x1_ref[...] + m[5:6] * ff) * fg_ref[...]

    @pl.when(i < lay.tiles_p)
    def _():
        yp_ref[...] = y

    @pl.when(i >= lay.tiles_p)
    def _():
        ys_ref[...] = y


def _combine(lay, yg, x1, wts, mod, final_g):
    return pl.pallas_call(
        functools.partial(_combine_kernel, lay),
        grid=(lay.tiles,),
        in_specs=[_row_spec(D_MODEL), pl.BlockSpec((TOK_TILE, D_MODEL), lambda i: (i + lay.tiles, 0)),
                  _row_spec(D_MODEL), _row_spec(LANES), _mod_spec(lay), _full_spec((1, D_MODEL))],
        out_specs=[pl.BlockSpec((TOK_TILE, D_MODEL), lambda i: (lay.xp_blk(i), 0)),
                   pl.BlockSpec((TOK_TILE, D_MODEL), lambda i: (lay.xs_blk(i), 0))],
        out_shape=[jax.ShapeDtypeStruct((lay.t_p, D_MODEL), F32),
                   jax.ShapeDtypeStruct((lay.t_s, D_MODEL), F32)],
        compiler_params=_cparams(("arbitrary",)),
        name="combine",
    )(yg, yg, x1, wts, mod, final_g)


def _dispatch_plan(ids, counts):
    n_tok = ids.shape[0]
    padded = (counts + ROUTE_BM - 1) // ROUTE_BM * ROUTE_BM
    pad_end = jnp.cumsum(padded)
    pad_start = pad_end - padded
    expert = ids[:, 0:2]
    start = jnp.sum(jnp.where(expert[:, :, None] == jnp.arange(N_EXPERTS, dtype=I32), pad_start, 0), axis=-1)
    dest = (start + ids[:, 2:4]).astype(I32)
    nb = -(-(2 * n_tok + N_EXPERTS * (ROUTE_BM - 1)) // ROUTE_BM)
    block_start = jnp.arange(nb, dtype=I32) * ROUTE_BM
    blk_e = jnp.minimum(jnp.searchsorted(pad_end, block_start, side='right'), N_EXPERTS - 1).astype(I32)
    nused = (pad_end[-1:] // ROUTE_BM).astype(I32)
    return dest[:, 0], dest[:, 1], blk_e, nused


def _grid_pos_embed(n_tokens):
    rows = n_tokens // GRID_W
    r = jnp.repeat(jnp.arange(rows, dtype=F32), GRID_W)
    col = jnp.tile(jnp.arange(GRID_W, dtype=F32), rows)
    n_freq = D_MODEL // 4
    freq = jnp.exp(jnp.arange(n_freq, dtype=F32) * (-math.log(10000.0) / n_freq))

    def enc(p):
        ang = p[:, None] * freq[None, :]
        return jnp.concatenate([jnp.sin(ang), jnp.cos(ang)], axis=-1)

    return jnp.concatenate([enc(r), enc(col)], axis=-1)


def _lane_pad(v, offset):
    return jnp.zeros((1, LANES), F32).at[0, offset:offset + v.shape[0]].set(v.astype(F32))


def kernel(x_prompt, x_sample, state_delta, c, c_ctx, norm1_g, w_mod, b_mod, w_in, conv_qkv_w, A_log, dt_bias, onorm_g, dw_w, dw_b, cln_g, cln_b, w_out, norm2_g, w_group, b_group, w_expert, b_expert, w_e_gate, w_e_up, w_e_down, final_g):
    n_p, l_p, _ = x_prompt.shape
    n_s, l_s, _ = x_sample.shape
    lay = _Layout(n_p, l_p, n_s, l_s)
    depth = w_in.shape[0]
    assert depth == 1
    xp = x_prompt.reshape(lay.t_p, D_MODEL)
    xs = x_sample.reshape(lay.t_s, D_MODEL)
    pos = _grid_pos_embed(l_s)

    cond = jnp.concatenate([c_ctx[None, :], c], axis=0)
    cond = jnp.pad(cond, ((0, (-cond.shape[0]) % 8), (0, 0)))
    mod = _modulation(cond, w_mod[0], b_mod[0]).reshape(cond.shape[0], 6, D_MODEL)

    n_gate = 4 * N_HEADS
    wi = w_in[0]
    w_main = jnp.concatenate([wi[:, :4 * MIX_A], wi[:, 4 * MIX_A + n_gate:]], axis=1).astype(BF16)
    w_small = jnp.pad(wi[:, 4 * MIX_A:4 * MIX_A + n_gate], ((0, 0), (0, LANES - n_gate))).astype(BF16)
    qkv_raw, z, ug, ba = _inproj(lay, xp, xs, pos, mod, norm1_g[0][None, :], w_main, w_small)

    cw = jnp.pad(conv_qkv_w[0], ((0, 8 - SHORT_CONV), (0, 0)))
    dw = jnp.pad(dw_w[0], ((0, 32 - CONV_W), (0, 0)))
    alog = _lane_pad(A_log[0].reshape(-1), 2 * N_HEADS)
    dtb = _lane_pad(dt_bias[0].reshape(-1), 2 * N_HEADS)
    qkv, u_conf, gate = _conv(lay, qkv_raw, ug, ba, cw, dw, dw_b[0][None, :], cln_g[0][None, :],
                              cln_b[0][None, :], alog, dtb)

    gate_t = gate[:, :16].reshape(lay.t // CHUNK, CHUNK, 16).transpose(0, 2, 1)
    o_pf, o_pb, s_fin = _delta(qkv, gate, gate_t, n_p, l_p, 0, l_p, None)
    s0 = state_delta[:, 0].reshape(n_s, 2 * N_HEADS, HEAD_DIM, HEAD_DIM)
    o_sf, o_sb = _delta(qkv, gate, gate_t, n_s, l_s, lay.t_p, DELTA_BLK, s0)

    w_route = jnp.pad(jnp.concatenate([w_group[0], w_expert[0]], axis=1),
                      ((0, 0), (0, LANES - N_GROUPS - N_EXPERTS))).astype(BF16)
    b_route = _lane_pad(jnp.concatenate([b_group[0], b_expert[0]]), 0)
    x1, h2p, ids, wts, cnt = _outproj(lay, xp, xs, pos, mod, o_pf, o_pb, o_sf, o_sb, z, u_conf, onorm_g[0][None, :],
                                 norm2_g[0][None, :], w_out[0].astype(BF16), w_route, b_route)

    counts = cnt[0, N_GROUPS:N_GROUPS + N_EXPERTS].astype(I32)
    dest0, dest1, blk_e, nused = _dispatch_plan(ids, counts)
    xb = _sc_scatter2(h2p, dest0, dest1, blk_e.shape[0] * ROUTE_BM, DISPATCH_CHUNK, "dispatch_scatter")
    yb = _experts(xb, blk_e, nused, w_e_gate[0], w_e_up[0], w_e_down[0])
    yg = _sc_gather(yb, jnp.concatenate([dest0, dest1]), COMBINE_CHUNK, "combine_gather")
    y_p, y_s = _combine(lay, yg, x1, wts, mod, final_g[None, :])

    new_state = s_fin.reshape(n_p, 1, 2, N_HEADS, HEAD_DIM, HEAD_DIM)
    return (y_p.reshape(x_prompt.shape), y_s.reshape(x_sample.shape), new_state)
```

```python
import functools
import math

import jax
import jax.numpy as jnp
from jax import lax
from jax.experimental import pallas as pl
from jax.experimental.pallas import tpu as pltpu
from jax.experimental.pallas import tpu_sc as plsc

F32 = jnp.float32
BF16 = jnp.bfloat16
I32 = jnp.int32

D_MODEL = 1024
MIX_A = 512
MIX_B = 512
HEAD_DIM = 128
N_HEADS = 4
SHORT_CONV = 5
CONV_W = 31
N_GROUPS = 4
EXPERTS_PER_GROUP = 8
N_EXPERTS = 32
D_EXPERT = 512
GRID_W = 64
EPS = 1e-6

LANES = 128
TOK_TILE = 256
CHUNK = 64
DELTA_BLK = 512
ROUTE_BM = 256
DISPATCH_CHUNK = 64
COMBINE_CHUNK = 32
Q_HALO = 8
U_HALO = 16
VMEM_LIMIT = 56 * 1024 * 1024

HIGHEST = lax.Precision.HIGHEST


def _cparams(sem):
    return pltpu.CompilerParams(dimension_semantics=sem, vmem_limit_bytes=VMEM_LIMIT)


def _sigmoid(x):
    return 1.0 / (1.0 + jnp.exp(-x))


def _silu(x):
    return x * _sigmoid(x)


def _softplus(x):
    return jnp.maximum(x, 0.0) + jnp.log1p(jnp.exp(-jnp.abs(x)))


def _rms(x):
    return x * lax.rsqrt(jnp.mean(x * x, axis=-1, keepdims=True) + EPS)


def _dot(a, b):
    return jnp.dot(a, b, preferred_element_type=F32)


def _dot_nt(a, b):
    return lax.dot_general(a, b, (((1,), (1,)), ((), ())), preferred_element_type=F32)


def _dot_tn(a, b):
    return lax.dot_general(a, b, (((0,), (0,)), ((), ())), preferred_element_type=F32)


def _mod_kernel(cond_ref, w_ref, b_ref, o_ref):
    s = _silu(cond_ref[...])
    o_ref[...] = jnp.dot(s, w_ref[...], preferred_element_type=F32, precision=HIGHEST) + b_ref[...]


def _modulation(cond, w_mod, b_mod):
    n = cond.shape[0]
    tn = D_MODEL
    return pl.pallas_call(
        _mod_kernel,
        grid=(6 * D_MODEL // tn,),
        in_specs=[pl.BlockSpec((n, D_MODEL), lambda j: (0, 0)),
                  pl.BlockSpec((D_MODEL, tn), lambda j: (0, j)),
                  pl.BlockSpec((1, tn), lambda j: (0, j))],
        out_specs=pl.BlockSpec((n, tn), lambda j: (0, j)),
        out_shape=jax.ShapeDtypeStruct((n, 6 * D_MODEL), F32),
        compiler_params=_cparams(("parallel",)),
        name="mod",
    )(cond, w_mod, b_mod.reshape(1, -1))


class _Layout:
    def __init__(self, n_p, l_p, n_s, l_s):
        self.n_p, self.l_p, self.n_s, self.l_s = n_p, l_p, n_s, l_s
        self.t_p = n_p * l_p
        self.t_s = n_s * l_s
        self.t = self.t_p + self.t_s
        assert l_p % TOK_TILE == 0 and l_s % TOK_TILE == 0 and l_s % DELTA_BLK == 0
        self.tiles_p = self.t_p // TOK_TILE
        self.tiles_s = self.t_s // TOK_TILE
        self.tiles = self.tiles_p + self.tiles_s
        self.tps_p = l_p // TOK_TILE
        self.tps_s = l_s // TOK_TILE

    def is_sample(self, i):
        return i >= self.tiles_p

    def mod_row(self, i):
        return jnp.where(i < self.tiles_p, 0, 1 + (i - self.tiles_p) // self.tps_s)

    def pos_blk(self, i):
        return jnp.where(i < self.tiles_p, 0, (i - self.tiles_p) % self.tps_s)

    def xp_blk(self, i):
        return jnp.minimum(i, self.tiles_p - 1)

    def xs_blk(self, i):
        return jnp.maximum(i - self.tiles_p, 0)

    def seq_pos(self, i):
        in_s = i >= self.tiles_p
        pos = jnp.where(in_s, (i - self.tiles_p) % self.tps_s, i % self.tps_p)
        n = jnp.where(in_s, self.tps_s, self.tps_p)
        return pos, n


def _load_x(lay, i, xp_ref, xs_ref, pos_ref):
    return jnp.where(lay.is_sample(i), xs_ref[...] + pos_ref[...], xp_ref[...])


def _x_specs(lay):
    return [pl.BlockSpec((TOK_TILE, D_MODEL), lambda i: (lay.xp_blk(i), 0)),
            pl.BlockSpec((TOK_TILE, D_MODEL), lambda i: (lay.xs_blk(i), 0)),
            pl.BlockSpec((TOK_TILE, D_MODEL), lambda i: (lay.pos_blk(i), 0))]


def _mod_spec(lay):
    return pl.BlockSpec((1, 6, D_MODEL), lambda i: (lay.mod_row(i), 0, 0))


def _row_spec(width):
    return pl.BlockSpec((TOK_TILE, width), lambda i: (i, 0))


def _full_spec(shape):
    nd = len(shape)
    return pl.BlockSpec(shape, lambda i: (0,) * nd)


def _inproj_kernel(lay, xp_ref, xs_ref, pos_ref, mod_ref, g_ref, wm_ref, ws_ref,
                   qkv_ref, z_ref, ug_ref, ba_ref):
    i = pl.program_id(0)
    x = _load_x(lay, i, xp_ref, xs_ref, pos_ref)
    m = mod_ref[0]
    h = _rms(x) * g_ref[...] * (1.0 + m[1:2]) + m[0:1]
    hb = h.astype(BF16)
    qkv_ref[...] = _dot(hb, wm_ref[:, 0:3 * MIX_A])
    z_ref[...] = _dot(hb, wm_ref[:, 3 * MIX_A:4 * MIX_A])
    glu = _dot(hb, wm_ref[:, 4 * MIX_A:4 * MIX_A + 2 * MIX_B])
    ug_ref[...] = glu[:, :MIX_B] * _sigmoid(glu[:, MIX_B:])
    ba_ref[...] = _dot(hb, ws_ref[...])


def _inproj(lay, xp, xs, pos, mod, norm_g, w_main, w_small):
    t = lay.t
    return pl.pallas_call(
        functools.partial(_inproj_kernel, lay),
        grid=(lay.tiles,),
        in_specs=_x_specs(lay) + [_mod_spec(lay), _full_spec((1, D_MODEL)),
                                  _full_spec(w_main.shape), _full_spec(w_small.shape)],
        out_specs=[_row_spec(3 * MIX_A), _row_spec(MIX_A), _row_spec(MIX_B), _row_spec(LANES)],
        out_shape=[jax.ShapeDtypeStruct((t, 3 * MIX_A), F32),
                   jax.ShapeDtypeStruct((t, MIX_A), F32),
                   jax.ShapeDtypeStruct((t, MIX_B), F32),
                   jax.ShapeDtypeStruct((t, LANES), F32)],
        compiler_params=_cparams(("parallel",)),
        name="inproj",
    )(xp, xs, pos, mod, norm_g, w_main, w_small)


SUBLANES = 8
CONV_PITCH = 33
CONV_OUT_ROWS = SUBLANES * CONV_PITCH
POST_ROWS = 64
Q_EXT_ROWS = 280
U_EXT_ROWS = 296
Q_GROUPS = 3 * MIX_A // LANES
U_GROUPS = MIX_B // LANES


def _strided_conv(ext_scr, res_scr, w_ref, g, off, n_taps, j_block):
    for j0 in range(0, CONV_PITCH, j_block):
        js = range(j0, min(j0 + j_block, CONV_PITCH))
        v = {m: ext_scr[g, pl.ds(off + m, SUBLANES, stride=CONV_PITCH), :]
             for m in range(js[0], js[-1] + n_taps)}
        for j in js:
            acc = v[j] * w_ref[g, 0:1, :]
            for s in range(1, n_taps):
                acc = acc + v[j + s] * w_ref[g, s:s + 1, :]
            res_scr[g, pl.ds(j, SUBLANES, stride=CONV_PITCH), :] = acc


def _conv_kernel(lay, qc_ref, qp_ref, qn_ref, uc_ref, up_ref, un_ref, ba_ref, cw_ref, dw_ref,
                 dwb_ref, lng_ref, lnb_ref, alog_ref, dtb_ref,
                 qkv_ref, uo_ref, gate_ref, eq_scr, eu_scr, rq_scr, ru_scr):
    i = pl.program_id(0)
    pos, n = lay.seq_pos(i)
    has_prev = pos != 0
    has_next = pos != n - 1
    qp = jnp.where(has_prev, qp_ref[...], 0.0)
    qn = jnp.where(has_next, qn_ref[...], 0.0)
    for g in range(Q_GROUPS):
        sl = slice(g * LANES, (g + 1) * LANES)
        eq_scr[g, 0:Q_HALO, :] = qp[:, sl]
        eq_scr[g, Q_HALO:Q_HALO + TOK_TILE, :] = qc_ref[:, sl]
        eq_scr[g, Q_HALO + TOK_TILE:Q_HALO + TOK_TILE + Q_HALO, :] = qn[:, sl]
        eq_scr[g, TOK_TILE + 2 * Q_HALO:, :] = jnp.zeros((Q_EXT_ROWS - TOK_TILE - 2 * Q_HALO, LANES), F32)
    up = jnp.where(has_prev, up_ref[...], 0.0)
    un_ = jnp.where(has_next, un_ref[...], 0.0)
    for g in range(U_GROUPS):
        sl = slice(g * LANES, (g + 1) * LANES)
        eu_scr[g, 0:U_HALO, :] = up[:, sl]
        eu_scr[g, U_HALO:U_HALO + TOK_TILE, :] = uc_ref[:, sl]
        eu_scr[g, U_HALO + TOK_TILE:U_HALO + TOK_TILE + U_HALO, :] = un_[:, sl]
        eu_scr[g, TOK_TILE + 2 * U_HALO:, :] = jnp.zeros((U_EXT_ROWS - TOK_TILE - 2 * U_HALO, LANES), F32)

    def q_group(g, carry):
        _strided_conv(eq_scr, rq_scr, cw_ref, g, Q_HALO - SHORT_CONV // 2, SHORT_CONV, 11)
        return carry

    def u_group(g, carry):
        _strided_conv(eu_scr, ru_scr, dw_ref, g, U_HALO - CONV_W // 2, CONV_W, 11)
        return carry

    lax.fori_loop(0, Q_GROUPS, q_group, 0)
    lax.fori_loop(0, U_GROUPS, u_group, 0)

    for rc in range(TOK_TILE // POST_ROWS):
        r0 = rc * POST_ROWS
        for g in range(Q_GROUPS):
            y = _silu(rq_scr[g, r0:r0 + POST_ROWS, :])
            if g < 2 * N_HEADS:
                y = y * lax.rsqrt(jnp.sum(y * y, axis=-1, keepdims=True) + EPS)
            qkv_ref[r0:r0 + POST_ROWS, g * LANES:(g + 1) * LANES] = y
        u = jnp.concatenate([ru_scr[g, r0:r0 + POST_ROWS, :] for g in range(U_GROUPS)], axis=1) + dwb_ref[...]
        uc = u - jnp.mean(u, axis=-1, keepdims=True)
        un = uc * lax.rsqrt(jnp.mean(uc * uc, axis=-1, keepdims=True) + EPS)
        uo_ref[r0:r0 + POST_ROWS, :] = _silu(un * lng_ref[...] + lnb_ref[...]).astype(BF16)

    x = ba_ref[...]
    lane = lax.broadcasted_iota(I32, x.shape, 1)
    g = -jnp.exp(alog_ref[...]) * _softplus(x + dtb_ref[...])
    gate_ref[...] = jnp.where(lane < 2 * N_HEADS, _sigmoid(x), g)


def _conv(lay, qkv_raw, ug, ba, cw, dw, dwb, lng, lnb, alog, dtb):
    t = lay.t
    qh = TOK_TILE // Q_HALO
    uh = TOK_TILE // U_HALO
    n_qh = t // Q_HALO
    n_uh = t // U_HALO
    in_specs = [
        _row_spec(3 * MIX_A),
        pl.BlockSpec((Q_HALO, 3 * MIX_A), lambda i: (jnp.maximum(i * qh - 1, 0), 0)),
        pl.BlockSpec((Q_HALO, 3 * MIX_A), lambda i: (jnp.minimum((i + 1) * qh, n_qh - 1), 0)),
        _row_spec(MIX_B),
        pl.BlockSpec((U_HALO, MIX_B), lambda i: (jnp.maximum(i * uh - 1, 0), 0)),
        pl.BlockSpec((U_HALO, MIX_B), lambda i: (jnp.minimum((i + 1) * uh, n_uh - 1), 0)),
        _row_spec(LANES),
        _full_spec(cw.shape), _full_spec(dw.shape), _full_spec(dwb.shape),
        _full_spec(lng.shape), _full_spec(lnb.shape), _full_spec(alog.shape), _full_spec(dtb.shape),
    ]
    return pl.pallas_call(
        functools.partial(_conv_kernel, lay),
        grid=(lay.tiles,),
        in_specs=in_specs,
        out_specs=[_row_spec(3 * MIX_A), _row_spec(MIX_B), _row_spec(LANES)],
        out_shape=[jax.ShapeDtypeStruct((t, 3 * MIX_A), F32),
                   jax.ShapeDtypeStruct((t, MIX_B), BF16),
                   jax.ShapeDtypeStruct((t, LANES), F32)],
        scratch_shapes=[pltpu.VMEM((Q_GROUPS, Q_EXT_ROWS, LANES), F32),
                        pltpu.VMEM((U_GROUPS, U_EXT_ROWS, LANES), F32),
                        pltpu.VMEM((Q_GROUPS, CONV_OUT_ROWS, LANES), F32),
                        pltpu.VMEM((U_GROUPS, CONV_OUT_ROWS, LANES), F32)],
        compiler_params=_cparams(("parallel",)),
        name="conv",
    )(qkv_raw, qkv_raw, qkv_raw, ug, ug, ug, ba, cw, dw, dwb, lng, lnb, alog, dtb)


INV_BASE = 8


def _b16(xs):
    return [x.astype(BF16) for x in xs]


def _tri_inverse_minus_eye(nmats, rows, cols):
    assert INV_BASE == 8
    c = nmats[0].shape[0]
    shift = int(math.log2(INV_BASE))
    same = (rows >> shift) == (cols >> shift)
    n1 = [jnp.where(same, n, 0.0) for n in nmats]
    n1b = _b16(n1)
    n2 = [_dot(x, x) for x in n1b]
    n2b = _b16(n2)
    r = [_dot(jnp.concatenate([a, b], axis=0), b) for a, b in zip(n1b, n2b)]
    q = [a + b + x[:c] for a, b, x in zip(n1, n2, r)]
    n4 = [x[c:] for x in r]
    qn4 = [_dot(a, b) for a, b in zip(_b16(q), _b16(n4))]
    q = [a + b + x for a, b, x in zip(q, n4, qn4)]
    while (1 << shift) < c:
        off = ((rows >> (shift + 1)) == (cols >> (shift + 1))) & ((rows >> shift) != (cols >> shift))
        a = [jnp.where(off, -n, 0.0) for n in nmats]
        x = [ai + _dot(ab, qb) for ai, ab, qb in zip(a, _b16(a), _b16(q))]
        qx = [_dot(qb, xb) for qb, xb in zip(_b16(q), _b16(x))]
        q = [qi - xi - qxi for qi, xi, qxi in zip(q, x, qx)]
        shift += 1
    return q


def _delta_units(units):
    c = CHUNK
    rows = lax.broadcasted_iota(I32, (c, c), 0)
    cols = lax.broadcasted_iota(I32, (c, c), 1)
    pre = []
    for d, hd, qkv_ref, r0, gate, gcol, grow, s in units:
        incl = (rows >= cols) if d == 0 else (rows <= cols)
        lane = 2 * N_HEADS + N_HEADS * d + hd
        beta = gate[:, N_HEADS * d + hd:N_HEADS * d + hd + 1]
        gc = gcol[:, lane:lane + 1]
        gr = grow[lane:lane + 1, :]
        last = c - 1 if d == 0 else 0
        gtot = gcol[last:last + 1, lane:lane + 1]
        dec = jnp.where(incl, jnp.exp(jnp.where(incl, gc - gr, 0.0)), 0.0)
        q = qkv_ref[pl.ds(r0, c), hd * HEAD_DIM:(hd + 1) * HEAD_DIM]
        k = qkv_ref[pl.ds(r0, c), MIX_A + hd * HEAD_DIM:MIX_A + (hd + 1) * HEAD_DIM]
        v = qkv_ref[pl.ds(r0, c), 2 * MIX_A + hd * HEAD_DIM:2 * MIX_A + (hd + 1) * HEAD_DIM]
        qs = q * (HEAD_DIM ** -0.5)
        kbeta = k * beta
        egc = jnp.exp(gc)
        pre.append(dict(d=d, beta=beta, gc=gc, gtot=gtot, dec=dec, k=k, v=v, qs=qs, kbeta=kbeta, egc=egc, s=s))

    ak = [_dot_nt(jnp.concatenate([p['kbeta'], p['qs']], axis=0).astype(BF16), p['k'].astype(BF16))
          for p in pre]
    nmats, qks = [], []
    for p, a in zip(pre, ak):
        strict = (rows > cols) if p['d'] == 0 else (rows < cols)
        nmats.append(jnp.where(strict, -a[:c] * p['dec'], 0.0))
        qks.append(a[c:] * p['dec'])
    qinv = _tri_inverse_minus_eye(nmats, rows, cols)

    rhs = [jnp.concatenate([p['v'] * p['beta'], p['kbeta'] * p['egc']], axis=1) for p in pre]
    uw = [r + _dot(qi, rb) for r, qi, rb in zip(rhs, _b16(qinv), _b16(rhs))]
    sw = [_dot(jnp.concatenate([x[:, HEAD_DIM:], p['qs'] * p['egc']], axis=0).astype(BF16), p['s'].astype(BF16))
          for x, p in zip(uw, pre)]
    vnb = [(x[:, :HEAD_DIM] - y[:c]).astype(BF16) for x, y in zip(uw, sw)]
    o = [y[c:] + _dot(qk, vb) for y, qk, vb in zip(sw, _b16(qks), vnb)]
    kdec = [(p['k'] * jnp.exp(p['gtot'] - p['gc'])).astype(BF16) for p in pre]
    s_new = [p['s'] * jnp.exp(p['gtot']) + _dot_tn(kd, vb) for p, kd, vb in zip(pre, kdec, vnb)]
    return list(zip(o, s_new))


def _delta_kernel(has_s0, emit_final, n_chunk, *refs):
    refs = list(refs)
    qkvf_ref, qkvb_ref, gatef_ref, gateb_ref, gtf_ref, gtb_ref = refs[:6]
    refs = refs[6:]
    s0_ref = refs.pop(0) if has_s0 else None
    of_ref, ob_ref = refs[:2]
    refs = refs[2:]
    sfin_ref = refs.pop(0) if emit_final else None
    s_scr = refs[0]

    j = pl.program_id(1)

    @pl.when(j == 0)
    def _():
        if has_s0:
            s_scr[...] = s0_ref[0]
        else:
            s_scr[...] = jnp.zeros(s_scr.shape, F32)

    c = CHUNK
    rows = lax.broadcasted_iota(I32, (c, c), 0)
    cols = lax.broadcasted_iota(I32, (c, c), 1)
    tri_lo = (rows >= cols).astype(F32)
    tri_up = (rows <= cols).astype(F32)

    def body(ci, carry):
        units = []
        for d in range(2):
            cidx = ci if d == 0 else n_chunk - 1 - ci
            r0 = pl.multiple_of(cidx * c, c)
            qkv_ref = qkvf_ref if d == 0 else qkvb_ref
            gate = (gatef_ref if d == 0 else gateb_ref)[pl.ds(r0, c), :]
            gt = (gtf_ref if d == 0 else gtb_ref)[cidx]
            tri = tri_lo if d == 0 else tri_up
            tri_t = tri_up if d == 0 else tri_lo
            gcol = jnp.dot(tri, gate, preferred_element_type=F32, precision=HIGHEST)
            grow = jnp.dot(gt, tri_t, preferred_element_type=F32, precision=HIGHEST)
            for hd in range(N_HEADS):
                units.append((d, hd, qkv_ref, r0, gate, gcol, grow, s_scr[N_HEADS * d + hd]))
        for (d, hd, _, r0, _, _, _, _), (o, s_new) in zip(units, _delta_units(units)):
            s_scr[N_HEADS * d + hd] = s_new
            (of_ref if d == 0 else ob_ref)[pl.ds(r0, c), hd * HEAD_DIM:(hd + 1) * HEAD_DIM] = o
        return carry

    lax.fori_loop(0, n_chunk, body, 0)

    if emit_final:
        @pl.when(j == pl.num_programs(1) - 1)
        def _():
            sfin_ref[0] = s_scr[...]


def _delta(qkv, gate, gate_t, n_seq, seq_len, row0, blk, s0):
    t = qkv.shape[0]
    nblk = seq_len // blk
    n_chunk = blk // CHUNK
    b0 = row0 // blk
    has_s0 = s0 is not None
    emit_final = not has_s0
    n_state = 2 * N_HEADS

    def fwd(b, j):
        return b0 + b * nblk + j

    def bwd(b, j):
        return b0 + b * nblk + (nblk - 1 - j)

    in_specs = [
        pl.BlockSpec((blk, 3 * MIX_A), lambda b, j: (fwd(b, j), 0)),
        pl.BlockSpec((blk, 3 * MIX_A), lambda b, j: (bwd(b, j), 0)),
        pl.BlockSpec((blk, LANES), lambda b, j: (fwd(b, j), 0)),
        pl.BlockSpec((blk, LANES), lambda b, j: (bwd(b, j), 0)),
        pl.BlockSpec((n_chunk, 16, CHUNK), lambda b, j: (fwd(b, j), 0, 0)),
        pl.BlockSpec((n_chunk, 16, CHUNK), lambda b, j: (bwd(b, j), 0, 0)),
    ]
    args = [qkv, qkv, gate, gate, gate_t, gate_t]
    if has_s0:
        in_specs.append(pl.BlockSpec((1, n_state, HEAD_DIM, HEAD_DIM), lambda b, j: (b, 0, 0, 0)))
        args.append(s0)
    n_rows = n_seq * seq_len
    out_specs = [pl.BlockSpec((blk, MIX_A), lambda b, j: (b * nblk + j, 0)),
                 pl.BlockSpec((blk, MIX_A), lambda b, j: (b * nblk + (nblk - 1 - j), 0))]
    out_shape = [jax.ShapeDtypeStruct((n_rows, MIX_A), F32), jax.ShapeDtypeStruct((n_rows, MIX_A), F32)]
    if emit_final:
        out_specs.append(pl.BlockSpec((1, n_state, HEAD_DIM, HEAD_DIM), lambda b, j: (b, 0, 0, 0)))
        out_shape.append(jax.ShapeDtypeStruct((n_seq, n_state, HEAD_DIM, HEAD_DIM), F32))
    del t
    return pl.pallas_call(
        functools.partial(_delta_kernel, has_s0, emit_final, n_chunk),
        grid=(n_seq, nblk),
        in_specs=in_specs,
        out_specs=out_specs,
        out_shape=out_shape,
        scratch_shapes=[pltpu.VMEM((n_state, HEAD_DIM, HEAD_DIM), F32)],
        compiler_params=_cparams(("parallel", "arbitrary")),
        name="delta_latent" if has_s0 else "delta_prompt",
    )(*args)


def _outproj_kernel(lay, xp_ref, xs_ref, pos_ref, mod_ref, opf_ref, opb_ref, osf_ref, osb_ref, z_ref, u_ref,
                    og_ref, n2_ref, wo_ref, wr_ref, br_ref, x1_ref, h2_ref, ids_ref, wts_ref, cnt_ref, cnt_scr):
    i = pl.program_id(0)
    x0 = _load_x(lay, i, xp_ref, xs_ref, pos_ref)
    m = mod_ref[0]
    o = jnp.where(lay.is_sample(i), osf_ref[...] + osb_ref[...], opf_ref[...] + opb_ref[...])
    z = z_ref[...]
    mix = _dot(u_ref[...], wo_ref[MIX_A:, :])
    for hd in range(N_HEADS):
        sl = slice(hd * HEAD_DIM, (hd + 1) * HEAD_DIM)
        oh = _rms(o[:, sl]) * og_ref[...] * _silu(z[:, sl])
        mix = mix + _dot(oh.astype(BF16), wo_ref[sl, :])
    x1 = x0 + m[2:3] * mix
    x1_ref[...] = x1
    h2 = _rms(x1) * n2_ref[...] * (1.0 + m[4:5]) + m[3:4]
    h2b = h2.astype(BF16)
    half = D_MODEL // 2
    hi = pltpu.bitcast(h2b[:, :half].astype(F32), jnp.uint32)
    lo = pltpu.bitcast(h2b[:, half:].astype(F32), jnp.uint32)
    h2_ref[...] = pltpu.bitcast((hi & jnp.uint32(0xFFFF0000)) | (lo >> 16), I32)

    logits = _dot(h2b, wr_ref[...]) + br_ref[...]
    lane = lax.broadcasted_iota(I32, logits.shape, 1)
    neg = jnp.float32(-jnp.inf)
    big = jnp.int32(LANES)
    gl = jnp.where(lane < N_GROUPS, logits, neg)
    gmax = jnp.max(gl, axis=-1, keepdims=True)
    grp = jnp.min(jnp.where(gl == gmax, lane, big), axis=-1, keepdims=True)
    p_grp = 1.0 / jnp.sum(jnp.where(lane < N_GROUPS, jnp.exp(gl - gmax), 0.0), axis=-1, keepdims=True)
    e_lane = lane - N_GROUPS
    in_grp = (e_lane >= grp * EXPERTS_PER_GROUP) & (e_lane < (grp + 1) * EXPERTS_PER_GROUP)
    el = jnp.where(in_grp, logits, neg)
    m1 = jnp.max(el, axis=-1, keepdims=True)
    i1 = jnp.min(jnp.where(el == m1, lane, big), axis=-1, keepdims=True)
    el2 = jnp.where(lane == i1, neg, el)
    m2 = jnp.max(el2, axis=-1, keepdims=True)
    i2 = jnp.min(jnp.where(el2 == m2, lane, big), axis=-1, keepdims=True)
    e2 = jnp.exp(m2 - m1)
    w1 = p_grp / (1.0 + e2)
    w2 = p_grp * e2 / (1.0 + e2)
    wts_ref[...] = jnp.where(lane == 0, w1, jnp.where(lane == 1, w2, 0.0))

    @pl.when(i == 0)
    def _():
        cnt_scr[...] = jnp.zeros(cnt_scr.shape, F32)

    oh1 = lane == i1
    oh2 = lane == i2
    oh = jnp.where(oh1, 1.0, jnp.where(oh2, 1.0, 0.0))
    tm = logits.shape[0]
    rows = lax.broadcasted_iota(I32, (tm, tm), 0)
    cols = lax.broadcasted_iota(I32, (tm, tm), 1)
    before = jnp.where(rows > cols, 1.0, 0.0).astype(BF16)
    seen = cnt_scr[...] + _dot(before, oh.astype(BF16))
    r1 = jnp.sum(jnp.where(oh1, seen, 0.0), axis=-1, keepdims=True).astype(I32)
    r2 = jnp.sum(jnp.where(oh2, seen, 0.0), axis=-1, keepdims=True).astype(I32)
    cnt_scr[...] = cnt_scr[...] + jnp.sum(oh, axis=0, keepdims=True)
    cnt_ref[...] = cnt_scr[...]
    ids_ref[...] = jnp.where(lane == 0, i1 - N_GROUPS,
                             jnp.where(lane == 1, i2 - N_GROUPS,
                                       jnp.where(lane == 2, r1, jnp.where(lane == 3, r2, 0))))


def _outproj(lay, xp, xs, pos, mod, o_pf, o_pb, o_sf, o_sb, z, u, onorm_g, norm2_g, w_out, w_route, b_route):
    t = lay.t
    p_spec = pl.BlockSpec((TOK_TILE, MIX_A), lambda i: (lay.xp_blk(i), 0))
    s_spec = pl.BlockSpec((TOK_TILE, MIX_A), lambda i: (lay.xs_blk(i), 0))
    return pl.pallas_call(
        functools.partial(_outproj_kernel, lay),
        grid=(lay.tiles,),
        in_specs=_x_specs(lay) + [_mod_spec(lay), p_spec, p_spec, s_spec, s_spec, _row_spec(MIX_A),
                                  _row_spec(MIX_B), _full_spec((1, HEAD_DIM)), _full_spec((1, D_MODEL)),
                                  _full_spec(w_out.shape), _full_spec(w_route.shape),
                                  _full_spec(b_route.shape)],
        out_specs=[_row_spec(D_MODEL), _row_spec(D_MODEL // 2), _row_spec(LANES), _row_spec(LANES),
                   _full_spec((1, LANES))],
        out_shape=[jax.ShapeDtypeStruct((t, D_MODEL), F32),
                   jax.ShapeDtypeStruct((t, D_MODEL // 2), I32),
                   jax.ShapeDtypeStruct((t, LANES), I32),
                   jax.ShapeDtypeStruct((t, LANES), F32),
                   jax.ShapeDtypeStruct((1, LANES), F32)],
        scratch_shapes=[pltpu.VMEM((1, LANES), F32)],
        compiler_params=_cparams(("arbitrary",)),
        name="outproj",
    )(xp, xs, pos, mod, o_pf, o_pb, o_sf, o_sb, z, u, onorm_g, norm2_g, w_out, w_route, b_route)


def _sc_gather(table, idx, chunk, name):
    n_rows, width = idx.shape[0], table.shape[1]
    mesh = plsc.VectorSubcoreMesh(core_axis_name="c", subcore_axis_name="s")
    n_workers = mesh.num_cores * mesh.num_subcores
    per_worker = n_rows // n_workers
    n_chunks = per_worker // chunk
    assert n_rows == n_workers * n_chunks * chunk and n_chunks % 2 == 0 and chunk % 8 == 0 and chunk <= LANES

    def body(table_hbm, idx_hbm, out_hbm, idx_v, rows_v, sem):
        base = (lax.axis_index("s") * mesh.num_cores + lax.axis_index("c")) * per_worker

        def gather(slot):
            return pltpu.make_async_copy(table_hbm.at[idx_v.at[slot]], rows_v.at[slot], sem.at[slot])

        def fetch(g, slot):
            off = pl.multiple_of(base + g * chunk, 8)
            pltpu.sync_copy(idx_hbm.at[pl.ds(off, chunk)], idx_v.at[slot])
            gather(slot).start()

        for slot in range(2):
            fetch(slot, slot)

        @pl.loop(0, n_chunks, step=2)
        def _(g):
            for slot in range(2):
                off = pl.multiple_of(base + (g + slot) * chunk, 8)
                gather(slot).wait()
                pltpu.sync_copy(rows_v.at[slot], out_hbm.at[pl.ds(off, chunk)])

                @pl.when(g + slot + 2 < n_chunks)
                def _():
                    fetch(g + slot + 2, slot)

    return pl.kernel(
        body,
        out_type=jax.ShapeDtypeStruct((n_rows, width), table.dtype),
        mesh=mesh,
        scratch_types=[pltpu.VMEM((2, chunk), I32), pltpu.VMEM((2, chunk, width), table.dtype),
                       pltpu.SemaphoreType.DMA((2,))],
        name=name,
    )(table, idx)


def _sc_scatter2(src, idx0, idx1, n_out, chunk, name):
    n_rows, width = src.shape
    mesh = plsc.VectorSubcoreMesh(core_axis_name="c", subcore_axis_name="s")
    n_workers = mesh.num_cores * mesh.num_subcores
    per_worker = n_rows // n_workers
    n_chunks = per_worker // chunk
    assert n_rows == n_workers * n_chunks * chunk and n_chunks % 2 == 0 and chunk % 8 == 0 and chunk <= LANES

    def body(src_hbm, i0_hbm, i1_hbm, out_hbm, i0_v, i1_v, rows_v, sem_in, sem_out):
        base = (lax.axis_index("s") * mesh.num_cores + lax.axis_index("c")) * per_worker

        def rows_in(g, slot):
            off = pl.multiple_of(base + g * chunk, 8)
            return pltpu.make_async_copy(src_hbm.at[pl.ds(off, chunk)], rows_v.at[slot], sem_in.at[slot])

        def fetch(g, slot):
            off = pl.multiple_of(base + g * chunk, 8)
            pltpu.sync_copy(i0_hbm.at[pl.ds(off, chunk)], i0_v.at[slot])
            pltpu.sync_copy(i1_hbm.at[pl.ds(off, chunk)], i1_v.at[slot])
            rows_in(g, slot).start()

        for slot in range(2):
            fetch(slot, slot)

        @pl.loop(0, n_chunks, step=2)
        def _(g):
            for slot in range(2):
                rows_in(g + slot, slot).wait()
                puts = [pltpu.make_async_copy(rows_v.at[slot], out_hbm.at[iv.at[slot]], sem_out.at[slot])
                        for iv in (i0_v, i1_v)]
                for put in puts:
                    put.start()
                for put in puts:
                    put.wait()

                @pl.when(g + slot + 2 < n_chunks)
                def _():
                    fetch(g + slot + 2, slot)

    return pl.kernel(
        body,
        out_type=jax.ShapeDtypeStruct((n_out, width), src.dtype),
        mesh=mesh,
        scratch_types=[pltpu.VMEM((2, chunk), I32), pltpu.VMEM((2, chunk), I32),
                       pltpu.VMEM((2, chunk, width), src.dtype),
                       pltpu.SemaphoreType.DMA((2,)), pltpu.SemaphoreType.DMA((2,))],
        name=name,
    )(src, idx0, idx1)


def _expert_kernel(blk_e_ref, nused_ref, xb_ref, wg_ref, wu_ref, wd_ref, yb_ref, wg_s, wu_s, wd_s):
    i = pl.program_id(0)
    nused = nused_ref[0]

    @pl.when(i < nused)
    def _():
        changed = (i == 0) | (blk_e_ref[i] != blk_e_ref[jnp.maximum(i - 1, 0)])

        @pl.when(changed)
        def _():
            wg_s[...] = wg_ref[0].astype(BF16)
            wu_s[...] = wu_ref[0].astype(BF16)
            wd_s[...] = wd_ref[0].astype(BF16)

        x = pltpu.bitcast(xb_ref[...], jnp.uint32)
        half = D_MODEL // 2
        xa = pltpu.bitcast(x & jnp.uint32(0xFFFF0000), F32).astype(BF16)
        xb = pltpu.bitcast(x << 16, F32).astype(BF16)
        g = _dot(xa, wg_s[:half, :]) + _dot(xb, wg_s[half:, :])
        u = _dot(xa, wu_s[:half, :]) + _dot(xb, wu_s[half:, :])
        hmid = (_silu(g) * u).astype(BF16)
        yb_ref[...] = _dot(hmid, wd_s[...])

    @pl.when(i >= nused)
    def _():
        yb_ref[...] = jnp.zeros(yb_ref.shape, F32)


def _experts(xb, blk_e, nused, w_gate, w_up, w_down):
    nb = blk_e.shape[0]
    grid_spec = pltpu.PrefetchScalarGridSpec(
        num_scalar_prefetch=2,
        grid=(nb,),
        in_specs=[
            pl.BlockSpec((ROUTE_BM, D_MODEL // 2), lambda i, be, nu: (i, 0)),
            pl.BlockSpec((1, D_MODEL, D_EXPERT), lambda i, be, nu: (be[i], 0, 0)),
            pl.BlockSpec((1, D_MODEL, D_EXPERT), lambda i, be, nu: (be[i], 0, 0)),
            pl.BlockSpec((1, D_EXPERT, D_MODEL), lambda i, be, nu: (be[i], 0, 0)),
        ],
        out_specs=pl.BlockSpec((ROUTE_BM, D_MODEL), lambda i, be, nu: (i, 0)),
        scratch_shapes=[pltpu.VMEM((D_MODEL, D_EXPERT), BF16),
                        pltpu.VMEM((D_MODEL, D_EXPERT), BF16),
                        pltpu.VMEM((D_EXPERT, D_MODEL), BF16)],
    )
    return pl.pallas_call(
        _expert_kernel,
        grid_spec=grid_spec,
        out_shape=jax.ShapeDtypeStruct((nb * ROUTE_BM, D_MODEL), F32),
        compiler_params=_cparams(("arbitrary",)),
        name="expert",
    )(blk_e, nused, xb, w_gate, w_up, w_down)


def _combine_kernel(lay, y0_ref, y1_ref, x1_ref, wts_ref, mod_ref, fg_ref, yp_ref, ys_ref):
    i = pl.program_id(0)
    m = mod_ref[0]
    w = wts_ref[...]
    ff = y0_ref[...] * w[:, 0:1] + y1_ref[...] * w[:, 1:2]
    y = _rms(x1_ref[...] + m[5:6] * ff) * fg_ref[...]

    @pl.when(i < lay.tiles_p)
    def _():
        yp_ref[...] = y

    @pl.when(i >= lay.tiles_p)
    def _():
        ys_ref[...] = y


def _combine(lay, yg, x1, wts, mod, final_g):
    return pl.pallas_call(
        functools.partial(_combine_kernel, lay),
        grid=(lay.tiles,),
        in_specs=[_row_spec(D_MODEL), pl.BlockSpec((TOK_TILE, D_MODEL), lambda i: (i + lay.tiles, 0)),
                  _row_spec(D_MODEL), _row_spec(LANES), _mod_spec(lay), _full_spec((1, D_MODEL))],
        out_specs=[pl.BlockSpec((TOK_TILE, D_MODEL), lambda i: (lay.xp_blk(i), 0)),
                   pl.BlockSpec((TOK_TILE, D_MODEL), lambda i: (lay.xs_blk(i), 0))],
        out_shape=[jax.ShapeDtypeStruct((lay.t_p, D_MODEL), F32),
                   jax.ShapeDtypeStruct((lay.t_s, D_MODEL), F32)],
        compiler_params=_cparams(("arbitrary",)),
        name="combine",
    )(yg, yg, x1, wts, mod, final_g)


def _dispatch_plan(ids, counts):
    n_tok = ids.shape[0]
    padded = (counts + ROUTE_BM - 1) // ROUTE_BM * ROUTE_BM
    pad_end = jnp.cumsum(padded)
    pad_start = pad_end - padded
    expert = ids[:, 0:2]
    start = jnp.sum(jnp.where(expert[:, :, None] == jnp.arange(N_EXPERTS, dtype=I32), pad_start, 0), axis=-1)
    dest = (start + ids[:, 2:4]).astype(I32)
    nb = -(-(2 * n_tok + N_EXPERTS * (ROUTE_BM - 1)) // ROUTE_BM)
    block_start = jnp.arange(nb, dtype=I32) * ROUTE_BM
    blk_e = jnp.minimum(jnp.sum(pad_end[None, :] <= block_start[:, None], axis=1), N_EXPERTS - 1).astype(I32)
    nused = (pad_end[-1:] // ROUTE_BM).astype(I32)
    return dest[:, 0], dest[:, 1], blk_e, nused


def _grid_pos_embed(n_tokens):
    rows = n_tokens // GRID_W
    r = jnp.repeat(jnp.arange(rows, dtype=F32), GRID_W)
    col = jnp.tile(jnp.arange(GRID_W, dtype=F32), rows)
    n_freq = D_MODEL // 4
    freq = jnp.exp(jnp.arange(n_freq, dtype=F32) * (-math.log(10000.0) / n_freq))

    def enc(p):
        ang = p[:, None] * freq[None, :]
        return jnp.concatenate([jnp.sin(ang), jnp.cos(ang)], axis=-1)

    return jnp.concatenate([enc(r), enc(col)], axis=-1)


def _lane_pad(v, offset):
    return jnp.zeros((1, LANES), F32).at[0, offset:offset + v.shape[0]].set(v.astype(F32))


def kernel(x_prompt, x_sample, state_delta, c, c_ctx, norm1_g, w_mod, b_mod, w_in, conv_qkv_w, A_log, dt_bias, onorm_g, dw_w, dw_b, cln_g, cln_b, w_out, norm2_g, w_group, b_group, w_expert, b_expert, w_e_gate, w_e_up, w_e_down, final_g):
    n_p, l_p, _ = x_prompt.shape
    n_s, l_s, _ = x_sample.shape
    lay = _Layout(n_p, l_p, n_s, l_s)
    depth = w_in.shape[0]
    assert depth == 1
    xp = x_prompt.reshape(lay.t_p, D_MODEL)
    xs = x_sample.reshape(lay.t_s, D_MODEL)
    pos = _grid_pos_embed(l_s)

    cond = jnp.concatenate([c_ctx[None, :], c], axis=0)
    cond = jnp.pad(cond, ((0, (-cond.shape[0]) % 8), (0, 0)))
    mod = _modulation(cond, w_mod[0], b_mod[0]).reshape(cond.shape[0], 6, D_MODEL)

    n_gate = 4 * N_HEADS
    wi = w_in[0]
    w_main = jnp.concatenate([wi[:, :4 * MIX_A], wi[:, 4 * MIX_A + n_gate:]], axis=1).astype(BF16)
    w_small = jnp.pad(wi[:, 4 * MIX_A:4 * MIX_A + n_gate], ((0, 0), (0, LANES - n_gate))).astype(BF16)
    qkv_raw, z, ug, ba = _inproj(lay, xp, xs, pos, mod, norm1_g[0][None, :], w_main, w_small)

    cw = jnp.pad(conv_qkv_w[0], ((0, 8 - SHORT_CONV), (0, 0))).reshape(8, Q_GROUPS, LANES).transpose(1, 0, 2)
    dw = jnp.pad(dw_w[0], ((0, 32 - CONV_W), (0, 0))).reshape(32, U_GROUPS, LANES).transpose(1, 0, 2)
    alog = _lane_pad(A_log[0].reshape(-1), 2 * N_HEADS)
    dtb = _lane_pad(dt_bias[0].reshape(-1), 2 * N_HEADS)
    qkv, u_conf, gate = _conv(lay, qkv_raw, ug, ba, cw, dw, dw_b[0][None, :], cln_g[0][None, :],
                              cln_b[0][None, :], alog, dtb)

    gate_t = gate[:, :16].reshape(lay.t // CHUNK, CHUNK, 16).transpose(0, 2, 1)
    o_pf, o_pb, s_fin = _delta(qkv, gate, gate_t, n_p, l_p, 0, l_p, None)
    s0 = state_delta[:, 0].reshape(n_s, 2 * N_HEADS, HEAD_DIM, HEAD_DIM)
    o_sf, o_sb = _delta(qkv, gate, gate_t, n_s, l_s, lay.t_p, DELTA_BLK, s0)

    w_route = jnp.pad(jnp.concatenate([w_group[0], w_expert[0]], axis=1),
                      ((0, 0), (0, LANES - N_GROUPS - N_EXPERTS))).astype(BF16)
    b_route = _lane_pad(jnp.concatenate([b_group[0], b_expert[0]]), 0)
    x1, h2p, ids, wts, cnt = _outproj(lay, xp, xs, pos, mod, o_pf, o_pb, o_sf, o_sb, z, u_conf, onorm_g[0][None, :],
                                 norm2_g[0][None, :], w_out[0].astype(BF16), w_route, b_route)

    counts = cnt[0, N_GROUPS:N_GROUPS + N_EXPERTS].astype(I32)
    dest0, dest1, blk_e, nused = _dispatch_plan(ids, counts)
    xb = _sc_scatter2(h2p, dest0, dest1, blk_e.shape[0] * ROUTE_BM, DISPATCH_CHUNK, "dispatch_scatter")
    yb = _experts(xb, blk_e, nused, w_e_gate[0], w_e_up[0], w_e_down[0])
    yg = _sc_gather(yb, jnp.concatenate([dest0, dest1]), COMBINE_CHUNK, "combine_gather")
    y_p, y_s = _combine(lay, yg, x1, wts, mod, final_g[None, :])

    new_state = s_fin.reshape(n_p, 1, 2, N_HEADS, HEAD_DIM, HEAD_DIM)
    return (y_p.reshape(x_prompt.shape), y_s.reshape(x_sample.shape), new_state)
```

```python
import functools
import math

import jax
import jax.numpy as jnp
from jax import lax
from jax.experimental import pallas as pl
from jax.experimental.pallas import tpu as pltpu
from jax.experimental.pallas import tpu_sc as plsc

F32 = jnp.float32
BF16 = jnp.bfloat16
I32 = jnp.int32

D_MODEL = 1024
MIX_A = 512
MIX_B = 512
HEAD_DIM = 128
N_HEADS = 4
SHORT_CONV = 5
CONV_W = 31
N_GROUPS = 4
EXPERTS_PER_GROUP = 8
N_EXPERTS = 32
D_EXPERT = 512
GRID_W = 64
EPS = 1e-6

LANES = 128
TOK_TILE = 256
CHUNK = 64
DELTA_BLK = 512
DELTA_SEQS = 2
ROUTE_BM = 256
DISPATCH_CHUNK = 64
COMBINE_CHUNK = 32
Q_HALO = 8
U_HALO = 16
VMEM_LIMIT = 56 * 1024 * 1024

HIGHEST = lax.Precision.HIGHEST


def _cparams(sem):
    return pltpu.CompilerParams(dimension_semantics=sem, vmem_limit_bytes=VMEM_LIMIT)


def _sigmoid(x):
    return 1.0 / (1.0 + jnp.exp(-x))


def _silu(x):
    return x * _sigmoid(x)


def _softplus(x):
    return jnp.maximum(x, 0.0) + jnp.log1p(jnp.exp(-jnp.abs(x)))


def _rms(x):
    return x * lax.rsqrt(jnp.mean(x * x, axis=-1, keepdims=True) + EPS)


def _dot(a, b):
    return jnp.dot(a, b, preferred_element_type=F32)


def _dot_nt(a, b):
    return lax.dot_general(a, b, (((1,), (1,)), ((), ())), preferred_element_type=F32)


def _dot_tn(a, b):
    return lax.dot_general(a, b, (((0,), (0,)), ((), ())), preferred_element_type=F32)


def _mod_kernel(cond_ref, w_ref, b_ref, o_ref):
    s = _silu(cond_ref[...])
    o_ref[...] = jnp.dot(s, w_ref[...], preferred_element_type=F32, precision=HIGHEST) + b_ref[...]


def _modulation(cond, w_mod, b_mod):
    n = cond.shape[0]
    tn = D_MODEL
    return pl.pallas_call(
        _mod_kernel,
        grid=(6 * D_MODEL // tn,),
        in_specs=[pl.BlockSpec((n, D_MODEL), lambda j: (0, 0)),
                  pl.BlockSpec((D_MODEL, tn), lambda j: (0, j)),
                  pl.BlockSpec((1, tn), lambda j: (0, j))],
        out_specs=pl.BlockSpec((n, tn), lambda j: (0, j)),
        out_shape=jax.ShapeDtypeStruct((n, 6 * D_MODEL), F32),
        compiler_params=_cparams(("parallel",)),
        name="mod",
    )(cond, w_mod, b_mod.reshape(1, -1))


class _Layout:
    def __init__(self, n_p, l_p, n_s, l_s):
        self.n_p, self.l_p, self.n_s, self.l_s = n_p, l_p, n_s, l_s
        self.t_p = n_p * l_p
        self.t_s = n_s * l_s
        self.t = self.t_p + self.t_s
        assert l_p % TOK_TILE == 0 and l_s % TOK_TILE == 0 and l_s % DELTA_BLK == 0
        self.tiles_p = self.t_p // TOK_TILE
        self.tiles_s = self.t_s // TOK_TILE
        self.tiles = self.tiles_p + self.tiles_s
        self.tps_p = l_p // TOK_TILE
        self.tps_s = l_s // TOK_TILE

    def is_sample(self, i):
        return i >= self.tiles_p

    def mod_row(self, i):
        return jnp.where(i < self.tiles_p, 0, 1 + (i - self.tiles_p) // self.tps_s)

    def pos_blk(self, i):
        return jnp.where(i < self.tiles_p, 0, (i - self.tiles_p) % self.tps_s)

    def xp_blk(self, i):
        return jnp.minimum(i, self.tiles_p - 1)

    def xs_blk(self, i):
        return jnp.maximum(i - self.tiles_p, 0)

    def seq_pos(self, i):
        in_s = i >= self.tiles_p
        pos = jnp.where(in_s, (i - self.tiles_p) % self.tps_s, i % self.tps_p)
        n = jnp.where(in_s, self.tps_s, self.tps_p)
        return pos, n


def _load_x(lay, i, xp_ref, xs_ref, pos_ref):
    return jnp.where(lay.is_sample(i), xs_ref[...] + pos_ref[...], xp_ref[...])


def _x_specs(lay):
    return [pl.BlockSpec((TOK_TILE, D_MODEL), lambda i: (lay.xp_blk(i), 0)),
            pl.BlockSpec((TOK_TILE, D_MODEL), lambda i: (lay.xs_blk(i), 0)),
            pl.BlockSpec((TOK_TILE, D_MODEL), lambda i: (lay.pos_blk(i), 0))]


def _mod_spec(lay):
    return pl.BlockSpec((1, 6, D_MODEL), lambda i: (lay.mod_row(i), 0, 0))


def _row_spec(width):
    return pl.BlockSpec((TOK_TILE, width), lambda i: (i, 0))


def _full_spec(shape):
    nd = len(shape)
    return pl.BlockSpec(shape, lambda i: (0,) * nd)


def _inproj_kernel(lay, xp_ref, xs_ref, pos_ref, mod_ref, g_ref, wm_ref, ws_ref,
                   qkv_ref, z_ref, ug_ref, ba_ref):
    i = pl.program_id(0)
    x = _load_x(lay, i, xp_ref, xs_ref, pos_ref)
    m = mod_ref[0]
    h = _rms(x) * g_ref[...] * (1.0 + m[1:2]) + m[0:1]
    hb = h.astype(BF16)
    qkv_ref[...] = _dot(hb, wm_ref[:, 0:3 * MIX_A])
    z_ref[...] = _dot(hb, wm_ref[:, 3 * MIX_A:4 * MIX_A])
    glu = _dot(hb, wm_ref[:, 4 * MIX_A:4 * MIX_A + 2 * MIX_B])
    ug_ref[...] = glu[:, :MIX_B] * _sigmoid(glu[:, MIX_B:])
    ba_ref[...] = _dot(hb, ws_ref[...])


def _inproj(lay, xp, xs, pos, mod, norm_g, w_main, w_small):
    t = lay.t
    return pl.pallas_call(
        functools.partial(_inproj_kernel, lay),
        grid=(lay.tiles,),
        in_specs=_x_specs(lay) + [_mod_spec(lay), _full_spec((1, D_MODEL)),
                                  _full_spec(w_main.shape), _full_spec(w_small.shape)],
        out_specs=[_row_spec(3 * MIX_A), _row_spec(MIX_A), _row_spec(MIX_B), _row_spec(LANES)],
        out_shape=[jax.ShapeDtypeStruct((t, 3 * MIX_A), F32),
                   jax.ShapeDtypeStruct((t, MIX_A), F32),
                   jax.ShapeDtypeStruct((t, MIX_B), F32),
                   jax.ShapeDtypeStruct((t, LANES), F32)],
        compiler_params=_cparams(("parallel",)),
        name="inproj",
    )(xp, xs, pos, mod, norm_g, w_main, w_small)


SUBLANES = 8
CONV_PITCH = 33
CONV_OUT_ROWS = SUBLANES * CONV_PITCH
POST_ROWS = 64
Q_EXT_ROWS = 280
U_EXT_ROWS = 296
Q_GROUPS = 3 * MIX_A // LANES
U_GROUPS = MIX_B // LANES


def _strided_conv(ext_scr, res_scr, w_ref, g, off, n_taps, j_block):
    for j0 in range(0, CONV_PITCH, j_block):
        js = range(j0, min(j0 + j_block, CONV_PITCH))
        v = {m: ext_scr[g, pl.ds(off + m, SUBLANES, stride=CONV_PITCH), :]
             for m in range(js[0], js[-1] + n_taps)}
        for j in js:
            acc = v[j] * w_ref[g, 0:1, :]
            for s in range(1, n_taps):
                acc = acc + v[j + s] * w_ref[g, s:s + 1, :]
            res_scr[g, pl.ds(j, SUBLANES, stride=CONV_PITCH), :] = acc


def _conv_kernel(lay, qc_ref, qp_ref, qn_ref, uc_ref, up_ref, un_ref, ba_ref, cw_ref, dw_ref,
                 dwb_ref, lng_ref, lnb_ref, alog_ref, dtb_ref,
                 qkv_ref, uo_ref, gate_ref, eq_scr, eu_scr, rq_scr, ru_scr):
    i = pl.program_id(0)
    pos, n = lay.seq_pos(i)
    has_prev = pos != 0
    has_next = pos != n - 1
    qp = jnp.where(has_prev, qp_ref[...], 0.0)
    qn = jnp.where(has_next, qn_ref[...], 0.0)
    for g in range(Q_GROUPS):
        sl = slice(g * LANES, (g + 1) * LANES)
        eq_scr[g, 0:Q_HALO, :] = qp[:, sl]
        eq_scr[g, Q_HALO:Q_HALO + TOK_TILE, :] = qc_ref[:, sl]
        eq_scr[g, Q_HALO + TOK_TILE:Q_HALO + TOK_TILE + Q_HALO, :] = qn[:, sl]
        eq_scr[g, TOK_TILE + 2 * Q_HALO:, :] = jnp.zeros((Q_EXT_ROWS - TOK_TILE - 2 * Q_HALO, LANES), F32)
    up = jnp.where(has_prev, up_ref[...], 0.0)
    un_ = jnp.where(has_next, un_ref[...], 0.0)
    for g in range(U_GROUPS):
        sl = slice(g * LANES, (g + 1) * LANES)
        eu_scr[g, 0:U_HALO, :] = up[:, sl]
        eu_scr[g, U_HALO:U_HALO + TOK_TILE, :] = uc_ref[:, sl]
        eu_scr[g, U_HALO + TOK_TILE:U_HALO + TOK_TILE + U_HALO, :] = un_[:, sl]
        eu_scr[g, TOK_TILE + 2 * U_HALO:, :] = jnp.zeros((U_EXT_ROWS - TOK_TILE - 2 * U_HALO, LANES), F32)

    def q_group(g, carry):
        _strided_conv(eq_scr, rq_scr, cw_ref, g, Q_HALO - SHORT_CONV // 2, SHORT_CONV, 11)
        return carry

    def u_group(g, carry):
        _strided_conv(eu_scr, ru_scr, dw_ref, g, U_HALO - CONV_W // 2, CONV_W, 11)
        return carry

    lax.fori_loop(0, Q_GROUPS, q_group, 0)
    lax.fori_loop(0, U_GROUPS, u_group, 0)

    for rc in range(TOK_TILE // POST_ROWS):
        r0 = rc * POST_ROWS
        for g in range(Q_GROUPS):
            y = _silu(rq_scr[g, r0:r0 + POST_ROWS, :])
            if g < 2 * N_HEADS:
                y = y * lax.rsqrt(jnp.sum(y * y, axis=-1, keepdims=True) + EPS)
            qkv_ref[r0:r0 + POST_ROWS, g * LANES:(g + 1) * LANES] = y
        u = jnp.concatenate([ru_scr[g, r0:r0 + POST_ROWS, :] for g in range(U_GROUPS)], axis=1) + dwb_ref[...]
        uc = u - jnp.mean(u, axis=-1, keepdims=True)
        un = uc * lax.rsqrt(jnp.mean(uc * uc, axis=-1, keepdims=True) + EPS)
        uo_ref[r0:r0 + POST_ROWS, :] = _silu(un * lng_ref[...] + lnb_ref[...]).astype(BF16)

    x = ba_ref[...]
    g = -jnp.exp(alog_ref[...]) * _softplus(x + dtb_ref[...])
    beta = _sigmoid(x)
    rows = lax.broadcasted_iota(I32, (CHUNK, CHUNK), 0)
    cols = lax.broadcasted_iota(I32, (CHUNK, CHUNK), 1)
    tri_lo = (rows >= cols).astype(F32)
    tri_up = (rows <= cols).astype(F32)
    lane = lax.broadcasted_iota(I32, (CHUNK, LANES), 1)
    for ch in range(TOK_TILE // CHUNK):
        sl = slice(ch * CHUNK, (ch + 1) * CHUNK)
        pre = jnp.dot(tri_lo, g[sl], preferred_element_type=F32, precision=HIGHEST)
        suf = jnp.dot(tri_up, g[sl], preferred_element_type=F32, precision=HIGHEST)
        gate_ref[sl, :] = jnp.where(lane < 2 * N_HEADS, beta[sl], jnp.where(lane < 3 * N_HEADS, pre, suf))


def _conv(lay, qkv_raw, ug, ba, cw, dw, dwb, lng, lnb, alog, dtb):
    t = lay.t
    qh = TOK_TILE // Q_HALO
    uh = TOK_TILE // U_HALO
    n_qh = t // Q_HALO
    n_uh = t // U_HALO
    in_specs = [
        _row_spec(3 * MIX_A),
        pl.BlockSpec((Q_HALO, 3 * MIX_A), lambda i: (jnp.maximum(i * qh - 1, 0), 0)),
        pl.BlockSpec((Q_HALO, 3 * MIX_A), lambda i: (jnp.minimum((i + 1) * qh, n_qh - 1), 0)),
        _row_spec(MIX_B),
        pl.BlockSpec((U_HALO, MIX_B), lambda i: (jnp.maximum(i * uh - 1, 0), 0)),
        pl.BlockSpec((U_HALO, MIX_B), lambda i: (jnp.minimum((i + 1) * uh, n_uh - 1), 0)),
        _row_spec(LANES),
        _full_spec(cw.shape), _full_spec(dw.shape), _full_spec(dwb.shape),
        _full_spec(lng.shape), _full_spec(lnb.shape), _full_spec(alog.shape), _full_spec(dtb.shape),
    ]
    return pl.pallas_call(
        functools.partial(_conv_kernel, lay),
        grid=(lay.tiles,),
        in_specs=in_specs,
        out_specs=[_row_spec(3 * MIX_A), _row_spec(MIX_B), _row_spec(LANES)],
        out_shape=[jax.ShapeDtypeStruct((t, 3 * MIX_A), F32),
                   jax.ShapeDtypeStruct((t, MIX_B), BF16),
                   jax.ShapeDtypeStruct((t, LANES), F32)],
        scratch_shapes=[pltpu.VMEM((Q_GROUPS, Q_EXT_ROWS, LANES), F32),
                        pltpu.VMEM((U_GROUPS, U_EXT_ROWS, LANES), F32),
                        pltpu.VMEM((Q_GROUPS, CONV_OUT_ROWS, LANES), F32),
                        pltpu.VMEM((U_GROUPS, CONV_OUT_ROWS, LANES), F32)],
        compiler_params=_cparams(("parallel",)),
        name="conv",
    )(qkv_raw, qkv_raw, qkv_raw, ug, ug, ug, ba, cw, dw, dwb, lng, lnb, alog, dtb)


INV_BASE = 8


def _b16(xs):
    return [x.astype(BF16) for x in xs]


def _tri_inverse_minus_eye(nmats, rows, cols):
    assert INV_BASE == 8
    c = nmats[0].shape[0]
    shift = int(math.log2(INV_BASE))
    same = (rows >> shift) == (cols >> shift)
    n1 = [jnp.where(same, n, 0.0) for n in nmats]
    n1b = _b16(n1)
    n2 = [_dot(x, x) for x in n1b]
    n2b = _b16(n2)
    r = [_dot(jnp.concatenate([a, b], axis=0), b) for a, b in zip(n1b, n2b)]
    q = [a + b + x[:c] for a, b, x in zip(n1, n2, r)]
    n4 = [x[c:] for x in r]
    qn4 = [_dot(a, b) for a, b in zip(_b16(q), _b16(n4))]
    q = [a + b + x for a, b, x in zip(q, n4, qn4)]
    while (1 << shift) < c:
        off = ((rows >> (shift + 1)) == (cols >> (shift + 1))) & ((rows >> shift) != (cols >> shift))
        a = [jnp.where(off, -n, 0.0) for n in nmats]
        x = [ai + _dot(ab, qb) for ai, ab, qb in zip(a, _b16(a), _b16(q))]
        qx = [_dot(qb, xb) for qb, xb in zip(_b16(q), _b16(x))]
        q = [qi - xi - qxi for qi, xi, qxi in zip(q, x, qx)]
        shift += 1
    return q


def _delta_units(units):
    c = CHUNK
    rows = lax.broadcasted_iota(I32, (c, c), 0)
    cols = lax.broadcasted_iota(I32, (c, c), 1)
    pre = []
    for d, hd, qkv_ref, r0, gate, gcol, grow, s in units:
        incl = (rows >= cols) if d == 0 else (rows <= cols)
        lane = 2 * N_HEADS + N_HEADS * d + hd
        beta = gate[:, N_HEADS * d + hd:N_HEADS * d + hd + 1]
        gc = gcol[:, lane:lane + 1]
        gr = grow[lane:lane + 1, :]
        last = c - 1 if d == 0 else 0
        gtot = gcol[last:last + 1, lane:lane + 1]
        dec = jnp.where(incl, jnp.exp(jnp.where(incl, gc - gr, 0.0)), 0.0)
        q = qkv_ref[pl.ds(r0, c), hd * HEAD_DIM:(hd + 1) * HEAD_DIM]
        k = qkv_ref[pl.ds(r0, c), MIX_A + hd * HEAD_DIM:MIX_A + (hd + 1) * HEAD_DIM]
        v = qkv_ref[pl.ds(r0, c), 2 * MIX_A + hd * HEAD_DIM:2 * MIX_A + (hd + 1) * HEAD_DIM]
        qs = q * (HEAD_DIM ** -0.5)
        kbeta = k * beta
        egc = jnp.exp(gc)
        pre.append(dict(d=d, beta=beta, gc=gc, gtot=gtot, dec=dec, k=k, v=v, qs=qs, kbeta=kbeta, egc=egc, s=s))

    ak = [_dot_nt(jnp.concatenate([p['kbeta'], p['qs']], axis=0).astype(BF16), p['k'].astype(BF16))
          for p in pre]
    nmats, qks = [], []
    for p, a in zip(pre, ak):
        strict = (rows > cols) if p['d'] == 0 else (rows < cols)
        nmats.append(jnp.where(strict, -a[:c] * p['dec'], 0.0))
        qks.append(a[c:] * p['dec'])
    qinv = _tri_inverse_minus_eye(nmats, rows, cols)

    rhs = [jnp.concatenate([p['v'] * p['beta'], p['kbeta'] * p['egc']], axis=1) for p in pre]
    uw = [r + _dot(qi, rb) for r, qi, rb in zip(rhs, _b16(qinv), _b16(rhs))]
    sw = [_dot(jnp.concatenate([x[:, HEAD_DIM:], p['qs'] * p['egc']], axis=0).astype(BF16), p['s'].astype(BF16))
          for x, p in zip(uw, pre)]
    vnb = [(x[:, :HEAD_DIM] - y[:c]).astype(BF16) for x, y in zip(uw, sw)]
    o = [y[c:] + _dot(qk, vb) for y, qk, vb in zip(sw, _b16(qks), vnb)]
    kdec = [(p['k'] * jnp.exp(p['gtot'] - p['gc'])).astype(BF16) for p in pre]
    s_new = [p['s'] * jnp.exp(p['gtot']) + _dot_tn(kd, vb) for p, kd, vb in zip(pre, kdec, vnb)]
    return list(zip(o, s_new))


def _delta_kernel(has_s0, emit_final, n_chunk, *refs):
    refs = list(refs)
    seq_in = [refs[6 * s:6 * s + 6] for s in range(DELTA_SEQS)]
    refs = refs[6 * DELTA_SEQS:]
    s0_ref = refs.pop(0) if has_s0 else None
    of_ref, ob_ref = refs[:2]
    refs = refs[2:]
    sfin_ref = refs.pop(0) if emit_final else None
    s_scr = refs[0]

    j = pl.program_id(1)

    @pl.when(j == 0)
    def _():
        if has_s0:
            s_scr[...] = s0_ref[...]
        else:
            s_scr[...] = jnp.zeros(s_scr.shape, F32)

    c = CHUNK

    def body(ci, carry):
        units, where = [], []
        for s, (qkvf_ref, qkvb_ref, gatef_ref, gateb_ref, gtf_ref, gtb_ref) in enumerate(seq_in):
            for d in range(2):
                cidx = ci if d == 0 else n_chunk - 1 - ci
                r0 = pl.multiple_of(cidx * c, c)
                qkv_ref = qkvf_ref if d == 0 else qkvb_ref
                gate = (gatef_ref if d == 0 else gateb_ref)[pl.ds(r0, c), :]
                gt = (gtf_ref if d == 0 else gtb_ref)[cidx]
                for hd in range(N_HEADS):
                    units.append((d, hd, qkv_ref, r0, gate, gate, gt, s_scr[s, N_HEADS * d + hd]))
                    where.append((s, d, hd, r0))
        for (s, d, hd, r0), (o, s_new) in zip(where, _delta_units(units)):
            s_scr[s, N_HEADS * d + hd] = s_new
            (of_ref if d == 0 else ob_ref)[s, pl.ds(r0, c), hd * HEAD_DIM:(hd + 1) * HEAD_DIM] = o
        return carry

    lax.fori_loop(0, n_chunk, body, 0)

    if emit_final:
        @pl.when(j == pl.num_programs(1) - 1)
        def _():
            sfin_ref[...] = s_scr[...]


def _delta(qkv, gate, gate_t, n_seq, seq_len, row0, blk, s0):
    nblk = seq_len // blk
    n_chunk = blk // CHUNK
    b0 = row0 // blk
    has_s0 = s0 is not None
    emit_final = not has_s0
    n_state = 2 * N_HEADS
    assert n_seq % DELTA_SEQS == 0 and row0 % blk == 0

    in_specs, args = [], []
    for s in range(DELTA_SEQS):
        def fwd(b, j, s=s):
            return b0 + (b * DELTA_SEQS + s) * nblk + j

        def bwd(b, j, s=s):
            return b0 + (b * DELTA_SEQS + s) * nblk + (nblk - 1 - j)

        in_specs += [
            pl.BlockSpec((blk, 3 * MIX_A), lambda b, j, f=fwd: (f(b, j), 0)),
            pl.BlockSpec((blk, 3 * MIX_A), lambda b, j, f=bwd: (f(b, j), 0)),
            pl.BlockSpec((blk, LANES), lambda b, j, f=fwd: (f(b, j), 0)),
            pl.BlockSpec((blk, LANES), lambda b, j, f=bwd: (f(b, j), 0)),
            pl.BlockSpec((n_chunk, 16, CHUNK), lambda b, j, f=fwd: (f(b, j), 0, 0)),
            pl.BlockSpec((n_chunk, 16, CHUNK), lambda b, j, f=bwd: (f(b, j), 0, 0)),
        ]
        args += [qkv, qkv, gate, gate, gate_t, gate_t]
    state_spec = pl.BlockSpec((DELTA_SEQS, n_state, HEAD_DIM, HEAD_DIM), lambda b, j: (b, 0, 0, 0))
    if has_s0:
        in_specs.append(state_spec)
        args.append(s0)
    out_specs = [pl.BlockSpec((DELTA_SEQS, blk, MIX_A), lambda b, j: (b, j, 0)),
                 pl.BlockSpec((DELTA_SEQS, blk, MIX_A), lambda b, j: (b, nblk - 1 - j, 0))]
    out_shape = [jax.ShapeDtypeStruct((n_seq, seq_len, MIX_A), F32)] * 2
    if emit_final:
        out_specs.append(state_spec)
        out_shape.append(jax.ShapeDtypeStruct((n_seq, n_state, HEAD_DIM, HEAD_DIM), F32))
    outs = pl.pallas_call(
        functools.partial(_delta_kernel, has_s0, emit_final, n_chunk),
        grid=(n_seq // DELTA_SEQS, nblk),
        in_specs=in_specs,
        out_specs=out_specs,
        out_shape=out_shape,
        scratch_shapes=[pltpu.VMEM((DELTA_SEQS, n_state, HEAD_DIM, HEAD_DIM), F32)],
        compiler_params=_cparams(("parallel", "arbitrary")),
        name="delta_latent" if has_s0 else "delta_prompt",
    )(*args)
    return [o.reshape(n_seq * seq_len, MIX_A) for o in outs[:2]] + list(outs[2:])


def _outproj_kernel(lay, xp_ref, xs_ref, pos_ref, mod_ref, opf_ref, opb_ref, osf_ref, osb_ref, z_ref, u_ref,
                    og_ref, n2_ref, wo_ref, wr_ref, br_ref, x1_ref, h2_ref, ids_ref, wts_ref, cnt_ref, cnt_scr):
    i = pl.program_id(0)
    x0 = _load_x(lay, i, xp_ref, xs_ref, pos_ref)
    m = mod_ref[0]
    o = jnp.where(lay.is_sample(i), osf_ref[...] + osb_ref[...], opf_ref[...] + opb_ref[...])
    z = z_ref[...]
    mix = _dot(u_ref[...], wo_ref[MIX_A:, :])
    for hd in range(N_HEADS):
        sl = slice(hd * HEAD_DIM, (hd + 1) * HEAD_DIM)
        oh = _rms(o[:, sl]) * og_ref[...] * _silu(z[:, sl])
        mix = mix + _dot(oh.astype(BF16), wo_ref[sl, :])
    x1 = x0 + m[2:3] * mix
    x1_ref[...] = x1
    h2 = _rms(x1) * n2_ref[...] * (1.0 + m[4:5]) + m[3:4]
    h2b = h2.astype(BF16)
    half = D_MODEL // 2
    hi = pltpu.bitcast(h2b[:, :half].astype(F32), jnp.uint32)
    lo = pltpu.bitcast(h2b[:, half:].astype(F32), jnp.uint32)
    h2_ref[...] = pltpu.bitcast((hi & jnp.uint32(0xFFFF0000)) | (lo >> 16), I32)

    logits = _dot(h2b, wr_ref[...]) + br_ref[...]
    lane = lax.broadcasted_iota(I32, logits.shape, 1)
    neg = jnp.float32(-jnp.inf)
    big = jnp.int32(LANES)
    gl = jnp.where(lane < N_GROUPS, logits, neg)
    gmax = jnp.max(gl, axis=-1, keepdims=True)
    grp = jnp.min(jnp.where(gl == gmax, lane, big), axis=-1, keepdims=True)
    p_grp = 1.0 / jnp.sum(jnp.where(lane < N_GROUPS, jnp.exp(gl - gmax), 0.0), axis=-1, keepdims=True)
    e_lane = lane - N_GROUPS
    in_grp = (e_lane >= grp * EXPERTS_PER_GROUP) & (e_lane < (grp + 1) * EXPERTS_PER_GROUP)
    el = jnp.where(in_grp, logits, neg)
    m1 = jnp.max(el, axis=-1, keepdims=True)
    i1 = jnp.min(jnp.where(el == m1, lane, big), axis=-1, keepdims=True)
    el2 = jnp.where(lane == i1, neg, el)
    m2 = jnp.max(el2, axis=-1, keepdims=True)
    i2 = jnp.min(jnp.where(el2 == m2, lane, big), axis=-1, keepdims=True)
    e2 = jnp.exp(m2 - m1)
    w1 = p_grp / (1.0 + e2)
    w2 = p_grp * e2 / (1.0 + e2)
    wts_ref[...] = jnp.where(lane == 0, w1, jnp.where(lane == 1, w2, 0.0))

    @pl.when(i == 0)
    def _():
        cnt_scr[...] = jnp.zeros(cnt_scr.shape, F32)

    oh1 = lane == i1
    oh2 = lane == i2
    oh = jnp.where(oh1, 1.0, jnp.where(oh2, 1.0, 0.0))
    tm = logits.shape[0]
    rows = lax.broadcasted_iota(I32, (tm, tm), 0)
    cols = lax.broadcasted_iota(I32, (tm, tm), 1)
    before = jnp.where(rows > cols, 1.0, 0.0).astype(BF16)
    seen = cnt_scr[...] + _dot(before, oh.astype(BF16))
    r1 = jnp.sum(jnp.where(oh1, seen, 0.0), axis=-1, keepdims=True).astype(I32)
    r2 = jnp.sum(jnp.where(oh2, seen, 0.0), axis=-1, keepdims=True).astype(I32)
    cnt_scr[...] = cnt_scr[...] + jnp.sum(oh, axis=0, keepdims=True)
    cnt_ref[...] = cnt_scr[...]
    ids_ref[...] = jnp.where(lane == 0, i1 - N_GROUPS,
                             jnp.where(lane == 1, i2 - N_GROUPS,
                                       jnp.where(lane == 2, r1, jnp.where(lane == 3, r2, 0))))


def _outproj(lay, xp, xs, pos, mod, o_pf, o_pb, o_sf, o_sb, z, u, onorm_g, norm2_g, w_out, w_route, b_route):
    t = lay.t
    p_spec = pl.BlockSpec((TOK_TILE, MIX_A), lambda i: (lay.xp_blk(i), 0))
    s_spec = pl.BlockSpec((TOK_TILE, MIX_A), lambda i: (lay.xs_blk(i), 0))
    return pl.pallas_call(
        functools.partial(_outproj_kernel, lay),
        grid=(lay.tiles,),
        in_specs=_x_specs(lay) + [_mod_spec(lay), p_spec, p_spec, s_spec, s_spec, _row_spec(MIX_A),
                                  _row_spec(MIX_B), _full_spec((1, HEAD_DIM)), _full_spec((1, D_MODEL)),
                                  _full_spec(w_out.shape), _full_spec(w_route.shape),
                                  _full_spec(b_route.shape)],
        out_specs=[_row_spec(D_MODEL), _row_spec(D_MODEL // 2), _row_spec(LANES), _row_spec(LANES),
                   _full_spec((1, LANES))],
        out_shape=[jax.ShapeDtypeStruct((t, D_MODEL), F32),
                   jax.ShapeDtypeStruct((t, D_MODEL // 2), I32),
                   jax.ShapeDtypeStruct((t, LANES), I32),
                   jax.ShapeDtypeStruct((t, LANES), F32),
                   jax.ShapeDtypeStruct((1, LANES), F32)],
        scratch_shapes=[pltpu.VMEM((1, LANES), F32)],
        compiler_params=_cparams(("arbitrary",)),
        name="outproj",
    )(xp, xs, pos, mod, o_pf, o_pb, o_sf, o_sb, z, u, onorm_g, norm2_g, w_out, w_route, b_route)


def _sc_gather(table, idx, chunk, name):
    n_rows, width = idx.shape[0], table.shape[1]
    mesh = plsc.VectorSubcoreMesh(core_axis_name="c", subcore_axis_name="s")
    n_workers = mesh.num_cores * mesh.num_subcores
    per_worker = n_rows // n_workers
    n_chunks = per_worker // chunk
    assert n_rows == n_workers * n_chunks * chunk and n_chunks % 2 == 0 and chunk % 8 == 0 and chunk <= LANES

    def body(table_hbm, idx_hbm, out_hbm, idx_v, rows_v, sem):
        base = (lax.axis_index("s") * mesh.num_cores + lax.axis_index("c")) * per_worker

        def gather(slot):
            return pltpu.make_async_copy(table_hbm.at[idx_v.at[slot]], rows_v.at[slot], sem.at[slot])

        def fetch(g, slot):
            off = pl.multiple_of(base + g * chunk, 8)
            pltpu.sync_copy(idx_hbm.at[pl.ds(off, chunk)], idx_v.at[slot])
            gather(slot).start()

        for slot in range(2):
            fetch(slot, slot)

        @pl.loop(0, n_chunks, step=2)
        def _(g):
            for slot in range(2):
                off = pl.multiple_of(base + (g + slot) * chunk, 8)
                gather(slot).wait()
                pltpu.sync_copy(rows_v.at[slot], out_hbm.at[pl.ds(off, chunk)])

                @pl.when(g + slot + 2 < n_chunks)
                def _():
                    fetch(g + slot + 2, slot)

    return pl.kernel(
        body,
        out_type=jax.ShapeDtypeStruct((n_rows, width), table.dtype),
        mesh=mesh,
        scratch_types=[pltpu.VMEM((2, chunk), I32), pltpu.VMEM((2, chunk, width), table.dtype),
                       pltpu.SemaphoreType.DMA((2,))],
        name=name,
    )(table, idx)


def _sc_scatter2(src, idx0, idx1, n_out, chunk, name):
    n_rows, width = src.shape
    mesh = plsc.VectorSubcoreMesh(core_axis_name="c", subcore_axis_name="s")
    n_workers = mesh.num_cores * mesh.num_subcores
    per_worker = n_rows // n_workers
    n_chunks = per_worker // chunk
    assert n_rows == n_workers * n_chunks * chunk and n_chunks % 2 == 0 and chunk % 8 == 0 and chunk <= LANES

    def body(src_hbm, i0_hbm, i1_hbm, out_hbm, i0_v, i1_v, rows_v, sem_in, sem_out):
        base = (lax.axis_index("s") * mesh.num_cores + lax.axis_index("c")) * per_worker

        def rows_in(g, slot):
            off = pl.multiple_of(base + g * chunk, 8)
            return pltpu.make_async_copy(src_hbm.at[pl.ds(off, chunk)], rows_v.at[slot], sem_in.at[slot])

        def fetch(g, slot):
            off = pl.multiple_of(base + g * chunk, 8)
            pltpu.sync_copy(i0_hbm.at[pl.ds(off, chunk)], i0_v.at[slot])
            pltpu.sync_copy(i1_hbm.at[pl.ds(off, chunk)], i1_v.at[slot])
            rows_in(g, slot).start()

        for slot in range(2):
            fetch(slot, slot)

        @pl.loop(0, n_chunks, step=2)
        def _(g):
            for slot in range(2):
                rows_in(g + slot, slot).wait()
                puts = [pltpu.make_async_copy(rows_v.at[slot], out_hbm.at[iv.at[slot]], sem_out.at[slot])
                        for iv in (i0_v, i1_v)]
                for put in puts:
                    put.start()
                for put in puts:
                    put.wait()

                @pl.when(g + slot + 2 < n_chunks)
                def _():
                    fetch(g + slot + 2, slot)

    return pl.kernel(
        body,
        out_type=jax.ShapeDtypeStruct((n_out, width), src.dtype),
        mesh=mesh,
        scratch_types=[pltpu.VMEM((2, chunk), I32), pltpu.VMEM((2, chunk), I32),
                       pltpu.VMEM((2, chunk, width), src.dtype),
                       pltpu.SemaphoreType.DMA((2,)), pltpu.SemaphoreType.DMA((2,))],
        name=name,
    )(src, idx0, idx1)


def _expert_kernel(blk_e_ref, nused_ref, xb_ref, wg_ref, wu_ref, wd_ref, yb_ref, wg_s, wu_s, wd_s):
    i = pl.program_id(0)
    nused = nused_ref[0]

    @pl.when(i < nused)
    def _():
        changed = (i == 0) | (blk_e_ref[i] != blk_e_ref[jnp.maximum(i - 1, 0)])

        @pl.when(changed)
        def _():
            wg_s[...] = wg_ref[0].astype(BF16)
            wu_s[...] = wu_ref[0].astype(BF16)
            wd_s[...] = wd_ref[0].astype(BF16)

        x = pltpu.bitcast(xb_ref[...], jnp.uint32)
        half = D_MODEL // 2
        xa = pltpu.bitcast(x & jnp.uint32(0xFFFF0000), F32).astype(BF16)
        xb = pltpu.bitcast(x << 16, F32).astype(BF16)
        g = _dot(xa, wg_s[:half, :]) + _dot(xb, wg_s[half:, :])
        u = _dot(xa, wu_s[:half, :]) + _dot(xb, wu_s[half:, :])
        hmid = (_silu(g) * u).astype(BF16)
        yb_ref[...] = _dot(hmid, wd_s[...])

    @pl.when(i >= nused)
    def _():
        yb_ref[...] = jnp.zeros(yb_ref.shape, F32)


def _experts(xb, blk_e, nused, w_gate, w_up, w_down):
    nb = blk_e.shape[0]
    grid_spec = pltpu.PrefetchScalarGridSpec(
        num_scalar_prefetch=2,
        grid=(nb,),
        in_specs=[
            pl.BlockSpec((ROUTE_BM, D_MODEL // 2), lambda i, be, nu: (i, 0)),
            pl.BlockSpec((1, D_MODEL, D_EXPERT), lambda i, be, nu: (be[i], 0, 0)),
            pl.BlockSpec((1, D_MODEL, D_EXPERT), lambda i, be, nu: (be[i], 0, 0)),
            pl.BlockSpec((1, D_EXPERT, D_MODEL), lambda i, be, nu: (be[i], 0, 0)),
        ],
        out_specs=pl.BlockSpec((ROUTE_BM, D_MODEL), lambda i, be, nu: (i, 0)),
        scratch_shapes=[pltpu.VMEM((D_MODEL, D_EXPERT), BF16),
                        pltpu.VMEM((D_MODEL, D_EXPERT), BF16),
                        pltpu.VMEM((D_EXPERT, D_MODEL), BF16)],
    )
    return pl.pallas_call(
        _expert_kernel,
        grid_spec=grid_spec,
        out_shape=jax.ShapeDtypeStruct((nb * ROUTE_BM, D_MODEL), F32),
        compiler_params=_cparams(("arbitrary",)),
        name="expert",
    )(blk_e, nused, xb, w_gate, w_up, w_down)


def _combine_kernel(lay, y0_ref, y1_ref, x1_ref, wts_ref, mod_ref, fg_ref, yp_ref, ys_ref):
    i = pl.program_id(0)
    m = mod_ref[0]
    w = wts_ref[...]
    ff = y0_ref[...] * w[:, 0:1] + y1_ref[...] * w[:, 1:2]
    y = _rms(x1_ref[...] + m[5:6] * ff) * fg_ref[...]

    @pl.when(i < lay.tiles_p)
    def _():
        yp_ref[...] = y

    @pl.when(i >= lay.tiles_p)
    def _():
        ys_ref[...] = y


def _combine(lay, yg, x1, wts, mod, final_g):
    return pl.pallas_call(
        functools.partial(_combine_kernel, lay),
        grid=(lay.tiles,),
        in_specs=[_row_spec(D_MODEL), pl.BlockSpec((TOK_TILE, D_MODEL), lambda i: (i + lay.tiles, 0)),
                  _row_spec(D_MODEL), _row_spec(LANES), _mod_spec(lay), _full_spec((1, D_MODEL))],
        out_specs=[pl.BlockSpec((TOK_TILE, D_MODEL), lambda i: (lay.xp_blk(i), 0)),
                   pl.BlockSpec((TOK_TILE, D_MODEL), lambda i: (lay.xs_blk(i), 0))],
        out_shape=[jax.ShapeDtypeStruct((lay.t_p, D_MODEL), F32),
                   jax.ShapeDtypeStruct((lay.t_s, D_MODEL), F32)],
        compiler_params=_cparams(("arbitrary",)),
        name="combine",
    )(yg, yg, x1, wts, mod, final_g)


def _dispatch_plan(ids, counts):
    n_tok = ids.shape[0]
    padded = (counts + ROUTE_BM - 1) // ROUTE_BM * ROUTE_BM
    pad_end = jnp.cumsum(padded)
    pad_start = pad_end - padded
    expert = ids[:, 0:2]
    start = jnp.sum(jnp.where(expert[:, :, None] == jnp.arange(N_EXPERTS, dtype=I32), pad_start, 0), axis=-1)
    dest = (start + ids[:, 2:4]).astype(I32)
    nb = -(-(2 * n_tok + N_EXPERTS * (ROUTE_BM - 1)) // ROUTE_BM)
    block_start = jnp.arange(nb, dtype=I32) * ROUTE_BM
    blk_e = jnp.minimum(jnp.sum(pad_end[None, :] <= block_start[:, None], axis=1), N_EXPERTS - 1).astype(I32)
    nused = (pad_end[-1:] // ROUTE_BM).astype(I32)
    return dest[:, 0], dest[:, 1], blk_e, nused


def _grid_pos_embed(n_tokens):
    rows = n_tokens // GRID_W
    r = jnp.repeat(jnp.arange(rows, dtype=F32), GRID_W)
    col = jnp.tile(jnp.arange(GRID_W, dtype=F32), rows)
    n_freq = D_MODEL // 4
    freq = jnp.exp(jnp.arange(n_freq, dtype=F32) * (-math.log(10000.0) / n_freq))

    def enc(p):
        ang = p[:, None] * freq[None, :]
        return jnp.concatenate([jnp.sin(ang), jnp.cos(ang)], axis=-1)

    return jnp.concatenate([enc(r), enc(col)], axis=-1)


def _lane_pad(v, offset):
    return jnp.zeros((1, LANES), F32).at[0, offset:offset + v.shape[0]].set(v.astype(F32))


def kernel(x_prompt, x_sample, state_delta, c, c_ctx, norm1_g, w_mod, b_mod, w_in, conv_qkv_w, A_log, dt_bias, onorm_g, dw_w, dw_b, cln_g, cln_b, w_out, norm2_g, w_group, b_group, w_expert, b_expert, w_e_gate, w_e_up, w_e_down, final_g):
    n_p, l_p, _ = x_prompt.shape
    n_s, l_s, _ = x_sample.shape
    lay = _Layout(n_p, l_p, n_s, l_s)
    depth = w_in.shape[0]
    assert depth == 1
    xp = x_prompt.reshape(lay.t_p, D_MODEL)
    xs = x_sample.reshape(lay.t_s, D_MODEL)
    pos = _grid_pos_embed(l_s)

    cond = jnp.concatenate([c_ctx[None, :], c], axis=0)
    cond = jnp.pad(cond, ((0, (-cond.shape[0]) % 8), (0, 0)))
    mod = _modulation(cond, w_mod[0], b_mod[0]).reshape(cond.shape[0], 6, D_MODEL)

    n_gate = 4 * N_HEADS
    wi = w_in[0]
    w_main = jnp.concatenate([wi[:, :4 * MIX_A], wi[:, 4 * MIX_A + n_gate:]], axis=1).astype(BF16)
    w_small = jnp.pad(wi[:, 4 * MIX_A:4 * MIX_A + n_gate], ((0, 0), (0, LANES - n_gate))).astype(BF16)
    qkv_raw, z, ug, ba = _inproj(lay, xp, xs, pos, mod, norm1_g[0][None, :], w_main, w_small)

    cw = jnp.pad(conv_qkv_w[0], ((0, 8 - SHORT_CONV), (0, 0))).reshape(8, Q_GROUPS, LANES).transpose(1, 0, 2)
    dw = jnp.pad(dw_w[0], ((0, 32 - CONV_W), (0, 0))).reshape(32, U_GROUPS, LANES).transpose(1, 0, 2)
    alog = _lane_pad(A_log[0].reshape(-1), 2 * N_HEADS)
    dtb = _lane_pad(dt_bias[0].reshape(-1), 2 * N_HEADS)
    qkv, u_conf, gate = _conv(lay, qkv_raw, ug, ba, cw, dw, dw_b[0][None, :], cln_g[0][None, :],
                              cln_b[0][None, :], alog, dtb)

    gate_t = gate[:, :16].reshape(lay.t // CHUNK, CHUNK, 16).transpose(0, 2, 1)
    o_pf, o_pb, s_fin = _delta(qkv, gate, gate_t, n_p, l_p, 0, l_p, None)
    s0 = state_delta[:, 0].reshape(n_s, 2 * N_HEADS, HEAD_DIM, HEAD_DIM)
    o_sf, o_sb = _delta(qkv, gate, gate_t, n_s, l_s, lay.t_p, DELTA_BLK, s0)

    w_route = jnp.pad(jnp.concatenate([w_group[0], w_expert[0]], axis=1),
                      ((0, 0), (0, LANES - N_GROUPS - N_EXPERTS))).astype(BF16)
    b_route = _lane_pad(jnp.concatenate([b_group[0], b_expert[0]]), 0)
    x1, h2p, ids, wts, cnt = _outproj(lay, xp, xs, pos, mod, o_pf, o_pb, o_sf, o_sb, z, u_conf, onorm_g[0][None, :],
                                 norm2_g[0][None, :], w_out[0].astype(BF16), w_route, b_route)

    counts = cnt[0, N_GROUPS:N_GROUPS + N_EXPERTS].astype(I32)
    dest0, dest1, blk_e, nused = _dispatch_plan(ids, counts)
    xb = _sc_scatter2(h2p, dest0, dest1, blk_e.shape[0] * ROUTE_BM, DISPATCH_CHUNK, "dispatch_scatter")
    yb = _experts(xb, blk_e, nused, w_e_gate[0], w_e_up[0], w_e_down[0])
    yg = _sc_gather(yb, jnp.concatenate([dest0, dest1]), COMBINE_CHUNK, "combine_gather")
    y_p, y_s = _combine(lay, yg, x1, wts, mod, final_g[None, :])

    new_state = s_fin.reshape(n_p, 1, 2, N_HEADS, HEAD_DIM, HEAD_DIM)
    return (y_p.reshape(x_prompt.shape), y_s.reshape(x_sample.shape), new_state)
```

```python
import functools
import math

import jax
import jax.numpy as jnp
from jax import lax
from jax.experimental import pallas as pl
from jax.experimental.pallas import tpu as pltpu
from jax.experimental.pallas import tpu_sc as plsc

F32 = jnp.float32
BF16 = jnp.bfloat16
I32 = jnp.int32

D_MODEL = 1024
MIX_A = 512
MIX_B = 512
HEAD_DIM = 128
N_HEADS = 4
SHORT_CONV = 5
CONV_W = 31
N_GROUPS = 4
EXPERTS_PER_GROUP = 8
N_EXPERTS = 32
D_EXPERT = 512
GRID_W = 64
EPS = 1e-6

LANES = 128
TOK_TILE = 256
PROJ_TILE = 512
CHUNK = 64
DELTA_BLK = 512
DELTA_SEQS = 2
ROUTE_BM = 512
DISPATCH_CHUNK = 64
COMBINE_CHUNK = 32
Q_HALO = 8
U_HALO = 16
VMEM_LIMIT = 56 * 1024 * 1024

HIGHEST = lax.Precision.HIGHEST


def _cparams(sem):
    return pltpu.CompilerParams(dimension_semantics=sem, vmem_limit_bytes=VMEM_LIMIT)


def _sigmoid(x):
    return 1.0 / (1.0 + jnp.exp(-x))


def _silu(x):
    return x * _sigmoid(x)


def _softplus(x):
    return jnp.maximum(x, 0.0) + jnp.log1p(jnp.exp(-jnp.abs(x)))


def _rms(x):
    return x * lax.rsqrt(jnp.mean(x * x, axis=-1, keepdims=True) + EPS)


def _dot(a, b):
    return jnp.dot(a, b, preferred_element_type=F32)


def _dot_nt(a, b):
    return lax.dot_general(a, b, (((1,), (1,)), ((), ())), preferred_element_type=F32)


def _dot_tn(a, b):
    return lax.dot_general(a, b, (((0,), (0,)), ((), ())), preferred_element_type=F32)


def _mod_kernel(cond_ref, w_ref, b_ref, o_ref):
    s = _silu(cond_ref[...])
    o_ref[...] = jnp.dot(s, w_ref[...], preferred_element_type=F32, precision=HIGHEST) + b_ref[...]


def _modulation(cond, w_mod, b_mod):
    n = cond.shape[0]
    tn = D_MODEL
    return pl.pallas_call(
        _mod_kernel,
        grid=(6 * D_MODEL // tn,),
        in_specs=[pl.BlockSpec((n, D_MODEL), lambda j: (0, 0)),
                  pl.BlockSpec((D_MODEL, tn), lambda j: (0, j)),
                  pl.BlockSpec((1, tn), lambda j: (0, j))],
        out_specs=pl.BlockSpec((n, tn), lambda j: (0, j)),
        out_shape=jax.ShapeDtypeStruct((n, 6 * D_MODEL), F32),
        compiler_params=_cparams(("parallel",)),
        name="mod",
    )(cond, w_mod, b_mod.reshape(1, -1))


class _Layout:
    def __init__(self, n_p, l_p, n_s, l_s, tile):
        self.n_p, self.l_p, self.n_s, self.l_s, self.tile = n_p, l_p, n_s, l_s, tile
        self.t_p = n_p * l_p
        self.t_s = n_s * l_s
        self.t = self.t_p + self.t_s
        assert self.t_p % tile == 0 and l_s % tile == 0 and l_s % DELTA_BLK == 0
        self.tiles_p = self.t_p // tile
        self.tiles_s = self.t_s // tile
        self.tiles = self.tiles_p + self.tiles_s
        self.tps_p = max(l_p // tile, 1)
        self.tps_s = l_s // tile

    def is_sample(self, i):
        return i >= self.tiles_p

    def mod_row(self, i):
        return jnp.where(i < self.tiles_p, 0, 1 + (i - self.tiles_p) // self.tps_s)

    def pos_blk(self, i):
        return jnp.where(i < self.tiles_p, 0, (i - self.tiles_p) % self.tps_s)

    def xp_blk(self, i):
        return jnp.minimum(i, self.tiles_p - 1)

    def xs_blk(self, i):
        return jnp.maximum(i - self.tiles_p, 0)

    def seq_pos(self, i):
        in_s = i >= self.tiles_p
        pos = jnp.where(in_s, (i - self.tiles_p) % self.tps_s, i % self.tps_p)
        n = jnp.where(in_s, self.tps_s, self.tps_p)
        return pos, n


def _load_x(lay, i, xp_ref, xs_ref, pos_ref):
    return jnp.where(lay.is_sample(i), xs_ref[...] + pos_ref[...], xp_ref[...])


def _x_specs(lay):
    return [pl.BlockSpec((lay.tile, D_MODEL), lambda i: (lay.xp_blk(i), 0)),
            pl.BlockSpec((lay.tile, D_MODEL), lambda i: (lay.xs_blk(i), 0)),
            pl.BlockSpec((lay.tile, D_MODEL), lambda i: (lay.pos_blk(i), 0))]


def _mod_spec(lay):
    return pl.BlockSpec((1, 6, D_MODEL), lambda i: (lay.mod_row(i), 0, 0))


def _row_spec(lay, width):
    return pl.BlockSpec((lay.tile, width), lambda i: (i, 0))


def _full_spec(shape):
    nd = len(shape)
    return pl.BlockSpec(shape, lambda i: (0,) * nd)


def _inproj_kernel(lay, xp_ref, xs_ref, pos_ref, mod_ref, g_ref, wm_ref, ws_ref,
                   qkv_ref, z_ref, ug_ref, ba_ref):
    i = pl.program_id(0)
    x = _load_x(lay, i, xp_ref, xs_ref, pos_ref)
    m = mod_ref[0]
    h = _rms(x) * g_ref[...] * (1.0 + m[1:2]) + m[0:1]
    hb = h.astype(BF16)
    qkv_ref[...] = _dot(hb, wm_ref[:, 0:3 * MIX_A])
    z_ref[...] = _dot(hb, wm_ref[:, 3 * MIX_A:4 * MIX_A])
    glu = _dot(hb, wm_ref[:, 4 * MIX_A:4 * MIX_A + 2 * MIX_B])
    ug_ref[...] = glu[:, :MIX_B] * _sigmoid(glu[:, MIX_B:])
    ba_ref[...] = _dot(hb, ws_ref[...])


def _inproj(lay, xp, xs, pos, mod, norm_g, w_main, w_small):
    t = lay.t
    return pl.pallas_call(
        functools.partial(_inproj_kernel, lay),
        grid=(lay.tiles,),
        in_specs=_x_specs(lay) + [_mod_spec(lay), _full_spec((1, D_MODEL)),
                                  _full_spec(w_main.shape), _full_spec(w_small.shape)],
        out_specs=[_row_spec(lay,3 * MIX_A), _row_spec(lay,MIX_A), _row_spec(lay,MIX_B), _row_spec(lay,LANES)],
        out_shape=[jax.ShapeDtypeStruct((t, 3 * MIX_A), F32),
                   jax.ShapeDtypeStruct((t, MIX_A), F32),
                   jax.ShapeDtypeStruct((t, MIX_B), F32),
                   jax.ShapeDtypeStruct((t, LANES), F32)],
        compiler_params=_cparams(("parallel",)),
        name="inproj",
    )(xp, xs, pos, mod, norm_g, w_main, w_small)


SUBLANES = 8
CONV_PITCH = 33
CONV_OUT_ROWS = SUBLANES * CONV_PITCH
POST_ROWS = 64
Q_EXT_ROWS = 280
U_EXT_ROWS = 296
Q_GROUPS = 3 * MIX_A // LANES
U_GROUPS = MIX_B // LANES


def _strided_conv(ext_scr, res_scr, w_ref, g, off, n_taps, j_block):
    for j0 in range(0, CONV_PITCH, j_block):
        js = range(j0, min(j0 + j_block, CONV_PITCH))
        v = {m: ext_scr[g, pl.ds(off + m, SUBLANES, stride=CONV_PITCH), :]
             for m in range(js[0], js[-1] + n_taps)}
        for j in js:
            acc = v[j] * w_ref[g, 0:1, :]
            for s in range(1, n_taps):
                acc = acc + v[j + s] * w_ref[g, s:s + 1, :]
            res_scr[g, pl.ds(j, SUBLANES, stride=CONV_PITCH), :] = acc


def _conv_kernel(lay, qc_ref, qp_ref, qn_ref, uc_ref, up_ref, un_ref, ba_ref, cw_ref, dw_ref,
                 dwb_ref, lng_ref, lnb_ref, alog_ref, dtb_ref,
                 qkv_ref, uo_ref, gate_ref, eq_scr, eu_scr, rq_scr, ru_scr):
    i = pl.program_id(0)
    pos, n = lay.seq_pos(i)
    has_prev = pos != 0
    has_next = pos != n - 1
    qp = jnp.where(has_prev, qp_ref[...], 0.0)
    qn = jnp.where(has_next, qn_ref[...], 0.0)
    for g in range(Q_GROUPS):
        sl = slice(g * LANES, (g + 1) * LANES)
        eq_scr[g, 0:Q_HALO, :] = qp[:, sl]
        eq_scr[g, Q_HALO:Q_HALO + TOK_TILE, :] = qc_ref[:, sl]
        eq_scr[g, Q_HALO + TOK_TILE:Q_HALO + TOK_TILE + Q_HALO, :] = qn[:, sl]
        eq_scr[g, TOK_TILE + 2 * Q_HALO:, :] = jnp.zeros((Q_EXT_ROWS - TOK_TILE - 2 * Q_HALO, LANES), F32)
    up = jnp.where(has_prev, up_ref[...], 0.0)
    un_ = jnp.where(has_next, un_ref[...], 0.0)
    for g in range(U_GROUPS):
        sl = slice(g * LANES, (g + 1) * LANES)
        eu_scr[g, 0:U_HALO, :] = up[:, sl]
        eu_scr[g, U_HALO:U_HALO + TOK_TILE, :] = uc_ref[:, sl]
        eu_scr[g, U_HALO + TOK_TILE:U_HALO + TOK_TILE + U_HALO, :] = un_[:, sl]
        eu_scr[g, TOK_TILE + 2 * U_HALO:, :] = jnp.zeros((U_EXT_ROWS - TOK_TILE - 2 * U_HALO, LANES), F32)

    def q_group(g, carry):
        _strided_conv(eq_scr, rq_scr, cw_ref, g, Q_HALO - SHORT_CONV // 2, SHORT_CONV, 11)
        return carry

    def u_group(g, carry):
        _strided_conv(eu_scr, ru_scr, dw_ref, g, U_HALO - CONV_W // 2, CONV_W, 11)
        return carry

    lax.fori_loop(0, Q_GROUPS, q_group, 0)
    lax.fori_loop(0, U_GROUPS, u_group, 0)

    for rc in range(TOK_TILE // POST_ROWS):
        r0 = rc * POST_ROWS
        for g in range(Q_GROUPS):
            y = _silu(rq_scr[g, r0:r0 + POST_ROWS, :])
            if g < 2 * N_HEADS:
                y = y * lax.rsqrt(jnp.sum(y * y, axis=-1, keepdims=True) + EPS)
            qkv_ref[r0:r0 + POST_ROWS, g * LANES:(g + 1) * LANES] = y
        u = jnp.concatenate([ru_scr[g, r0:r0 + POST_ROWS, :] for g in range(U_GROUPS)], axis=1) + dwb_ref[...]
        uc = u - jnp.mean(u, axis=-1, keepdims=True)
        un = uc * lax.rsqrt(jnp.mean(uc * uc, axis=-1, keepdims=True) + EPS)
        uo_ref[r0:r0 + POST_ROWS, :] = _silu(un * lng_ref[...] + lnb_ref[...]).astype(BF16)

    x = ba_ref[...]
    g = -jnp.exp(alog_ref[...]) * _softplus(x + dtb_ref[...])
    beta = _sigmoid(x)
    rows = lax.broadcasted_iota(I32, (CHUNK, CHUNK), 0)
    cols = lax.broadcasted_iota(I32, (CHUNK, CHUNK), 1)
    tri_lo = (rows >= cols).astype(F32)
    tri_up = (rows <= cols).astype(F32)
    lane = lax.broadcasted_iota(I32, (CHUNK, LANES), 1)
    for ch in range(TOK_TILE // CHUNK):
        sl = slice(ch * CHUNK, (ch + 1) * CHUNK)
        pre = jnp.dot(tri_lo, g[sl], preferred_element_type=F32, precision=HIGHEST)
        suf = jnp.dot(tri_up, g[sl], preferred_element_type=F32, precision=HIGHEST)
        gate_ref[sl, :] = jnp.where(lane < 2 * N_HEADS, beta[sl], jnp.where(lane < 3 * N_HEADS, pre, suf))


def _conv(lay, qkv_raw, ug, ba, cw, dw, dwb, lng, lnb, alog, dtb):
    t = lay.t
    qh = TOK_TILE // Q_HALO
    uh = TOK_TILE // U_HALO
    n_qh = t // Q_HALO
    n_uh = t // U_HALO
    in_specs = [
        _row_spec(lay,3 * MIX_A),
        pl.BlockSpec((Q_HALO, 3 * MIX_A), lambda i: (jnp.maximum(i * qh - 1, 0), 0)),
        pl.BlockSpec((Q_HALO, 3 * MIX_A), lambda i: (jnp.minimum((i + 1) * qh, n_qh - 1), 0)),
        _row_spec(lay,MIX_B),
        pl.BlockSpec((U_HALO, MIX_B), lambda i: (jnp.maximum(i * uh - 1, 0), 0)),
        pl.BlockSpec((U_HALO, MIX_B), lambda i: (jnp.minimum((i + 1) * uh, n_uh - 1), 0)),
        _row_spec(lay,LANES),
        _full_spec(cw.shape), _full_spec(dw.shape), _full_spec(dwb.shape),
        _full_spec(lng.shape), _full_spec(lnb.shape), _full_spec(alog.shape), _full_spec(dtb.shape),
    ]
    return pl.pallas_call(
        functools.partial(_conv_kernel, lay),
        grid=(lay.tiles,),
        in_specs=in_specs,
        out_specs=[_row_spec(lay,3 * MIX_A), _row_spec(lay,MIX_B), _row_spec(lay,LANES)],
        out_shape=[jax.ShapeDtypeStruct((t, 3 * MIX_A), F32),
                   jax.ShapeDtypeStruct((t, MIX_B), BF16),
                   jax.ShapeDtypeStruct((t, LANES), F32)],
        scratch_shapes=[pltpu.VMEM((Q_GROUPS, Q_EXT_ROWS, LANES), F32),
                        pltpu.VMEM((U_GROUPS, U_EXT_ROWS, LANES), F32),
                        pltpu.VMEM((Q_GROUPS, CONV_OUT_ROWS, LANES), F32),
                        pltpu.VMEM((U_GROUPS, CONV_OUT_ROWS, LANES), F32)],
        compiler_params=_cparams(("parallel",)),
        name="conv",
    )(qkv_raw, qkv_raw, qkv_raw, ug, ug, ug, ba, cw, dw, dwb, lng, lnb, alog, dtb)


INV_BASE = 8


def _b16(xs):
    return [x.astype(BF16) for x in xs]


def _tri_inverse_minus_eye(nmats, rows, cols):
    assert INV_BASE == 8
    c = nmats[0].shape[0]
    shift = int(math.log2(INV_BASE))
    same = (rows >> shift) == (cols >> shift)
    n1 = [jnp.where(same, n, 0.0) for n in nmats]
    n1b = _b16(n1)
    n2 = [_dot(x, x) for x in n1b]
    n2b = _b16(n2)
    r = [_dot(jnp.concatenate([a, b], axis=0), b) for a, b in zip(n1b, n2b)]
    q = [a + b + x[:c] for a, b, x in zip(n1, n2, r)]
    n4 = [x[c:] for x in r]
    qn4 = [_dot(a, b) for a, b in zip(_b16(q), _b16(n4))]
    q = [a + b + x for a, b, x in zip(q, n4, qn4)]
    while (1 << shift) < c:
        off = ((rows >> (shift + 1)) == (cols >> (shift + 1))) & ((rows >> shift) != (cols >> shift))
        a = [jnp.where(off, -n, 0.0) for n in nmats]
        x = [ai + _dot(ab, qb) for ai, ab, qb in zip(a, _b16(a), _b16(q))]
        qx = [_dot(qb, xb) for qb, xb in zip(_b16(q), _b16(x))]
        q = [qi - xi - qxi for qi, xi, qxi in zip(q, x, qx)]
        shift += 1
    return q


def _delta_units(units):
    c = CHUNK
    rows = lax.broadcasted_iota(I32, (c, c), 0)
    cols = lax.broadcasted_iota(I32, (c, c), 1)
    pre = []
    for d, hd, qkv_ref, r0, gate, gcol, grow, s in units:
        incl = (rows >= cols) if d == 0 else (rows <= cols)
        lane = 2 * N_HEADS + N_HEADS * d + hd
        beta = gate[:, N_HEADS * d + hd:N_HEADS * d + hd + 1]
        gc = gcol[:, lane:lane + 1]
        gr = grow[lane:lane + 1, :]
        last = c - 1 if d == 0 else 0
        gtot = gcol[last:last + 1, lane:lane + 1]
        dec = jnp.where(incl, jnp.exp(jnp.where(incl, gc - gr, 0.0)), 0.0)
        q = qkv_ref[pl.ds(r0, c), hd * HEAD_DIM:(hd + 1) * HEAD_DIM]
        k = qkv_ref[pl.ds(r0, c), MIX_A + hd * HEAD_DIM:MIX_A + (hd + 1) * HEAD_DIM]
        v = qkv_ref[pl.ds(r0, c), 2 * MIX_A + hd * HEAD_DIM:2 * MIX_A + (hd + 1) * HEAD_DIM]
        qs = q * (HEAD_DIM ** -0.5)
        kbeta = k * beta
        egc = jnp.exp(gc)
        pre.append(dict(d=d, beta=beta, gc=gc, gtot=gtot, dec=dec, k=k, v=v, qs=qs, kbeta=kbeta, egc=egc, s=s))

    ak = [_dot_nt(jnp.concatenate([p['kbeta'], p['qs']], axis=0).astype(BF16), p['k'].astype(BF16))
          for p in pre]
    nmats, qks = [], []
    for p, a in zip(pre, ak):
        strict = (rows > cols) if p['d'] == 0 else (rows < cols)
        nmats.append(jnp.where(strict, -a[:c] * p['dec'], 0.0))
        qks.append(a[c:] * p['dec'])
    qinv = _tri_inverse_minus_eye(nmats, rows, cols)

    rhs = [jnp.concatenate([p['v'] * p['beta'], p['kbeta'] * p['egc']], axis=1) for p in pre]
    uw = [r + _dot(qi, rb) for r, qi, rb in zip(rhs, _b16(qinv), _b16(rhs))]
    sw = [_dot(jnp.concatenate([x[:, HEAD_DIM:], p['qs'] * p['egc']], axis=0).astype(BF16), p['s'].astype(BF16))
          for x, p in zip(uw, pre)]
    vnb = [(x[:, :HEAD_DIM] - y[:c]).astype(BF16) for x, y in zip(uw, sw)]
    o = [y[c:] + _dot(qk, vb) for y, qk, vb in zip(sw, _b16(qks), vnb)]
    kdec = [(p['k'] * jnp.exp(p['gtot'] - p['gc'])).astype(BF16) for p in pre]
    s_new = [p['s'] * jnp.exp(p['gtot']) + _dot_tn(kd, vb) for p, kd, vb in zip(pre, kdec, vnb)]
    return list(zip(o, s_new))


def _delta_kernel(has_s0, emit_final, n_chunk, *refs):
    refs = list(refs)
    seq_in = [refs[6 * s:6 * s + 6] for s in range(DELTA_SEQS)]
    refs = refs[6 * DELTA_SEQS:]
    s0_ref = refs.pop(0) if has_s0 else None
    of_ref, ob_ref = refs[:2]
    refs = refs[2:]
    sfin_ref = refs.pop(0) if emit_final else None
    s_scr = refs[0]

    j = pl.program_id(1)

    @pl.when(j == 0)
    def _():
        if has_s0:
            s_scr[...] = s0_ref[...]
        else:
            s_scr[...] = jnp.zeros(s_scr.shape, F32)

    c = CHUNK

    def body(ci, carry):
        units, where = [], []
        for s, (qkvf_ref, qkvb_ref, gatef_ref, gateb_ref, gtf_ref, gtb_ref) in enumerate(seq_in):
            for d in range(2):
                cidx = ci if d == 0 else n_chunk - 1 - ci
                r0 = pl.multiple_of(cidx * c, c)
                qkv_ref = qkvf_ref if d == 0 else qkvb_ref
                gate = (gatef_ref if d == 0 else gateb_ref)[pl.ds(r0, c), :]
                gt = (gtf_ref if d == 0 else gtb_ref)[cidx]
                for hd in range(N_HEADS):
                    units.append((d, hd, qkv_ref, r0, gate, gate, gt, s_scr[s, N_HEADS * d + hd]))
                    where.append((s, d, hd, r0))
        for (s, d, hd, r0), (o, s_new) in zip(where, _delta_units(units)):
            s_scr[s, N_HEADS * d + hd] = s_new
            (of_ref if d == 0 else ob_ref)[s, pl.ds(r0, c), hd * HEAD_DIM:(hd + 1) * HEAD_DIM] = o
        return carry

    lax.fori_loop(0, n_chunk, body, 0)

    if emit_final:
        @pl.when(j == pl.num_programs(1) - 1)
        def _():
            sfin_ref[...] = s_scr[...]


def _delta(qkv, gate, gate_t, n_seq, seq_len, row0, blk, s0):
    nblk = seq_len // blk
    n_chunk = blk // CHUNK
    b0 = row0 // blk
    has_s0 = s0 is not None
    emit_final = not has_s0
    n_state = 2 * N_HEADS
    assert n_seq % DELTA_SEQS == 0 and row0 % blk == 0

    in_specs, args = [], []
    for s in range(DELTA_SEQS):
        def fwd(b, j, s=s):
            return b0 + (b * DELTA_SEQS + s) * nblk + j

        def bwd(b, j, s=s):
            return b0 + (b * DELTA_SEQS + s) * nblk + (nblk - 1 - j)

        in_specs += [
            pl.BlockSpec((blk, 3 * MIX_A), lambda b, j, f=fwd: (f(b, j), 0)),
            pl.BlockSpec((blk, 3 * MIX_A), lambda b, j, f=bwd: (f(b, j), 0)),
            pl.BlockSpec((blk, LANES), lambda b, j, f=fwd: (f(b, j), 0)),
            pl.BlockSpec((blk, LANES), lambda b, j, f=bwd: (f(b, j), 0)),
            pl.BlockSpec((n_chunk, 16, CHUNK), lambda b, j, f=fwd: (f(b, j), 0, 0)),
            pl.BlockSpec((n_chunk, 16, CHUNK), lambda b, j, f=bwd: (f(b, j), 0, 0)),
        ]
        args += [qkv, qkv, gate, gate, gate_t, gate_t]
    state_spec = pl.BlockSpec((DELTA_SEQS, n_state, HEAD_DIM, HEAD_DIM), lambda b, j: (b, 0, 0, 0))
    if has_s0:
        in_specs.append(state_spec)
        args.append(s0)
    out_specs = [pl.BlockSpec((DELTA_SEQS, blk, MIX_A), lambda b, j: (b, j, 0)),
                 pl.BlockSpec((DELTA_SEQS, blk, MIX_A), lambda b, j: (b, nblk - 1 - j, 0))]
    out_shape = [jax.ShapeDtypeStruct((n_seq, seq_len, MIX_A), F32)] * 2
    if emit_final:
        out_specs.append(state_spec)
        out_shape.append(jax.ShapeDtypeStruct((n_seq, n_state, HEAD_DIM, HEAD_DIM), F32))
    outs = pl.pallas_call(
        functools.partial(_delta_kernel, has_s0, emit_final, n_chunk),
        grid=(n_seq // DELTA_SEQS, nblk),
        in_specs=in_specs,
        out_specs=out_specs,
        out_shape=out_shape,
        scratch_shapes=[pltpu.VMEM((DELTA_SEQS, n_state, HEAD_DIM, HEAD_DIM), F32)],
        compiler_params=_cparams(("parallel", "arbitrary")),
        name="delta_latent" if has_s0 else "delta_prompt",
    )(*args)
    return [o.reshape(n_seq * seq_len, MIX_A) for o in outs[:2]] + list(outs[2:])


def _outproj_kernel(lay, xp_ref, xs_ref, pos_ref, mod_ref, opf_ref, opb_ref, osf_ref, osb_ref, z_ref, u_ref,
                    og_ref, n2_ref, wo_ref, wr_ref, br_ref, x1_ref, h2_ref, ids_ref, wts_ref, cnt_ref, cnt_scr):
    i = pl.program_id(0)
    x0 = _load_x(lay, i, xp_ref, xs_ref, pos_ref)
    m = mod_ref[0]
    o = jnp.where(lay.is_sample(i), osf_ref[...] + osb_ref[...], opf_ref[...] + opb_ref[...])
    z = z_ref[...]
    mix = _dot(u_ref[...], wo_ref[MIX_A:, :])
    for hd in range(N_HEADS):
        sl = slice(hd * HEAD_DIM, (hd + 1) * HEAD_DIM)
        oh = _rms(o[:, sl]) * og_ref[...] * _silu(z[:, sl])
        mix = mix + _dot(oh.astype(BF16), wo_ref[sl, :])
    x1 = x0 + m[2:3] * mix
    x1_ref[...] = x1
    h2 = _rms(x1) * n2_ref[...] * (1.0 + m[4:5]) + m[3:4]
    h2b = h2.astype(BF16)
    half = D_MODEL // 2
    hi = pltpu.bitcast(h2b[:, :half].astype(F32), jnp.uint32)
    lo = pltpu.bitcast(h2b[:, half:].astype(F32), jnp.uint32)
    h2_ref[...] = pltpu.bitcast((hi & jnp.uint32(0xFFFF0000)) | (lo >> 16), I32)

    logits = _dot(h2b, wr_ref[...]) + br_ref[...]
    lane = lax.broadcasted_iota(I32, logits.shape, 1)
    neg = jnp.float32(-jnp.inf)
    big = jnp.int32(LANES)
    gl = jnp.where(lane < N_GROUPS, logits, neg)
    gmax = jnp.max(gl, axis=-1, keepdims=True)
    grp = jnp.min(jnp.where(gl == gmax, lane, big), axis=-1, keepdims=True)
    p_grp = 1.0 / jnp.sum(jnp.where(lane < N_GROUPS, jnp.exp(gl - gmax), 0.0), axis=-1, keepdims=True)
    e_lane = lane - N_GROUPS
    in_grp = (e_lane >= grp * EXPERTS_PER_GROUP) & (e_lane < (grp + 1) * EXPERTS_PER_GROUP)
    el = jnp.where(in_grp, logits, neg)
    m1 = jnp.max(el, axis=-1, keepdims=True)
    i1 = jnp.min(jnp.where(el == m1, lane, big), axis=-1, keepdims=True)
    el2 = jnp.where(lane == i1, neg, el)
    m2 = jnp.max(el2, axis=-1, keepdims=True)
    i2 = jnp.min(jnp.where(el2 == m2, lane, big), axis=-1, keepdims=True)
    e2 = jnp.exp(m2 - m1)
    w1 = p_grp / (1.0 + e2)
    w2 = p_grp * e2 / (1.0 + e2)
    wts_ref[...] = jnp.where(lane == 0, w1, jnp.where(lane == 1, w2, 0.0))

    @pl.when(i == 0)
    def _():
        cnt_scr[...] = jnp.zeros(cnt_scr.shape, F32)

    oh1 = lane == i1
    oh2 = lane == i2
    oh = jnp.where(oh1, 1.0, jnp.where(oh2, 1.0, 0.0))
    tm = logits.shape[0]
    rows = lax.broadcasted_iota(I32, (tm, tm), 0)
    cols = lax.broadcasted_iota(I32, (tm, tm), 1)
    before = jnp.where(rows > cols, 1.0, 0.0).astype(BF16)
    seen = cnt_scr[...] + _dot(before, oh.astype(BF16))
    r1 = jnp.sum(jnp.where(oh1, seen, 0.0), axis=-1, keepdims=True).astype(I32)
    r2 = jnp.sum(jnp.where(oh2, seen, 0.0), axis=-1, keepdims=True).astype(I32)
    cnt_scr[...] = cnt_scr[...] + jnp.sum(oh, axis=0, keepdims=True)
    cnt_ref[...] = cnt_scr[...]
    ids_ref[...] = jnp.where(lane == 0, i1 - N_GROUPS,
                             jnp.where(lane == 1, i2 - N_GROUPS,
                                       jnp.where(lane == 2, r1, jnp.where(lane == 3, r2, 0))))


def _outproj(lay, xp, xs, pos, mod, o_pf, o_pb, o_sf, o_sb, z, u, onorm_g, norm2_g, w_out, w_route, b_route):
    t = lay.t
    p_spec = pl.BlockSpec((lay.tile, MIX_A), lambda i: (lay.xp_blk(i), 0))
    s_spec = pl.BlockSpec((lay.tile, MIX_A), lambda i: (lay.xs_blk(i), 0))
    return pl.pallas_call(
        functools.partial(_outproj_kernel, lay),
        grid=(lay.tiles,),
        in_specs=_x_specs(lay) + [_mod_spec(lay), p_spec, p_spec, s_spec, s_spec, _row_spec(lay,MIX_A),
                                  _row_spec(lay,MIX_B), _full_spec((1, HEAD_DIM)), _full_spec((1, D_MODEL)),
                                  _full_spec(w_out.shape), _full_spec(w_route.shape),
                                  _full_spec(b_route.shape)],
        out_specs=[_row_spec(lay,D_MODEL), _row_spec(lay,D_MODEL // 2), _row_spec(lay,LANES), _row_spec(lay,LANES),
                   _full_spec((1, LANES))],
        out_shape=[jax.ShapeDtypeStruct((t, D_MODEL), F32),
                   jax.ShapeDtypeStruct((t, D_MODEL // 2), I32),
                   jax.ShapeDtypeStruct((t, LANES), I32),
                   jax.ShapeDtypeStruct((t, LANES), F32),
                   jax.ShapeDtypeStruct((1, LANES), F32)],
        scratch_shapes=[pltpu.VMEM((1, LANES), F32)],
        compiler_params=_cparams(("arbitrary",)),
        name="outproj",
    )(xp, xs, pos, mod, o_pf, o_pb, o_sf, o_sb, z, u, onorm_g, norm2_g, w_out, w_route, b_route)


def _sc_gather(table, idx, chunk, name):
    n_rows, width = idx.shape[0], table.shape[1]
    mesh = plsc.VectorSubcoreMesh(core_axis_name="c", subcore_axis_name="s")
    n_workers = mesh.num_cores * mesh.num_subcores
    per_worker = n_rows // n_workers
    n_chunks = per_worker // chunk
    assert n_rows == n_workers * n_chunks * chunk and n_chunks % 2 == 0 and chunk % 8 == 0 and chunk <= LANES

    def body(table_hbm, idx_hbm, out_hbm, idx_v, rows_v, sem):
        base = (lax.axis_index("s") * mesh.num_cores + lax.axis_index("c")) * per_worker

        def gather(slot):
            return pltpu.make_async_copy(table_hbm.at[idx_v.at[slot]], rows_v.at[slot], sem.at[slot])

        def fetch(g, slot):
            off = pl.multiple_of(base + g * chunk, 8)
            pltpu.sync_copy(idx_hbm.at[pl.ds(off, chunk)], idx_v.at[slot])
            gather(slot).start()

        for slot in range(2):
            fetch(slot, slot)

        @pl.loop(0, n_chunks, step=2)
        def _(g):
            for slot in range(2):
                off = pl.multiple_of(base + (g + slot) * chunk, 8)
                gather(slot).wait()
                pltpu.sync_copy(rows_v.at[slot], out_hbm.at[pl.ds(off, chunk)])

                @pl.when(g + slot + 2 < n_chunks)
                def _():
                    fetch(g + slot + 2, slot)

    return pl.kernel(
        body,
        out_type=jax.ShapeDtypeStruct((n_rows, width), table.dtype),
        mesh=mesh,
        scratch_types=[pltpu.VMEM((2, chunk), I32), pltpu.VMEM((2, chunk, width), table.dtype),
                       pltpu.SemaphoreType.DMA((2,))],
        name=name,
    )(table, idx)


def _sc_scatter2(src, idx0, idx1, n_out, chunk, name):
    n_rows, width = src.shape
    mesh = plsc.VectorSubcoreMesh(core_axis_name="c", subcore_axis_name="s")
    n_workers = mesh.num_cores * mesh.num_subcores
    per_worker = n_rows // n_workers
    n_chunks = per_worker // chunk
    assert n_rows == n_workers * n_chunks * chunk and n_chunks % 2 == 0 and chunk % 8 == 0 and chunk <= LANES

    def body(src_hbm, i0_hbm, i1_hbm, out_hbm, i0_v, i1_v, rows_v, sem_in, sem_out):
        base = (lax.axis_index("s") * mesh.num_cores + lax.axis_index("c")) * per_worker

        def rows_in(g, slot):
            off = pl.multiple_of(base + g * chunk, 8)
            return pltpu.make_async_copy(src_hbm.at[pl.ds(off, chunk)], rows_v.at[slot], sem_in.at[slot])

        def fetch(g, slot):
            off = pl.multiple_of(base + g * chunk, 8)
            pltpu.sync_copy(i0_hbm.at[pl.ds(off, chunk)], i0_v.at[slot])
            pltpu.sync_copy(i1_hbm.at[pl.ds(off, chunk)], i1_v.at[slot])
            rows_in(g, slot).start()

        for slot in range(2):
            fetch(slot, slot)

        @pl.loop(0, n_chunks, step=2)
        def _(g):
            for slot in range(2):
                rows_in(g + slot, slot).wait()
                puts = [pltpu.make_async_copy(rows_v.at[slot], out_hbm.at[iv.at[slot]], sem_out.at[slot])
                        for iv in (i0_v, i1_v)]
                for put in puts:
                    put.start()
                for put in puts:
                    put.wait()

                @pl.when(g + slot + 2 < n_chunks)
                def _():
                    fetch(g + slot + 2, slot)

    return pl.kernel(
        body,
        out_type=jax.ShapeDtypeStruct((n_out, width), src.dtype),
        mesh=mesh,
        scratch_types=[pltpu.VMEM((2, chunk), I32), pltpu.VMEM((2, chunk), I32),
                       pltpu.VMEM((2, chunk, width), src.dtype),
                       pltpu.SemaphoreType.DMA((2,)), pltpu.SemaphoreType.DMA((2,))],
        name=name,
    )(src, idx0, idx1)


def _expert_kernel(blk_e_ref, nused_ref, xb_ref, wg_ref, wu_ref, wd_ref, yb_ref, wg_s, wu_s, wd_s):
    i = pl.program_id(0)
    nused = nused_ref[0]

    @pl.when(i < nused)
    def _():
        changed = (i == 0) | (blk_e_ref[i] != blk_e_ref[jnp.maximum(i - 1, 0)])

        @pl.when(changed)
        def _():
            wg_s[...] = wg_ref[0].astype(BF16)
            wu_s[...] = wu_ref[0].astype(BF16)
            wd_s[...] = wd_ref[0].astype(BF16)

        x = pltpu.bitcast(xb_ref[...], jnp.uint32)
        half = D_MODEL // 2
        xa = pltpu.bitcast(x & jnp.uint32(0xFFFF0000), F32).astype(BF16)
        xb = pltpu.bitcast(x << 16, F32).astype(BF16)
        g = _dot(xa, wg_s[:half, :]) + _dot(xb, wg_s[half:, :])
        u = _dot(xa, wu_s[:half, :]) + _dot(xb, wu_s[half:, :])
        hmid = (_silu(g) * u).astype(BF16)
        yb_ref[...] = _dot(hmid, wd_s[...])

    @pl.when(i >= nused)
    def _():
        yb_ref[...] = jnp.zeros(yb_ref.shape, F32)


def _experts(xb, blk_e, nused, w_gate, w_up, w_down):
    nb = blk_e.shape[0]
    grid_spec = pltpu.PrefetchScalarGridSpec(
        num_scalar_prefetch=2,
        grid=(nb,),
        in_specs=[
            pl.BlockSpec((ROUTE_BM, D_MODEL // 2), lambda i, be, nu: (i, 0)),
            pl.BlockSpec((1, D_MODEL, D_EXPERT), lambda i, be, nu: (be[i], 0, 0)),
            pl.BlockSpec((1, D_MODEL, D_EXPERT), lambda i, be, nu: (be[i], 0, 0)),
            pl.BlockSpec((1, D_EXPERT, D_MODEL), lambda i, be, nu: (be[i], 0, 0)),
        ],
        out_specs=pl.BlockSpec((ROUTE_BM, D_MODEL), lambda i, be, nu: (i, 0)),
        scratch_shapes=[pltpu.VMEM((D_MODEL, D_EXPERT), BF16),
                        pltpu.VMEM((D_MODEL, D_EXPERT), BF16),
                        pltpu.VMEM((D_EXPERT, D_MODEL), BF16)],
    )
    return pl.pallas_call(
        _expert_kernel,
        grid_spec=grid_spec,
        out_shape=jax.ShapeDtypeStruct((nb * ROUTE_BM, D_MODEL), F32),
        compiler_params=_cparams(("arbitrary",)),
        name="expert",
    )(blk_e, nused, xb, w_gate, w_up, w_down)


def _combine_kernel(lay, y0_ref, y1_ref, x1_ref, wts_ref, mod_ref, fg_ref, yp_ref, ys_ref):
    i = pl.program_id(0)
    m = mod_ref[0]
    w = wts_ref[...]
    ff = y0_ref[...] * w[:, 0:1] + y1_ref[...] * w[:, 1:2]
    y = _rms(x1_ref[...] + m[5:6] * ff) * fg_ref[...]

    @pl.when(i < lay.tiles_p)
    def _():
        yp_ref[...] = y

    @pl.when(i >= lay.tiles_p)
    def _():
        ys_ref[...] = y


def _combine(lay, yg, x1, wts, mod, final_g):
    return pl.pallas_call(
        functools.partial(_combine_kernel, lay),
        grid=(lay.tiles,),
        in_specs=[_row_spec(lay,D_MODEL), pl.BlockSpec((lay.tile, D_MODEL), lambda i: (i + lay.tiles, 0)),
                  _row_spec(lay,D_MODEL), _row_spec(lay,LANES), _mod_spec(lay), _full_spec((1, D_MODEL))],
        out_specs=[pl.BlockSpec((lay.tile, D_MODEL), lambda i: (lay.xp_blk(i), 0)),
                   pl.BlockSpec((lay.tile, D_MODEL), lambda i: (lay.xs_blk(i), 0))],
        out_shape=[jax.ShapeDtypeStruct((lay.t_p, D_MODEL), F32),
                   jax.ShapeDtypeStruct((lay.t_s, D_MODEL), F32)],
        compiler_params=_cparams(("arbitrary",)),
        name="combine",
    )(yg, yg, x1, wts, mod, final_g)


def _dispatch_plan(ids, counts):
    n_tok = ids.shape[0]
    padded = (counts + ROUTE_BM - 1) // ROUTE_BM * ROUTE_BM
    pad_end = jnp.cumsum(padded)
    pad_start = pad_end - padded
    expert = ids[:, 0:2]
    start = jnp.sum(jnp.where(expert[:, :, None] == jnp.arange(N_EXPERTS, dtype=I32), pad_start, 0), axis=-1)
    dest = (start + ids[:, 2:4]).astype(I32)
    nb = -(-(2 * n_tok + N_EXPERTS * (ROUTE_BM - 1)) // ROUTE_BM)
    block_start = jnp.arange(nb, dtype=I32) * ROUTE_BM
    blk_e = jnp.minimum(jnp.sum(pad_end[None, :] <= block_start[:, None], axis=1), N_EXPERTS - 1).astype(I32)
    nused = (pad_end[-1:] // ROUTE_BM).astype(I32)
    return dest[:, 0], dest[:, 1], blk_e, nused


def _grid_pos_embed(n_tokens):
    rows = n_tokens // GRID_W
    r = jnp.repeat(jnp.arange(rows, dtype=F32), GRID_W)
    col = jnp.tile(jnp.arange(GRID_W, dtype=F32), rows)
    n_freq = D_MODEL // 4
    freq = jnp.exp(jnp.arange(n_freq, dtype=F32) * (-math.log(10000.0) / n_freq))

    def enc(p):
        ang = p[:, None] * freq[None, :]
        return jnp.concatenate([jnp.sin(ang), jnp.cos(ang)], axis=-1)

    return jnp.concatenate([enc(r), enc(col)], axis=-1)


def _lane_pad(v, offset):
    return jnp.zeros((1, LANES), F32).at[0, offset:offset + v.shape[0]].set(v.astype(F32))


def kernel(x_prompt, x_sample, state_delta, c, c_ctx, norm1_g, w_mod, b_mod, w_in, conv_qkv_w, A_log, dt_bias, onorm_g, dw_w, dw_b, cln_g, cln_b, w_out, norm2_g, w_group, b_group, w_expert, b_expert, w_e_gate, w_e_up, w_e_down, final_g):
    n_p, l_p, _ = x_prompt.shape
    n_s, l_s, _ = x_sample.shape
    lay = _Layout(n_p, l_p, n_s, l_s, TOK_TILE)
    lay_proj = _Layout(n_p, l_p, n_s, l_s, PROJ_TILE)
    depth = w_in.shape[0]
    assert depth == 1
    xp = x_prompt.reshape(lay.t_p, D_MODEL)
    xs = x_sample.reshape(lay.t_s, D_MODEL)
    pos = _grid_pos_embed(l_s)

    cond = jnp.concatenate([c_ctx[None, :], c], axis=0)
    cond = jnp.pad(cond, ((0, (-cond.shape[0]) % 8), (0, 0)))
    mod = _modulation(cond, w_mod[0], b_mod[0]).reshape(cond.shape[0], 6, D_MODEL)

    n_gate = 4 * N_HEADS
    wi = w_in[0]
    w_main = jnp.concatenate([wi[:, :4 * MIX_A], wi[:, 4 * MIX_A + n_gate:]], axis=1).astype(BF16)
    w_small = jnp.pad(wi[:, 4 * MIX_A:4 * MIX_A + n_gate], ((0, 0), (0, LANES - n_gate))).astype(BF16)
    qkv_raw, z, ug, ba = _inproj(lay_proj, xp, xs, pos, mod, norm1_g[0][None, :], w_main, w_small)

    cw = jnp.pad(conv_qkv_w[0], ((0, 8 - SHORT_CONV), (0, 0))).reshape(8, Q_GROUPS, LANES).transpose(1, 0, 2)
    dw = jnp.pad(dw_w[0], ((0, 32 - CONV_W), (0, 0))).reshape(32, U_GROUPS, LANES).transpose(1, 0, 2)
    alog = _lane_pad(A_log[0].reshape(-1), 2 * N_HEADS)
    dtb = _lane_pad(dt_bias[0].reshape(-1), 2 * N_HEADS)
    qkv, u_conf, gate = _conv(lay, qkv_raw, ug, ba, cw, dw, dw_b[0][None, :], cln_g[0][None, :],
                              cln_b[0][None, :], alog, dtb)

    gate_t = gate[:, :16].reshape(lay.t // CHUNK, CHUNK, 16).transpose(0, 2, 1)
    o_pf, o_pb, s_fin = _delta(qkv, gate, gate_t, n_p, l_p, 0, l_p, None)
    s0 = state_delta[:, 0].reshape(n_s, 2 * N_HEADS, HEAD_DIM, HEAD_DIM)
    o_sf, o_sb = _delta(qkv, gate, gate_t, n_s, l_s, lay.t_p, DELTA_BLK, s0)

    w_route = jnp.pad(jnp.concatenate([w_group[0], w_expert[0]], axis=1),
                      ((0, 0), (0, LANES - N_GROUPS - N_EXPERTS))).astype(BF16)
    b_route = _lane_pad(jnp.concatenate([b_group[0], b_expert[0]]), 0)
    x1, h2p, ids, wts, cnt = _outproj(lay_proj, xp, xs, pos, mod, o_pf, o_pb, o_sf, o_sb, z, u_conf, onorm_g[0][None, :],
                                 norm2_g[0][None, :], w_out[0].astype(BF16), w_route, b_route)

    counts = cnt[0, N_GROUPS:N_GROUPS + N_EXPERTS].astype(I32)
    dest0, dest1, blk_e, nused = _dispatch_plan(ids, counts)
    xb = _sc_scatter2(h2p, dest0, dest1, blk_e.shape[0] * ROUTE_BM, DISPATCH_CHUNK, "dispatch_scatter")
    yb = _experts(xb, blk_e, nused, w_e_gate[0], w_e_up[0], w_e_down[0])
    yg = _sc_gather(yb, jnp.concatenate([dest0, dest1]), COMBINE_CHUNK, "combine_gather")
    y_p, y_s = _combine(lay, yg, x1, wts, mod, final_g[None, :])

    new_state = s_fin.reshape(n_p, 1, 2, N_HEADS, HEAD_DIM, HEAD_DIM)
    return (y_p.reshape(x_prompt.shape), y_s.reshape(x_sample.shape), new_state)
```

```python
import functools
import math

import jax
import jax.numpy as jnp
from jax import lax
from jax.experimental import pallas as pl
from jax.experimental.pallas import tpu as pltpu
from jax.experimental.pallas import tpu_sc as plsc

F32 = jnp.float32
BF16 = jnp.bfloat16
I32 = jnp.int32

D_MODEL = 1024
MIX_A = 512
MIX_B = 512
HEAD_DIM = 128
N_HEADS = 4
SHORT_CONV = 5
CONV_W = 31
N_GROUPS = 4
EXPERTS_PER_GROUP = 8
N_EXPERTS = 32
D_EXPERT = 512
GRID_W = 64
EPS = 1e-6

LANES = 128
TOK_TILE = 256
PROJ_TILE = 512
CHUNK = 128
DELTA_BLK = 512
DELTA_SEQS = 2
ROUTE_BM = 512
DISPATCH_CHUNK = 64
COMBINE_CHUNK = 64
Q_HALO = 8
U_HALO = 16
VMEM_LIMIT = 56 * 1024 * 1024

HIGHEST = lax.Precision.HIGHEST


def _cparams(sem):
    return pltpu.CompilerParams(dimension_semantics=sem, vmem_limit_bytes=VMEM_LIMIT)


def _sigmoid(x):
    return 1.0 / (1.0 + jnp.exp(-x))


def _silu(x):
    return x * _sigmoid(x)


def _softplus(x):
    return jnp.maximum(x, 0.0) + jnp.log1p(jnp.exp(-jnp.abs(x)))


def _rms(x):
    return x * lax.rsqrt(jnp.mean(x * x, axis=-1, keepdims=True) + EPS)


def _pack_bf16_pairs(x):
    h = x.shape[1] // 2
    hi = pltpu.bitcast(x[:, :h].astype(F32), jnp.uint32)
    lo = pltpu.bitcast(x[:, h:].astype(F32), jnp.uint32)
    return pltpu.bitcast((hi & jnp.uint32(0xFFFF0000)) | (lo >> 16), I32)


def _unpack_bf16_pairs(p):
    u = pltpu.bitcast(p, jnp.uint32)
    return pltpu.bitcast(u & jnp.uint32(0xFFFF0000), F32), pltpu.bitcast(u << 16, F32)


def _dot(a, b):
    return jnp.dot(a, b, preferred_element_type=F32)


def _dot_nt(a, b):
    return lax.dot_general(a, b, (((1,), (1,)), ((), ())), preferred_element_type=F32)


def _dot_tn(a, b):
    return lax.dot_general(a, b, (((0,), (0,)), ((), ())), preferred_element_type=F32)


def _mod_kernel(cond_ref, w_ref, b_ref, o_ref):
    s = _silu(cond_ref[...])
    o_ref[...] = jnp.dot(s, w_ref[...], preferred_element_type=F32, precision=HIGHEST) + b_ref[...]


def _modulation(cond, w_mod, b_mod):
    n = cond.shape[0]
    tn = D_MODEL
    return pl.pallas_call(
        _mod_kernel,
        grid=(6 * D_MODEL // tn,),
        in_specs=[pl.BlockSpec((n, D_MODEL), lambda j: (0, 0)),
                  pl.BlockSpec((D_MODEL, tn), lambda j: (0, j)),
                  pl.BlockSpec((1, tn), lambda j: (0, j))],
        out_specs=pl.BlockSpec((n, tn), lambda j: (0, j)),
        out_shape=jax.ShapeDtypeStruct((n, 6 * D_MODEL), F32),
        compiler_params=_cparams(("parallel",)),
        name="mod",
    )(cond, w_mod, b_mod.reshape(1, -1))


class _Layout:
    def __init__(self, n_p, l_p, n_s, l_s, tile):
        self.n_p, self.l_p, self.n_s, self.l_s, self.tile = n_p, l_p, n_s, l_s, tile
        self.t_p = n_p * l_p
        self.t_s = n_s * l_s
        self.t = self.t_p + self.t_s
        assert self.t_p % tile == 0 and l_s % tile == 0 and l_s % DELTA_BLK == 0
        self.tiles_p = self.t_p // tile
        self.tiles_s = self.t_s // tile
        self.tiles = self.tiles_p + self.tiles_s
        self.tps_p = max(l_p // tile, 1)
        self.tps_s = l_s // tile

    def is_sample(self, i):
        return i >= self.tiles_p

    def mod_row(self, i):
        return jnp.where(i < self.tiles_p, 0, 1 + (i - self.tiles_p) // self.tps_s)

    def pos_blk(self, i):
        return jnp.where(i < self.tiles_p, 0, (i - self.tiles_p) % self.tps_s)

    def xp_blk(self, i):
        return jnp.minimum(i, self.tiles_p - 1)

    def xs_blk(self, i):
        return jnp.maximum(i - self.tiles_p, 0)

    def seq_pos(self, i):
        in_s = i >= self.tiles_p
        pos = jnp.where(in_s, (i - self.tiles_p) % self.tps_s, i % self.tps_p)
        n = jnp.where(in_s, self.tps_s, self.tps_p)
        return pos, n


def _load_x(lay, i, xp_ref, xs_ref, prow_ref, pcol_ref):
    n_grid_rows = lay.tile // GRID_W
    half = D_MODEL // 2
    prow = jnp.concatenate([jnp.broadcast_to(prow_ref[r:r + 1, :], (GRID_W, half)) for r in range(n_grid_rows)],
                           axis=0)
    pcol = jnp.concatenate([pcol_ref[...]] * n_grid_rows, axis=0)
    pos = jnp.concatenate([prow, pcol], axis=1)
    return jnp.where(lay.is_sample(i), xs_ref[...] + pos, xp_ref[...])


def _x_specs(lay):
    assert lay.tile % (SUBLANES * GRID_W) == 0
    return [pl.BlockSpec((lay.tile, D_MODEL), lambda i: (lay.xp_blk(i), 0)),
            pl.BlockSpec((lay.tile, D_MODEL), lambda i: (lay.xs_blk(i), 0)),
            pl.BlockSpec((lay.tile // GRID_W, D_MODEL // 2), lambda i: (lay.pos_blk(i), 0)),
            _full_spec((GRID_W, D_MODEL // 2))]


def _mod_spec(lay):
    return pl.BlockSpec((1, 6, D_MODEL), lambda i: (lay.mod_row(i), 0, 0))


def _row_spec(lay, width):
    return pl.BlockSpec((lay.tile, width), lambda i: (i, 0))


def _full_spec(shape):
    nd = len(shape)
    return pl.BlockSpec(shape, lambda i: (0,) * nd)


def _inproj_kernel(lay, xp_ref, xs_ref, prow_ref, pcol_ref, mod_ref, g_ref, wm_ref, ws_ref,
                   qkv_ref, z_ref, ug_ref, ba_ref):
    i = pl.program_id(0)
    x = _load_x(lay, i, xp_ref, xs_ref, prow_ref, pcol_ref)
    m = mod_ref[0]
    h = _rms(x) * g_ref[...] * (1.0 + m[1:2]) + m[0:1]
    hb = h.astype(BF16)
    qkv_ref[...] = _dot(hb, wm_ref[:, 0:3 * MIX_A])
    z_ref[...] = _dot(hb, wm_ref[:, 3 * MIX_A:4 * MIX_A])
    glu = _dot(hb, wm_ref[:, 4 * MIX_A:4 * MIX_A + 2 * MIX_B])
    ug_ref[...] = glu[:, :MIX_B] * _sigmoid(glu[:, MIX_B:])
    ba_ref[...] = _dot(hb, ws_ref[...])


def _inproj(lay, xp, xs, pos, mod, norm_g, w_main, w_small):
    t = lay.t
    return pl.pallas_call(
        functools.partial(_inproj_kernel, lay),
        grid=(lay.tiles,),
        in_specs=_x_specs(lay) + [_mod_spec(lay), _full_spec((1, D_MODEL)),
                                  _full_spec(w_main.shape), _full_spec(w_small.shape)],
        out_specs=[_row_spec(lay,3 * MIX_A), _row_spec(lay,MIX_A), _row_spec(lay,MIX_B), _row_spec(lay,LANES)],
        out_shape=[jax.ShapeDtypeStruct((t, 3 * MIX_A), F32),
                   jax.ShapeDtypeStruct((t, MIX_A), F32),
                   jax.ShapeDtypeStruct((t, MIX_B), F32),
                   jax.ShapeDtypeStruct((t, LANES), F32)],
        compiler_params=_cparams(("parallel",)),
        name="inproj",
    )(xp, xs, *pos, mod, norm_g, w_main, w_small)


SUBLANES = 8
CONV_PITCH = 33
CONV_OUT_ROWS = SUBLANES * CONV_PITCH
POST_ROWS = 64
Q_EXT_ROWS = 280
U_EXT_ROWS = 296
Q_GROUPS = 3 * MIX_A // LANES
U_GROUPS = MIX_B // LANES


def _strided_conv(ext_scr, res_scr, w_ref, g, off, n_taps, j_block):
    for j0 in range(0, CONV_PITCH, j_block):
        js = range(j0, min(j0 + j_block, CONV_PITCH))
        v = {m: ext_scr[g, pl.ds(off + m, SUBLANES, stride=CONV_PITCH), :]
             for m in range(js[0], js[-1] + n_taps)}
        for j in js:
            acc = v[j] * w_ref[g, 0:1, :]
            for s in range(1, n_taps):
                acc = acc + v[j + s] * w_ref[g, s:s + 1, :]
            res_scr[g, pl.ds(j, SUBLANES, stride=CONV_PITCH), :] = acc


def _conv_kernel(lay, qc_ref, qp_ref, qn_ref, uc_ref, up_ref, un_ref, ba_ref, cw_ref, dw_ref,
                 dwb_ref, lng_ref, lnb_ref, alog_ref, dtb_ref,
                 qkv_ref, uo_ref, gate_ref, eq_scr, eu_scr, rq_scr, ru_scr):
    i = pl.program_id(0)
    pos, n = lay.seq_pos(i)
    has_prev = pos != 0
    has_next = pos != n - 1
    qp = jnp.where(has_prev, qp_ref[...], 0.0)
    qn = jnp.where(has_next, qn_ref[...], 0.0)
    for g in range(Q_GROUPS):
        sl = slice(g * LANES, (g + 1) * LANES)
        eq_scr[g, 0:Q_HALO, :] = qp[:, sl]
        eq_scr[g, Q_HALO:Q_HALO + TOK_TILE, :] = qc_ref[:, sl]
        eq_scr[g, Q_HALO + TOK_TILE:Q_HALO + TOK_TILE + Q_HALO, :] = qn[:, sl]
        eq_scr[g, TOK_TILE + 2 * Q_HALO:, :] = jnp.zeros((Q_EXT_ROWS - TOK_TILE - 2 * Q_HALO, LANES), F32)
    up = jnp.where(has_prev, up_ref[...], 0.0)
    un_ = jnp.where(has_next, un_ref[...], 0.0)
    for g in range(U_GROUPS):
        sl = slice(g * LANES, (g + 1) * LANES)
        eu_scr[g, 0:U_HALO, :] = up[:, sl]
        eu_scr[g, U_HALO:U_HALO + TOK_TILE, :] = uc_ref[:, sl]
        eu_scr[g, U_HALO + TOK_TILE:U_HALO + TOK_TILE + U_HALO, :] = un_[:, sl]
        eu_scr[g, TOK_TILE + 2 * U_HALO:, :] = jnp.zeros((U_EXT_ROWS - TOK_TILE - 2 * U_HALO, LANES), F32)

    def q_group(g, carry):
        _strided_conv(eq_scr, rq_scr, cw_ref, g, Q_HALO - SHORT_CONV // 2, SHORT_CONV, 11)
        return carry

    def u_group(g, carry):
        _strided_conv(eu_scr, ru_scr, dw_ref, g, U_HALO - CONV_W // 2, CONV_W, 11)
        return carry

    lax.fori_loop(0, Q_GROUPS, q_group, 0)
    lax.fori_loop(0, U_GROUPS, u_group, 0)

    for rc in range(TOK_TILE // POST_ROWS):
        r0 = rc * POST_ROWS
        for g in range(Q_GROUPS):
            y = _silu(rq_scr[g, r0:r0 + POST_ROWS, :])
            if g < 2 * N_HEADS:
                y = y * lax.rsqrt(jnp.sum(y * y, axis=-1, keepdims=True) + EPS)
            qkv_ref[r0:r0 + POST_ROWS, g * LANES:(g + 1) * LANES] = y
        u = jnp.concatenate([ru_scr[g, r0:r0 + POST_ROWS, :] for g in range(U_GROUPS)], axis=1) + dwb_ref[...]
        uc = u - jnp.mean(u, axis=-1, keepdims=True)
        un = uc * lax.rsqrt(jnp.mean(uc * uc, axis=-1, keepdims=True) + EPS)
        uo_ref[r0:r0 + POST_ROWS, :] = _silu(un * lng_ref[...] + lnb_ref[...]).astype(BF16)

    x = ba_ref[...]
    g = -jnp.exp(alog_ref[...]) * _softplus(x + dtb_ref[...])
    beta = _sigmoid(x)
    rows = lax.broadcasted_iota(I32, (CHUNK, CHUNK), 0)
    cols = lax.broadcasted_iota(I32, (CHUNK, CHUNK), 1)
    tri_lo = (rows >= cols).astype(F32)
    tri_up = (rows <= cols).astype(F32)
    lane = lax.broadcasted_iota(I32, (CHUNK, LANES), 1)
    for ch in range(TOK_TILE // CHUNK):
        sl = slice(ch * CHUNK, (ch + 1) * CHUNK)
        pre = jnp.dot(tri_lo, g[sl], preferred_element_type=F32, precision=HIGHEST)
        suf = jnp.dot(tri_up, g[sl], preferred_element_type=F32, precision=HIGHEST)
        gate_ref[sl, :] = jnp.where(lane < 2 * N_HEADS, beta[sl], jnp.where(lane < 3 * N_HEADS, pre, suf))


def _conv(lay, qkv_raw, ug, ba, cw, dw, dwb, lng, lnb, alog, dtb):
    t = lay.t
    qh = TOK_TILE // Q_HALO
    uh = TOK_TILE // U_HALO
    n_qh = t // Q_HALO
    n_uh = t // U_HALO
    in_specs = [
        _row_spec(lay,3 * MIX_A),
        pl.BlockSpec((Q_HALO, 3 * MIX_A), lambda i: (jnp.maximum(i * qh - 1, 0), 0)),
        pl.BlockSpec((Q_HALO, 3 * MIX_A), lambda i: (jnp.minimum((i + 1) * qh, n_qh - 1), 0)),
        _row_spec(lay,MIX_B),
        pl.BlockSpec((U_HALO, MIX_B), lambda i: (jnp.maximum(i * uh - 1, 0), 0)),
        pl.BlockSpec((U_HALO, MIX_B), lambda i: (jnp.minimum((i + 1) * uh, n_uh - 1), 0)),
        _row_spec(lay,LANES),
        _full_spec(cw.shape), _full_spec(dw.shape), _full_spec(dwb.shape),
        _full_spec(lng.shape), _full_spec(lnb.shape), _full_spec(alog.shape), _full_spec(dtb.shape),
    ]
    return pl.pallas_call(
        functools.partial(_conv_kernel, lay),
        grid=(lay.tiles,),
        in_specs=in_specs,
        out_specs=[_row_spec(lay,3 * MIX_A), _row_spec(lay,MIX_B), _row_spec(lay,LANES)],
        out_shape=[jax.ShapeDtypeStruct((t, 3 * MIX_A), F32),
                   jax.ShapeDtypeStruct((t, MIX_B), BF16),
                   jax.ShapeDtypeStruct((t, LANES), F32)],
        scratch_shapes=[pltpu.VMEM((Q_GROUPS, Q_EXT_ROWS, LANES), F32),
                        pltpu.VMEM((U_GROUPS, U_EXT_ROWS, LANES), F32),
                        pltpu.VMEM((Q_GROUPS, CONV_OUT_ROWS, LANES), F32),
                        pltpu.VMEM((U_GROUPS, CONV_OUT_ROWS, LANES), F32)],
        compiler_params=_cparams(("parallel",)),
        name="conv",
    )(qkv_raw, qkv_raw, qkv_raw, ug, ug, ug, ba, cw, dw, dwb, lng, lnb, alog, dtb)


INV_BASE = 8


def _b16(xs):
    return [x.astype(BF16) for x in xs]


def _tri_inverse_minus_eye(nmats, rows, cols):
    assert INV_BASE == 8
    c = nmats[0].shape[0]
    shift = int(math.log2(INV_BASE))
    same = (rows >> shift) == (cols >> shift)
    n1 = [jnp.where(same, n, 0.0) for n in nmats]
    n1b = _b16(n1)
    n2 = [_dot(x, x) for x in n1b]
    n2b = _b16(n2)
    r = [_dot(jnp.concatenate([a, b], axis=0), b) for a, b in zip(n1b, n2b)]
    q = [a + b + x[:c] for a, b, x in zip(n1, n2, r)]
    n4 = [x[c:] for x in r]
    qn4 = [_dot(a, b) for a, b in zip(_b16(q), _b16(n4))]
    q = [a + b + x for a, b, x in zip(q, n4, qn4)]
    while (1 << shift) < c:
        off = ((rows >> (shift + 1)) == (cols >> (shift + 1))) & ((rows >> shift) != (cols >> shift))
        a = [jnp.where(off, -n, 0.0) for n in nmats]
        x = [ai + _dot(ab, qb) for ai, ab, qb in zip(a, _b16(a), _b16(q))]
        qx = [_dot(qb, xb) for qb, xb in zip(_b16(q), _b16(x))]
        q = [qi - xi - qxi for qi, xi, qxi in zip(q, x, qx)]
        shift += 1
    return q


def _delta_units(units):
    c = CHUNK
    rows = lax.broadcasted_iota(I32, (c, c), 0)
    cols = lax.broadcasted_iota(I32, (c, c), 1)
    pre = []
    for d, hd, qkv_ref, r0, gate, gcol, grow, s in units:
        incl = (rows >= cols) if d == 0 else (rows <= cols)
        lane = 2 * N_HEADS + N_HEADS * d + hd
        beta = gate[:, N_HEADS * d + hd:N_HEADS * d + hd + 1]
        gc = gcol[:, lane:lane + 1]
        gr = grow[lane:lane + 1, :]
        last = c - 1 if d == 0 else 0
        gtot = gcol[last:last + 1, lane:lane + 1]
        dec = jnp.where(incl, jnp.exp(jnp.where(incl, gc - gr, 0.0)), 0.0)
        q = qkv_ref[pl.ds(r0, c), hd * HEAD_DIM:(hd + 1) * HEAD_DIM]
        k = qkv_ref[pl.ds(r0, c), MIX_A + hd * HEAD_DIM:MIX_A + (hd + 1) * HEAD_DIM]
        v = qkv_ref[pl.ds(r0, c), 2 * MIX_A + hd * HEAD_DIM:2 * MIX_A + (hd + 1) * HEAD_DIM]
        qs = q * (HEAD_DIM ** -0.5)
        kbeta = k * beta
        egc = jnp.exp(gc)
        pre.append(dict(d=d, beta=beta, gc=gc, gtot=gtot, dec=dec, k=k, v=v, qs=qs, kbeta=kbeta, egc=egc, s=s))

    ak = [_dot_nt(jnp.concatenate([p['kbeta'], p['qs']], axis=0).astype(BF16), p['k'].astype(BF16))
          for p in pre]
    nmats, qks = [], []
    for p, a in zip(pre, ak):
        strict = (rows > cols) if p['d'] == 0 else (rows < cols)
        nmats.append(jnp.where(strict, -a[:c] * p['dec'], 0.0))
        qks.append(a[c:] * p['dec'])
    qinv = _tri_inverse_minus_eye(nmats, rows, cols)

    rhs = [jnp.concatenate([p['v'] * p['beta'], p['kbeta'] * p['egc']], axis=1) for p in pre]
    uw = [r + _dot(qi, rb) for r, qi, rb in zip(rhs, _b16(qinv), _b16(rhs))]
    sw = [_dot(jnp.concatenate([x[:, HEAD_DIM:], p['qs'] * p['egc']], axis=0).astype(BF16), p['s'].astype(BF16))
          for x, p in zip(uw, pre)]
    vnb = [(x[:, :HEAD_DIM] - y[:c]).astype(BF16) for x, y in zip(uw, sw)]
    o = [y[c:] + _dot(qk, vb) for y, qk, vb in zip(sw, _b16(qks), vnb)]
    kdec = [(p['k'] * jnp.exp(p['gtot'] - p['gc'])).astype(BF16) for p in pre]
    s_new = [p['s'] * jnp.exp(p['gtot']) + _dot_tn(kd, vb) for p, kd, vb in zip(pre, kdec, vnb)]
    return list(zip(o, s_new))


def _delta_kernel(has_s0, emit_final, one_block, n_chunk, *refs):
    refs = list(refs)
    if one_block:
        seq_in = [[r for r in refs[3 * s:3 * s + 3] for _ in range(2)] for s in range(DELTA_SEQS)]
        refs = refs[3 * DELTA_SEQS:]
    else:
        seq_in = [refs[6 * s:6 * s + 6] for s in range(DELTA_SEQS)]
        refs = refs[6 * DELTA_SEQS:]
    s0_ref = refs.pop(0) if has_s0 else None
    of_ref, ob_ref = refs[:2]
    refs = refs[2:]
    sfin_ref = refs.pop(0) if emit_final else None
    s_scr = refs[0]

    j = pl.program_id(1)

    @pl.when(j == 0)
    def _():
        if has_s0:
            s_scr[...] = s0_ref[...]
        else:
            s_scr[...] = jnp.zeros(s_scr.shape, F32)

    c = CHUNK

    def body(ci, carry):
        units, where = [], []
        for s, (qkvf_ref, qkvb_ref, gatef_ref, gateb_ref, gtf_ref, gtb_ref) in enumerate(seq_in):
            for d in range(2):
                cidx = ci if d == 0 else n_chunk - 1 - ci
                r0 = pl.multiple_of(cidx * c, c)
                qkv_ref = qkvf_ref if d == 0 else qkvb_ref
                gate = (gatef_ref if d == 0 else gateb_ref)[pl.ds(r0, c), :]
                gt = (gtf_ref if d == 0 else gtb_ref)[cidx]
                for hd in range(N_HEADS):
                    units.append((d, hd, qkv_ref, r0, gate, gate, gt, s_scr[s, N_HEADS * d + hd]))
                    where.append((s, d, hd, r0))
        for (s, d, hd, r0), (o, s_new) in zip(where, _delta_units(units)):
            s_scr[s, N_HEADS * d + hd] = s_new
            (of_ref if d == 0 else ob_ref)[s, pl.ds(r0, c), hd * HEAD_DIM:(hd + 1) * HEAD_DIM] = o
        return carry

    lax.fori_loop(0, n_chunk, body, 0)

    if emit_final:
        @pl.when(j == pl.num_programs(1) - 1)
        def _():
            sfin_ref[...] = s_scr[...]


def _delta(qkv, gate, gate_t, n_seq, seq_len, row0, blk, s0):
    nblk = seq_len // blk
    n_chunk = blk // CHUNK
    b0 = row0 // blk
    has_s0 = s0 is not None
    emit_final = not has_s0
    n_state = 2 * N_HEADS
    one_block = nblk == 1
    assert n_seq % DELTA_SEQS == 0 and row0 % blk == 0

    in_specs, args = [], []
    for s in range(DELTA_SEQS):
        def fwd(b, j, s=s):
            return b0 + (b * DELTA_SEQS + s) * nblk + j

        def bwd(b, j, s=s):
            return b0 + (b * DELTA_SEQS + s) * nblk + (nblk - 1 - j)

        for arr, shape in ((qkv, (blk, 3 * MIX_A)), (gate, (blk, LANES)), (gate_t, (n_chunk, 16, CHUNK))):
            tail = (0,) * (len(shape) - 1)
            for f in (fwd,) if one_block else (fwd, bwd):
                in_specs.append(pl.BlockSpec(shape, lambda b, j, f=f, tail=tail: (f(b, j),) + tail))
                args.append(arr)
    state_spec = pl.BlockSpec((DELTA_SEQS, n_state, HEAD_DIM, HEAD_DIM), lambda b, j: (b, 0, 0, 0))
    if has_s0:
        in_specs.append(state_spec)
        args.append(s0)
    out_specs = [pl.BlockSpec((DELTA_SEQS, blk, MIX_A), lambda b, j: (b, j, 0)),
                 pl.BlockSpec((DELTA_SEQS, blk, MIX_A), lambda b, j: (b, nblk - 1 - j, 0))]
    out_shape = [jax.ShapeDtypeStruct((n_seq, seq_len, MIX_A), F32)] * 2
    if emit_final:
        out_specs.append(state_spec)
        out_shape.append(jax.ShapeDtypeStruct((n_seq, n_state, HEAD_DIM, HEAD_DIM), F32))
    outs = pl.pallas_call(
        functools.partial(_delta_kernel, has_s0, emit_final, one_block, n_chunk),
        grid=(n_seq // DELTA_SEQS, nblk),
        in_specs=in_specs,
        out_specs=out_specs,
        out_shape=out_shape,
        scratch_shapes=[pltpu.VMEM((DELTA_SEQS, n_state, HEAD_DIM, HEAD_DIM), F32)],
        compiler_params=_cparams(("parallel", "arbitrary")),
        name="delta_latent" if has_s0 else "delta_prompt",
    )(*args)
    return [o.reshape(n_seq * seq_len, MIX_A) for o in outs[:2]] + list(outs[2:])


def _outproj_kernel(lay, xp_ref, xs_ref, prow_ref, pcol_ref, mod_ref, opf_ref, opb_ref, osf_ref, osb_ref, z_ref,
                    u_ref, og_ref, n2_ref, wo_ref, wr_ref, br_ref, x1_ref, h2_ref, ids_ref, wts_ref, cnt_ref,
                    cnt_scr):
    i = pl.program_id(0)
    x0 = _load_x(lay, i, xp_ref, xs_ref, prow_ref, pcol_ref)
    m = mod_ref[0]
    o = jnp.where(lay.is_sample(i), osf_ref[...] + osb_ref[...], opf_ref[...] + opb_ref[...])
    z = z_ref[...]
    mix = _dot(u_ref[...], wo_ref[MIX_A:, :])
    for hd in range(N_HEADS):
        sl = slice(hd * HEAD_DIM, (hd + 1) * HEAD_DIM)
        oh = _rms(o[:, sl]) * og_ref[...] * _silu(z[:, sl])
        mix = mix + _dot(oh.astype(BF16), wo_ref[sl, :])
    x1 = x0 + m[2:3] * mix
    x1_ref[...] = x1
    h2 = _rms(x1) * n2_ref[...] * (1.0 + m[4:5]) + m[3:4]
    h2b = h2.astype(BF16)
    h2_ref[...] = _pack_bf16_pairs(h2b)

    logits = _dot(h2b, wr_ref[...]) + br_ref[...]
    lane = lax.broadcasted_iota(I32, logits.shape, 1)
    neg = jnp.float32(-jnp.inf)
    big = jnp.int32(LANES)
    gl = jnp.where(lane < N_GROUPS, logits, neg)
    gmax = jnp.max(gl, axis=-1, keepdims=True)
    grp = jnp.min(jnp.where(gl == gmax, lane, big), axis=-1, keepdims=True)
    p_grp = 1.0 / jnp.sum(jnp.where(lane < N_GROUPS, jnp.exp(gl - gmax), 0.0), axis=-1, keepdims=True)
    e_lane = lane - N_GROUPS
    in_grp = (e_lane >= grp * EXPERTS_PER_GROUP) & (e_lane < (grp + 1) * EXPERTS_PER_GROUP)
    el = jnp.where(in_grp, logits, neg)
    m1 = jnp.max(el, axis=-1, keepdims=True)
    i1 = jnp.min(jnp.where(el == m1, lane, big), axis=-1, keepdims=True)
    el2 = jnp.where(lane == i1, neg, el)
    m2 = jnp.max(el2, axis=-1, keepdims=True)
    i2 = jnp.min(jnp.where(el2 == m2, lane, big), axis=-1, keepdims=True)
    e2 = jnp.exp(m2 - m1)
    w1 = p_grp / (1.0 + e2)
    w2 = p_grp * e2 / (1.0 + e2)
    wts_ref[...] = jnp.where(lane == 0, w1, jnp.where(lane == 1, w2, 0.0))

    @pl.when(i == 0)
    def _():
        cnt_scr[...] = jnp.zeros(cnt_scr.shape, F32)

    oh1 = lane == i1
    oh2 = lane == i2
    oh = jnp.where(oh1, 1.0, jnp.where(oh2, 1.0, 0.0))
    tm = logits.shape[0]
    rows = lax.broadcasted_iota(I32, (tm, tm), 0)
    cols = lax.broadcasted_iota(I32, (tm, tm), 1)
    before = jnp.where(rows > cols, 1.0, 0.0).astype(BF16)
    seen = cnt_scr[...] + _dot(before, oh.astype(BF16))
    r1 = jnp.sum(jnp.where(oh1, seen, 0.0), axis=-1, keepdims=True).astype(I32)
    r2 = jnp.sum(jnp.where(oh2, seen, 0.0), axis=-1, keepdims=True).astype(I32)
    cnt_scr[...] = cnt_scr[...] + jnp.sum(oh, axis=0, keepdims=True)
    cnt_ref[...] = cnt_scr[...]
    ids_ref[...] = jnp.where(lane == 0, i1 - N_GROUPS,
                             jnp.where(lane == 1, i2 - N_GROUPS,
                                       jnp.where(lane == 2, r1, jnp.where(lane == 3, r2, 0))))


def _outproj(lay, xp, xs, pos, mod, o_pf, o_pb, o_sf, o_sb, z, u, onorm_g, norm2_g, w_out, w_route, b_route):
    t = lay.t
    p_spec = pl.BlockSpec((lay.tile, MIX_A), lambda i: (lay.xp_blk(i), 0))
    s_spec = pl.BlockSpec((lay.tile, MIX_A), lambda i: (lay.xs_blk(i), 0))
    return pl.pallas_call(
        functools.partial(_outproj_kernel, lay),
        grid=(lay.tiles,),
        in_specs=_x_specs(lay) + [_mod_spec(lay), p_spec, p_spec, s_spec, s_spec, _row_spec(lay,MIX_A),
                                  _row_spec(lay,MIX_B), _full_spec((1, HEAD_DIM)), _full_spec((1, D_MODEL)),
                                  _full_spec(w_out.shape), _full_spec(w_route.shape),
                                  _full_spec(b_route.shape)],
        out_specs=[_row_spec(lay,D_MODEL), _row_spec(lay,D_MODEL // 2), _row_spec(lay,LANES), _row_spec(lay,LANES),
                   _full_spec((1, LANES))],
        out_shape=[jax.ShapeDtypeStruct((t, D_MODEL), F32),
                   jax.ShapeDtypeStruct((t, D_MODEL // 2), I32),
                   jax.ShapeDtypeStruct((t, LANES), I32),
                   jax.ShapeDtypeStruct((t, LANES), F32),
                   jax.ShapeDtypeStruct((1, LANES), F32)],
        scratch_shapes=[pltpu.VMEM((1, LANES), F32)],
        compiler_params=_cparams(("arbitrary",)),
        name="outproj",
    )(xp, xs, *pos, mod, o_pf, o_pb, o_sf, o_sb, z, u, onorm_g, norm2_g, w_out, w_route, b_route)


def _sc_gather(table, idx, chunk, name):
    n_rows, width = idx.shape[0], table.shape[1]
    mesh = plsc.VectorSubcoreMesh(core_axis_name="c", subcore_axis_name="s")
    n_workers = mesh.num_cores * mesh.num_subcores
    per_worker = n_rows // n_workers
    n_chunks = per_worker // chunk
    assert n_rows == n_workers * n_chunks * chunk and n_chunks % 2 == 0 and chunk % 8 == 0 and chunk <= LANES

    def body(table_hbm, idx_hbm, out_hbm, idx_v, rows_v, sem):
        base = (lax.axis_index("s") * mesh.num_cores + lax.axis_index("c")) * per_worker

        def gather(slot):
            return pltpu.make_async_copy(table_hbm.at[idx_v.at[slot]], rows_v.at[slot], sem.at[slot])

        def fetch(g, slot):
            off = pl.multiple_of(base + g * chunk, 8)
            pltpu.sync_copy(idx_hbm.at[pl.ds(off, chunk)], idx_v.at[slot])
            gather(slot).start()

        for slot in range(2):
            fetch(slot, slot)

        @pl.loop(0, n_chunks, step=2)
        def _(g):
            for slot in range(2):
                off = pl.multiple_of(base + (g + slot) * chunk, 8)
                gather(slot).wait()
                pltpu.sync_copy(rows_v.at[slot], out_hbm.at[pl.ds(off, chunk)])

                @pl.when(g + slot + 2 < n_chunks)
                def _():
                    fetch(g + slot + 2, slot)

    return pl.kernel(
        body,
        out_type=jax.ShapeDtypeStruct((n_rows, width), table.dtype),
        mesh=mesh,
        scratch_types=[pltpu.VMEM((2, chunk), I32), pltpu.VMEM((2, chunk, width), table.dtype),
                       pltpu.SemaphoreType.DMA((2,))],
        name=name,
    )(table, idx)


def _sc_scatter2(src, idx0, idx1, n_out, chunk, name):
    n_rows, width = src.shape
    mesh = plsc.VectorSubcoreMesh(core_axis_name="c", subcore_axis_name="s")
    n_workers = mesh.num_cores * mesh.num_subcores
    per_worker = n_rows // n_workers
    n_chunks = per_worker // chunk
    assert n_rows == n_workers * n_chunks * chunk and n_chunks % 2 == 0 and chunk % 8 == 0 and chunk <= LANES

    def body(src_hbm, i0_hbm, i1_hbm, out_hbm, i0_v, i1_v, rows_v, sem_in, sem_out):
        base = (lax.axis_index("s") * mesh.num_cores + lax.axis_index("c")) * per_worker

        def rows_in(g, slot):
            off = pl.multiple_of(base + g * chunk, 8)
            return pltpu.make_async_copy(src_hbm.at[pl.ds(off, chunk)], rows_v.at[slot], sem_in.at[slot])

        def fetch(g, slot):
            off = pl.multiple_of(base + g * chunk, 8)
            pltpu.sync_copy(i0_hbm.at[pl.ds(off, chunk)], i0_v.at[slot])
            pltpu.sync_copy(i1_hbm.at[pl.ds(off, chunk)], i1_v.at[slot])
            rows_in(g, slot).start()

        for slot in range(2):
            fetch(slot, slot)

        @pl.loop(0, n_chunks, step=2)
        def _(g):
            for slot in range(2):
                rows_in(g + slot, slot).wait()
                puts = [pltpu.make_async_copy(rows_v.at[slot], out_hbm.at[iv.at[slot]], sem_out.at[slot])
                        for iv in (i0_v, i1_v)]
                for put in puts:
                    put.start()
                for put in puts:
                    put.wait()

                @pl.when(g + slot + 2 < n_chunks)
                def _():
                    fetch(g + slot + 2, slot)

    return pl.kernel(
        body,
        out_type=jax.ShapeDtypeStruct((n_out, width), src.dtype),
        mesh=mesh,
        scratch_types=[pltpu.VMEM((2, chunk), I32), pltpu.VMEM((2, chunk), I32),
                       pltpu.VMEM((2, chunk, width), src.dtype),
                       pltpu.SemaphoreType.DMA((2,)), pltpu.SemaphoreType.DMA((2,))],
        name=name,
    )(src, idx0, idx1)


def _expert_kernel(blk_e_ref, nused_ref, xb_ref, wg_ref, wu_ref, wd_ref, yb_ref, wg_s, wu_s, wd_s):
    i = pl.program_id(0)
    nused = nused_ref[0]

    @pl.when(i < nused)
    def _():
        changed = (i == 0) | (blk_e_ref[i] != blk_e_ref[jnp.maximum(i - 1, 0)])

        @pl.when(changed)
        def _():
            wg_s[...] = wg_ref[0].astype(BF16)
            wu_s[...] = wu_ref[0].astype(BF16)
            wd_s[...] = wd_ref[0].astype(BF16)

        half = D_MODEL // 2
        xa, xb = (v.astype(BF16) for v in _unpack_bf16_pairs(xb_ref[...]))
        g = _dot(xa, wg_s[:half, :]) + _dot(xb, wg_s[half:, :])
        u = _dot(xa, wu_s[:half, :]) + _dot(xb, wu_s[half:, :])
        hmid = (_silu(g) * u).astype(BF16)
        yb_ref[...] = _pack_bf16_pairs(_dot(hmid, wd_s[...]).astype(BF16))

    @pl.when(i >= nused)
    def _():
        yb_ref[...] = jnp.zeros(yb_ref.shape, I32)


def _experts(xb, blk_e, nused, w_gate, w_up, w_down):
    nb = blk_e.shape[0]
    grid_spec = pltpu.PrefetchScalarGridSpec(
        num_scalar_prefetch=2,
        grid=(nb,),
        in_specs=[
            pl.BlockSpec((ROUTE_BM, D_MODEL // 2), lambda i, be, nu: (i, 0)),
            pl.BlockSpec((1, D_MODEL, D_EXPERT), lambda i, be, nu: (be[i], 0, 0)),
            pl.BlockSpec((1, D_MODEL, D_EXPERT), lambda i, be, nu: (be[i], 0, 0)),
            pl.BlockSpec((1, D_EXPERT, D_MODEL), lambda i, be, nu: (be[i], 0, 0)),
        ],
        out_specs=pl.BlockSpec((ROUTE_BM, D_MODEL // 2), lambda i, be, nu: (i, 0)),
        scratch_shapes=[pltpu.VMEM((D_MODEL, D_EXPERT), BF16),
                        pltpu.VMEM((D_MODEL, D_EXPERT), BF16),
                        pltpu.VMEM((D_EXPERT, D_MODEL), BF16)],
    )
    return pl.pallas_call(
        _expert_kernel,
        grid_spec=grid_spec,
        out_shape=jax.ShapeDtypeStruct((nb * ROUTE_BM, D_MODEL // 2), I32),
        compiler_params=_cparams(("arbitrary",)),
        name="expert",
    )(blk_e, nused, xb, w_gate, w_up, w_down)


def _combine_kernel(lay, y0_ref, y1_ref, x1_ref, wts_ref, mod_ref, fg_ref, yp_ref, ys_ref):
    i = pl.program_id(0)
    m = mod_ref[0]
    w = wts_ref[...]
    a0, b0 = _unpack_bf16_pairs(y0_ref[...])
    a1, b1 = _unpack_bf16_pairs(y1_ref[...])
    ff = jnp.concatenate([a0 * w[:, 0:1] + a1 * w[:, 1:2], b0 * w[:, 0:1] + b1 * w[:, 1:2]], axis=1)
    y = _rms(x1_ref[...] + m[5:6] * ff) * fg_ref[...]

    @pl.when(i < lay.tiles_p)
    def _():
        yp_ref[...] = y

    @pl.when(i >= lay.tiles_p)
    def _():
        ys_ref[...] = y


def _combine(lay, yg, x1, wts, mod, final_g):
    return pl.pallas_call(
        functools.partial(_combine_kernel, lay),
        grid=(lay.tiles,),
        in_specs=[_row_spec(lay, D_MODEL // 2),
                  pl.BlockSpec((lay.tile, D_MODEL // 2), lambda i: (i + lay.tiles, 0)),
                  _row_spec(lay,D_MODEL), _row_spec(lay,LANES), _mod_spec(lay), _full_spec((1, D_MODEL))],
        out_specs=[pl.BlockSpec((lay.tile, D_MODEL), lambda i: (lay.xp_blk(i), 0)),
                   pl.BlockSpec((lay.tile, D_MODEL), lambda i: (lay.xs_blk(i), 0))],
        out_shape=[jax.ShapeDtypeStruct((lay.t_p, D_MODEL), F32),
                   jax.ShapeDtypeStruct((lay.t_s, D_MODEL), F32)],
        compiler_params=_cparams(("arbitrary",)),
        name="combine",
    )(yg, yg, x1, wts, mod, final_g)


def _dispatch_plan(ids, counts):
    n_tok = ids.shape[0]
    padded = (counts + ROUTE_BM - 1) // ROUTE_BM * ROUTE_BM
    pad_end = jnp.cumsum(padded)
    pad_start = pad_end - padded
    expert = ids[:, 0:2]
    start = jnp.sum(jnp.where(expert[:, :, None] == jnp.arange(N_EXPERTS, dtype=I32), pad_start, 0), axis=-1)
    dest = (start + ids[:, 2:4]).astype(I32)
    nb = -(-(2 * n_tok + N_EXPERTS * (ROUTE_BM - 1)) // ROUTE_BM)
    block_start = jnp.arange(nb, dtype=I32) * ROUTE_BM
    blk_e = jnp.minimum(jnp.sum(pad_end[None, :] <= block_start[:, None], axis=1), N_EXPERTS - 1).astype(I32)
    nused = (pad_end[-1:] // ROUTE_BM).astype(I32)
    return dest[:, 0], dest[:, 1], blk_e, nused


def _grid_pos_tables(n_tokens):
    rows = n_tokens // GRID_W
    n_freq = D_MODEL // 4
    freq = jnp.exp(jnp.arange(n_freq, dtype=F32) * (-math.log(10000.0) / n_freq))

    def enc(p):
        ang = p[:, None] * freq[None, :]
        return jnp.concatenate([jnp.sin(ang), jnp.cos(ang)], axis=-1)

    return enc(jnp.arange(rows, dtype=F32)), enc(jnp.arange(GRID_W, dtype=F32))


def _lane_pad(v, offset):
    return jnp.zeros((1, LANES), F32).at[0, offset:offset + v.shape[0]].set(v.astype(F32))


def kernel(x_prompt, x_sample, state_delta, c, c_ctx, norm1_g, w_mod, b_mod, w_in, conv_qkv_w, A_log, dt_bias, onorm_g, dw_w, dw_b, cln_g, cln_b, w_out, norm2_g, w_group, b_group, w_expert, b_expert, w_e_gate, w_e_up, w_e_down, final_g):
    n_p, l_p, _ = x_prompt.shape
    n_s, l_s, _ = x_sample.shape
    lay = _Layout(n_p, l_p, n_s, l_s, TOK_TILE)
    lay_proj = _Layout(n_p, l_p, n_s, l_s, PROJ_TILE)
    depth = w_in.shape[0]
    assert depth == 1
    xp = x_prompt.reshape(lay.t_p, D_MODEL)
    xs = x_sample.reshape(lay.t_s, D_MODEL)
    pos = _grid_pos_tables(l_s)

    cond = jnp.concatenate([c_ctx[None, :], c], axis=0)
    cond = jnp.pad(cond, ((0, (-cond.shape[0]) % 8), (0, 0)))
    mod = _modulation(cond, w_mod[0], b_mod[0]).reshape(cond.shape[0], 6, D_MODEL)

    n_gate = 4 * N_HEADS
    wi = w_in[0]
    w_main = jnp.concatenate([wi[:, :4 * MIX_A], wi[:, 4 * MIX_A + n_gate:]], axis=1).astype(BF16)
    w_small = jnp.pad(wi[:, 4 * MIX_A:4 * MIX_A + n_gate], ((0, 0), (0, LANES - n_gate))).astype(BF16)
    qkv_raw, z, ug, ba = _inproj(lay_proj, xp, xs, pos, mod, norm1_g[0][None, :], w_main, w_small)

    cw = jnp.pad(conv_qkv_w[0], ((0, 8 - SHORT_CONV), (0, 0))).reshape(8, Q_GROUPS, LANES).transpose(1, 0, 2)
    dw = jnp.pad(dw_w[0], ((0, 32 - CONV_W), (0, 0))).reshape(32, U_GROUPS, LANES).transpose(1, 0, 2)
    alog = _lane_pad(A_log[0].reshape(-1), 2 * N_HEADS)
    dtb = _lane_pad(dt_bias[0].reshape(-1), 2 * N_HEADS)
    qkv, u_conf, gate = _conv(lay, qkv_raw, ug, ba, cw, dw, dw_b[0][None, :], cln_g[0][None, :],
                              cln_b[0][None, :], alog, dtb)

    gate_t = gate[:, :16].reshape(lay.t // CHUNK, CHUNK, 16).transpose(0, 2, 1)
    o_pf, o_pb, s_fin = _delta(qkv, gate, gate_t, n_p, l_p, 0, l_p, None)
    s0 = state_delta[:, 0].reshape(n_s, 2 * N_HEADS, HEAD_DIM, HEAD_DIM)
    o_sf, o_sb = _delta(qkv, gate, gate_t, n_s, l_s, lay.t_p, DELTA_BLK, s0)

    w_route = jnp.pad(jnp.concatenate([w_group[0], w_expert[0]], axis=1),
                      ((0, 0), (0, LANES - N_GROUPS - N_EXPERTS))).astype(BF16)
    b_route = _lane_pad(jnp.concatenate([b_group[0], b_expert[0]]), 0)
    x1, h2p, ids, wts, cnt = _outproj(lay_proj, xp, xs, pos, mod, o_pf, o_pb, o_sf, o_sb, z, u_conf, onorm_g[0][None, :],
                                 norm2_g[0][None, :], w_out[0].astype(BF16), w_route, b_route)

    counts = cnt[0, N_GROUPS:N_GROUPS + N_EXPERTS].astype(I32)
    dest0, dest1, blk_e, nused = _dispatch_plan(ids, counts)
    xb = _sc_scatter2(h2p, dest0, dest1, blk_e.shape[0] * ROUTE_BM, DISPATCH_CHUNK, "dispatch_scatter")
    yb = _experts(xb, blk_e, nused, w_e_gate[0], w_e_up[0], w_e_down[0])
    yg = _sc_gather(yb, jnp.concatenate([dest0, dest1]), COMBINE_CHUNK, "combine_gather")
    y_p, y_s = _combine(lay, yg, x1, wts, mod, final_g[None, :])

    new_state = s_fin.reshape(n_p, 1, 2, N_HEADS, HEAD_DIM, HEAD_DIM)
    return (y_p.reshape(x_prompt.shape), y_s.reshape(x_sample.shape), new_state)
```

```python
import functools
import math

import jax
import jax.numpy as jnp
from jax import lax
from jax.experimental import pallas as pl
from jax.experimental.pallas import tpu as pltpu
from jax.experimental.pallas import tpu_sc as plsc

F32 = jnp.float32
BF16 = jnp.bfloat16
I32 = jnp.int32

D_MODEL = 1024
MIX_A = 512
MIX_B = 512
HEAD_DIM = 128
N_HEADS = 4
SHORT_CONV = 5
CONV_W = 31
N_GROUPS = 4
EXPERTS_PER_GROUP = 8
N_EXPERTS = 32
D_EXPERT = 512
GRID_W = 64
EPS = 1e-6

LANES = 128
TOK_TILE = 256
PROJ_TILE = 512
CHUNK = 128
DELTA_BLK = 512
DELTA_SEQS = 2
DELTA_GROUP = 16
ROUTE_BM = 512
DISPATCH_CHUNK = 64
COMBINE_CHUNK = 64
Q_HALO = 8
U_HALO = 16
VMEM_LIMIT = 56 * 1024 * 1024

HIGHEST = lax.Precision.HIGHEST


def _cparams(sem):
    return pltpu.CompilerParams(dimension_semantics=sem, vmem_limit_bytes=VMEM_LIMIT)


def _sigmoid(x):
    return 0.5 * jnp.tanh(0.5 * x) + 0.5


def _silu(x):
    return x * _sigmoid(x)


def _softplus(x):
    return jnp.maximum(x, 0.0) + jnp.log1p(jnp.exp(-jnp.abs(x)))


def _rms(x):
    return x * lax.rsqrt(jnp.mean(x * x, axis=-1, keepdims=True) + EPS)


def _pack_bf16_pairs(x):
    h = x.shape[1] // 2
    hi = pltpu.bitcast(x[:, :h].astype(F32), jnp.uint32)
    lo = pltpu.bitcast(x[:, h:].astype(F32), jnp.uint32)
    return pltpu.bitcast((hi & jnp.uint32(0xFFFF0000)) | (lo >> 16), I32)


def _unpack_bf16_pairs(p):
    u = pltpu.bitcast(p, jnp.uint32)
    return pltpu.bitcast(u & jnp.uint32(0xFFFF0000), F32), pltpu.bitcast(u << 16, F32)


def _dot(a, b):
    return jnp.dot(a, b, preferred_element_type=F32)


def _dot_nt(a, b):
    return lax.dot_general(a, b, (((1,), (1,)), ((), ())), preferred_element_type=F32)


def _dot_tn(a, b):
    return lax.dot_general(a, b, (((0,), (0,)), ((), ())), preferred_element_type=F32)


def _mod_kernel(cond_ref, w_ref, b_ref, o_ref):
    s = _silu(cond_ref[...])
    o_ref[...] = jnp.dot(s, w_ref[...], preferred_element_type=F32, precision=HIGHEST) + b_ref[...]


def _modulation(cond, w_mod, b_mod):
    n = cond.shape[0]
    tn = D_MODEL
    return pl.pallas_call(
        _mod_kernel,
        grid=(6 * D_MODEL // tn,),
        in_specs=[pl.BlockSpec((n, D_MODEL), lambda j: (0, 0)),
                  pl.BlockSpec((D_MODEL, tn), lambda j: (0, j)),
                  pl.BlockSpec((1, tn), lambda j: (0, j))],
        out_specs=pl.BlockSpec((n, tn), lambda j: (0, j)),
        out_shape=jax.ShapeDtypeStruct((n, 6 * D_MODEL), F32),
        compiler_params=_cparams(("parallel",)),
        name="mod",
    )(cond, w_mod, b_mod.reshape(1, -1))


class _Layout:
    def __init__(self, n_p, l_p, n_s, l_s, tile):
        self.n_p, self.l_p, self.n_s, self.l_s, self.tile = n_p, l_p, n_s, l_s, tile
        self.t_p = n_p * l_p
        self.t_s = n_s * l_s
        self.t = self.t_p + self.t_s
        assert self.t_p % tile == 0 and l_s % tile == 0 and l_s % DELTA_BLK == 0
        self.tiles_p = self.t_p // tile
        self.tiles_s = self.t_s // tile
        self.tiles = self.tiles_p + self.tiles_s
        self.tps_p = max(l_p // tile, 1)
        self.tps_s = l_s // tile

    def is_sample(self, i):
        return i >= self.tiles_p

    def mod_row(self, i):
        return jnp.where(i < self.tiles_p, 0, 1 + (i - self.tiles_p) // self.tps_s)

    def pos_blk(self, i):
        return jnp.where(i < self.tiles_p, 0, (i - self.tiles_p) % self.tps_s)

    def xp_blk(self, i):
        return jnp.minimum(i, self.tiles_p - 1)

    def xs_blk(self, i):
        return jnp.maximum(i - self.tiles_p, 0)

    def seq_pos(self, i):
        in_s = i >= self.tiles_p
        pos = jnp.where(in_s, (i - self.tiles_p) % self.tps_s, i % self.tps_p)
        n = jnp.where(in_s, self.tps_s, self.tps_p)
        return pos, n


def _load_x(lay, i, xp_ref, xs_ref, prow_ref, pcol_ref):
    n_grid_rows = lay.tile // GRID_W
    half = D_MODEL // 2

    prow = jnp.concatenate([jnp.broadcast_to(prow_ref[r:r + 1, :], (GRID_W, half)) for r in range(n_grid_rows)],
                           axis=0)
    pcol = jnp.concatenate([pcol_ref[...]] * n_grid_rows, axis=0)
    pos = jnp.concatenate([prow, pcol], axis=1)
    return jnp.where(lay.is_sample(i), xs_ref[...] + pos, xp_ref[...])


def _x_specs(lay):
    assert lay.tile % (SUBLANES * GRID_W) == 0
    return [pl.BlockSpec((lay.tile, D_MODEL), lambda i: (lay.xp_blk(i), 0)),
            pl.BlockSpec((lay.tile, D_MODEL), lambda i: (lay.xs_blk(i), 0)),
            pl.BlockSpec((lay.tile // GRID_W, D_MODEL // 2), lambda i: (lay.pos_blk(i), 0)),
            _full_spec((GRID_W, D_MODEL // 2))]


def _mod_spec(lay):
    return pl.BlockSpec((1, 6, D_MODEL), lambda i: (lay.mod_row(i), 0, 0))


def _row_spec(lay, width):
    return pl.BlockSpec((lay.tile, width), lambda i: (i, 0))


def _full_spec(shape):
    nd = len(shape)
    return pl.BlockSpec(shape, lambda i: (0,) * nd)


def _inproj_kernel(lay, xp_ref, xs_ref, prow_ref, pcol_ref, mod_ref, g_ref, wm_ref, ws_ref,
                   qkv_ref, z_ref, ug_ref, ba_ref):
    i = pl.program_id(0)
    x = _load_x(lay, i, xp_ref, xs_ref, prow_ref, pcol_ref)
    m = mod_ref[0]
    h = _rms(x) * g_ref[...] * (1.0 + m[1:2]) + m[0:1]
    hb = h.astype(BF16)
    qkv_ref[...] = _dot(hb, wm_ref[:, 0:3 * MIX_A])
    z_ref[...] = _dot(hb, wm_ref[:, 3 * MIX_A:4 * MIX_A])
    glu = _dot(hb, wm_ref[:, 4 * MIX_A:4 * MIX_A + 2 * MIX_B])
    ug_ref[...] = glu[:, :MIX_B] * _sigmoid(glu[:, MIX_B:])
    ba_ref[...] = _dot(hb, ws_ref[...])


def _inproj(lay, xp, xs, pos, mod, norm_g, w_main, w_small):
    t = lay.t
    return pl.pallas_call(
        functools.partial(_inproj_kernel, lay),
        grid=(lay.tiles,),
        in_specs=_x_specs(lay) + [_mod_spec(lay), _full_spec((1, D_MODEL)),
                                  _full_spec(w_main.shape), _full_spec(w_small.shape)],
        out_specs=[_row_spec(lay,3 * MIX_A), _row_spec(lay,MIX_A), _row_spec(lay,MIX_B), _row_spec(lay,LANES)],
        out_shape=[jax.ShapeDtypeStruct((t, 3 * MIX_A), F32),
                   jax.ShapeDtypeStruct((t, MIX_A), F32),
                   jax.ShapeDtypeStruct((t, MIX_B), F32),
                   jax.ShapeDtypeStruct((t, LANES), F32)],
        compiler_params=_cparams(("parallel",)),
        name="inproj",
    )(xp, xs, *pos, mod, norm_g, w_main, w_small)


SUBLANES = 8
CONV_PITCH = 33
CONV_OUT_ROWS = SUBLANES * CONV_PITCH
POST_ROWS = 64
Q_EXT_ROWS = 280
U_EXT_ROWS = 296
Q_GROUPS = 3 * MIX_A // LANES
U_GROUPS = MIX_B // LANES


def _strided_conv(ext_scr, res_scr, w_ref, g, off, n_taps, j_block):
    for j0 in range(0, CONV_PITCH, j_block):
        js = range(j0, min(j0 + j_block, CONV_PITCH))
        v = {m: ext_scr[g, pl.ds(off + m, SUBLANES, stride=CONV_PITCH), :]
             for m in range(js[0], js[-1] + n_taps)}
        for j in js:
            acc = v[j] * w_ref[g, 0:1, :]
            for s in range(1, n_taps):
                acc = acc + v[j + s] * w_ref[g, s:s + 1, :]
            res_scr[g, pl.ds(j, SUBLANES, stride=CONV_PITCH), :] = acc


def _conv_kernel(lay, qc_ref, qp_ref, qn_ref, uc_ref, up_ref, un_ref, ba_ref, cw_ref, dw_ref,
                 dwb_ref, lng_ref, lnb_ref, alog_ref, dtb_ref,
                 qkv_ref, uo_ref, gate_ref, eq_scr, eu_scr, rq_scr, ru_scr):
    i = pl.program_id(0)
    pos, n = lay.seq_pos(i)
    has_prev = pos != 0
    has_next = pos != n - 1
    qp = jnp.where(has_prev, qp_ref[...], 0.0)
    qn = jnp.where(has_next, qn_ref[...], 0.0)
    for g in range(Q_GROUPS):
        sl = slice(g * LANES, (g + 1) * LANES)
        eq_scr[g, 0:Q_HALO, :] = qp[:, sl]
        eq_scr[g, Q_HALO:Q_HALO + TOK_TILE, :] = qc_ref[:, sl]
        eq_scr[g, Q_HALO + TOK_TILE:Q_HALO + TOK_TILE + Q_HALO, :] = qn[:, sl]
        eq_scr[g, TOK_TILE + 2 * Q_HALO:, :] = jnp.zeros((Q_EXT_ROWS - TOK_TILE - 2 * Q_HALO, LANES), F32)
    up = jnp.where(has_prev, up_ref[...], 0.0)
    un_ = jnp.where(has_next, un_ref[...], 0.0)
    for g in range(U_GROUPS):
        sl = slice(g * LANES, (g + 1) * LANES)
        eu_scr[g, 0:U_HALO, :] = up[:, sl]
        eu_scr[g, U_HALO:U_HALO + TOK_TILE, :] = uc_ref[:, sl]
        eu_scr[g, U_HALO + TOK_TILE:U_HALO + TOK_TILE + U_HALO, :] = un_[:, sl]
        eu_scr[g, TOK_TILE + 2 * U_HALO:, :] = jnp.zeros((U_EXT_ROWS - TOK_TILE - 2 * U_HALO, LANES), F32)

    def q_group(g, carry):
        _strided_conv(eq_scr, rq_scr, cw_ref, g, Q_HALO - SHORT_CONV // 2, SHORT_CONV, 11)
        return carry

    def u_group(g, carry):
        _strided_conv(eu_scr, ru_scr, dw_ref, g, U_HALO - CONV_W // 2, CONV_W, 11)
        return carry

    lax.fori_loop(0, Q_GROUPS, q_group, 0)
    lax.fori_loop(0, U_GROUPS, u_group, 0)

    for rc in range(TOK_TILE // POST_ROWS):
        r0 = rc * POST_ROWS
        for g in range(Q_GROUPS):
            y = _silu(rq_scr[g, r0:r0 + POST_ROWS, :])
            if g < 2 * N_HEADS:
                y = y * lax.rsqrt(jnp.sum(y * y, axis=-1, keepdims=True) + EPS)
            qkv_ref[r0:r0 + POST_ROWS, g * LANES:(g + 1) * LANES] = y
        u = jnp.concatenate([ru_scr[g, r0:r0 + POST_ROWS, :] for g in range(U_GROUPS)], axis=1) + dwb_ref[...]
        uc = u - jnp.mean(u, axis=-1, keepdims=True)
        un = uc * lax.rsqrt(jnp.mean(uc * uc, axis=-1, keepdims=True) + EPS)
        uo_ref[r0:r0 + POST_ROWS, :] = _silu(un * lng_ref[...] + lnb_ref[...]).astype(BF16)

    x = ba_ref[...]
    g = -jnp.exp(alog_ref[...]) * _softplus(x + dtb_ref[...])
    beta = _sigmoid(x)
    rows = lax.broadcasted_iota(I32, (CHUNK, CHUNK), 0)
    cols = lax.broadcasted_iota(I32, (CHUNK, CHUNK), 1)
    tri_lo = (rows >= cols).astype(F32)
    tri_up = (rows <= cols).astype(F32)
    lane = lax.broadcasted_iota(I32, (CHUNK, LANES), 1)
    for ch in range(TOK_TILE // CHUNK):
        sl = slice(ch * CHUNK, (ch + 1) * CHUNK)
        pre = jnp.dot(tri_lo, g[sl], preferred_element_type=F32, precision=HIGHEST)
        suf = jnp.dot(tri_up, g[sl], preferred_element_type=F32, precision=HIGHEST)
        gate_ref[sl, :] = jnp.where(lane < 2 * N_HEADS, beta[sl], jnp.where(lane < 3 * N_HEADS, pre, suf))


def _conv(lay, qkv_raw, ug, ba, cw, dw, dwb, lng, lnb, alog, dtb):
    t = lay.t
    qh = TOK_TILE // Q_HALO
    uh = TOK_TILE // U_HALO
    n_qh = t // Q_HALO
    n_uh = t // U_HALO
    in_specs = [
        _row_spec(lay,3 * MIX_A),
        pl.BlockSpec((Q_HALO, 3 * MIX_A), lambda i: (jnp.maximum(i * qh - 1, 0), 0)),
        pl.BlockSpec((Q_HALO, 3 * MIX_A), lambda i: (jnp.minimum((i + 1) * qh, n_qh - 1), 0)),
        _row_spec(lay,MIX_B),
        pl.BlockSpec((U_HALO, MIX_B), lambda i: (jnp.maximum(i * uh - 1, 0), 0)),
        pl.BlockSpec((U_HALO, MIX_B), lambda i: (jnp.minimum((i + 1) * uh, n_uh - 1), 0)),
        _row_spec(lay,LANES),
        _full_spec(cw.shape), _full_spec(dw.shape), _full_spec(dwb.shape),
        _full_spec(lng.shape), _full_spec(lnb.shape), _full_spec(alog.shape), _full_spec(dtb.shape),
    ]
    return pl.pallas_call(
        functools.partial(_conv_kernel, lay),
        grid=(lay.tiles,),
        in_specs=in_specs,
        out_specs=[_row_spec(lay,3 * MIX_A), _row_spec(lay,MIX_B), _row_spec(lay,LANES)],
        out_shape=[jax.ShapeDtypeStruct((t, 3 * MIX_A), F32),
                   jax.ShapeDtypeStruct((t, MIX_B), BF16),
                   jax.ShapeDtypeStruct((t, LANES), F32)],
        scratch_shapes=[pltpu.VMEM((Q_GROUPS, Q_EXT_ROWS, LANES), F32),
                        pltpu.VMEM((U_GROUPS, U_EXT_ROWS, LANES), F32),
                        pltpu.VMEM((Q_GROUPS, CONV_OUT_ROWS, LANES), F32),
                        pltpu.VMEM((U_GROUPS, CONV_OUT_ROWS, LANES), F32)],
        compiler_params=_cparams(("parallel",)),
        name="conv",
    )(qkv_raw, qkv_raw, qkv_raw, ug, ug, ug, ba, cw, dw, dwb, lng, lnb, alog, dtb)


INV_BASE = 8


def _b16(xs):
    return [x.astype(BF16) for x in xs]


def _tri_inverse_minus_eye(nmats, rows, cols):
    assert INV_BASE == 8
    c = nmats[0].shape[0]
    shift = int(math.log2(INV_BASE))
    same = (rows >> shift) == (cols >> shift)
    n1 = [jnp.where(same, n, 0.0) for n in nmats]
    n1b = _b16(n1)
    n2 = [_dot(x, x) for x in n1b]
    n2b = _b16(n2)
    r = [_dot(jnp.concatenate([a, b], axis=0), b) for a, b in zip(n1b, n2b)]
    q = [a + b + x[:c] for a, b, x in zip(n1, n2, r)]
    n4 = [x[c:] for x in r]
    qn4 = [_dot(a, b) for a, b in zip(_b16(q), _b16(n4))]
    q = [a + b + x for a, b, x in zip(q, n4, qn4)]
    while (1 << shift) < c:
        off = ((rows >> (shift + 1)) == (cols >> (shift + 1))) & ((rows >> shift) != (cols >> shift))
        a = [jnp.where(off, -n, 0.0) for n in nmats]
        x = [ai + _dot(ab, qb) for ai, ab, qb in zip(a, _b16(a), _b16(q))]
        qx = [_dot(qb, xb) for qb, xb in zip(_b16(q), _b16(x))]
        q = [qi - xi - qxi for qi, xi, qxi in zip(q, x, qx)]
        shift += 1
    return q


def _delta_units(units):
    c = CHUNK
    scale = HEAD_DIM ** -0.5
    rows = lax.broadcasted_iota(I32, (c, c), 0)
    cols = lax.broadcasted_iota(I32, (c, c), 1)
    pre = []
    for d, hd, qkv_ref, r0, gate, gcol, grow, load_s in units:
        lane = 2 * N_HEADS + N_HEADS * d + hd
        last = c - 1 if d == 0 else 0
        pre.append(dict(
            d=d, load_s=load_s,
            load=lambda part, qkv_ref=qkv_ref, r0=r0, hd=hd: qkv_ref[
                pl.ds(r0, c), part * MIX_A + hd * HEAD_DIM:part * MIX_A + (hd + 1) * HEAD_DIM],
            beta=gate[:, N_HEADS * d + hd:N_HEADS * d + hd + 1],
            gc=gcol[:, lane:lane + 1],
            gr=grow[lane:lane + 1, :],
            gtot=gcol[last:last + 1, lane:lane + 1]))

    ak = []
    for p in pre:
        k = p['load'](1)
        lhs = jnp.concatenate([k * p['beta'], p['load'](0) * scale], axis=0).astype(BF16)
        ak.append(_dot_nt(lhs, k.astype(BF16)))
    nmats, qks = [], []
    for p, a in zip(pre, ak):
        incl = (rows >= cols) if p['d'] == 0 else (rows <= cols)
        strict = (rows > cols) if p['d'] == 0 else (rows < cols)
        dec = jnp.where(incl, jnp.exp(jnp.where(incl, p['gc'] - p['gr'], 0.0)), 0.0)
        nmats.append(jnp.where(strict, -a[:c] * dec, 0.0))
        qks.append((a[c:] * dec).astype(BF16))
    qinv = _tri_inverse_minus_eye(nmats, rows, cols)

    uw = []
    for p, qi in zip(pre, qinv):
        kb = p['load'](1) * p['beta']
        rhs = jnp.concatenate([p['load'](2) * p['beta'], kb * jnp.exp(p['gc'])], axis=1)
        uw.append(rhs + _dot(qi.astype(BF16), rhs.astype(BF16)))
    sw = []
    for p, x in zip(pre, uw):
        qdec = p['load'](0) * scale * jnp.exp(p['gc'])
        sw.append(_dot(jnp.concatenate([x[:, HEAD_DIM:], qdec], axis=0).astype(BF16), p['load_s']().astype(BF16)))
    vnb = [(x[:, :HEAD_DIM] - y[:c]).astype(BF16) for x, y in zip(uw, sw)]
    o = [y[c:] + _dot(qk, vb) for y, qk, vb in zip(sw, qks, vnb)]
    s_new = []
    for p, vb in zip(pre, vnb):
        kdec = (p['load'](1) * jnp.exp(p['gtot'] - p['gc'])).astype(BF16)
        s_new.append(p['load_s']() * jnp.exp(p['gtot']) + _dot_tn(kdec, vb))
    return list(zip(o, s_new))


def _delta_kernel(has_s0, emit_final, one_block, n_chunk, *refs):
    refs = list(refs)
    if one_block:
        seq_in = [[r for r in refs[3 * s:3 * s + 3] for _ in range(2)] for s in range(DELTA_SEQS)]
        refs = refs[3 * DELTA_SEQS:]
    else:
        seq_in = [refs[6 * s:6 * s + 6] for s in range(DELTA_SEQS)]
        refs = refs[6 * DELTA_SEQS:]
    s0_ref = refs.pop(0) if has_s0 else None
    of_ref, ob_ref = refs[:2]
    refs = refs[2:]
    sfin_ref = refs.pop(0) if emit_final else None
    s_scr = refs[0]

    j = pl.program_id(1)

    @pl.when(j == 0)
    def _():
        if has_s0:
            s_scr[...] = s0_ref[...]
        else:
            s_scr[...] = jnp.zeros(s_scr.shape, F32)

    c = CHUNK

    def body(ci, carry):
        units, where = [], []
        for s, (qkvf_ref, qkvb_ref, gatef_ref, gateb_ref, gtf_ref, gtb_ref) in enumerate(seq_in):
            for d in range(2):
                cidx = ci if d == 0 else n_chunk - 1 - ci
                r0 = pl.multiple_of(cidx * c, c)
                qkv_ref = qkvf_ref if d == 0 else qkvb_ref
                gate = (gatef_ref if d == 0 else gateb_ref)[pl.ds(r0, c), :]
                gt = (gtf_ref if d == 0 else gtb_ref)[cidx]
                for hd in range(N_HEADS):
                    units.append((d, hd, qkv_ref, r0, gate, gate, gt,
                                  lambda s=s, idx=N_HEADS * d + hd: s_scr[s, idx]))
                    where.append((s, d, hd, r0))
        results = []
        for g0 in range(0, len(units), DELTA_GROUP):
            results += _delta_units(units[g0:g0 + DELTA_GROUP])
        for (s, d, hd, r0), (o, s_new) in zip(where, results):
            s_scr[s, N_HEADS * d + hd] = s_new
            (of_ref if d == 0 else ob_ref)[s, pl.ds(r0, c), hd * HEAD_DIM:(hd + 1) * HEAD_DIM] = o
        return carry

    lax.fori_loop(0, n_chunk, body, 0)

    if emit_final:
        @pl.when(j == pl.num_programs(1) - 1)
        def _():
            sfin_ref[...] = s_scr[...]


def _delta(qkv, gate, gate_t, n_seq, seq_len, row0, blk, s0):
    nblk = seq_len // blk
    n_chunk = blk // CHUNK
    b0 = row0 // blk
    has_s0 = s0 is not None
    emit_final = not has_s0
    n_state = 2 * N_HEADS
    one_block = nblk == 1
    assert n_seq % DELTA_SEQS == 0 and row0 % blk == 0

    in_specs, args = [], []
    for s in range(DELTA_SEQS):
        def fwd(b, j, s=s):
            return b0 + (b * DELTA_SEQS + s) * nblk + j

        def bwd(b, j, s=s):
            return b0 + (b * DELTA_SEQS + s) * nblk + (nblk - 1 - j)

        for arr, shape in ((qkv, (blk, 3 * MIX_A)), (gate, (blk, LANES)), (gate_t, (n_chunk, 16, CHUNK))):
            tail = (0,) * (len(shape) - 1)
            for f in (fwd,) if one_block else (fwd, bwd):
                in_specs.append(pl.BlockSpec(shape, lambda b, j, f=f, tail=tail: (f(b, j),) + tail))
                args.append(arr)
    state_spec = pl.BlockSpec((DELTA_SEQS, n_state, HEAD_DIM, HEAD_DIM), lambda b, j: (b, 0, 0, 0))
    if has_s0:
        in_specs.append(state_spec)
        args.append(s0)
    out_specs = [pl.BlockSpec((DELTA_SEQS, blk, MIX_A), lambda b, j: (b, j, 0)),
                 pl.BlockSpec((DELTA_SEQS, blk, MIX_A), lambda b, j: (b, nblk - 1 - j, 0))]
    out_shape = [jax.ShapeDtypeStruct((n_seq, seq_len, MIX_A), F32)] * 2
    if emit_final:
        out_specs.append(state_spec)
        out_shape.append(jax.ShapeDtypeStruct((n_seq, n_state, HEAD_DIM, HEAD_DIM), F32))
    outs = pl.pallas_call(
        functools.partial(_delta_kernel, has_s0, emit_final, one_block, n_chunk),
        grid=(n_seq // DELTA_SEQS, nblk),
        in_specs=in_specs,
        out_specs=out_specs,
        out_shape=out_shape,
        scratch_shapes=[pltpu.VMEM((DELTA_SEQS, n_state, HEAD_DIM, HEAD_DIM), F32)],
        compiler_params=_cparams(("parallel", "arbitrary")),
        name="delta_latent" if has_s0 else "delta_prompt",
    )(*args)
    return [o.reshape(n_seq * seq_len, MIX_A) for o in outs[:2]] + list(outs[2:])


def _outproj_kernel(lay, xp_ref, xs_ref, prow_ref, pcol_ref, mod_ref, opf_ref, opb_ref, osf_ref, osb_ref, z_ref,
                    u_ref, og_ref, n2_ref, wo_ref, wr_ref, br_ref, x1_ref, h2_ref, ids_ref, wts_ref, cnt_ref,
                    cnt_scr):
    i = pl.program_id(0)
    x0 = _load_x(lay, i, xp_ref, xs_ref, prow_ref, pcol_ref)
    m = mod_ref[0]
    o = jnp.where(lay.is_sample(i), osf_ref[...] + osb_ref[...], opf_ref[...] + opb_ref[...])
    z = z_ref[...]
    mix = _dot(u_ref[...], wo_ref[MIX_A:, :])
    for hd in range(N_HEADS):
        sl = slice(hd * HEAD_DIM, (hd + 1) * HEAD_DIM)
        oh = _rms(o[:, sl]) * og_ref[...] * _silu(z[:, sl])
        mix = mix + _dot(oh.astype(BF16), wo_ref[sl, :])
    x1 = x0 + m[2:3] * mix
    x1_ref[...] = x1
    h2 = _rms(x1) * n2_ref[...] * (1.0 + m[4:5]) + m[3:4]
    h2b = h2.astype(BF16)
    h2_ref[...] = _pack_bf16_pairs(h2b)

    logits = _dot(h2b, wr_ref[...]) + br_ref[...]
    lane = lax.broadcasted_iota(I32, logits.shape, 1)
    lane_f = lane.astype(F32)
    neg = jnp.float32(-jnp.inf)
    big = jnp.float32(LANES)

    def first_lane(mask):
        return jnp.min(jnp.where(mask, lane_f, big), axis=-1, keepdims=True)

    gl = jnp.where(lane < N_GROUPS, logits, neg)
    gmax = jnp.max(gl, axis=-1, keepdims=True)
    grp = first_lane(gl == gmax)
    p_grp = 1.0 / jnp.sum(jnp.where(lane < N_GROUPS, jnp.exp(gl - gmax), 0.0), axis=-1, keepdims=True)
    e_lane = lane_f - N_GROUPS
    in_grp = (e_lane >= grp * EXPERTS_PER_GROUP) & (e_lane < (grp + 1.0) * EXPERTS_PER_GROUP)
    el = jnp.where(in_grp, logits, neg)
    m1 = jnp.max(el, axis=-1, keepdims=True)
    i1f = first_lane(el == m1)
    el2 = jnp.where(lane_f == i1f, neg, el)
    m2 = jnp.max(el2, axis=-1, keepdims=True)
    i2f = first_lane(el2 == m2)
    i1 = i1f.astype(I32)
    i2 = i2f.astype(I32)
    e2 = jnp.exp(m2 - m1)
    w1 = p_grp / (1.0 + e2)
    w2 = p_grp * e2 / (1.0 + e2)
    wts_ref[...] = jnp.where(lane == 0, w1, jnp.where(lane == 1, w2, 0.0))

    @pl.when(i == 0)
    def _():
        cnt_scr[...] = jnp.zeros(cnt_scr.shape, F32)

    oh1 = lane == i1
    oh2 = lane == i2
    oh = jnp.where(oh1, 1.0, jnp.where(oh2, 1.0, 0.0))
    tm = logits.shape[0]
    rows = lax.broadcasted_iota(I32, (tm, tm), 0)
    cols = lax.broadcasted_iota(I32, (tm, tm), 1)
    before = jnp.where(rows > cols, 1.0, 0.0).astype(BF16)
    seen = cnt_scr[...] + _dot(before, oh.astype(BF16))
    r1 = jnp.sum(jnp.where(oh1, seen, 0.0), axis=-1, keepdims=True).astype(I32)
    r2 = jnp.sum(jnp.where(oh2, seen, 0.0), axis=-1, keepdims=True).astype(I32)
    cnt_scr[...] = cnt_scr[...] + jnp.sum(oh, axis=0, keepdims=True)
    cnt_ref[...] = cnt_scr[...]
    ids_ref[...] = jnp.where(lane == 0, i1 - N_GROUPS,
                             jnp.where(lane == 1, i2 - N_GROUPS,
                                       jnp.where(lane == 2, r1, jnp.where(lane == 3, r2, 0))))


def _outproj(lay, xp, xs, pos, mod, o_pf, o_pb, o_sf, o_sb, z, u, onorm_g, norm2_g, w_out, w_route, b_route):
    t = lay.t
    p_spec = pl.BlockSpec((lay.tile, MIX_A), lambda i: (lay.xp_blk(i), 0))
    s_spec = pl.BlockSpec((lay.tile, MIX_A), lambda i: (lay.xs_blk(i), 0))
    return pl.pallas_call(
        functools.partial(_outproj_kernel, lay),
        grid=(lay.tiles,),
        in_specs=_x_specs(lay) + [_mod_spec(lay), p_spec, p_spec, s_spec, s_spec, _row_spec(lay,MIX_A),
                                  _row_spec(lay,MIX_B), _full_spec((1, HEAD_DIM)), _full_spec((1, D_MODEL)),
                                  _full_spec(w_out.shape), _full_spec(w_route.shape),
                                  _full_spec(b_route.shape)],
        out_specs=[_row_spec(lay,D_MODEL), _row_spec(lay,D_MODEL // 2), _row_spec(lay,LANES), _row_spec(lay,LANES),
                   _full_spec((1, LANES))],
        out_shape=[jax.ShapeDtypeStruct((t, D_MODEL), F32),
                   jax.ShapeDtypeStruct((t, D_MODEL // 2), I32),
                   jax.ShapeDtypeStruct((t, LANES), I32),
                   jax.ShapeDtypeStruct((t, LANES), F32),
                   jax.ShapeDtypeStruct((1, LANES), F32)],
        scratch_shapes=[pltpu.VMEM((1, LANES), F32)],
        compiler_params=_cparams(("arbitrary",)),
        name="outproj",
    )(xp, xs, *pos, mod, o_pf, o_pb, o_sf, o_sb, z, u, onorm_g, norm2_g, w_out, w_route, b_route)


def _sc_gather(table, idx, chunk, name):
    n_rows, width = idx.shape[0], table.shape[1]
    mesh = plsc.VectorSubcoreMesh(core_axis_name="c", subcore_axis_name="s")
    n_workers = mesh.num_cores * mesh.num_subcores
    per_worker = n_rows // n_workers
    n_chunks = per_worker // chunk
    assert n_rows == n_workers * n_chunks * chunk and n_chunks % 2 == 0 and chunk % 8 == 0 and chunk <= LANES

    def body(table_hbm, idx_hbm, out_hbm, idx_v, rows_v, sem):
        base = (lax.axis_index("s") * mesh.num_cores + lax.axis_index("c")) * per_worker

        def gather(slot):
            return pltpu.make_async_copy(table_hbm.at[idx_v.at[slot]], rows_v.at[slot], sem.at[slot])

        def fetch(g, slot):
            off = pl.multiple_of(base + g * chunk, 8)
            pltpu.sync_copy(idx_hbm.at[pl.ds(off, chunk)], idx_v.at[slot])
            gather(slot).start()

        for slot in range(2):
            fetch(slot, slot)

        @pl.loop(0, n_chunks, step=2)
        def _(g):
            for slot in range(2):
                off = pl.multiple_of(base + (g + slot) * chunk, 8)
                gather(slot).wait()
                pltpu.sync_copy(rows_v.at[slot], out_hbm.at[pl.ds(off, chunk)])

                @pl.when(g + slot + 2 < n_chunks)
                def _():
                    fetch(g + slot + 2, slot)

    return pl.kernel(
        body,
        out_type=jax.ShapeDtypeStruct((n_rows, width), table.dtype),
        mesh=mesh,
        scratch_types=[pltpu.VMEM((2, chunk), I32), pltpu.VMEM((2, chunk, width), table.dtype),
                       pltpu.SemaphoreType.DMA((2,))],
        name=name,
    )(table, idx)


def _sc_scatter2(src, idx0, idx1, n_out, chunk, name):
    n_rows, width = src.shape
    mesh = plsc.VectorSubcoreMesh(core_axis_name="c", subcore_axis_name="s")
    n_workers = mesh.num_cores * mesh.num_subcores
    per_worker = n_rows // n_workers
    n_chunks = per_worker // chunk
    assert n_rows == n_workers * n_chunks * chunk and n_chunks % 2 == 0 and chunk % 8 == 0 and chunk <= LANES

    def body(src_hbm, i0_hbm, i1_hbm, out_hbm, i0_v, i1_v, rows_v, sem_in, sem_out):
        base = (lax.axis_index("s") * mesh.num_cores + lax.axis_index("c")) * per_worker

        def rows_in(g, slot):
            off = pl.multiple_of(base + g * chunk, 8)
            return pltpu.make_async_copy(src_hbm.at[pl.ds(off, chunk)], rows_v.at[slot], sem_in.at[slot])

        def fetch(g, slot):
            off = pl.multiple_of(base + g * chunk, 8)
            pltpu.sync_copy(i0_hbm.at[pl.ds(off, chunk)], i0_v.at[slot])
            pltpu.sync_copy(i1_hbm.at[pl.ds(off, chunk)], i1_v.at[slot])
            rows_in(g, slot).start()

        for slot in range(2):
            fetch(slot, slot)

        @pl.loop(0, n_chunks, step=2)
        def _(g):
            for slot in range(2):
                rows_in(g + slot, slot).wait()
                puts = [pltpu.make_async_copy(rows_v.at[slot], out_hbm.at[iv.at[slot]], sem_out.at[slot])
                        for iv in (i0_v, i1_v)]
                for put in puts:
                    put.start()
                for put in puts:
                    put.wait()

                @pl.when(g + slot + 2 < n_chunks)
                def _():
                    fetch(g + slot + 2, slot)

    return pl.kernel(
        body,
        out_type=jax.ShapeDtypeStruct((n_out, width), src.dtype),
        mesh=mesh,
        scratch_types=[pltpu.VMEM((2, chunk), I32), pltpu.VMEM((2, chunk), I32),
                       pltpu.VMEM((2, chunk, width), src.dtype),
                       pltpu.SemaphoreType.DMA((2,)), pltpu.SemaphoreType.DMA((2,))],
        name=name,
    )(src, idx0, idx1)


def _expert_kernel(blk_e_ref, nused_ref, xb_ref, wg_ref, wu_ref, wd_ref, yb_ref, wg_s, wu_s, wd_s):
    i = pl.program_id(0)
    nused = nused_ref[0]

    @pl.when(i < nused)
    def _():
        changed = (i == 0) | (blk_e_ref[i] != blk_e_ref[jnp.maximum(i - 1, 0)])

        @pl.when(changed)
        def _():
            wg_s[...] = wg_ref[0].astype(BF16)
            wu_s[...] = wu_ref[0].astype(BF16)
            wd_s[...] = wd_ref[0].astype(BF16)

        half = D_MODEL // 2
        xa, xb = (v.astype(BF16) for v in _unpack_bf16_pairs(xb_ref[...]))
        g = _dot(xa, wg_s[:half, :]) + _dot(xb, wg_s[half:, :])
        u = _dot(xa, wu_s[:half, :]) + _dot(xb, wu_s[half:, :])
        hmid = (_silu(g) * u).astype(BF16)
        yb_ref[...] = _pack_bf16_pairs(_dot(hmid, wd_s[...]).astype(BF16))

    @pl.when(i >= nused)
    def _():
        yb_ref[...] = jnp.zeros(yb_ref.shape, I32)


def _experts(xb, blk_e, nused, w_gate, w_up, w_down):
    nb = blk_e.shape[0]
    grid_spec = pltpu.PrefetchScalarGridSpec(
        num_scalar_prefetch=2,
        grid=(nb,),
        in_specs=[
            pl.BlockSpec((ROUTE_BM, D_MODEL // 2), lambda i, be, nu: (i, 0)),
            pl.BlockSpec((1, D_MODEL, D_EXPERT), lambda i, be, nu: (be[i], 0, 0)),
            pl.BlockSpec((1, D_MODEL, D_EXPERT), lambda i, be, nu: (be[i], 0, 0)),
            pl.BlockSpec((1, D_EXPERT, D_MODEL), lambda i, be, nu: (be[i], 0, 0)),
        ],
        out_specs=pl.BlockSpec((ROUTE_BM, D_MODEL // 2), lambda i, be, nu: (i, 0)),
        scratch_shapes=[pltpu.VMEM((D_MODEL, D_EXPERT), BF16),
                        pltpu.VMEM((D_MODEL, D_EXPERT), BF16),
                        pltpu.VMEM((D_EXPERT, D_MODEL), BF16)],
    )
    return pl.pallas_call(
        _expert_kernel,
        grid_spec=grid_spec,
        out_shape=jax.ShapeDtypeStruct((nb * ROUTE_BM, D_MODEL // 2), I32),
        compiler_params=_cparams(("arbitrary",)),
        name="expert",
    )(blk_e, nused, xb, w_gate, w_up, w_down)


def _combine_kernel(lay, y0_ref, y1_ref, x1_ref, wts_ref, mod_ref, fg_ref, yp_ref, ys_ref):
    i = pl.program_id(0)
    m = mod_ref[0]
    w = wts_ref[...]
    a0, b0 = _unpack_bf16_pairs(y0_ref[...])
    a1, b1 = _unpack_bf16_pairs(y1_ref[...])
    ff = jnp.concatenate([a0 * w[:, 0:1] + a1 * w[:, 1:2], b0 * w[:, 0:1] + b1 * w[:, 1:2]], axis=1)
    y = _rms(x1_ref[...] + m[5:6] * ff) * fg_ref[...]

    @pl.when(i < lay.tiles_p)
    def _():
        yp_ref[...] = y

    @pl.when(i >= lay.tiles_p)
    def _():
        ys_ref[...] = y


def _combine(lay, yg, x1, wts, mod, final_g):
    return pl.pallas_call(
        functools.partial(_combine_kernel, lay),
        grid=(lay.tiles,),
        in_specs=[_row_spec(lay, D_MODEL // 2),
                  pl.BlockSpec((lay.tile, D_MODEL // 2), lambda i: (i + lay.tiles, 0)),
                  _row_spec(lay,D_MODEL), _row_spec(lay,LANES), _mod_spec(lay), _full_spec((1, D_MODEL))],
        out_specs=[pl.BlockSpec((lay.tile, D_MODEL), lambda i: (lay.xp_blk(i), 0)),
                   pl.BlockSpec((lay.tile, D_MODEL), lambda i: (lay.xs_blk(i), 0))],
        out_shape=[jax.ShapeDtypeStruct((lay.t_p, D_MODEL), F32),
                   jax.ShapeDtypeStruct((lay.t_s, D_MODEL), F32)],
        compiler_params=_cparams(("arbitrary",)),
        name="combine",
    )(yg, yg, x1, wts, mod, final_g)


def _dispatch_plan(ids, counts):
    n_tok = ids.shape[0]
    padded = (counts + ROUTE_BM - 1) // ROUTE_BM * ROUTE_BM
    pad_end = jnp.cumsum(padded)
    pad_start = pad_end - padded
    expert = ids[:, 0:2]
    start = jnp.sum(jnp.where(expert[:, :, None] == jnp.arange(N_EXPERTS, dtype=I32), pad_start, 0), axis=-1)
    dest = (start + ids[:, 2:4]).astype(I32)
    nb = -(-(2 * n_tok + N_EXPERTS * (ROUTE_BM - 1)) // ROUTE_BM)
    block_start = jnp.arange(nb, dtype=I32) * ROUTE_BM
    blk_e = jnp.minimum(jnp.sum(pad_end[None, :] <= block_start[:, None], axis=1), N_EXPERTS - 1).astype(I32)
    nused = (pad_end[-1:] // ROUTE_BM).astype(I32)
    return dest[:, 0], dest[:, 1], blk_e, nused


def _grid_pos_tables(n_tokens):
    rows = n_tokens // GRID_W
    n_freq = D_MODEL // 4
    freq = jnp.exp(jnp.arange(n_freq, dtype=F32) * (-math.log(10000.0) / n_freq))

    def enc(p):
        ang = p[:, None] * freq[None, :]
        return jnp.concatenate([jnp.sin(ang), jnp.cos(ang)], axis=-1)

    return enc(jnp.arange(rows, dtype=F32)), enc(jnp.arange(GRID_W, dtype=F32))


def _lane_pad(v, offset):
    return jnp.zeros((1, LANES), F32).at[0, offset:offset + v.shape[0]].set(v.astype(F32))


def kernel(x_prompt, x_sample, state_delta, c, c_ctx, norm1_g, w_mod, b_mod, w_in, conv_qkv_w, A_log, dt_bias, onorm_g, dw_w, dw_b, cln_g, cln_b, w_out, norm2_g, w_group, b_group, w_expert, b_expert, w_e_gate, w_e_up, w_e_down, final_g):
    n_p, l_p, _ = x_prompt.shape
    n_s, l_s, _ = x_sample.shape
    lay = _Layout(n_p, l_p, n_s, l_s, TOK_TILE)
    lay_proj = _Layout(n_p, l_p, n_s, l_s, PROJ_TILE)
    depth = w_in.shape[0]
    assert depth == 1
    xp = x_prompt.reshape(lay.t_p, D_MODEL)
    xs = x_sample.reshape(lay.t_s, D_MODEL)
    pos = _grid_pos_tables(l_s)

    cond = jnp.concatenate([c_ctx[None, :], c], axis=0)
    cond = jnp.pad(cond, ((0, (-cond.shape[0]) % 8), (0, 0)))
    mod = _modulation(cond, w_mod[0], b_mod[0]).reshape(cond.shape[0], 6, D_MODEL)

    n_gate = 4 * N_HEADS
    wi = w_in[0]
    w_main = jnp.concatenate([wi[:, :4 * MIX_A], wi[:, 4 * MIX_A + n_gate:]], axis=1).astype(BF16)
    w_small = jnp.pad(wi[:, 4 * MIX_A:4 * MIX_A + n_gate], ((0, 0), (0, LANES - n_gate))).astype(BF16)
    qkv_raw, z, ug, ba = _inproj(lay_proj, xp, xs, pos, mod, norm1_g[0][None, :], w_main, w_small)

    cw = jnp.pad(conv_qkv_w[0], ((0, 8 - SHORT_CONV), (0, 0))).reshape(8, Q_GROUPS, LANES).transpose(1, 0, 2)
    dw = jnp.pad(dw_w[0], ((0, 32 - CONV_W), (0, 0))).reshape(32, U_GROUPS, LANES).transpose(1, 0, 2)
    alog = _lane_pad(A_log[0].reshape(-1), 2 * N_HEADS)
    dtb = _lane_pad(dt_bias[0].reshape(-1), 2 * N_HEADS)
    qkv, u_conf, gate = _conv(lay, qkv_raw, ug, ba, cw, dw, dw_b[0][None, :], cln_g[0][None, :],
                              cln_b[0][None, :], alog, dtb)

    gate_t = gate[:, :16].reshape(lay.t // CHUNK, CHUNK, 16).transpose(0, 2, 1)
    o_pf, o_pb, s_fin = _delta(qkv, gate, gate_t, n_p, l_p, 0, l_p, None)
    s0 = state_delta[:, 0].reshape(n_s, 2 * N_HEADS, HEAD_DIM, HEAD_DIM)
    o_sf, o_sb = _delta(qkv, gate, gate_t, n_s, l_s, lay.t_p, DELTA_BLK, s0)

    w_route = jnp.pad(jnp.concatenate([w_group[0], w_expert[0]], axis=1),
                      ((0, 0), (0, LANES - N_GROUPS - N_EXPERTS))).astype(BF16)
    b_route = _lane_pad(jnp.concatenate([b_group[0], b_expert[0]]), 0)
    x1, h2p, ids, wts, cnt = _outproj(lay_proj, xp, xs, pos, mod, o_pf, o_pb, o_sf, o_sb, z, u_conf, onorm_g[0][None, :],
                                 norm2_g[0][None, :], w_out[0].astype(BF16), w_route, b_route)

    counts = cnt[0, N_GROUPS:N_GROUPS + N_EXPERTS].astype(I32)
    dest0, dest1, blk_e, nused = _dispatch_plan(ids, counts)
    xb = _sc_scatter2(h2p, dest0, dest1, blk_e.shape[0] * ROUTE_BM, DISPATCH_CHUNK, "dispatch_scatter")
    yb = _experts(xb, blk_e, nused, w_e_gate[0], w_e_up[0], w_e_down[0])
    yg = _sc_gather(yb, jnp.concatenate([dest0, dest1]), COMBINE_CHUNK, "combine_gather")
    y_p, y_s = _combine(lay, yg, x1, wts, mod, final_g[None, :])

    new_state = s_fin.reshape(n_p, 1, 2, N_HEADS, HEAD_DIM, HEAD_DIM)
    return (y_p.reshape(x_prompt.shape), y_s.reshape(x_sample.shape), new_state)
```

```python
import functools
import math

import jax
import jax.numpy as jnp
from jax import lax
from jax.experimental import pallas as pl
from jax.experimental.pallas import tpu as pltpu
from jax.experimental.pallas import tpu_sc as plsc

F32 = jnp.float32
BF16 = jnp.bfloat16
I32 = jnp.int32

D_MODEL = 1024
MIX_A = 512
MIX_B = 512
HEAD_DIM = 128
N_HEADS = 4
SHORT_CONV = 5
CONV_W = 31
N_GROUPS = 4
EXPERTS_PER_GROUP = 8
N_EXPERTS = 32
D_EXPERT = 512
GRID_W = 64
EPS = 1e-6

LANES = 128
TOK_TILE = 256
PROJ_TILE = 512
CHUNK = 128
DELTA_BLK = 512
DELTA_SEQS = 2
DELTA_GROUP = 16
ROUTE_BM = 512
MOE_PARTS = 2
DISPATCH_CHUNK = 64
COMBINE_CHUNK = 64
Q_HALO = 8
U_HALO = 16
VMEM_LIMIT = 56 * 1024 * 1024

HIGHEST = lax.Precision.HIGHEST


def _cparams(sem):
    return pltpu.CompilerParams(dimension_semantics=sem, vmem_limit_bytes=VMEM_LIMIT)


def _sigmoid(x):
    return 0.5 * jnp.tanh(0.5 * x) + 0.5


def _silu(x):
    return x * _sigmoid(x)


def _softplus(x):
    return jnp.maximum(x, 0.0) + jnp.log1p(jnp.exp(-jnp.abs(x)))


def _rms(x):
    return x * lax.rsqrt(jnp.mean(x * x, axis=-1, keepdims=True) + EPS)


def _pack_bf16_pairs(x):
    h = x.shape[1] // 2
    hi = pltpu.bitcast(x[:, :h].astype(F32), jnp.uint32)
    lo = pltpu.bitcast(x[:, h:].astype(F32), jnp.uint32)
    return pltpu.bitcast((hi & jnp.uint32(0xFFFF0000)) | (lo >> 16), I32)


def _unpack_bf16_pairs(p):
    u = pltpu.bitcast(p, jnp.uint32)
    return pltpu.bitcast(u & jnp.uint32(0xFFFF0000), F32), pltpu.bitcast(u << 16, F32)


def _dot(a, b):
    return jnp.dot(a, b, preferred_element_type=F32)


def _dot_nt(a, b):
    return lax.dot_general(a, b, (((1,), (1,)), ((), ())), preferred_element_type=F32)


def _dot_tn(a, b):
    return lax.dot_general(a, b, (((0,), (0,)), ((), ())), preferred_element_type=F32)


def _mod_kernel(cond_ref, w_ref, b_ref, o_ref):
    s = _silu(cond_ref[...])
    o_ref[...] = jnp.dot(s, w_ref[...], preferred_element_type=F32, precision=HIGHEST) + b_ref[...]


def _modulation(cond, w_mod, b_mod):
    n = cond.shape[0]
    tn = D_MODEL
    return pl.pallas_call(
        _mod_kernel,
        grid=(6 * D_MODEL // tn,),
        in_specs=[pl.BlockSpec((n, D_MODEL), lambda j: (0, 0)),
                  pl.BlockSpec((D_MODEL, tn), lambda j: (0, j)),
                  pl.BlockSpec((1, tn), lambda j: (0, j))],
        out_specs=pl.BlockSpec((n, tn), lambda j: (0, j)),
        out_shape=jax.ShapeDtypeStruct((n, 6 * D_MODEL), F32),
        compiler_params=_cparams(("parallel",)),
        name="mod",
    )(cond, w_mod, b_mod.reshape(1, -1))


class _Layout:
    def __init__(self, n_p, l_p, n_s, l_s, tile, tile0=0, grid=None):
        self.n_p, self.l_p, self.n_s, self.l_s, self.tile = n_p, l_p, n_s, l_s, tile
        self.t_p = n_p * l_p
        self.t_s = n_s * l_s
        self.t = self.t_p + self.t_s
        assert self.t_p % tile == 0 and l_s % tile == 0 and l_s % DELTA_BLK == 0
        self.tiles_p = self.t_p // tile
        self.tiles_s = self.t_s // tile
        self.tiles = self.tiles_p + self.tiles_s
        self.tps_p = max(l_p // tile, 1)
        self.tps_s = l_s // tile
        self.tile0 = tile0
        self.grid = self.tiles if grid is None else grid
        self.rows = self.grid * tile

    def part(self, index, count):
        assert self.tile0 == 0 and self.tiles % count == 0
        grid = self.tiles // count
        return _Layout(self.n_p, self.l_p, self.n_s, self.l_s, self.tile, index * grid, grid)

    def glob(self, i):
        return i + self.tile0

    def is_sample(self, i):
        return self.glob(i) >= self.tiles_p

    def mod_row(self, i):
        i = self.glob(i)
        return jnp.where(i < self.tiles_p, 0, 1 + (i - self.tiles_p) // self.tps_s)

    def pos_blk(self, i):
        i = self.glob(i)
        return jnp.where(i < self.tiles_p, 0, (i - self.tiles_p) % self.tps_s)

    def xp_blk(self, i):
        return jnp.minimum(self.glob(i), self.tiles_p - 1)

    def xs_blk(self, i):
        return jnp.maximum(self.glob(i) - self.tiles_p, 0)

    def seq_pos(self, i):
        i = self.glob(i)
        in_s = i >= self.tiles_p
        pos = jnp.where(in_s, (i - self.tiles_p) % self.tps_s, i % self.tps_p)
        n = jnp.where(in_s, self.tps_s, self.tps_p)
        return pos, n


def _load_x(lay, i, xp_ref, xs_ref, prow_ref, pcol_ref):
    n_grid_rows = lay.tile // GRID_W
    half = D_MODEL // 2

    prow = jnp.concatenate([jnp.broadcast_to(prow_ref[r:r + 1, :], (GRID_W, half)) for r in range(n_grid_rows)],
                           axis=0)
    pcol = jnp.concatenate([pcol_ref[...]] * n_grid_rows, axis=0)
    pos = jnp.concatenate([prow, pcol], axis=1)
    return jnp.where(lay.is_sample(i), xs_ref[...] + pos, xp_ref[...])


def _x_specs(lay):
    assert lay.tile % (SUBLANES * GRID_W) == 0
    return [pl.BlockSpec((lay.tile, D_MODEL), lambda i: (lay.xp_blk(i), 0)),
            pl.BlockSpec((lay.tile, D_MODEL), lambda i: (lay.xs_blk(i), 0)),
            pl.BlockSpec((lay.tile // GRID_W, D_MODEL // 2), lambda i: (lay.pos_blk(i), 0)),
            _full_spec((GRID_W, D_MODEL // 2))]


def _mod_spec(lay):
    return pl.BlockSpec((1, 6, D_MODEL), lambda i: (lay.mod_row(i), 0, 0))


def _row_spec(lay, width):
    return pl.BlockSpec((lay.tile, width), lambda i: (lay.glob(i), 0))


def _part_row_spec(lay, width):
    return pl.BlockSpec((lay.tile, width), lambda i: (i, 0))


def _full_spec(shape):
    nd = len(shape)
    return pl.BlockSpec(shape, lambda i: (0,) * nd)


def _inproj_kernel(lay, xp_ref, xs_ref, prow_ref, pcol_ref, mod_ref, g_ref, wm_ref, ws_ref,
                   qkv_ref, z_ref, ug_ref, ba_ref):
    i = pl.program_id(0)
    x = _load_x(lay, i, xp_ref, xs_ref, prow_ref, pcol_ref)
    m = mod_ref[0]
    h = _rms(x) * g_ref[...] * (1.0 + m[1:2]) + m[0:1]
    hb = h.astype(BF16)
    qkv_ref[...] = _dot(hb, wm_ref[:, 0:3 * MIX_A])
    z_ref[...] = _dot(hb, wm_ref[:, 3 * MIX_A:4 * MIX_A])
    glu = _dot(hb, wm_ref[:, 4 * MIX_A:4 * MIX_A + 2 * MIX_B])
    ug_ref[...] = glu[:, :MIX_B] * _sigmoid(glu[:, MIX_B:])
    ba_ref[...] = _dot(hb, ws_ref[...])


def _inproj(lay, xp, xs, pos, mod, norm_g, w_main, w_small):
    t = lay.t
    return pl.pallas_call(
        functools.partial(_inproj_kernel, lay),
        grid=(lay.tiles,),
        in_specs=_x_specs(lay) + [_mod_spec(lay), _full_spec((1, D_MODEL)),
                                  _full_spec(w_main.shape), _full_spec(w_small.shape)],
        out_specs=[_row_spec(lay,3 * MIX_A), _row_spec(lay,MIX_A), _row_spec(lay,MIX_B), _row_spec(lay,LANES)],
        out_shape=[jax.ShapeDtypeStruct((t, 3 * MIX_A), F32),
                   jax.ShapeDtypeStruct((t, MIX_A), F32),
                   jax.ShapeDtypeStruct((t, MIX_B), F32),
                   jax.ShapeDtypeStruct((t, LANES), F32)],
        compiler_params=_cparams(("parallel",)),
        name="inproj",
    )(xp, xs, *pos, mod, norm_g, w_main, w_small)


SUBLANES = 8
CONV_PITCH = 33
CONV_OUT_ROWS = SUBLANES * CONV_PITCH
POST_ROWS = 64
Q_EXT_ROWS = 280
U_EXT_ROWS = 296
Q_GROUPS = 3 * MIX_A // LANES
U_GROUPS = MIX_B // LANES


def _strided_conv(ext_scr, res_scr, w_ref, g, off, n_taps, j_block):
    for j0 in range(0, CONV_PITCH, j_block):
        js = range(j0, min(j0 + j_block, CONV_PITCH))
        v = {m: ext_scr[g, pl.ds(off + m, SUBLANES, stride=CONV_PITCH), :]
             for m in range(js[0], js[-1] + n_taps)}
        for j in js:
            acc = v[j] * w_ref[g, 0:1, :]
            for s in range(1, n_taps):
                acc = acc + v[j + s] * w_ref[g, s:s + 1, :]
            res_scr[g, pl.ds(j, SUBLANES, stride=CONV_PITCH), :] = acc


def _conv_kernel(lay, qc_ref, qp_ref, qn_ref, uc_ref, up_ref, un_ref, ba_ref, cw_ref, dw_ref,
                 dwb_ref, lng_ref, lnb_ref, alog_ref, dtb_ref,
                 qkv_ref, uo_ref, gate_ref, eq_scr, eu_scr, rq_scr, ru_scr):
    i = pl.program_id(0)
    pos, n = lay.seq_pos(i)
    has_prev = pos != 0
    has_next = pos != n - 1
    qp = jnp.where(has_prev, qp_ref[...], 0.0)
    qn = jnp.where(has_next, qn_ref[...], 0.0)
    for g in range(Q_GROUPS):
        sl = slice(g * LANES, (g + 1) * LANES)
        eq_scr[g, 0:Q_HALO, :] = qp[:, sl]
        eq_scr[g, Q_HALO:Q_HALO + TOK_TILE, :] = qc_ref[:, sl]
        eq_scr[g, Q_HALO + TOK_TILE:Q_HALO + TOK_TILE + Q_HALO, :] = qn[:, sl]
        eq_scr[g, TOK_TILE + 2 * Q_HALO:, :] = jnp.zeros((Q_EXT_ROWS - TOK_TILE - 2 * Q_HALO, LANES), F32)
    up = jnp.where(has_prev, up_ref[...], 0.0)
    un_ = jnp.where(has_next, un_ref[...], 0.0)
    for g in range(U_GROUPS):
        sl = slice(g * LANES, (g + 1) * LANES)
        eu_scr[g, 0:U_HALO, :] = up[:, sl]
        eu_scr[g, U_HALO:U_HALO + TOK_TILE, :] = uc_ref[:, sl]
        eu_scr[g, U_HALO + TOK_TILE:U_HALO + TOK_TILE + U_HALO, :] = un_[:, sl]
        eu_scr[g, TOK_TILE + 2 * U_HALO:, :] = jnp.zeros((U_EXT_ROWS - TOK_TILE - 2 * U_HALO, LANES), F32)

    def q_group(g, carry):
        _strided_conv(eq_scr, rq_scr, cw_ref, g, Q_HALO - SHORT_CONV // 2, SHORT_CONV, 11)
        return carry

    def u_group(g, carry):
        _strided_conv(eu_scr, ru_scr, dw_ref, g, U_HALO - CONV_W // 2, CONV_W, 11)
        return carry

    lax.fori_loop(0, Q_GROUPS, q_group, 0)
    lax.fori_loop(0, U_GROUPS, u_group, 0)

    for rc in range(TOK_TILE // POST_ROWS):
        r0 = rc * POST_ROWS
        for g in range(Q_GROUPS):
            y = _silu(rq_scr[g, r0:r0 + POST_ROWS, :])
            if g < 2 * N_HEADS:
                y = y * lax.rsqrt(jnp.sum(y * y, axis=-1, keepdims=True) + EPS)
            qkv_ref[r0:r0 + POST_ROWS, g * LANES:(g + 1) * LANES] = y
        u = jnp.concatenate([ru_scr[g, r0:r0 + POST_ROWS, :] for g in range(U_GROUPS)], axis=1) + dwb_ref[...]
        uc = u - jnp.mean(u, axis=-1, keepdims=True)
        un = uc * lax.rsqrt(jnp.mean(uc * uc, axis=-1, keepdims=True) + EPS)
        uo_ref[r0:r0 + POST_ROWS, :] = _silu(un * lng_ref[...] + lnb_ref[...]).astype(BF16)

    x = ba_ref[...]
    g = -jnp.exp(alog_ref[...]) * _softplus(x + dtb_ref[...])
    beta = _sigmoid(x)
    rows = lax.broadcasted_iota(I32, (CHUNK, CHUNK), 0)
    cols = lax.broadcasted_iota(I32, (CHUNK, CHUNK), 1)
    tri_lo = (rows >= cols).astype(F32)
    tri_up = (rows <= cols).astype(F32)
    lane = lax.broadcasted_iota(I32, (CHUNK, LANES), 1)
    for ch in range(TOK_TILE // CHUNK):
        sl = slice(ch * CHUNK, (ch + 1) * CHUNK)
        pre = jnp.dot(tri_lo, g[sl], preferred_element_type=F32, precision=HIGHEST)
        suf = jnp.dot(tri_up, g[sl], preferred_element_type=F32, precision=HIGHEST)
        gate_ref[sl, :] = jnp.where(lane < 2 * N_HEADS, beta[sl], jnp.where(lane < 3 * N_HEADS, pre, suf))


def _conv(lay, qkv_raw, ug, ba, cw, dw, dwb, lng, lnb, alog, dtb):
    t = lay.t
    qh = TOK_TILE // Q_HALO
    uh = TOK_TILE // U_HALO
    n_qh = t // Q_HALO
    n_uh = t // U_HALO
    in_specs = [
        _row_spec(lay,3 * MIX_A),
        pl.BlockSpec((Q_HALO, 3 * MIX_A), lambda i: (jnp.maximum(i * qh - 1, 0), 0)),
        pl.BlockSpec((Q_HALO, 3 * MIX_A), lambda i: (jnp.minimum((i + 1) * qh, n_qh - 1), 0)),
        _row_spec(lay,MIX_B),
        pl.BlockSpec((U_HALO, MIX_B), lambda i: (jnp.maximum(i * uh - 1, 0), 0)),
        pl.BlockSpec((U_HALO, MIX_B), lambda i: (jnp.minimum((i + 1) * uh, n_uh - 1), 0)),
        _row_spec(lay,LANES),
        _full_spec(cw.shape), _full_spec(dw.shape), _full_spec(dwb.shape),
        _full_spec(lng.shape), _full_spec(lnb.shape), _full_spec(alog.shape), _full_spec(dtb.shape),
    ]
    return pl.pallas_call(
        functools.partial(_conv_kernel, lay),
        grid=(lay.tiles,),
        in_specs=in_specs,
        out_specs=[_row_spec(lay,3 * MIX_A), _row_spec(lay,MIX_B), _row_spec(lay,LANES)],
        out_shape=[jax.ShapeDtypeStruct((t, 3 * MIX_A), F32),
                   jax.ShapeDtypeStruct((t, MIX_B), BF16),
                   jax.ShapeDtypeStruct((t, LANES), F32)],
        scratch_shapes=[pltpu.VMEM((Q_GROUPS, Q_EXT_ROWS, LANES), F32),
                        pltpu.VMEM((U_GROUPS, U_EXT_ROWS, LANES), F32),
                        pltpu.VMEM((Q_GROUPS, CONV_OUT_ROWS, LANES), F32),
                        pltpu.VMEM((U_GROUPS, CONV_OUT_ROWS, LANES), F32)],
        compiler_params=_cparams(("parallel",)),
        name="conv",
    )(qkv_raw, qkv_raw, qkv_raw, ug, ug, ug, ba, cw, dw, dwb, lng, lnb, alog, dtb)


INV_BASE = 8


def _b16(xs):
    return [x.astype(BF16) for x in xs]


def _tri_inverse_minus_eye(nmats, rows, cols):
    assert INV_BASE == 8
    c = nmats[0].shape[0]
    shift = int(math.log2(INV_BASE))
    same = (rows >> shift) == (cols >> shift)
    n1 = [jnp.where(same, n, 0.0) for n in nmats]
    n1b = _b16(n1)
    n2 = [_dot(x, x) for x in n1b]
    n2b = _b16(n2)
    r = [_dot(jnp.concatenate([a, b], axis=0), b) for a, b in zip(n1b, n2b)]
    q = [a + b + x[:c] for a, b, x in zip(n1, n2, r)]
    n4 = [x[c:] for x in r]
    qn4 = [_dot(a, b) for a, b in zip(_b16(q), _b16(n4))]
    q = [a + b + x for a, b, x in zip(q, n4, qn4)]
    while (1 << shift) < c:
        off = ((rows >> (shift + 1)) == (cols >> (shift + 1))) & ((rows >> shift) != (cols >> shift))
        a = [jnp.where(off, -n, 0.0) for n in nmats]
        x = [ai + _dot(ab, qb) for ai, ab, qb in zip(a, _b16(a), _b16(q))]
        qx = [_dot(qb, xb) for qb, xb in zip(_b16(q), _b16(x))]
        q = [qi - xi - qxi for qi, xi, qxi in zip(q, x, qx)]
        shift += 1
    return q


def _delta_units(units):
    c = CHUNK
    scale = HEAD_DIM ** -0.5
    rows = lax.broadcasted_iota(I32, (c, c), 0)
    cols = lax.broadcasted_iota(I32, (c, c), 1)
    pre = []
    for d, hd, qkv_ref, r0, gate, gcol, grow, load_s in units:
        lane = 2 * N_HEADS + N_HEADS * d + hd
        last = c - 1 if d == 0 else 0
        pre.append(dict(
            d=d, load_s=load_s,
            load=lambda part, qkv_ref=qkv_ref, r0=r0, hd=hd: qkv_ref[
                pl.ds(r0, c), part * MIX_A + hd * HEAD_DIM:part * MIX_A + (hd + 1) * HEAD_DIM],
            beta=gate[:, N_HEADS * d + hd:N_HEADS * d + hd + 1],
            gc=gcol[:, lane:lane + 1],
            gr=grow[lane:lane + 1, :],
            gtot=gcol[last:last + 1, lane:lane + 1]))

    ak = []
    for p in pre:
        k = p['load'](1)
        lhs = jnp.concatenate([k * p['beta'], p['load'](0) * scale], axis=0).astype(BF16)
        ak.append(_dot_nt(lhs, k.astype(BF16)))
    nmats, qks = [], []
    for p, a in zip(pre, ak):
        incl = (rows >= cols) if p['d'] == 0 else (rows <= cols)
        strict = (rows > cols) if p['d'] == 0 else (rows < cols)
        dec = jnp.where(incl, jnp.exp(jnp.where(incl, p['gc'] - p['gr'], 0.0)), 0.0)
        nmats.append(jnp.where(strict, -a[:c] * dec, 0.0))
        qks.append((a[c:] * dec).astype(BF16))
    qinv = _tri_inverse_minus_eye(nmats, rows, cols)

    uw = []
    for p, qi in zip(pre, qinv):
        kb = p['load'](1) * p['beta']
        rhs = jnp.concatenate([p['load'](2) * p['beta'], kb * jnp.exp(p['gc'])], axis=1)
        uw.append(rhs + _dot(qi.astype(BF16), rhs.astype(BF16)))
    sw = []
    for p, x in zip(pre, uw):
        qdec = p['load'](0) * scale * jnp.exp(p['gc'])
        sw.append(_dot(jnp.concatenate([x[:, HEAD_DIM:], qdec], axis=0).astype(BF16), p['load_s']().astype(BF16)))
    vnb = [(x[:, :HEAD_DIM] - y[:c]).astype(BF16) for x, y in zip(uw, sw)]
    o = [y[c:] + _dot(qk, vb) for y, qk, vb in zip(sw, qks, vnb)]
    s_new = []
    for p, vb in zip(pre, vnb):
        kdec = (p['load'](1) * jnp.exp(p['gtot'] - p['gc'])).astype(BF16)
        s_new.append(p['load_s']() * jnp.exp(p['gtot']) + _dot_tn(kdec, vb))
    return list(zip(o, s_new))


def _delta_kernel(has_s0, emit_final, one_block, n_chunk, *refs):
    refs = list(refs)
    if one_block:
        seq_in = [[r for r in refs[3 * s:3 * s + 3] for _ in range(2)] for s in range(DELTA_SEQS)]
        refs = refs[3 * DELTA_SEQS:]
    else:
        seq_in = [refs[6 * s:6 * s + 6] for s in range(DELTA_SEQS)]
        refs = refs[6 * DELTA_SEQS:]
    s0_ref = refs.pop(0) if has_s0 else None
    of_ref, ob_ref = refs[:2]
    refs = refs[2:]
    sfin_ref = refs.pop(0) if emit_final else None
    s_scr = refs[0]

    j = pl.program_id(1)

    @pl.when(j == 0)
    def _():
        if has_s0:
            s_scr[...] = s0_ref[...]
        else:
            s_scr[...] = jnp.zeros(s_scr.shape, F32)

    c = CHUNK

    def body(ci, carry):
        units, where = [], []
        for s, (qkvf_ref, qkvb_ref, gatef_ref, gateb_ref, gtf_ref, gtb_ref) in enumerate(seq_in):
            for d in range(2):
                cidx = ci if d == 0 else n_chunk - 1 - ci
                r0 = pl.multiple_of(cidx * c, c)
                qkv_ref = qkvf_ref if d == 0 else qkvb_ref
                gate = (gatef_ref if d == 0 else gateb_ref)[pl.ds(r0, c), :]
                gt = (gtf_ref if d == 0 else gtb_ref)[cidx]
                for hd in range(N_HEADS):
                    units.append((d, hd, qkv_ref, r0, gate, gate, gt,
                                  lambda s=s, idx=N_HEADS * d + hd: s_scr[s, idx]))
                    where.append((s, d, hd, r0))
        results = []
        for g0 in range(0, len(units), DELTA_GROUP):
            results += _delta_units(units[g0:g0 + DELTA_GROUP])
        for (s, d, hd, r0), (o, s_new) in zip(where, results):
            s_scr[s, N_HEADS * d + hd] = s_new
            (of_ref if d == 0 else ob_ref)[s, pl.ds(r0, c), hd * HEAD_DIM:(hd + 1) * HEAD_DIM] = o
        return carry

    lax.fori_loop(0, n_chunk, body, 0)

    if emit_final:
        @pl.when(j == pl.num_programs(1) - 1)
        def _():
            sfin_ref[...] = s_scr[...]


def _delta(qkv, gate, gate_t, n_seq, seq_len, row0, blk, s0):
    nblk = seq_len // blk
    n_chunk = blk // CHUNK
    b0 = row0 // blk
    has_s0 = s0 is not None
    emit_final = not has_s0
    n_state = 2 * N_HEADS
    one_block = nblk == 1
    assert n_seq % DELTA_SEQS == 0 and row0 % blk == 0

    in_specs, args = [], []
    for s in range(DELTA_SEQS):
        def fwd(b, j, s=s):
            return b0 + (b * DELTA_SEQS + s) * nblk + j

        def bwd(b, j, s=s):
            return b0 + (b * DELTA_SEQS + s) * nblk + (nblk - 1 - j)

        for arr, shape in ((qkv, (blk, 3 * MIX_A)), (gate, (blk, LANES)), (gate_t, (n_chunk, 16, CHUNK))):
            tail = (0,) * (len(shape) - 1)
            for f in (fwd,) if one_block else (fwd, bwd):
                in_specs.append(pl.BlockSpec(shape, lambda b, j, f=f, tail=tail: (f(b, j),) + tail))
                args.append(arr)
    state_spec = pl.BlockSpec((DELTA_SEQS, n_state, HEAD_DIM, HEAD_DIM), lambda b, j: (b, 0, 0, 0))
    if has_s0:
        in_specs.append(state_spec)
        args.append(s0)
    out_specs = [pl.BlockSpec((DELTA_SEQS, blk, MIX_A), lambda b, j: (b, j, 0)),
                 pl.BlockSpec((DELTA_SEQS, blk, MIX_A), lambda b, j: (b, nblk - 1 - j, 0))]
    out_shape = [jax.ShapeDtypeStruct((n_seq, seq_len, MIX_A), F32)] * 2
    if emit_final:
        out_specs.append(state_spec)
        out_shape.append(jax.ShapeDtypeStruct((n_seq, n_state, HEAD_DIM, HEAD_DIM), F32))
    outs = pl.pallas_call(
        functools.partial(_delta_kernel, has_s0, emit_final, one_block, n_chunk),
        grid=(n_seq // DELTA_SEQS, nblk),
        in_specs=in_specs,
        out_specs=out_specs,
        out_shape=out_shape,
        scratch_shapes=[pltpu.VMEM((DELTA_SEQS, n_state, HEAD_DIM, HEAD_DIM), F32)],
        compiler_params=_cparams(("parallel", "arbitrary")),
        name="delta_latent" if has_s0 else "delta_prompt",
    )(*args)
    return [o.reshape(n_seq * seq_len, MIX_A) for o in outs[:2]] + list(outs[2:])


def _outproj_kernel(lay, xp_ref, xs_ref, prow_ref, pcol_ref, mod_ref, opf_ref, opb_ref, osf_ref, osb_ref, z_ref,
                    u_ref, og_ref, n2_ref, wo_ref, wr_ref, br_ref, x1_ref, h2_ref, ids_ref, wts_ref, cnt_ref,
                    cnt_scr):
    i = pl.program_id(0)
    x0 = _load_x(lay, i, xp_ref, xs_ref, prow_ref, pcol_ref)
    m = mod_ref[0]
    o = jnp.where(lay.is_sample(i), osf_ref[...] + osb_ref[...], opf_ref[...] + opb_ref[...])
    z = z_ref[...]
    mix = _dot(u_ref[...], wo_ref[MIX_A:, :])
    for hd in range(N_HEADS):
        sl = slice(hd * HEAD_DIM, (hd + 1) * HEAD_DIM)
        oh = _rms(o[:, sl]) * og_ref[...] * _silu(z[:, sl])
        mix = mix + _dot(oh.astype(BF16), wo_ref[sl, :])
    x1 = x0 + m[2:3] * mix
    x1_ref[...] = x1
    h2 = _rms(x1) * n2_ref[...] * (1.0 + m[4:5]) + m[3:4]
    h2b = h2.astype(BF16)
    h2_ref[...] = _pack_bf16_pairs(h2b)

    logits = _dot(h2b, wr_ref[...]) + br_ref[...]
    lane = lax.broadcasted_iota(I32, logits.shape, 1)
    lane_f = lane.astype(F32)
    neg = jnp.float32(-jnp.inf)
    big = jnp.float32(LANES)

    def first_lane(mask):
        return jnp.min(jnp.where(mask, lane_f, big), axis=-1, keepdims=True)

    gl = jnp.where(lane < N_GROUPS, logits, neg)
    gmax = jnp.max(gl, axis=-1, keepdims=True)
    grp = first_lane(gl == gmax)
    p_grp = 1.0 / jnp.sum(jnp.where(lane < N_GROUPS, jnp.exp(gl - gmax), 0.0), axis=-1, keepdims=True)
    e_lane = lane_f - N_GROUPS
    in_grp = (e_lane >= grp * EXPERTS_PER_GROUP) & (e_lane < (grp + 1.0) * EXPERTS_PER_GROUP)
    el = jnp.where(in_grp, logits, neg)
    m1 = jnp.max(el, axis=-1, keepdims=True)
    i1f = first_lane(el == m1)
    el2 = jnp.where(lane_f == i1f, neg, el)
    m2 = jnp.max(el2, axis=-1, keepdims=True)
    i2f = first_lane(el2 == m2)
    i1 = i1f.astype(I32)
    i2 = i2f.astype(I32)
    e2 = jnp.exp(m2 - m1)
    w1 = p_grp / (1.0 + e2)
    w2 = p_grp * e2 / (1.0 + e2)
    wts_ref[...] = jnp.where(lane == 0, w1, jnp.where(lane == 1, w2, 0.0))

    @pl.when(i == 0)
    def _():
        cnt_scr[...] = jnp.zeros(cnt_scr.shape, F32)

    oh1 = lane == i1
    oh2 = lane == i2
    oh = jnp.where(oh1, 1.0, jnp.where(oh2, 1.0, 0.0))
    tm = logits.shape[0]
    rows = lax.broadcasted_iota(I32, (tm, tm), 0)
    cols = lax.broadcasted_iota(I32, (tm, tm), 1)
    before = jnp.where(rows > cols, 1.0, 0.0).astype(BF16)
    seen = cnt_scr[...] + _dot(before, oh.astype(BF16))
    r1 = jnp.sum(jnp.where(oh1, seen, 0.0), axis=-1, keepdims=True).astype(I32)
    r2 = jnp.sum(jnp.where(oh2, seen, 0.0), axis=-1, keepdims=True).astype(I32)
    cnt_scr[...] = cnt_scr[...] + jnp.sum(oh, axis=0, keepdims=True)
    cnt_ref[...] = cnt_scr[...]
    ids_ref[...] = jnp.where(lane == 0, i1 - N_GROUPS,
                             jnp.where(lane == 1, i2 - N_GROUPS,
                                       jnp.where(lane == 2, r1, jnp.where(lane == 3, r2, 0))))


def _outproj(lay, xp, xs, pos, mod, o_pf, o_pb, o_sf, o_sb, z, u, onorm_g, norm2_g, w_out, w_route, b_route):
    t = lay.rows
    p_spec = pl.BlockSpec((lay.tile, MIX_A), lambda i: (lay.xp_blk(i), 0))
    s_spec = pl.BlockSpec((lay.tile, MIX_A), lambda i: (lay.xs_blk(i), 0))
    return pl.pallas_call(
        functools.partial(_outproj_kernel, lay),
        grid=(lay.grid,),
        in_specs=_x_specs(lay) + [_mod_spec(lay), p_spec, p_spec, s_spec, s_spec, _row_spec(lay, MIX_A),
                                  _row_spec(lay, MIX_B), _full_spec((1, HEAD_DIM)), _full_spec((1, D_MODEL)),
                                  _full_spec(w_out.shape), _full_spec(w_route.shape),
                                  _full_spec(b_route.shape)],
        out_specs=[_part_row_spec(lay, D_MODEL), _part_row_spec(lay, D_MODEL // 2), _part_row_spec(lay, LANES),
                   _part_row_spec(lay, LANES), _full_spec((1, LANES))],
        out_shape=[jax.ShapeDtypeStruct((t, D_MODEL), F32),
                   jax.ShapeDtypeStruct((t, D_MODEL // 2), I32),
                   jax.ShapeDtypeStruct((t, LANES), I32),
                   jax.ShapeDtypeStruct((t, LANES), F32),
                   jax.ShapeDtypeStruct((1, LANES), F32)],
        scratch_shapes=[pltpu.VMEM((1, LANES), F32)],
        compiler_params=_cparams(("arbitrary",)),
        name="outproj",
    )(xp, xs, *pos, mod, o_pf, o_pb, o_sf, o_sb, z, u, onorm_g, norm2_g, w_out, w_route, b_route)


def _sc_gather(table, idx, chunk, name):
    n_rows, width = idx.shape[0], table.shape[1]
    mesh = plsc.VectorSubcoreMesh(core_axis_name="c", subcore_axis_name="s")
    n_workers = mesh.num_cores * mesh.num_subcores
    per_worker = n_rows // n_workers
    n_chunks = per_worker // chunk
    assert n_rows == n_workers * n_chunks * chunk and n_chunks % 2 == 0 and chunk % 8 == 0 and chunk <= LANES

    def body(table_hbm, idx_hbm, out_hbm, idx_v, rows_v, sem):
        base = (lax.axis_index("s") * mesh.num_cores + lax.axis_index("c")) * per_worker

        def gather(slot):
            return pltpu.make_async_copy(table_hbm.at[idx_v.at[slot]], rows_v.at[slot], sem.at[slot])

        def fetch(g, slot):
            off = pl.multiple_of(base + g * chunk, 8)
            pltpu.sync_copy(idx_hbm.at[pl.ds(off, chunk)], idx_v.at[slot])
            gather(slot).start()

        for slot in range(2):
            fetch(slot, slot)

        @pl.loop(0, n_chunks, step=2)
        def _(g):
            for slot in range(2):
                off = pl.multiple_of(base + (g + slot) * chunk, 8)
                gather(slot).wait()
                pltpu.sync_copy(rows_v.at[slot], out_hbm.at[pl.ds(off, chunk)])

                @pl.when(g + slot + 2 < n_chunks)
                def _():
                    fetch(g + slot + 2, slot)

    return pl.kernel(
        body,
        out_type=jax.ShapeDtypeStruct((n_rows, width), table.dtype),
        mesh=mesh,
        scratch_types=[pltpu.VMEM((2, chunk), I32), pltpu.VMEM((2, chunk, width), table.dtype),
                       pltpu.SemaphoreType.DMA((2,))],
        name=name,
    )(table, idx)


def _sc_scatter2(src, idx0, idx1, n_out, chunk, name):
    n_rows, width = src.shape
    mesh = plsc.VectorSubcoreMesh(core_axis_name="c", subcore_axis_name="s")
    n_workers = mesh.num_cores * mesh.num_subcores
    per_worker = n_rows // n_workers
    n_chunks = per_worker // chunk
    assert n_rows == n_workers * n_chunks * chunk and n_chunks % 2 == 0 and chunk % 8 == 0 and chunk <= LANES

    def body(src_hbm, i0_hbm, i1_hbm, out_hbm, i0_v, i1_v, rows_v, sem_in, sem_out):
        base = (lax.axis_index("s") * mesh.num_cores + lax.axis_index("c")) * per_worker

        def rows_in(g, slot):
            off = pl.multiple_of(base + g * chunk, 8)
            return pltpu.make_async_copy(src_hbm.at[pl.ds(off, chunk)], rows_v.at[slot], sem_in.at[slot])

        def fetch(g, slot):
            off = pl.multiple_of(base + g * chunk, 8)
            pltpu.sync_copy(i0_hbm.at[pl.ds(off, chunk)], i0_v.at[slot])
            pltpu.sync_copy(i1_hbm.at[pl.ds(off, chunk)], i1_v.at[slot])
            rows_in(g, slot).start()

        for slot in range(2):
            fetch(slot, slot)

        @pl.loop(0, n_chunks, step=2)
        def _(g):
            for slot in range(2):
                rows_in(g + slot, slot).wait()
                puts = [pltpu.make_async_copy(rows_v.at[slot], out_hbm.at[iv.at[slot]], sem_out.at[slot])
                        for iv in (i0_v, i1_v)]
                for put in puts:
                    put.start()
                for put in puts:
                    put.wait()

                @pl.when(g + slot + 2 < n_chunks)
                def _():
                    fetch(g + slot + 2, slot)

    return pl.kernel(
        body,
        out_type=jax.ShapeDtypeStruct((n_out, width), src.dtype),
        mesh=mesh,
        scratch_types=[pltpu.VMEM((2, chunk), I32), pltpu.VMEM((2, chunk), I32),
                       pltpu.VMEM((2, chunk, width), src.dtype),
                       pltpu.SemaphoreType.DMA((2,)), pltpu.SemaphoreType.DMA((2,))],
        name=name,
    )(src, idx0, idx1)


def _expert_kernel(blk_e_ref, nused_ref, xb_ref, wg_ref, wu_ref, wd_ref, yb_ref, wg_s, wu_s, wd_s):
    i = pl.program_id(0)
    nused = nused_ref[0]

    @pl.when(i < nused)
    def _():
        changed = (i == 0) | (blk_e_ref[i] != blk_e_ref[jnp.maximum(i - 1, 0)])

        @pl.when(changed)
        def _():
            wg_s[...] = wg_ref[0].astype(BF16)
            wu_s[...] = wu_ref[0].astype(BF16)
            wd_s[...] = wd_ref[0].astype(BF16)

        half = D_MODEL // 2
        xa, xb = (v.astype(BF16) for v in _unpack_bf16_pairs(xb_ref[...]))
        g = _dot(xa, wg_s[:half, :]) + _dot(xb, wg_s[half:, :])
        u = _dot(xa, wu_s[:half, :]) + _dot(xb, wu_s[half:, :])
        hmid = (_silu(g) * u).astype(BF16)
        yb_ref[...] = _pack_bf16_pairs(_dot(hmid, wd_s[...]).astype(BF16))

    @pl.when(i >= nused)
    def _():
        yb_ref[...] = jnp.zeros(yb_ref.shape, I32)


def _experts(xb, blk_e, nused, w_gate, w_up, w_down):
    nb = blk_e.shape[0]
    grid_spec = pltpu.PrefetchScalarGridSpec(
        num_scalar_prefetch=2,
        grid=(nb,),
        in_specs=[
            pl.BlockSpec((ROUTE_BM, D_MODEL // 2), lambda i, be, nu: (i, 0)),
            pl.BlockSpec((1, D_MODEL, D_EXPERT), lambda i, be, nu: (be[i], 0, 0)),
            pl.BlockSpec((1, D_MODEL, D_EXPERT), lambda i, be, nu: (be[i], 0, 0)),
            pl.BlockSpec((1, D_EXPERT, D_MODEL), lambda i, be, nu: (be[i], 0, 0)),
        ],
        out_specs=pl.BlockSpec((ROUTE_BM, D_MODEL // 2), lambda i, be, nu: (i, 0)),
        scratch_shapes=[pltpu.VMEM((D_MODEL, D_EXPERT), BF16),
                        pltpu.VMEM((D_MODEL, D_EXPERT), BF16),
                        pltpu.VMEM((D_EXPERT, D_MODEL), BF16)],
    )
    return pl.pallas_call(
        _expert_kernel,
        grid_spec=grid_spec,
        out_shape=jax.ShapeDtypeStruct((nb * ROUTE_BM, D_MODEL // 2), I32),
        compiler_params=_cparams(("arbitrary",)),
        name="expert",
    )(blk_e, nused, xb, w_gate, w_up, w_down)


def _combine_kernel(lay, has_p, has_s, has_prev, y0_ref, y1_ref, x1_ref, wts_ref, mod_ref, fg_ref, *refs):
    refs = list(refs)
    if has_prev:
        refs.pop(0)
    yp_ref = refs.pop(0) if has_p else None
    ys_ref = refs.pop(0) if has_s else None
    i = pl.program_id(0)
    m = mod_ref[0]
    w = wts_ref[...]
    a0, b0 = _unpack_bf16_pairs(y0_ref[...])
    a1, b1 = _unpack_bf16_pairs(y1_ref[...])
    ff = jnp.concatenate([a0 * w[:, 0:1] + a1 * w[:, 1:2], b0 * w[:, 0:1] + b1 * w[:, 1:2]], axis=1)
    y = _rms(x1_ref[...] + m[5:6] * ff) * fg_ref[...]

    if has_p and has_s:
        @pl.when(jnp.logical_not(lay.is_sample(i)))
        def _():
            yp_ref[...] = y

        @pl.when(lay.is_sample(i))
        def _():
            ys_ref[...] = y
    elif has_p:
        yp_ref[...] = y
    else:
        ys_ref[...] = y


def _combine(lay, yg, x1, wts, mod, final_g, ys_prev):
    has_p = lay.tile0 < lay.tiles_p
    has_s = lay.tile0 + lay.grid > lay.tiles_p
    assert lay.tile0 == 0 or not has_p, "only the first tile range may contain prompt tiles"
    assert ys_prev is None or has_s
    in_specs = [_part_row_spec(lay, D_MODEL // 2),
                pl.BlockSpec((lay.tile, D_MODEL // 2), lambda i: (i + lay.grid, 0)),
                _part_row_spec(lay, D_MODEL), _part_row_spec(lay, LANES), _mod_spec(lay),
                _full_spec((1, D_MODEL))]
    args = [yg, yg, x1, wts, mod, final_g]
    out_specs, out_shape, aliases = [], [], {}
    if ys_prev is not None:
        in_specs.append(pl.BlockSpec(memory_space=pl.ANY))
        args.append(ys_prev)
        aliases = {len(args) - 1: 1 if has_p else 0}
    if has_p:
        out_specs.append(pl.BlockSpec((lay.tile, D_MODEL), lambda i: (lay.xp_blk(i), 0)))
        out_shape.append(jax.ShapeDtypeStruct((lay.t_p, D_MODEL), F32))
    if has_s:
        out_specs.append(pl.BlockSpec((lay.tile, D_MODEL), lambda i: (lay.xs_blk(i), 0)))
        out_shape.append(jax.ShapeDtypeStruct((lay.t_s, D_MODEL), F32))
    outs = pl.pallas_call(
        functools.partial(_combine_kernel, lay, has_p, has_s, ys_prev is not None),
        grid=(lay.grid,),
        in_specs=in_specs,
        out_specs=out_specs,
        out_shape=out_shape,
        input_output_aliases=aliases,
        compiler_params=_cparams(("arbitrary",)),
        name="combine",
    )(*args)
    outs = list(outs)
    return (outs.pop(0) if has_p else None), (outs.pop(0) if has_s else None)


def _dispatch_plan(ids, counts):
    n_tok = ids.shape[0]
    padded = (counts + ROUTE_BM - 1) // ROUTE_BM * ROUTE_BM
    pad_end = jnp.cumsum(padded)
    pad_start = pad_end - padded
    expert = ids[:, 0:2]
    start = jnp.sum(jnp.where(expert[:, :, None] == jnp.arange(N_EXPERTS, dtype=I32), pad_start, 0), axis=-1)
    dest = (start + ids[:, 2:4]).astype(I32)
    nb = -(-(2 * n_tok + N_EXPERTS * (ROUTE_BM - 1)) // ROUTE_BM)
    block_start = jnp.arange(nb, dtype=I32) * ROUTE_BM
    blk_e = jnp.minimum(jnp.sum(pad_end[None, :] <= block_start[:, None], axis=1), N_EXPERTS - 1).astype(I32)
    nused = (pad_end[-1:] // ROUTE_BM).astype(I32)
    return dest[:, 0], dest[:, 1], blk_e, nused


def _grid_pos_tables(n_tokens):
    rows = n_tokens // GRID_W
    n_freq = D_MODEL // 4
    freq = jnp.exp(jnp.arange(n_freq, dtype=F32) * (-math.log(10000.0) / n_freq))

    def enc(p):
        ang = p[:, None] * freq[None, :]
        return jnp.concatenate([jnp.sin(ang), jnp.cos(ang)], axis=-1)

    return enc(jnp.arange(rows, dtype=F32)), enc(jnp.arange(GRID_W, dtype=F32))


def _lane_pad(v, offset):
    return jnp.zeros((1, LANES), F32).at[0, offset:offset + v.shape[0]].set(v.astype(F32))


def kernel(x_prompt, x_sample, state_delta, c, c_ctx, norm1_g, w_mod, b_mod, w_in, conv_qkv_w, A_log, dt_bias, onorm_g, dw_w, dw_b, cln_g, cln_b, w_out, norm2_g, w_group, b_group, w_expert, b_expert, w_e_gate, w_e_up, w_e_down, final_g):
    n_p, l_p, _ = x_prompt.shape
    n_s, l_s, _ = x_sample.shape
    lay = _Layout(n_p, l_p, n_s, l_s, TOK_TILE)
    lay_proj = _Layout(n_p, l_p, n_s, l_s, PROJ_TILE)
    depth = w_in.shape[0]
    assert depth == 1
    xp = x_prompt.reshape(lay.t_p, D_MODEL)
    xs = x_sample.reshape(lay.t_s, D_MODEL)
    pos = _grid_pos_tables(l_s)

    cond = jnp.concatenate([c_ctx[None, :], c], axis=0)
    cond = jnp.pad(cond, ((0, (-cond.shape[0]) % 8), (0, 0)))
    mod = _modulation(cond, w_mod[0], b_mod[0]).reshape(cond.shape[0], 6, D_MODEL)

    n_gate = 4 * N_HEADS
    wi = w_in[0]
    w_main = jnp.concatenate([wi[:, :4 * MIX_A], wi[:, 4 * MIX_A + n_gate:]], axis=1).astype(BF16)
    w_small = jnp.pad(wi[:, 4 * MIX_A:4 * MIX_A + n_gate], ((0, 0), (0, LANES - n_gate))).astype(BF16)
    qkv_raw, z, ug, ba = _inproj(lay_proj, xp, xs, pos, mod, norm1_g[0][None, :], w_main, w_small)

    cw = jnp.pad(conv_qkv_w[0], ((0, 8 - SHORT_CONV), (0, 0))).reshape(8, Q_GROUPS, LANES).transpose(1, 0, 2)
    dw = jnp.pad(dw_w[0], ((0, 32 - CONV_W), (0, 0))).reshape(32, U_GROUPS, LANES).transpose(1, 0, 2)
    alog = _lane_pad(A_log[0].reshape(-1), 2 * N_HEADS)
    dtb = _lane_pad(dt_bias[0].reshape(-1), 2 * N_HEADS)
    qkv, u_conf, gate = _conv(lay, qkv_raw, ug, ba, cw, dw, dw_b[0][None, :], cln_g[0][None, :],
                              cln_b[0][None, :], alog, dtb)

    gate_t = gate[:, :16].reshape(lay.t // CHUNK, CHUNK, 16).transpose(0, 2, 1)
    o_pf, o_pb, s_fin = _delta(qkv, gate, gate_t, n_p, l_p, 0, l_p, None)
    s0 = state_delta[:, 0].reshape(n_s, 2 * N_HEADS, HEAD_DIM, HEAD_DIM)
    o_sf, o_sb = _delta(qkv, gate, gate_t, n_s, l_s, lay.t_p, DELTA_BLK, s0)

    w_route = jnp.pad(jnp.concatenate([w_group[0], w_expert[0]], axis=1),
                      ((0, 0), (0, LANES - N_GROUPS - N_EXPERTS))).astype(BF16)
    b_route = _lane_pad(jnp.concatenate([b_group[0], b_expert[0]]), 0)
    w_out_b = w_out[0].astype(BF16)

    parts = []
    for part in range(MOE_PARTS):
        x1, h2p, ids, wts, cnt = _outproj(lay_proj.part(part, MOE_PARTS), xp, xs, pos, mod, o_pf, o_pb, o_sf, o_sb,
                                          z, u_conf, onorm_g[0][None, :], norm2_g[0][None, :], w_out_b, w_route,
                                          b_route)
        counts = cnt[0, N_GROUPS:N_GROUPS + N_EXPERTS].astype(I32)
        dest0, dest1, blk_e, nused = _dispatch_plan(ids, counts)
        xb = _sc_scatter2(h2p, dest0, dest1, blk_e.shape[0] * ROUTE_BM, DISPATCH_CHUNK, "dispatch_scatter")
        parts.append((x1, wts, dest0, dest1, blk_e, nused, xb))
    gathered = []
    for x1, wts, dest0, dest1, blk_e, nused, xb in parts:
        yb = _experts(xb, blk_e, nused, w_e_gate[0], w_e_up[0], w_e_down[0])
        gathered.append(_sc_gather(yb, jnp.concatenate([dest0, dest1]), COMBINE_CHUNK, "combine_gather"))
    y_p = y_s = None
    for part, ((x1, wts, *_), yg) in enumerate(zip(parts, gathered)):
        y_p_part, y_s_part = _combine(lay.part(part, MOE_PARTS), yg, x1, wts, mod, final_g[None, :], y_s)
        y_p = y_p if y_p_part is None else y_p_part
        y_s = y_s if y_s_part is None else y_s_part

    new_state = s_fin.reshape(n_p, 1, 2, N_HEADS, HEAD_DIM, HEAD_DIM)
    return (y_p.reshape(x_prompt.shape), y_s.reshape(x_sample.shape), new_state)
```

```python
import functools
import math

import jax
import jax.numpy as jnp
from jax import lax
from jax.experimental import pallas as pl
from jax.experimental.pallas import tpu as pltpu
from jax.experimental.pallas import tpu_sc as plsc

F32 = jnp.float32
BF16 = jnp.bfloat16
I32 = jnp.int32

D_MODEL = 1024
MIX_A = 512
MIX_B = 512
HEAD_DIM = 128
N_HEADS = 4
SHORT_CONV = 5
CONV_W = 31
N_GROUPS = 4
EXPERTS_PER_GROUP = 8
N_EXPERTS = 32
D_EXPERT = 512
GRID_W = 64
EPS = 1e-6

LANES = 128
SUBLANES = 8
TOK_TILE = 256
PROJ_TILE = 512
CHUNK = 128
DELTA_BLK = 512
DELTA_SEQS = 2
ROUTE_BM = 512
DISPATCH_CHUNK = 64
COMBINE_CHUNK = 64
Q_HALO = 8
U_HALO = 16
VMEM_LIMIT = 56 * 1024 * 1024

HIGHEST = lax.Precision.HIGHEST


def _cparams(sem):
    return pltpu.CompilerParams(dimension_semantics=sem, vmem_limit_bytes=VMEM_LIMIT)


def _sigmoid(x):
    return 0.5 * jnp.tanh(0.5 * x) + 0.5


def _silu(x):
    return x * _sigmoid(x)


def _softplus(x):
    return jnp.maximum(x, 0.0) + jnp.log1p(jnp.exp(-jnp.abs(x)))


def _rms(x):
    return x * lax.rsqrt(jnp.mean(x * x, axis=-1, keepdims=True) + EPS)


def _pack_bf16_pairs(x):
    h = x.shape[1] // 2
    hi = pltpu.bitcast(x[:, :h].astype(F32), jnp.uint32)
    lo = pltpu.bitcast(x[:, h:].astype(F32), jnp.uint32)
    return pltpu.bitcast((hi & jnp.uint32(0xFFFF0000)) | (lo >> 16), I32)


def _unpack_bf16_pairs(p):
    u = pltpu.bitcast(p, jnp.uint32)
    return pltpu.bitcast(u & jnp.uint32(0xFFFF0000), F32), pltpu.bitcast(u << 16, F32)


def _dot(a, b):
    return jnp.dot(a, b, preferred_element_type=F32)


def _dot_nt(a, b):
    return lax.dot_general(a, b, (((1,), (1,)), ((), ())), preferred_element_type=F32)


def _dot_tn(a, b):
    return lax.dot_general(a, b, (((0,), (0,)), ((), ())), preferred_element_type=F32)


def _mod_kernel(cond_ref, w_ref, b_ref, o_ref):
    s = _silu(cond_ref[...])
    o_ref[...] = jnp.dot(s, w_ref[...], preferred_element_type=F32, precision=HIGHEST) + b_ref[...]


def _modulation(cond, w_mod, b_mod):
    n = cond.shape[0]
    tn = D_MODEL
    return pl.pallas_call(
        _mod_kernel,
        grid=(6 * D_MODEL // tn,),
        in_specs=[pl.BlockSpec((n, D_MODEL), lambda j: (0, 0)),
                  pl.BlockSpec((D_MODEL, tn), lambda j: (0, j)),
                  pl.BlockSpec((1, tn), lambda j: (0, j))],
        out_specs=pl.BlockSpec((n, tn), lambda j: (0, j)),
        out_shape=jax.ShapeDtypeStruct((n, 6 * D_MODEL), F32),
        compiler_params=_cparams(("parallel",)),
        name="mod",
    )(cond, w_mod, b_mod.reshape(1, -1))


class _Layout:
    def __init__(self, n_p, l_p, n_s, l_s, tile):
        self.n_p, self.l_p, self.n_s, self.l_s, self.tile = n_p, l_p, n_s, l_s, tile
        self.t_p = n_p * l_p
        self.t_s = n_s * l_s
        self.t = self.t_p + self.t_s
        assert self.t_p % tile == 0 and l_s % tile == 0 and l_s % DELTA_BLK == 0
        self.tiles_p = self.t_p // tile
        self.tiles_s = self.t_s // tile
        self.tiles = self.tiles_p + self.tiles_s
        self.tps_p = max(l_p // tile, 1)
        self.tps_s = l_s // tile

    def is_sample(self, i):
        return i >= self.tiles_p

    def mod_row(self, i):
        return jnp.where(i < self.tiles_p, 0, 1 + (i - self.tiles_p) // self.tps_s)

    def pos_blk(self, i):
        return jnp.where(i < self.tiles_p, 0, (i - self.tiles_p) % self.tps_s)

    def xp_blk(self, i):
        return jnp.minimum(i, self.tiles_p - 1)

    def xs_blk(self, i):
        return jnp.maximum(i - self.tiles_p, 0)

    def seq_pos(self, i):
        in_s = i >= self.tiles_p
        pos = jnp.where(in_s, (i - self.tiles_p) % self.tps_s, i % self.tps_p)
        n = jnp.where(in_s, self.tps_s, self.tps_p)
        return pos, n


def _load_x(lay, i, xp_ref, xs_ref, prow_ref, pcol_ref):
    n_grid_rows = lay.tile // GRID_W
    half = D_MODEL // 2
    prow = jnp.concatenate([jnp.broadcast_to(prow_ref[r:r + 1, :], (GRID_W, half)) for r in range(n_grid_rows)],
                           axis=0)
    pcol = jnp.concatenate([pcol_ref[...]] * n_grid_rows, axis=0)
    pos = jnp.concatenate([prow, pcol], axis=1)
    return jnp.where(lay.is_sample(i), xs_ref[...] + pos, xp_ref[...])


def _x_specs(lay):
    assert lay.tile % (SUBLANES * GRID_W) == 0
    return [pl.BlockSpec((lay.tile, D_MODEL), lambda i: (lay.xp_blk(i), 0)),
            pl.BlockSpec((lay.tile, D_MODEL), lambda i: (lay.xs_blk(i), 0)),
            pl.BlockSpec((lay.tile // GRID_W, D_MODEL // 2), lambda i: (lay.pos_blk(i), 0)),
            _full_spec((GRID_W, D_MODEL // 2))]


def _mod_spec(lay):
    return pl.BlockSpec((1, 6, D_MODEL), lambda i: (lay.mod_row(i), 0, 0))


def _row_spec(lay, width):
    return pl.BlockSpec((lay.tile, width), lambda i: (i, 0))


def _full_spec(shape):
    nd = len(shape)
    return pl.BlockSpec(shape, lambda i: (0,) * nd)


def _inproj_kernel(lay, xp_ref, xs_ref, prow_ref, pcol_ref, mod_ref, g_ref, wm_ref, ws_ref,
                   qkv_ref, z_ref, ug_ref, ba_ref):
    i = pl.program_id(0)
    x = _load_x(lay, i, xp_ref, xs_ref, prow_ref, pcol_ref)
    m = mod_ref[0]
    h = _rms(x) * g_ref[...] * (1.0 + m[1:2]) + m[0:1]
    hb = h.astype(BF16)
    qkv_ref[...] = _dot(hb, wm_ref[:, 0:3 * MIX_A])
    z_ref[...] = _dot(hb, wm_ref[:, 3 * MIX_A:4 * MIX_A])
    glu = _dot(hb, wm_ref[:, 4 * MIX_A:4 * MIX_A + 2 * MIX_B])
    ug_ref[...] = glu[:, :MIX_B] * _sigmoid(glu[:, MIX_B:])
    ba_ref[...] = _dot(hb, ws_ref[...])


def _inproj(lay, xp, xs, pos, mod, norm_g, w_main, w_small):
    t = lay.t
    return pl.pallas_call(
        functools.partial(_inproj_kernel, lay),
        grid=(lay.tiles,),
        in_specs=_x_specs(lay) + [_mod_spec(lay), _full_spec((1, D_MODEL)),
                                  _full_spec(w_main.shape), _full_spec(w_small.shape)],
        out_specs=[_row_spec(lay, 3 * MIX_A), _row_spec(lay, MIX_A), _row_spec(lay, MIX_B),
                   _row_spec(lay, LANES)],
        out_shape=[jax.ShapeDtypeStruct((t, 3 * MIX_A), F32),
                   jax.ShapeDtypeStruct((t, MIX_A), F32),
                   jax.ShapeDtypeStruct((t, MIX_B), F32),
                   jax.ShapeDtypeStruct((t, LANES), F32)],
        compiler_params=_cparams(("parallel",)),
        name="inproj",
    )(xp, xs, *pos, mod, norm_g, w_main, w_small)


CONV_PITCH = 33
CONV_OUT_ROWS = SUBLANES * CONV_PITCH
POST_ROWS = 64
Q_EXT_ROWS = 280
U_EXT_ROWS = 296
Q_GROUPS = 3 * MIX_A // LANES
U_GROUPS = MIX_B // LANES
CONV_J_BLOCK = 11


def _strided_conv(ext_scr, res_scr, w_ref, g, off, n_taps):
    for j0 in range(0, CONV_PITCH, CONV_J_BLOCK):
        js = range(j0, min(j0 + CONV_J_BLOCK, CONV_PITCH))
        v = {m: ext_scr[g, pl.ds(off + m, SUBLANES, stride=CONV_PITCH), :]
             for m in range(js[0], js[-1] + n_taps)}
        for j in js:
            acc = v[j] * w_ref[g, 0:1, :]
            for s in range(1, n_taps):
                acc = acc + v[j + s] * w_ref[g, s:s + 1, :]
            res_scr[g, pl.ds(j, SUBLANES, stride=CONV_PITCH), :] = acc


def _conv_kernel(lay, qc_ref, qp_ref, qn_ref, uc_ref, up_ref, un_ref, ba_ref, cw_ref, dw_ref,
                 dwb_ref, lng_ref, lnb_ref, alog_ref, dtb_ref,
                 qkv_ref, uo_ref, gate_ref, eq_scr, eu_scr, rq_scr, ru_scr):
    i = pl.program_id(0)
    pos, n = lay.seq_pos(i)
    has_prev = pos != 0
    has_next = pos != n - 1
    qp = jnp.where(has_prev, qp_ref[...], 0.0)
    qn = jnp.where(has_next, qn_ref[...], 0.0)
    for g in range(Q_GROUPS):
        sl = slice(g * LANES, (g + 1) * LANES)
        eq_scr[g, 0:Q_HALO, :] = qp[:, sl]
        eq_scr[g, Q_HALO:Q_HALO + TOK_TILE, :] = qc_ref[:, sl]
        eq_scr[g, Q_HALO + TOK_TILE:Q_HALO + TOK_TILE + Q_HALO, :] = qn[:, sl]
        eq_scr[g, TOK_TILE + 2 * Q_HALO:, :] = jnp.zeros((Q_EXT_ROWS - TOK_TILE - 2 * Q_HALO, LANES), F32)
    up = jnp.where(has_prev, up_ref[...], 0.0)
    un_ = jnp.where(has_next, un_ref[...], 0.0)
    for g in range(U_GROUPS):
        sl = slice(g * LANES, (g + 1) * LANES)
        eu_scr[g, 0:U_HALO, :] = up[:, sl]
        eu_scr[g, U_HALO:U_HALO + TOK_TILE, :] = uc_ref[:, sl]
        eu_scr[g, U_HALO + TOK_TILE:U_HALO + TOK_TILE + U_HALO, :] = un_[:, sl]
        eu_scr[g, TOK_TILE + 2 * U_HALO:, :] = jnp.zeros((U_EXT_ROWS - TOK_TILE - 2 * U_HALO, LANES), F32)

    def q_group(g, carry):
        _strided_conv(eq_scr, rq_scr, cw_ref, g, Q_HALO - SHORT_CONV // 2, SHORT_CONV)
        return carry

    def u_group(g, carry):
        _strided_conv(eu_scr, ru_scr, dw_ref, g, U_HALO - CONV_W // 2, CONV_W)
        return carry

    lax.fori_loop(0, Q_GROUPS, q_group, 0)
    lax.fori_loop(0, U_GROUPS, u_group, 0)

    for rc in range(TOK_TILE // POST_ROWS):
        r0 = rc * POST_ROWS
        for g in range(Q_GROUPS):
            y = _silu(rq_scr[g, r0:r0 + POST_ROWS, :])
            if g < 2 * N_HEADS:
                y = y * lax.rsqrt(jnp.sum(y * y, axis=-1, keepdims=True) + EPS)
            qkv_ref[r0:r0 + POST_ROWS, g * LANES:(g + 1) * LANES] = y
        u = jnp.concatenate([ru_scr[g, r0:r0 + POST_ROWS, :] for g in range(U_GROUPS)], axis=1) + dwb_ref[...]
        uc = u - jnp.mean(u, axis=-1, keepdims=True)
        un = uc * lax.rsqrt(jnp.mean(uc * uc, axis=-1, keepdims=True) + EPS)
        uo_ref[r0:r0 + POST_ROWS, :] = _silu(un * lng_ref[...] + lnb_ref[...]).astype(BF16)

    x = ba_ref[...]
    g = -jnp.exp(alog_ref[...]) * _softplus(x + dtb_ref[...])
    beta = _sigmoid(x)
    rows = lax.broadcasted_iota(I32, (CHUNK, CHUNK), 0)
    cols = lax.broadcasted_iota(I32, (CHUNK, CHUNK), 1)
    tri_lo = (rows >= cols).astype(F32)
    tri_up = (rows <= cols).astype(F32)
    lane = lax.broadcasted_iota(I32, (CHUNK, LANES), 1)
    for ch in range(TOK_TILE // CHUNK):
        sl = slice(ch * CHUNK, (ch + 1) * CHUNK)
        pre = jnp.dot(tri_lo, g[sl], preferred_element_type=F32, precision=HIGHEST)
        suf = jnp.dot(tri_up, g[sl], preferred_element_type=F32, precision=HIGHEST)
        gate_ref[sl, :] = jnp.where(lane < 2 * N_HEADS, beta[sl], jnp.where(lane < 3 * N_HEADS, pre, suf))


def _conv(lay, qkv_raw, ug, ba, cw, dw, dwb, lng, lnb, alog, dtb):
    t = lay.t
    qh = TOK_TILE // Q_HALO
    uh = TOK_TILE // U_HALO
    n_qh = t // Q_HALO
    n_uh = t // U_HALO
    in_specs = [
        _row_spec(lay, 3 * MIX_A),
        pl.BlockSpec((Q_HALO, 3 * MIX_A), lambda i: (jnp.maximum(i * qh - 1, 0), 0)),
        pl.BlockSpec((Q_HALO, 3 * MIX_A), lambda i: (jnp.minimum((i + 1) * qh, n_qh - 1), 0)),
        _row_spec(lay, MIX_B),
        pl.BlockSpec((U_HALO, MIX_B), lambda i: (jnp.maximum(i * uh - 1, 0), 0)),
        pl.BlockSpec((U_HALO, MIX_B), lambda i: (jnp.minimum((i + 1) * uh, n_uh - 1), 0)),
        _row_spec(lay, LANES),
        _full_spec(cw.shape), _full_spec(dw.shape), _full_spec(dwb.shape),
        _full_spec(lng.shape), _full_spec(lnb.shape), _full_spec(alog.shape), _full_spec(dtb.shape),
    ]
    return pl.pallas_call(
        functools.partial(_conv_kernel, lay),
        grid=(lay.tiles,),
        in_specs=in_specs,
        out_specs=[_row_spec(lay, 3 * MIX_A), _row_spec(lay, MIX_B), _row_spec(lay, LANES)],
        out_shape=[jax.ShapeDtypeStruct((t, 3 * MIX_A), F32),
                   jax.ShapeDtypeStruct((t, MIX_B), BF16),
                   jax.ShapeDtypeStruct((t, LANES), F32)],
        scratch_shapes=[pltpu.VMEM((Q_GROUPS, Q_EXT_ROWS, LANES), F32),
                        pltpu.VMEM((U_GROUPS, U_EXT_ROWS, LANES), F32),
                        pltpu.VMEM((Q_GROUPS, CONV_OUT_ROWS, LANES), F32),
                        pltpu.VMEM((U_GROUPS, CONV_OUT_ROWS, LANES), F32)],
        compiler_params=_cparams(("parallel",)),
        name="conv",
    )(qkv_raw, qkv_raw, qkv_raw, ug, ug, ug, ba, cw, dw, dwb, lng, lnb, alog, dtb)


INV_BASE = 8


def _b16(xs):
    return [x.astype(BF16) for x in xs]


def _tri_inverse_minus_eye(nmats, rows, cols):
    assert INV_BASE == 8
    c = nmats[0].shape[0]
    shift = int(math.log2(INV_BASE))
    same = (rows >> shift) == (cols >> shift)
    n1 = [jnp.where(same, n, 0.0) for n in nmats]
    n1b = _b16(n1)
    n2 = [_dot(x, x) for x in n1b]
    n2b = _b16(n2)
    r = [_dot(jnp.concatenate([a, b], axis=0), b) for a, b in zip(n1b, n2b)]
    q = [a + b + x[:c] for a, b, x in zip(n1, n2, r)]
    n4 = [x[c:] for x in r]
    qn4 = [_dot(a, b) for a, b in zip(_b16(q), _b16(n4))]
    q = [a + b + x for a, b, x in zip(q, n4, qn4)]
    while (1 << shift) < c:
        off = ((rows >> (shift + 1)) == (cols >> (shift + 1))) & ((rows >> shift) != (cols >> shift))
        a = [jnp.where(off, -n, 0.0) for n in nmats]
        x = [ai + _dot(ab, qb) for ai, ab, qb in zip(a, _b16(a), _b16(q))]
        qx = [_dot(qb, xb) for qb, xb in zip(_b16(q), _b16(x))]
        q = [qi - xi - qxi for qi, xi, qxi in zip(q, x, qx)]
        shift += 1
    return q


def _delta_units(units):
    c = CHUNK
    scale = HEAD_DIM ** -0.5
    rows = lax.broadcasted_iota(I32, (c, c), 0)
    cols = lax.broadcasted_iota(I32, (c, c), 1)
    pre = []
    for d, hd, qkv_ref, r0, gate, gt, load_s in units:
        lane = 2 * N_HEADS + N_HEADS * d + hd
        last = c - 1 if d == 0 else 0
        pre.append(dict(
            d=d, load_s=load_s,
            load=lambda part, qkv_ref=qkv_ref, r0=r0, hd=hd: qkv_ref[
                pl.ds(r0, c), part * MIX_A + hd * HEAD_DIM:part * MIX_A + (hd + 1) * HEAD_DIM],
            beta=gate[:, N_HEADS * d + hd:N_HEADS * d + hd + 1],
            gc=gate[:, lane:lane + 1],
            gr=gt[lane:lane + 1, :],
            gtot=gate[last:last + 1, lane:lane + 1]))

    ak = []
    for p in pre:
        k = p['load'](1)
        lhs = jnp.concatenate([k * p['beta'], p['load'](0) * scale], axis=0).astype(BF16)
        ak.append(_dot_nt(lhs, k.astype(BF16)))
    nmats, qks = [], []
    for p, a in zip(pre, ak):
        incl = (rows >= cols) if p['d'] == 0 else (rows <= cols)
        strict = (rows > cols) if p['d'] == 0 else (rows < cols)
        dec = jnp.where(incl, jnp.exp(jnp.where(incl, p['gc'] - p['gr'], 0.0)), 0.0)
        nmats.append(jnp.where(strict, -a[:c] * dec, 0.0))
        qks.append((a[c:] * dec).astype(BF16))
    qinv = _tri_inverse_minus_eye(nmats, rows, cols)

    uw = []
    for p, qi in zip(pre, qinv):
        kb = p['load'](1) * p['beta']
        rhs = jnp.concatenate([p['load'](2) * p['beta'], kb * jnp.exp(p['gc'])], axis=1)
        uw.append(rhs + _dot(qi.astype(BF16), rhs.astype(BF16)))
    sw = []
    for p, x in zip(pre, uw):
        qdec = p['load'](0) * scale * jnp.exp(p['gc'])
        sw.append(_dot(jnp.concatenate([x[:, HEAD_DIM:], qdec], axis=0).astype(BF16), p['load_s']().astype(BF16)))
    vnb = [(x[:, :HEAD_DIM] - y[:c]).astype(BF16) for x, y in zip(uw, sw)]
    o = [y[c:] + _dot(qk, vb) for y, qk, vb in zip(sw, qks, vnb)]
    s_new = []
    for p, vb in zip(pre, vnb):
        kdec = (p['load'](1) * jnp.exp(p['gtot'] - p['gc'])).astype(BF16)
        s_new.append(p['load_s']() * jnp.exp(p['gtot']) + _dot_tn(kdec, vb))
    return list(zip(o, s_new))


def _delta_kernel(has_s0, emit_final, one_block, n_chunk, *refs):
    refs = list(refs)
    if one_block:
        seq_in = [[r for r in refs[3 * s:3 * s + 3] for _ in range(2)] for s in range(DELTA_SEQS)]
        refs = refs[3 * DELTA_SEQS:]
    else:
        seq_in = [refs[6 * s:6 * s + 6] for s in range(DELTA_SEQS)]
        refs = refs[6 * DELTA_SEQS:]
    s0_ref = refs.pop(0) if has_s0 else None
    of_ref, ob_ref = refs[:2]
    refs = refs[2:]
    sfin_ref = refs.pop(0) if emit_final else None
    s_scr = refs[0]

    j = pl.program_id(1)

    @pl.when(j == 0)
    def _():
        if has_s0:
            s_scr[...] = s0_ref[...]
        else:
            s_scr[...] = jnp.zeros(s_scr.shape, F32)

    c = CHUNK

    def body(ci, carry):
        units, where = [], []
        for s, (qkvf_ref, qkvb_ref, gatef_ref, gateb_ref, gtf_ref, gtb_ref) in enumerate(seq_in):
            for d in range(2):
                cidx = ci if d == 0 else n_chunk - 1 - ci
                r0 = pl.multiple_of(cidx * c, c)
                qkv_ref = qkvf_ref if d == 0 else qkvb_ref
                gate = (gatef_ref if d == 0 else gateb_ref)[pl.ds(r0, c), :]
                gt = (gtf_ref if d == 0 else gtb_ref)[cidx]
                for hd in range(N_HEADS):
                    units.append((d, hd, qkv_ref, r0, gate, gt,
                                  lambda s=s, idx=N_HEADS * d + hd: s_scr[s, idx]))
                    where.append((s, d, hd, r0))
        for (s, d, hd, r0), (o, s_new) in zip(where, _delta_units(units)):
            s_scr[s, N_HEADS * d + hd] = s_new
            (of_ref if d == 0 else ob_ref)[s, pl.ds(r0, c), hd * HEAD_DIM:(hd + 1) * HEAD_DIM] = o
        return carry

    lax.fori_loop(0, n_chunk, body, 0)

    if emit_final:
        @pl.when(j == pl.num_programs(1) - 1)
        def _():
            sfin_ref[...] = s_scr[...]


def _delta(qkv, gate, gate_t, n_seq, seq_len, row0, blk, s0):
    nblk = seq_len // blk
    n_chunk = blk // CHUNK
    b0 = row0 // blk
    has_s0 = s0 is not None
    emit_final = not has_s0
    n_state = 2 * N_HEADS
    one_block = nblk == 1
    assert n_seq % DELTA_SEQS == 0 and row0 % blk == 0

    in_specs, args = [], []
    for s in range(DELTA_SEQS):
        def fwd(b, j, s=s):
            return b0 + (b * DELTA_SEQS + s) * nblk + j

        def bwd(b, j, s=s):
            return b0 + (b * DELTA_SEQS + s) * nblk + (nblk - 1 - j)

        for arr, shape in ((qkv, (blk, 3 * MIX_A)), (gate, (blk, LANES)), (gate_t, (n_chunk, 16, CHUNK))):
            tail = (0,) * (len(shape) - 1)
            for f in (fwd,) if one_block else (fwd, bwd):
                in_specs.append(pl.BlockSpec(shape, lambda b, j, f=f, tail=tail: (f(b, j),) + tail))
                args.append(arr)
    state_spec = pl.BlockSpec((DELTA_SEQS, n_state, HEAD_DIM, HEAD_DIM), lambda b, j: (b, 0, 0, 0))
    if has_s0:
        in_specs.append(state_spec)
        args.append(s0)
    out_specs = [pl.BlockSpec((DELTA_SEQS, blk, MIX_A), lambda b, j: (b, j, 0)),
                 pl.BlockSpec((DELTA_SEQS, blk, MIX_A), lambda b, j: (b, nblk - 1 - j, 0))]
    out_shape = [jax.ShapeDtypeStruct((n_seq, seq_len, MIX_A), F32)] * 2
    if emit_final:
        out_specs.append(state_spec)
        out_shape.append(jax.ShapeDtypeStruct((n_seq, n_state, HEAD_DIM, HEAD_DIM), F32))
    outs = pl.pallas_call(
        functools.partial(_delta_kernel, has_s0, emit_final, one_block, n_chunk),
        grid=(n_seq // DELTA_SEQS, nblk),
        in_specs=in_specs,
        out_specs=out_specs,
        out_shape=out_shape,
        scratch_shapes=[pltpu.VMEM((DELTA_SEQS, n_state, HEAD_DIM, HEAD_DIM), F32)],
        compiler_params=_cparams(("parallel", "arbitrary")),
        name="delta_latent" if has_s0 else "delta_prompt",
    )(*args)
    return [o.reshape(n_seq * seq_len, MIX_A) for o in outs[:2]] + list(outs[2:])


def _outproj_kernel(lay, xp_ref, xs_ref, prow_ref, pcol_ref, mod_ref, opf_ref, opb_ref, osf_ref, osb_ref, z_ref,
                    u_ref, og_ref, n2_ref, wo_ref, wr_ref, br_ref, x1_ref, h2_ref, ids_ref, wts_ref, cnt_ref,
                    cnt_scr):
    i = pl.program_id(0)
    x0 = _load_x(lay, i, xp_ref, xs_ref, prow_ref, pcol_ref)
    m = mod_ref[0]
    o = jnp.where(lay.is_sample(i), osf_ref[...] + osb_ref[...], opf_ref[...] + opb_ref[...])
    z = z_ref[...]
    mix = _dot(u_ref[...], wo_ref[MIX_A:, :])
    for hd in range(N_HEADS):
        sl = slice(hd * HEAD_DIM, (hd + 1) * HEAD_DIM)
        oh = _rms(o[:, sl]) * og_ref[...] * _silu(z[:, sl])
        mix = mix + _dot(oh.astype(BF16), wo_ref[sl, :])
    x1 = x0 + m[2:3] * mix
    x1_ref[...] = x1
    h2 = _rms(x1) * n2_ref[...] * (1.0 + m[4:5]) + m[3:4]
    h2b = h2.astype(BF16)
    h2_ref[...] = _pack_bf16_pairs(h2b)

    logits = _dot(h2b, wr_ref[...]) + br_ref[...]
    lane = lax.broadcasted_iota(I32, logits.shape, 1)
    lane_f = lane.astype(F32)
    neg = jnp.float32(-jnp.inf)
    big = jnp.float32(LANES)

    def first_lane(mask):
        return jnp.min(jnp.where(mask, lane_f, big), axis=-1, keepdims=True)

    gl = jnp.where(lane < N_GROUPS, logits, neg)
    gmax = jnp.max(gl, axis=-1, keepdims=True)
    grp = first_lane(gl == gmax)
    p_grp = 1.0 / jnp.sum(jnp.where(lane < N_GROUPS, jnp.exp(gl - gmax), 0.0), axis=-1, keepdims=True)
    e_lane = lane_f - N_GROUPS
    in_grp = (e_lane >= grp * EXPERTS_PER_GROUP) & (e_lane < (grp + 1.0) * EXPERTS_PER_GROUP)
    el = jnp.where(in_grp, logits, neg)
    m1 = jnp.max(el, axis=-1, keepdims=True)
    i1f = first_lane(el == m1)
    el2 = jnp.where(lane_f == i1f, neg, el)
    m2 = jnp.max(el2, axis=-1, keepdims=True)
    i2f = first_lane(el2 == m2)
    i1 = i1f.astype(I32)
    i2 = i2f.astype(I32)
    e2 = jnp.exp(m2 - m1)
    w1 = p_grp / (1.0 + e2)
    w2 = p_grp * e2 / (1.0 + e2)
    wts_ref[...] = jnp.where(lane == 0, w1, jnp.where(lane == 1, w2, 0.0))

    @pl.when(i == 0)
    def _():
        cnt_scr[...] = jnp.zeros(cnt_scr.shape, F32)

    oh1 = lane == i1
    oh2 = lane == i2
    oh = jnp.where(oh1, 1.0, jnp.where(oh2, 1.0, 0.0))
    tm = logits.shape[0]
    rows = lax.broadcasted_iota(I32, (tm, tm), 0)
    cols = lax.broadcasted_iota(I32, (tm, tm), 1)
    before = jnp.where(rows > cols, 1.0, 0.0).astype(BF16)
    seen = cnt_scr[...] + _dot(before, oh.astype(BF16))
    r1 = jnp.sum(jnp.where(oh1, seen, 0.0), axis=-1, keepdims=True).astype(I32)
    r2 = jnp.sum(jnp.where(oh2, seen, 0.0), axis=-1, keepdims=True).astype(I32)
    cnt_scr[...] = cnt_scr[...] + jnp.sum(oh, axis=0, keepdims=True)
    cnt_ref[...] = cnt_scr[...]
    ids_ref[...] = jnp.where(lane == 0, i1 - N_GROUPS,
                             jnp.where(lane == 1, i2 - N_GROUPS,
                                       jnp.where(lane == 2, r1, jnp.where(lane == 3, r2, 0))))


def _outproj(lay, xp, xs, pos, mod, o_pf, o_pb, o_sf, o_sb, z, u, onorm_g, norm2_g, w_out, w_route, b_route):
    t = lay.t
    p_spec = pl.BlockSpec((lay.tile, MIX_A), lambda i: (lay.xp_blk(i), 0))
    s_spec = pl.BlockSpec((lay.tile, MIX_A), lambda i: (lay.xs_blk(i), 0))
    return pl.pallas_call(
        functools.partial(_outproj_kernel, lay),
        grid=(lay.tiles,),
        in_specs=_x_specs(lay) + [_mod_spec(lay), p_spec, p_spec, s_spec, s_spec, _row_spec(lay, MIX_A),
                                  _row_spec(lay, MIX_B), _full_spec((1, HEAD_DIM)), _full_spec((1, D_MODEL)),
                                  _full_spec(w_out.shape), _full_spec(w_route.shape),
                                  _full_spec(b_route.shape)],
        out_specs=[_row_spec(lay, D_MODEL), _row_spec(lay, D_MODEL // 2), _row_spec(lay, LANES),
                   _row_spec(lay, LANES), _full_spec((1, LANES))],
        out_shape=[jax.ShapeDtypeStruct((t, D_MODEL), F32),
                   jax.ShapeDtypeStruct((t, D_MODEL // 2), I32),
                   jax.ShapeDtypeStruct((t, LANES), I32),
                   jax.ShapeDtypeStruct((t, LANES), F32),
                   jax.ShapeDtypeStruct((1, LANES), F32)],
        scratch_shapes=[pltpu.VMEM((1, LANES), F32)],
        compiler_params=_cparams(("arbitrary",)),
        name="outproj",
    )(xp, xs, *pos, mod, o_pf, o_pb, o_sf, o_sb, z, u, onorm_g, norm2_g, w_out, w_route, b_route)


def _sc_gather(table, idx, chunk, name):
    n_rows, width = idx.shape[0], table.shape[1]
    mesh = plsc.VectorSubcoreMesh(core_axis_name="c", subcore_axis_name="s")
    n_workers = mesh.num_cores * mesh.num_subcores
    per_worker = n_rows // n_workers
    n_chunks = per_worker // chunk
    assert n_rows == n_workers * n_chunks * chunk and n_chunks % 2 == 0 and chunk % 8 == 0 and chunk <= LANES

    def body(table_hbm, idx_hbm, out_hbm, idx_v, rows_v, sem):
        base = (lax.axis_index("s") * mesh.num_cores + lax.axis_index("c")) * per_worker

        def gather(slot):
            return pltpu.make_async_copy(table_hbm.at[idx_v.at[slot]], rows_v.at[slot], sem.at[slot])

        def fetch(g, slot):
            off = pl.multiple_of(base + g * chunk, 8)
            pltpu.sync_copy(idx_hbm.at[pl.ds(off, chunk)], idx_v.at[slot])
            gather(slot).start()

        for slot in range(2):
            fetch(slot, slot)

        @pl.loop(0, n_chunks, step=2)
        def _(g):
            for slot in range(2):
                off = pl.multiple_of(base + (g + slot) * chunk, 8)
                gather(slot).wait()
                pltpu.sync_copy(rows_v.at[slot], out_hbm.at[pl.ds(off, chunk)])

                @pl.when(g + slot + 2 < n_chunks)
                def _():
                    fetch(g + slot + 2, slot)

    return pl.kernel(
        body,
        out_type=jax.ShapeDtypeStruct((n_rows, width), table.dtype),
        mesh=mesh,
        scratch_types=[pltpu.VMEM((2, chunk), I32), pltpu.VMEM((2, chunk, width), table.dtype),
                       pltpu.SemaphoreType.DMA((2,))],
        name=name,
    )(table, idx)


def _sc_scatter2(src, idx0, idx1, n_out, chunk, name):
    n_rows, width = src.shape
    mesh = plsc.VectorSubcoreMesh(core_axis_name="c", subcore_axis_name="s")
    n_workers = mesh.num_cores * mesh.num_subcores
    per_worker = n_rows // n_workers
    n_chunks = per_worker // chunk
    assert n_rows == n_workers * n_chunks * chunk and n_chunks % 2 == 0 and chunk % 8 == 0 and chunk <= LANES

    def body(src_hbm, i0_hbm, i1_hbm, out_hbm, i0_v, i1_v, rows_v, sem_in, sem_out):
        base = (lax.axis_index("s") * mesh.num_cores + lax.axis_index("c")) * per_worker

        def rows_in(g, slot):
            off = pl.multiple_of(base + g * chunk, 8)
            return pltpu.make_async_copy(src_hbm.at[pl.ds(off, chunk)], rows_v.at[slot], sem_in.at[slot])

        def fetch(g, slot):
            off = pl.multiple_of(base + g * chunk, 8)
            pltpu.sync_copy(i0_hbm.at[pl.ds(off, chunk)], i0_v.at[slot])
            pltpu.sync_copy(i1_hbm.at[pl.ds(off, chunk)], i1_v.at[slot])
            rows_in(g, slot).start()

        for slot in range(2):
            fetch(slot, slot)

        @pl.loop(0, n_chunks, step=2)
        def _(g):
            for slot in range(2):
                rows_in(g + slot, slot).wait()
                puts = [pltpu.make_async_copy(rows_v.at[slot], out_hbm.at[iv.at[slot]], sem_out.at[slot])
                        for iv in (i0_v, i1_v)]
                for put in puts:
                    put.start()
                for put in puts:
                    put.wait()

                @pl.when(g + slot + 2 < n_chunks)
                def _():
                    fetch(g + slot + 2, slot)

    return pl.kernel(
        body,
        out_type=jax.ShapeDtypeStruct((n_out, width), src.dtype),
        mesh=mesh,
        scratch_types=[pltpu.VMEM((2, chunk), I32), pltpu.VMEM((2, chunk), I32),
                       pltpu.VMEM((2, chunk, width), src.dtype),
                       pltpu.SemaphoreType.DMA((2,)), pltpu.SemaphoreType.DMA((2,))],
        name=name,
    )(src, idx0, idx1)


def _expert_kernel(blk_e_ref, nused_ref, xb_ref, wg_ref, wu_ref, wd_ref, yb_ref, wg_s, wu_s, wd_s):
    i = pl.program_id(0)
    nused = nused_ref[0]

    @pl.when(i < nused)
    def _():
        changed = (i == 0) | (blk_e_ref[i] != blk_e_ref[jnp.maximum(i - 1, 0)])

        @pl.when(changed)
        def _():
            wg_s[...] = wg_ref[0].astype(BF16)
            wu_s[...] = wu_ref[0].astype(BF16)
            wd_s[...] = wd_ref[0].astype(BF16)

        half = D_MODEL // 2
        xa, xb = (v.astype(BF16) for v in _unpack_bf16_pairs(xb_ref[...]))
        g = _dot(xa, wg_s[:half, :]) + _dot(xb, wg_s[half:, :])
        u = _dot(xa, wu_s[:half, :]) + _dot(xb, wu_s[half:, :])
        hmid = (_silu(g) * u).astype(BF16)
        yb_ref[...] = _pack_bf16_pairs(_dot(hmid, wd_s[...]).astype(BF16))

    @pl.when(i >= nused)
    def _():
        yb_ref[...] = jnp.zeros(yb_ref.shape, I32)


def _experts(xb, blk_e, nused, w_gate, w_up, w_down):
    nb = blk_e.shape[0]

    def weight_spec(shape):
        return pl.BlockSpec((1,) + shape, lambda i, be, nu: (be[i], 0, 0))

    grid_spec = pltpu.PrefetchScalarGridSpec(
        num_scalar_prefetch=2,
        grid=(nb,),
        in_specs=[
            pl.BlockSpec((ROUTE_BM, D_MODEL // 2), lambda i, be, nu: (i, 0)),
            weight_spec((D_MODEL, D_EXPERT)),
            weight_spec((D_MODEL, D_EXPERT)),
            weight_spec((D_EXPERT, D_MODEL)),
        ],
        out_specs=pl.BlockSpec((ROUTE_BM, D_MODEL // 2), lambda i, be, nu: (i, 0)),
        scratch_shapes=[pltpu.VMEM((D_MODEL, D_EXPERT), BF16),
                        pltpu.VMEM((D_MODEL, D_EXPERT), BF16),
                        pltpu.VMEM((D_EXPERT, D_MODEL), BF16)],
    )
    return pl.pallas_call(
        _expert_kernel,
        grid_spec=grid_spec,
        out_shape=jax.ShapeDtypeStruct((nb * ROUTE_BM, D_MODEL // 2), I32),
        compiler_params=_cparams(("arbitrary",)),
        name="expert",
    )(blk_e, nused, xb, w_gate, w_up, w_down)


def _combine_kernel(lay, y0_ref, y1_ref, x1_ref, wts_ref, mod_ref, fg_ref, yp_ref, ys_ref):
    i = pl.program_id(0)
    m = mod_ref[0]
    w = wts_ref[...]
    a0, b0 = _unpack_bf16_pairs(y0_ref[...])
    a1, b1 = _unpack_bf16_pairs(y1_ref[...])
    ff = jnp.concatenate([a0 * w[:, 0:1] + a1 * w[:, 1:2], b0 * w[:, 0:1] + b1 * w[:, 1:2]], axis=1)
    y = _rms(x1_ref[...] + m[5:6] * ff) * fg_ref[...]

    @pl.when(i < lay.tiles_p)
    def _():
        yp_ref[...] = y

    @pl.when(i >= lay.tiles_p)
    def _():
        ys_ref[...] = y


def _combine(lay, yg, x1, wts, mod, final_g):
    return pl.pallas_call(
        functools.partial(_combine_kernel, lay),
        grid=(lay.tiles,),
        in_specs=[_row_spec(lay, D_MODEL // 2),
                  pl.BlockSpec((lay.tile, D_MODEL // 2), lambda i: (i + lay.tiles, 0)),
                  _row_spec(lay, D_MODEL), _row_spec(lay, LANES), _mod_spec(lay), _full_spec((1, D_MODEL))],
        out_specs=[pl.BlockSpec((lay.tile, D_MODEL), lambda i: (lay.xp_blk(i), 0)),
                   pl.BlockSpec((lay.tile, D_MODEL), lambda i: (lay.xs_blk(i), 0))],
        out_shape=[jax.ShapeDtypeStruct((lay.t_p, D_MODEL), F32),
                   jax.ShapeDtypeStruct((lay.t_s, D_MODEL), F32)],
        compiler_params=_cparams(("arbitrary",)),
        name="combine",
    )(yg, yg, x1, wts, mod, final_g)


def _dispatch_plan(ids, counts):
    n_tok = ids.shape[0]
    padded = (counts + ROUTE_BM - 1) // ROUTE_BM * ROUTE_BM
    pad_end = jnp.cumsum(padded)
    pad_start = pad_end - padded
    expert = ids[:, 0:2]
    start = jnp.sum(jnp.where(expert[:, :, None] == jnp.arange(N_EXPERTS, dtype=I32), pad_start, 0), axis=-1)
    dest = (start + ids[:, 2:4]).astype(I32)
    nb = -(-(2 * n_tok + N_EXPERTS * (ROUTE_BM - 1)) // ROUTE_BM)
    block_start = jnp.arange(nb, dtype=I32) * ROUTE_BM
    blk_e = jnp.minimum(jnp.sum(pad_end[None, :] <= block_start[:, None], axis=1), N_EXPERTS - 1).astype(I32)
    nused = (pad_end[-1:] // ROUTE_BM).astype(I32)
    return dest[:, 0], dest[:, 1], blk_e, nused


def _grid_pos_tables(n_tokens):
    rows = n_tokens // GRID_W
    n_freq = D_MODEL // 4
    freq = jnp.exp(jnp.arange(n_freq, dtype=F32) * (-math.log(10000.0) / n_freq))

    def enc(p):
        ang = p[:, None] * freq[None, :]
        return jnp.concatenate([jnp.sin(ang), jnp.cos(ang)], axis=-1)

    return enc(jnp.arange(rows, dtype=F32)), enc(jnp.arange(GRID_W, dtype=F32))


def _lane_pad(v, offset):
    return jnp.zeros((1, LANES), F32).at[0, offset:offset + v.shape[0]].set(v.astype(F32))


def kernel(x_prompt, x_sample, state_delta, c, c_ctx, norm1_g, w_mod, b_mod, w_in, conv_qkv_w, A_log, dt_bias, onorm_g, dw_w, dw_b, cln_g, cln_b, w_out, norm2_g, w_group, b_group, w_expert, b_expert, w_e_gate, w_e_up, w_e_down, final_g):
    n_p, l_p, _ = x_prompt.shape
    n_s, l_s, _ = x_sample.shape
    lay = _Layout(n_p, l_p, n_s, l_s, TOK_TILE)
    lay_proj = _Layout(n_p, l_p, n_s, l_s, PROJ_TILE)
    depth = w_in.shape[0]
    assert depth == 1
    xp = x_prompt.reshape(lay.t_p, D_MODEL)
    xs = x_sample.reshape(lay.t_s, D_MODEL)
    pos = _grid_pos_tables(l_s)

    cond = jnp.concatenate([c_ctx[None, :], c], axis=0)
    cond = jnp.pad(cond, ((0, (-cond.shape[0]) % SUBLANES), (0, 0)))
    mod = _modulation(cond, w_mod[0], b_mod[0]).reshape(cond.shape[0], 6, D_MODEL)

    n_gate = 4 * N_HEADS
    wi = w_in[0]
    w_main = jnp.concatenate([wi[:, :4 * MIX_A], wi[:, 4 * MIX_A + n_gate:]], axis=1).astype(BF16)
    w_small = jnp.pad(wi[:, 4 * MIX_A:4 * MIX_A + n_gate], ((0, 0), (0, LANES - n_gate))).astype(BF16)
    qkv_raw, z, ug, ba = _inproj(lay_proj, xp, xs, pos, mod, norm1_g[0][None, :], w_main, w_small)

    cw = jnp.pad(conv_qkv_w[0], ((0, 8 - SHORT_CONV), (0, 0))).reshape(8, Q_GROUPS, LANES).transpose(1, 0, 2)
    dw = jnp.pad(dw_w[0], ((0, 32 - CONV_W), (0, 0))).reshape(32, U_GROUPS, LANES).transpose(1, 0, 2)
    alog = _lane_pad(A_log[0].reshape(-1), 2 * N_HEADS)
    dtb = _lane_pad(dt_bias[0].reshape(-1), 2 * N_HEADS)
    qkv, u_conf, gate = _conv(lay, qkv_raw, ug, ba, cw, dw, dw_b[0][None, :], cln_g[0][None, :],
                              cln_b[0][None, :], alog, dtb)

    gate_t = gate[:, :16].reshape(lay.t // CHUNK, CHUNK, 16).transpose(0, 2, 1)
    o_pf, o_pb, s_fin = _delta(qkv, gate, gate_t, n_p, l_p, 0, l_p, None)
    s0 = state_delta[:, 0].reshape(n_s, 2 * N_HEADS, HEAD_DIM, HEAD_DIM)
    o_sf, o_sb = _delta(qkv, gate, gate_t, n_s, l_s, lay.t_p, DELTA_BLK, s0)

    w_route = jnp.pad(jnp.concatenate([w_group[0], w_expert[0]], axis=1),
                      ((0, 0), (0, LANES - N_GROUPS - N_EXPERTS))).astype(BF16)
    b_route = _lane_pad(jnp.concatenate([b_group[0], b_expert[0]]), 0)
    x1, h2p, ids, wts, cnt = _outproj(lay_proj, xp, xs, pos, mod, o_pf, o_pb, o_sf, o_sb, z, u_conf,
                                      onorm_g[0][None, :], norm2_g[0][None, :], w_out[0].astype(BF16), w_route,
                                      b_route)

    counts = cnt[0, N_GROUPS:N_GROUPS + N_EXPERTS].astype(I32)
    dest0, dest1, blk_e, nused = _dispatch_plan(ids, counts)
    xb = _sc_scatter2(h2p, dest0, dest1, blk_e.shape[0] * ROUTE_BM, DISPATCH_CHUNK, "dispatch_scatter")
    yb = _experts(xb, blk_e, nused, w_e_gate[0], w_e_up[0], w_e_down[0])
    yg = _sc_gather(yb, jnp.concatenate([dest0, dest1]), COMBINE_CHUNK, "combine_gather")
    y_p, y_s = _combine(lay, yg, x1, wts, mod, final_g[None, :])

    new_state = s_fin.reshape(n_p, 1, 2, N_HEADS, HEAD_DIM, HEAD_DIM)
    return (y_p.reshape(x_prompt.shape), y_s.reshape(x_sample.shape), new_state)
```

```python
import functools
import math

import jax
import jax.numpy as jnp
from jax import lax
from jax.experimental import pallas as pl
from jax.experimental.pallas import tpu as pltpu
from jax.experimental.pallas import tpu_sc as plsc

F32 = jnp.float32
BF16 = jnp.bfloat16
I32 = jnp.int32

D_MODEL = 1024
MIX_A = 512
MIX_B = 512
HEAD_DIM = 128
N_HEADS = 4
SHORT_CONV = 5
CONV_W = 31
N_GROUPS = 4
EXPERTS_PER_GROUP = 8
N_EXPERTS = 32
D_EXPERT = 512
GRID_W = 64
EPS = 1e-6

LANES = 128
SUBLANES = 8
TOK_TILE = 256
PROJ_TILE = 512
CHUNK = 128
DELTA_BLK = 512
DELTA_SEQS = 2
ROUTE_BM = 512
DISPATCH_CHUNK = 64
COMBINE_CHUNK = 64
Q_HALO = 8
U_HALO = 16
VMEM_LIMIT = 56 * 1024 * 1024

HIGHEST = lax.Precision.HIGHEST


def _cparams(sem):
    return pltpu.CompilerParams(dimension_semantics=sem, vmem_limit_bytes=VMEM_LIMIT)


def _sigmoid(x):
    return 0.5 * jnp.tanh(0.5 * x) + 0.5


def _silu(x):
    return x * _sigmoid(x)


def _softplus(x):
    return jnp.maximum(x, 0.0) + jnp.log1p(jnp.exp(-jnp.abs(x)))


def _rms(x):
    return x * lax.rsqrt(jnp.mean(x * x, axis=-1, keepdims=True) + EPS)


def _pack_bf16_pairs(x):
    h = x.shape[1] // 2
    hi = pltpu.bitcast(x[:, :h].astype(F32), jnp.uint32)
    lo = pltpu.bitcast(x[:, h:].astype(F32), jnp.uint32)
    return pltpu.bitcast((hi & jnp.uint32(0xFFFF0000)) | (lo >> 16), I32)


def _unpack_bf16_pairs(p):
    u = pltpu.bitcast(p, jnp.uint32)
    return pltpu.bitcast(u & jnp.uint32(0xFFFF0000), F32), pltpu.bitcast(u << 16, F32)


def _dot(a, b):
    return jnp.dot(a, b, preferred_element_type=F32)


def _dot_nt(a, b):
    return lax.dot_general(a, b, (((1,), (1,)), ((), ())), preferred_element_type=F32)


def _dot_tn(a, b):
    return lax.dot_general(a, b, (((0,), (0,)), ((), ())), preferred_element_type=F32)


def _mod_kernel(cond_ref, w_ref, b_ref, o_ref):
    s = _silu(cond_ref[...])
    o_ref[...] = jnp.dot(s, w_ref[...], preferred_element_type=F32, precision=HIGHEST) + b_ref[...]


def _modulation(cond, w_mod, b_mod):
    n = cond.shape[0]
    tn = D_MODEL
    return pl.pallas_call(
        _mod_kernel,
        grid=(6 * D_MODEL // tn,),
        in_specs=[pl.BlockSpec((n, D_MODEL), lambda j: (0, 0)),
                  pl.BlockSpec((D_MODEL, tn), lambda j: (0, j)),
                  pl.BlockSpec((1, tn), lambda j: (0, j))],
        out_specs=pl.BlockSpec((n, tn), lambda j: (0, j)),
        out_shape=jax.ShapeDtypeStruct((n, 6 * D_MODEL), F32),
        compiler_params=_cparams(("parallel",)),
        name="mod",
    )(cond, w_mod, b_mod.reshape(1, -1))


class _Layout:
    def __init__(self, n_p, l_p, n_s, l_s, tile):
        self.n_p, self.l_p, self.n_s, self.l_s, self.tile = n_p, l_p, n_s, l_s, tile
        self.t_p = n_p * l_p
        self.t_s = n_s * l_s
        self.t = self.t_p + self.t_s
        assert self.t_p % tile == 0 and l_s % tile == 0 and l_s % DELTA_BLK == 0
        self.tiles_p = self.t_p // tile
        self.tiles_s = self.t_s // tile
        self.tiles = self.tiles_p + self.tiles_s
        self.tps_p = max(l_p // tile, 1)
        self.tps_s = l_s // tile

    def is_sample(self, i):
        return i >= self.tiles_p

    def mod_row(self, i):
        return jnp.where(i < self.tiles_p, 0, 1 + (i - self.tiles_p) // self.tps_s)

    def pos_blk(self, i):
        return jnp.where(i < self.tiles_p, 0, (i - self.tiles_p) % self.tps_s)

    def xp_blk(self, i):
        return jnp.minimum(i, self.tiles_p - 1)

    def xs_blk(self, i):
        return jnp.maximum(i - self.tiles_p, 0)

    def seq_pos(self, i):
        in_s = i >= self.tiles_p
        pos = jnp.where(in_s, (i - self.tiles_p) % self.tps_s, i % self.tps_p)
        n = jnp.where(in_s, self.tps_s, self.tps_p)
        return pos, n


def _load_x(lay, i, xp_ref, xs_ref, prow_ref, pcol_ref):
    n_grid_rows = lay.tile // GRID_W
    half = D_MODEL // 2
    prow = jnp.concatenate([jnp.broadcast_to(prow_ref[r:r + 1, :], (GRID_W, half)) for r in range(n_grid_rows)],
                           axis=0)
    pcol = jnp.concatenate([pcol_ref[...]] * n_grid_rows, axis=0)
    pos = jnp.concatenate([prow, pcol], axis=1)
    return jnp.where(lay.is_sample(i), xs_ref[...] + pos, xp_ref[...])


def _x_specs(lay):
    assert lay.tile % (SUBLANES * GRID_W) == 0
    return [pl.BlockSpec((lay.tile, D_MODEL), lambda i: (lay.xp_blk(i), 0)),
            pl.BlockSpec((lay.tile, D_MODEL), lambda i: (lay.xs_blk(i), 0)),
            pl.BlockSpec((lay.tile // GRID_W, D_MODEL // 2), lambda i: (lay.pos_blk(i), 0)),
            _full_spec((GRID_W, D_MODEL // 2))]


def _mod_spec(lay):
    return pl.BlockSpec((1, 6, D_MODEL), lambda i: (lay.mod_row(i), 0, 0))


def _row_spec(lay, width):
    return pl.BlockSpec((lay.tile, width), lambda i: (i, 0))


def _full_spec(shape):
    nd = len(shape)
    return pl.BlockSpec(shape, lambda i: (0,) * nd)


def _inproj_kernel(lay, xp_ref, xs_ref, prow_ref, pcol_ref, mod_ref, g_ref, wm_ref, ws_ref,
                   qkv_ref, z_ref, ug_ref, ba_ref):
    i = pl.program_id(0)
    x = _load_x(lay, i, xp_ref, xs_ref, prow_ref, pcol_ref)
    m = mod_ref[0]
    h = _rms(x) * g_ref[...] * (1.0 + m[1:2]) + m[0:1]
    hb = h.astype(BF16)
    qkv_ref[...] = _dot(hb, wm_ref[:, 0:3 * MIX_A])
    z_ref[...] = _dot(hb, wm_ref[:, 3 * MIX_A:4 * MIX_A])
    glu = _dot(hb, wm_ref[:, 4 * MIX_A:4 * MIX_A + 2 * MIX_B])
    ug_ref[...] = glu[:, :MIX_B] * _sigmoid(glu[:, MIX_B:])
    ba_ref[...] = _dot(hb, ws_ref[...])


def _inproj(lay, xp, xs, pos, mod, norm_g, w_main, w_small):
    t = lay.t
    return pl.pallas_call(
        functools.partial(_inproj_kernel, lay),
        grid=(lay.tiles,),
        in_specs=_x_specs(lay) + [_mod_spec(lay), _full_spec((1, D_MODEL)),
                                  _full_spec(w_main.shape), _full_spec(w_small.shape)],
        out_specs=[_row_spec(lay, 3 * MIX_A), _row_spec(lay, MIX_A), _row_spec(lay, MIX_B),
                   _row_spec(lay, LANES)],
        out_shape=[jax.ShapeDtypeStruct((t, 3 * MIX_A), F32),
                   jax.ShapeDtypeStruct((t, MIX_A), F32),
                   jax.ShapeDtypeStruct((t, MIX_B), F32),
                   jax.ShapeDtypeStruct((t, LANES), F32)],
        compiler_params=_cparams(("parallel",)),
        name="inproj",
    )(xp, xs, *pos, mod, norm_g, w_main, w_small)


CONV_PITCH = 33
CONV_OUT_ROWS = SUBLANES * CONV_PITCH
POST_ROWS = 64
Q_EXT_ROWS = 280
U_EXT_ROWS = 296
Q_GROUPS = 3 * MIX_A // LANES
U_GROUPS = MIX_B // LANES
CONV_J_BLOCK = 11


def _strided_conv(ext_scr, res_scr, w_ref, g, off, n_taps):
    for j0 in range(0, CONV_PITCH, CONV_J_BLOCK):
        js = range(j0, min(j0 + CONV_J_BLOCK, CONV_PITCH))
        v = {m: ext_scr[g, pl.ds(off + m, SUBLANES, stride=CONV_PITCH), :]
             for m in range(js[0], js[-1] + n_taps)}
        for j in js:
            acc = v[j] * w_ref[g, 0:1, :]
            for s in range(1, n_taps):
                acc = acc + v[j + s] * w_ref[g, s:s + 1, :]
            res_scr[g, pl.ds(j, SUBLANES, stride=CONV_PITCH), :] = acc


def _conv_kernel(lay, qc_ref, qp_ref, qn_ref, uc_ref, up_ref, un_ref, ba_ref, cw_ref, dw_ref,
                 dwb_ref, lng_ref, lnb_ref, alog_ref, dtb_ref,
                 qkv_ref, uo_ref, gate_ref, eq_scr, eu_scr, rq_scr, ru_scr):
    i = pl.program_id(0)
    pos, n = lay.seq_pos(i)
    has_prev = pos != 0
    has_next = pos != n - 1
    qp = jnp.where(has_prev, qp_ref[...], 0.0)
    qn = jnp.where(has_next, qn_ref[...], 0.0)
    for g in range(Q_GROUPS):
        sl = slice(g * LANES, (g + 1) * LANES)
        eq_scr[g, 0:Q_HALO, :] = qp[:, sl]
        eq_scr[g, Q_HALO:Q_HALO + TOK_TILE, :] = qc_ref[:, sl]
        eq_scr[g, Q_HALO + TOK_TILE:Q_HALO + TOK_TILE + Q_HALO, :] = qn[:, sl]
        eq_scr[g, TOK_TILE + 2 * Q_HALO:, :] = jnp.zeros((Q_EXT_ROWS - TOK_TILE - 2 * Q_HALO, LANES), F32)
    up = jnp.where(has_prev, up_ref[...], 0.0)
    un_ = jnp.where(has_next, un_ref[...], 0.0)
    for g in range(U_GROUPS):
        sl = slice(g * LANES, (g + 1) * LANES)
        eu_scr[g, 0:U_HALO, :] = up[:, sl]
        eu_scr[g, U_HALO:U_HALO + TOK_TILE, :] = uc_ref[:, sl]
        eu_scr[g, U_HALO + TOK_TILE:U_HALO + TOK_TILE + U_HALO, :] = un_[:, sl]
        eu_scr[g, TOK_TILE + 2 * U_HALO:, :] = jnp.zeros((U_EXT_ROWS - TOK_TILE - 2 * U_HALO, LANES), F32)

    def q_group(g, carry):
        _strided_conv(eq_scr, rq_scr, cw_ref, g, Q_HALO - SHORT_CONV // 2, SHORT_CONV)
        return carry

    def u_group(g, carry):
        _strided_conv(eu_scr, ru_scr, dw_ref, g, U_HALO - CONV_W // 2, CONV_W)
        return carry

    lax.fori_loop(0, Q_GROUPS, q_group, 0)
    lax.fori_loop(0, U_GROUPS, u_group, 0)

    for rc in range(TOK_TILE // POST_ROWS):
        r0 = rc * POST_ROWS
        for g in range(Q_GROUPS):
            y = _silu(rq_scr[g, r0:r0 + POST_ROWS, :])
            if g < 2 * N_HEADS:
                y = y * lax.rsqrt(jnp.sum(y * y, axis=-1, keepdims=True) + EPS)
            qkv_ref[r0:r0 + POST_ROWS, g * LANES:(g + 1) * LANES] = y
        u = jnp.concatenate([ru_scr[g, r0:r0 + POST_ROWS, :] for g in range(U_GROUPS)], axis=1) + dwb_ref[...]
        uc = u - jnp.mean(u, axis=-1, keepdims=True)
        un = uc * lax.rsqrt(jnp.mean(uc * uc, axis=-1, keepdims=True) + EPS)
        uo_ref[r0:r0 + POST_ROWS, :] = _silu(un * lng_ref[...] + lnb_ref[...]).astype(BF16)

    x = ba_ref[...]
    g = -jnp.exp(alog_ref[...]) * _softplus(x + dtb_ref[...])
    beta = _sigmoid(x)
    rows = lax.broadcasted_iota(I32, (CHUNK, CHUNK), 0)
    cols = lax.broadcasted_iota(I32, (CHUNK, CHUNK), 1)
    tri_lo = (rows >= cols).astype(F32)
    tri_up = (rows <= cols).astype(F32)
    lane = lax.broadcasted_iota(I32, (CHUNK, LANES), 1)
    for ch in range(TOK_TILE // CHUNK):
        sl = slice(ch * CHUNK, (ch + 1) * CHUNK)
        pre = jnp.dot(tri_lo, g[sl], preferred_element_type=F32, precision=HIGHEST)
        suf = jnp.dot(tri_up, g[sl], preferred_element_type=F32, precision=HIGHEST)
        gate_ref[sl, :] = jnp.where(lane < 2 * N_HEADS, beta[sl], jnp.where(lane < 3 * N_HEADS, pre, suf))


def _conv(lay, qkv_raw, ug, ba, cw, dw, dwb, lng, lnb, alog, dtb):
    t = lay.t
    qh = TOK_TILE // Q_HALO
    uh = TOK_TILE // U_HALO
    n_qh = t // Q_HALO
    n_uh = t // U_HALO
    in_specs = [
        _row_spec(lay, 3 * MIX_A),
        pl.BlockSpec((Q_HALO, 3 * MIX_A), lambda i: (jnp.maximum(i * qh - 1, 0), 0)),
        pl.BlockSpec((Q_HALO, 3 * MIX_A), lambda i: (jnp.minimum((i + 1) * qh, n_qh - 1), 0)),
        _row_spec(lay, MIX_B),
        pl.BlockSpec((U_HALO, MIX_B), lambda i: (jnp.maximum(i * uh - 1, 0), 0)),
        pl.BlockSpec((U_HALO, MIX_B), lambda i: (jnp.minimum((i + 1) * uh, n_uh - 1), 0)),
        _row_spec(lay, LANES),
        _full_spec(cw.shape), _full_spec(dw.shape), _full_spec(dwb.shape),
        _full_spec(lng.shape), _full_spec(lnb.shape), _full_spec(alog.shape), _full_spec(dtb.shape),
    ]
    return pl.pallas_call(
        functools.partial(_conv_kernel, lay),
        grid=(lay.tiles,),
        in_specs=in_specs,
        out_specs=[_row_spec(lay, 3 * MIX_A), _row_spec(lay, MIX_B), _row_spec(lay, LANES)],
        out_shape=[jax.ShapeDtypeStruct((t, 3 * MIX_A), F32),
                   jax.ShapeDtypeStruct((t, MIX_B), BF16),
                   jax.ShapeDtypeStruct((t, LANES), F32)],
        scratch_shapes=[pltpu.VMEM((Q_GROUPS, Q_EXT_ROWS, LANES), F32),
                        pltpu.VMEM((U_GROUPS, U_EXT_ROWS, LANES), F32),
                        pltpu.VMEM((Q_GROUPS, CONV_OUT_ROWS, LANES), F32),
                        pltpu.VMEM((U_GROUPS, CONV_OUT_ROWS, LANES), F32)],
        compiler_params=_cparams(("parallel",)),
        name="conv",
    )(qkv_raw, qkv_raw, qkv_raw, ug, ug, ug, ba, cw, dw, dwb, lng, lnb, alog, dtb)


INV_BASE = 8


def _b16(xs):
    return [x.astype(BF16) for x in xs]


def _tri_inverse_minus_eye(nmats, rows, cols):
    assert INV_BASE == 8
    c = nmats[0].shape[0]
    shift = int(math.log2(INV_BASE))
    same = (rows >> shift) == (cols >> shift)
    n1 = [jnp.where(same, n, 0.0) for n in nmats]
    n1b = _b16(n1)
    n2 = [_dot(x, x) for x in n1b]
    n2b = _b16(n2)
    r = [_dot(jnp.concatenate([a, b], axis=0), b) for a, b in zip(n1b, n2b)]
    q = [a + b + x[:c] for a, b, x in zip(n1, n2, r)]
    n4 = [x[c:] for x in r]
    qn4 = [_dot(a, b) for a, b in zip(_b16(q), _b16(n4))]
    q = [a + b + x for a, b, x in zip(q, n4, qn4)]
    while (1 << shift) < c:
        off = ((rows >> (shift + 1)) == (cols >> (shift + 1))) & ((rows >> shift) != (cols >> shift))
        a = [jnp.where(off, -n, 0.0) for n in nmats]
        x = [ai + _dot(ab, qb) for ai, ab, qb in zip(a, _b16(a), _b16(q))]
        qx = [_dot(qb, xb) for qb, xb in zip(_b16(q), _b16(x))]
        q = [qi - xi - qxi for qi, xi, qxi in zip(q, x, qx)]
        shift += 1
    return q


def _delta_units(units):
    c = CHUNK
    scale = HEAD_DIM ** -0.5
    rows = lax.broadcasted_iota(I32, (c, c), 0)
    cols = lax.broadcasted_iota(I32, (c, c), 1)
    pre = []
    for d, hd, qkv_ref, r0, gate, gt, load_s in units:
        lane = 2 * N_HEADS + N_HEADS * d + hd
        last = c - 1 if d == 0 else 0
        pre.append(dict(
            d=d, load_s=load_s,
            load=lambda part, qkv_ref=qkv_ref, r0=r0, hd=hd: qkv_ref[
                pl.ds(r0, c), part * MIX_A + hd * HEAD_DIM:part * MIX_A + (hd + 1) * HEAD_DIM],
            beta=gate[:, N_HEADS * d + hd:N_HEADS * d + hd + 1],
            gc=gate[:, lane:lane + 1],
            gr=gt[lane:lane + 1, :],
            gtot=gate[last:last + 1, lane:lane + 1]))

    ak = []
    for p in pre:
        k = p['load'](1)
        lhs = jnp.concatenate([k * p['beta'], p['load'](0) * scale], axis=0).astype(BF16)
        ak.append(_dot_nt(lhs, k.astype(BF16)))
    nmats, qks = [], []
    for p, a in zip(pre, ak):
        incl = (rows >= cols) if p['d'] == 0 else (rows <= cols)
        strict = (rows > cols) if p['d'] == 0 else (rows < cols)
        dec = jnp.where(incl, jnp.exp(p['gc'] - p['gr']), 0.0)
        nmats.append(jnp.where(strict, -a[:c] * dec, 0.0))
        qks.append((a[c:] * dec).astype(BF16))
    qinv = _tri_inverse_minus_eye(nmats, rows, cols)

    uw = []
    for p, qi in zip(pre, qinv):
        kb = p['load'](1) * p['beta']
        rhs = jnp.concatenate([p['load'](2) * p['beta'], kb * jnp.exp(p['gc'])], axis=1)
        uw.append(rhs + _dot(qi.astype(BF16), rhs.astype(BF16)))
    sw = []
    for p, x in zip(pre, uw):
        qdec = p['load'](0) * scale * jnp.exp(p['gc'])
        sw.append(_dot(jnp.concatenate([x[:, HEAD_DIM:], qdec], axis=0).astype(BF16), p['load_s']().astype(BF16)))
    vnb = [(x[:, :HEAD_DIM] - y[:c]).astype(BF16) for x, y in zip(uw, sw)]
    o = [y[c:] + _dot(qk, vb) for y, qk, vb in zip(sw, qks, vnb)]
    s_new = []
    for p, vb in zip(pre, vnb):
        kdec = (p['load'](1) * jnp.exp(p['gtot'] - p['gc'])).astype(BF16)
        s_new.append(p['load_s']() * jnp.exp(p['gtot']) + _dot_tn(kdec, vb))
    return list(zip(o, s_new))


def _delta_kernel(has_s0, emit_final, one_block, n_chunk, *refs):
    refs = list(refs)
    if one_block:
        seq_in = [[r for r in refs[3 * s:3 * s + 3] for _ in range(2)] for s in range(DELTA_SEQS)]
        refs = refs[3 * DELTA_SEQS:]
    else:
        seq_in = [refs[6 * s:6 * s + 6] for s in range(DELTA_SEQS)]
        refs = refs[6 * DELTA_SEQS:]
    s0_ref = refs.pop(0) if has_s0 else None
    of_ref, ob_ref = refs[:2]
    refs = refs[2:]
    sfin_ref = refs.pop(0) if emit_final else None
    s_scr = refs[0]

    j = pl.program_id(1)

    @pl.when(j == 0)
    def _():
        if has_s0:
            s_scr[...] = s0_ref[...]
        else:
            s_scr[...] = jnp.zeros(s_scr.shape, F32)

    c = CHUNK

    def body(ci, carry):
        units, where = [], []
        for s, (qkvf_ref, qkvb_ref, gatef_ref, gateb_ref, gtf_ref, gtb_ref) in enumerate(seq_in):
            for d in range(2):
                cidx = ci if d == 0 else n_chunk - 1 - ci
                r0 = pl.multiple_of(cidx * c, c)
                qkv_ref = qkvf_ref if d == 0 else qkvb_ref
                gate = (gatef_ref if d == 0 else gateb_ref)[pl.ds(r0, c), :]
                gt = (gtf_ref if d == 0 else gtb_ref)[cidx]
                for hd in range(N_HEADS):
                    units.append((d, hd, qkv_ref, r0, gate, gt,
                                  lambda s=s, idx=N_HEADS * d + hd: s_scr[s, idx]))
                    where.append((s, d, hd, r0))
        for (s, d, hd, r0), (o, s_new) in zip(where, _delta_units(units)):
            s_scr[s, N_HEADS * d + hd] = s_new
            (of_ref if d == 0 else ob_ref)[s, pl.ds(r0, c), hd * HEAD_DIM:(hd + 1) * HEAD_DIM] = o
        return carry

    lax.fori_loop(0, n_chunk, body, 0)

    if emit_final:
        @pl.when(j == pl.num_programs(1) - 1)
        def _():
            sfin_ref[...] = s_scr[...]


def _delta(qkv, gate, gate_t, n_seq, seq_len, row0, blk, s0):
    nblk = seq_len // blk
    n_chunk = blk // CHUNK
    b0 = row0 // blk
    has_s0 = s0 is not None
    emit_final = not has_s0
    n_state = 2 * N_HEADS
    one_block = nblk == 1
    assert n_seq % DELTA_SEQS == 0 and row0 % blk == 0

    in_specs, args = [], []
    for s in range(DELTA_SEQS):
        def fwd(b, j, s=s):
            return b0 + (b * DELTA_SEQS + s) * nblk + j

        def bwd(b, j, s=s):
            return b0 + (b * DELTA_SEQS + s) * nblk + (nblk - 1 - j)

        for arr, shape in ((qkv, (blk, 3 * MIX_A)), (gate, (blk, LANES)), (gate_t, (n_chunk, 16, CHUNK))):
            tail = (0,) * (len(shape) - 1)
            for f in (fwd,) if one_block else (fwd, bwd):
                in_specs.append(pl.BlockSpec(shape, lambda b, j, f=f, tail=tail: (f(b, j),) + tail))
                args.append(arr)
    state_spec = pl.BlockSpec((DELTA_SEQS, n_state, HEAD_DIM, HEAD_DIM), lambda b, j: (b, 0, 0, 0))
    if has_s0:
        in_specs.append(state_spec)
        args.append(s0)
    out_specs = [pl.BlockSpec((DELTA_SEQS, blk, MIX_A), lambda b, j: (b, j, 0)),
                 pl.BlockSpec((DELTA_SEQS, blk, MIX_A), lambda b, j: (b, nblk - 1 - j, 0))]
    out_shape = [jax.ShapeDtypeStruct((n_seq, seq_len, MIX_A), F32)] * 2
    if emit_final:
        out_specs.append(state_spec)
        out_shape.append(jax.ShapeDtypeStruct((n_seq, n_state, HEAD_DIM, HEAD_DIM), F32))
    outs = pl.pallas_call(
        functools.partial(_delta_kernel, has_s0, emit_final, one_block, n_chunk),
        grid=(n_seq // DELTA_SEQS, nblk),
        in_specs=in_specs,
        out_specs=out_specs,
        out_shape=out_shape,
        scratch_shapes=[pltpu.VMEM((DELTA_SEQS, n_state, HEAD_DIM, HEAD_DIM), F32)],
        compiler_params=_cparams(("parallel", "arbitrary")),
        name="delta_latent" if has_s0 else "delta_prompt",
    )(*args)
    return [o.reshape(n_seq * seq_len, MIX_A) for o in outs[:2]] + list(outs[2:])


def _outproj_kernel(lay, xp_ref, xs_ref, prow_ref, pcol_ref, mod_ref, opf_ref, opb_ref, osf_ref, osb_ref, z_ref,
                    u_ref, og_ref, n2_ref, wo_ref, wr_ref, br_ref, x1_ref, h2_ref, ids_ref, wts_ref, cnt_ref,
                    cnt_scr):
    i = pl.program_id(0)
    x0 = _load_x(lay, i, xp_ref, xs_ref, prow_ref, pcol_ref)
    m = mod_ref[0]
    o = jnp.where(lay.is_sample(i), osf_ref[...] + osb_ref[...], opf_ref[...] + opb_ref[...])
    z = z_ref[...]
    mix = _dot(u_ref[...], wo_ref[MIX_A:, :])
    for hd in range(N_HEADS):
        sl = slice(hd * HEAD_DIM, (hd + 1) * HEAD_DIM)
        oh = _rms(o[:, sl]) * og_ref[...] * _silu(z[:, sl])
        mix = mix + _dot(oh.astype(BF16), wo_ref[sl, :])
    x1 = x0 + m[2:3] * mix
    x1_ref[...] = x1
    h2 = _rms(x1) * n2_ref[...] * (1.0 + m[4:5]) + m[3:4]
    h2b = h2.astype(BF16)
    h2_ref[...] = _pack_bf16_pairs(h2b)

    logits = _dot(h2b, wr_ref[...]) + br_ref[...]
    lane = lax.broadcasted_iota(I32, logits.shape, 1)
    lane_f = lane.astype(F32)
    neg = jnp.float32(-jnp.inf)
    big = jnp.float32(LANES)

    def first_lane(mask):
        return jnp.min(jnp.where(mask, lane_f, big), axis=-1, keepdims=True)

    gl = jnp.where(lane < N_GROUPS, logits, neg)
    gmax = jnp.max(gl, axis=-1, keepdims=True)
    grp = first_lane(gl == gmax)
    p_grp = 1.0 / jnp.sum(jnp.where(lane < N_GROUPS, jnp.exp(gl - gmax), 0.0), axis=-1, keepdims=True)
    e_lane = lane_f - N_GROUPS
    in_grp = (e_lane >= grp * EXPERTS_PER_GROUP) & (e_lane < (grp + 1.0) * EXPERTS_PER_GROUP)
    el = jnp.where(in_grp, logits, neg)
    m1 = jnp.max(el, axis=-1, keepdims=True)
    i1f = first_lane(el == m1)
    el2 = jnp.where(lane_f == i1f, neg, el)
    m2 = jnp.max(el2, axis=-1, keepdims=True)
    i2f = first_lane(el2 == m2)
    i1 = i1f.astype(I32)
    i2 = i2f.astype(I32)
    e2 = jnp.exp(m2 - m1)
    w1 = p_grp / (1.0 + e2)
    w2 = p_grp * e2 / (1.0 + e2)
    wts_ref[...] = jnp.where(lane == 0, w1, jnp.where(lane == 1, w2, 0.0))

    @pl.when(i == 0)
    def _():
        cnt_scr[...] = jnp.zeros(cnt_scr.shape, F32)

    oh1 = lane == i1
    oh2 = lane == i2
    oh = jnp.where(oh1, 1.0, jnp.where(oh2, 1.0, 0.0))
    tm = logits.shape[0]
    rows = lax.broadcasted_iota(I32, (tm, tm), 0)
    cols = lax.broadcasted_iota(I32, (tm, tm), 1)
    before = jnp.where(rows > cols, 1.0, 0.0).astype(BF16)
    seen = cnt_scr[...] + _dot(before, oh.astype(BF16))
    r1 = jnp.sum(jnp.where(oh1, seen, 0.0), axis=-1, keepdims=True).astype(I32)
    r2 = jnp.sum(jnp.where(oh2, seen, 0.0), axis=-1, keepdims=True).astype(I32)
    cnt_scr[...] = cnt_scr[...] + jnp.sum(oh, axis=0, keepdims=True)
    cnt_ref[...] = cnt_scr[...]
    ids_ref[...] = jnp.where(lane == 0, i1 - N_GROUPS,
                             jnp.where(lane == 1, i2 - N_GROUPS,
                                       jnp.where(lane == 2, r1, jnp.where(lane == 3, r2, 0))))


def _outproj(lay, xp, xs, pos, mod, o_pf, o_pb, o_sf, o_sb, z, u, onorm_g, norm2_g, w_out, w_route, b_route):
    t = lay.t
    p_spec = pl.BlockSpec((lay.tile, MIX_A), lambda i: (lay.xp_blk(i), 0))
    s_spec = pl.BlockSpec((lay.tile, MIX_A), lambda i: (lay.xs_blk(i), 0))
    return pl.pallas_call(
        functools.partial(_outproj_kernel, lay),
        grid=(lay.tiles,),
        in_specs=_x_specs(lay) + [_mod_spec(lay), p_spec, p_spec, s_spec, s_spec, _row_spec(lay, MIX_A),
                                  _row_spec(lay, MIX_B), _full_spec((1, HEAD_DIM)), _full_spec((1, D_MODEL)),
                                  _full_spec(w_out.shape), _full_spec(w_route.shape),
                                  _full_spec(b_route.shape)],
        out_specs=[_row_spec(lay, D_MODEL), _row_spec(lay, D_MODEL // 2), _row_spec(lay, LANES),
                   _row_spec(lay, LANES), _full_spec((1, LANES))],
        out_shape=[jax.ShapeDtypeStruct((t, D_MODEL), F32),
                   jax.ShapeDtypeStruct((t, D_MODEL // 2), I32),
                   jax.ShapeDtypeStruct((t, LANES), I32),
                   jax.ShapeDtypeStruct((t, LANES), F32),
                   jax.ShapeDtypeStruct((1, LANES), F32)],
        scratch_shapes=[pltpu.VMEM((1, LANES), F32)],
        compiler_params=_cparams(("arbitrary",)),
        name="outproj",
    )(xp, xs, *pos, mod, o_pf, o_pb, o_sf, o_sb, z, u, onorm_g, norm2_g, w_out, w_route, b_route)


def _sc_gather(table, idx, chunk, name):
    n_rows, width = idx.shape[0], table.shape[1]
    mesh = plsc.VectorSubcoreMesh(core_axis_name="c", subcore_axis_name="s")
    n_workers = mesh.num_cores * mesh.num_subcores
    per_worker = n_rows // n_workers
    n_chunks = per_worker // chunk
    assert n_rows == n_workers * n_chunks * chunk and n_chunks % 2 == 0 and chunk % 8 == 0 and chunk <= LANES

    def body(table_hbm, idx_hbm, out_hbm, idx_v, rows_v, sem):
        base = (lax.axis_index("s") * mesh.num_cores + lax.axis_index("c")) * per_worker

        def gather(slot):
            return pltpu.make_async_copy(table_hbm.at[idx_v.at[slot]], rows_v.at[slot], sem.at[slot])

        def fetch(g, slot):
            off = pl.multiple_of(base + g * chunk, 8)
            pltpu.sync_copy(idx_hbm.at[pl.ds(off, chunk)], idx_v.at[slot])
            gather(slot).start()

        for slot in range(2):
            fetch(slot, slot)

        @pl.loop(0, n_chunks, step=2)
        def _(g):
            for slot in range(2):
                off = pl.multiple_of(base + (g + slot) * chunk, 8)
                gather(slot).wait()
                pltpu.sync_copy(rows_v.at[slot], out_hbm.at[pl.ds(off, chunk)])

                @pl.when(g + slot + 2 < n_chunks)
                def _():
                    fetch(g + slot + 2, slot)

    return pl.kernel(
        body,
        out_type=jax.ShapeDtypeStruct((n_rows, width), table.dtype),
        mesh=mesh,
        scratch_types=[pltpu.VMEM((2, chunk), I32), pltpu.VMEM((2, chunk, width), table.dtype),
                       pltpu.SemaphoreType.DMA((2,))],
        name=name,
    )(table, idx)


def _sc_scatter2(src, idx0, idx1, n_out, chunk, name):
    n_rows, width = src.shape
    mesh = plsc.VectorSubcoreMesh(core_axis_name="c", subcore_axis_name="s")
    n_workers = mesh.num_cores * mesh.num_subcores
    per_worker = n_rows // n_workers
    n_chunks = per_worker // chunk
    assert n_rows == n_workers * n_chunks * chunk and n_chunks % 2 == 0 and chunk % 8 == 0 and chunk <= LANES

    def body(src_hbm, i0_hbm, i1_hbm, out_hbm, i0_v, i1_v, rows_v, sem_in, sem_out):
        base = (lax.axis_index("s") * mesh.num_cores + lax.axis_index("c")) * per_worker

        def rows_in(g, slot):
            off = pl.multiple_of(base + g * chunk, 8)
            return pltpu.make_async_copy(src_hbm.at[pl.ds(off, chunk)], rows_v.at[slot], sem_in.at[slot])

        def fetch(g, slot):
            off = pl.multiple_of(base + g * chunk, 8)
            pltpu.sync_copy(i0_hbm.at[pl.ds(off, chunk)], i0_v.at[slot])
            pltpu.sync_copy(i1_hbm.at[pl.ds(off, chunk)], i1_v.at[slot])
            rows_in(g, slot).start()

        for slot in range(2):
            fetch(slot, slot)

        @pl.loop(0, n_chunks, step=2)
        def _(g):
            for slot in range(2):
                rows_in(g + slot, slot).wait()
                puts = [pltpu.make_async_copy(rows_v.at[slot], out_hbm.at[iv.at[slot]], sem_out.at[slot])
                        for iv in (i0_v, i1_v)]
                for put in puts:
                    put.start()
                for put in puts:
                    put.wait()

                @pl.when(g + slot + 2 < n_chunks)
                def _():
                    fetch(g + slot + 2, slot)

    return pl.kernel(
        body,
        out_type=jax.ShapeDtypeStruct((n_out, width), src.dtype),
        mesh=mesh,
        scratch_types=[pltpu.VMEM((2, chunk), I32), pltpu.VMEM((2, chunk), I32),
                       pltpu.VMEM((2, chunk, width), src.dtype),
                       pltpu.SemaphoreType.DMA((2,)), pltpu.SemaphoreType.DMA((2,))],
        name=name,
    )(src, idx0, idx1)


def _expert_kernel(blk_e_ref, nused_ref, xb_ref, wg_ref, wu_ref, wd_ref, yb_ref, wg_s, wu_s, wd_s):
    i = pl.program_id(0)
    nused = nused_ref[0]

    @pl.when(i < nused)
    def _():
        changed = (i == 0) | (blk_e_ref[i] != blk_e_ref[jnp.maximum(i - 1, 0)])

        @pl.when(changed)
        def _():
            wg_s[...] = wg_ref[0].astype(BF16)
            wu_s[...] = wu_ref[0].astype(BF16)
            wd_s[...] = wd_ref[0].astype(BF16)

        half = D_MODEL // 2
        xa, xb = (v.astype(BF16) for v in _unpack_bf16_pairs(xb_ref[...]))
        g = _dot(xa, wg_s[:half, :]) + _dot(xb, wg_s[half:, :])
        u = _dot(xa, wu_s[:half, :]) + _dot(xb, wu_s[half:, :])
        hmid = (_silu(g) * u).astype(BF16)
        yb_ref[...] = _pack_bf16_pairs(_dot(hmid, wd_s[...]).astype(BF16))

    @pl.when(i >= nused)
    def _():
        yb_ref[...] = jnp.zeros(yb_ref.shape, I32)


def _experts(xb, blk_e, nused, w_gate, w_up, w_down):
    nb = blk_e.shape[0]

    def weight_spec(shape):
        return pl.BlockSpec((1,) + shape, lambda i, be, nu: (be[i], 0, 0))

    grid_spec = pltpu.PrefetchScalarGridSpec(
        num_scalar_prefetch=2,
        grid=(nb,),
        in_specs=[
            pl.BlockSpec((ROUTE_BM, D_MODEL // 2), lambda i, be, nu: (i, 0)),
            weight_spec((D_MODEL, D_EXPERT)),
            weight_spec((D_MODEL, D_EXPERT)),
            weight_spec((D_EXPERT, D_MODEL)),
        ],
        out_specs=pl.BlockSpec((ROUTE_BM, D_MODEL // 2), lambda i, be, nu: (i, 0)),
        scratch_shapes=[pltpu.VMEM((D_MODEL, D_EXPERT), BF16),
                        pltpu.VMEM((D_MODEL, D_EXPERT), BF16),
                        pltpu.VMEM((D_EXPERT, D_MODEL), BF16)],
    )
    return pl.pallas_call(
        _expert_kernel,
        grid_spec=grid_spec,
        out_shape=jax.ShapeDtypeStruct((nb * ROUTE_BM, D_MODEL // 2), I32),
        compiler_params=_cparams(("arbitrary",)),
        name="expert",
    )(blk_e, nused, xb, w_gate, w_up, w_down)


def _combine_kernel(lay, y0_ref, y1_ref, x1_ref, wts_ref, mod_ref, fg_ref, yp_ref, ys_ref):
    i = pl.program_id(0)
    m = mod_ref[0]
    w = wts_ref[...]
    a0, b0 = _unpack_bf16_pairs(y0_ref[...])
    a1, b1 = _unpack_bf16_pairs(y1_ref[...])
    ff = jnp.concatenate([a0 * w[:, 0:1] + a1 * w[:, 1:2], b0 * w[:, 0:1] + b1 * w[:, 1:2]], axis=1)
    y = _rms(x1_ref[...] + m[5:6] * ff) * fg_ref[...]

    @pl.when(i < lay.tiles_p)
    def _():
        yp_ref[...] = y

    @pl.when(i >= lay.tiles_p)
    def _():
        ys_ref[...] = y


def _combine(lay, yg, x1, wts, mod, final_g):
    return pl.pallas_call(
        functools.partial(_combine_kernel, lay),
        grid=(lay.tiles,),
        in_specs=[_row_spec(lay, D_MODEL // 2),
                  pl.BlockSpec((lay.tile, D_MODEL // 2), lambda i: (i + lay.tiles, 0)),
                  _row_spec(lay, D_MODEL), _row_spec(lay, LANES), _mod_spec(lay), _full_spec((1, D_MODEL))],
        out_specs=[pl.BlockSpec((lay.tile, D_MODEL), lambda i: (lay.xp_blk(i), 0)),
                   pl.BlockSpec((lay.tile, D_MODEL), lambda i: (lay.xs_blk(i), 0))],
        out_shape=[jax.ShapeDtypeStruct((lay.t_p, D_MODEL), F32),
                   jax.ShapeDtypeStruct((lay.t_s, D_MODEL), F32)],
        compiler_params=_cparams(("arbitrary",)),
        name="combine",
    )(yg, yg, x1, wts, mod, final_g)


def _dispatch_plan(ids, counts):
    n_tok = ids.shape[0]
    padded = (counts + ROUTE_BM - 1) // ROUTE_BM * ROUTE_BM
    pad_end = jnp.cumsum(padded)
    pad_start = pad_end - padded
    expert = ids[:, 0:2]
    start = jnp.sum(jnp.where(expert[:, :, None] == jnp.arange(N_EXPERTS, dtype=I32), pad_start, 0), axis=-1)
    dest = (start + ids[:, 2:4]).astype(I32)
    nb = -(-(2 * n_tok + N_EXPERTS * (ROUTE_BM - 1)) // ROUTE_BM)
    block_start = jnp.arange(nb, dtype=I32) * ROUTE_BM
    blk_e = jnp.minimum(jnp.sum(pad_end[None, :] <= block_start[:, None], axis=1), N_EXPERTS - 1).astype(I32)
    nused = (pad_end[-1:] // ROUTE_BM).astype(I32)
    return dest[:, 0], dest[:, 1], blk_e, nused


def _grid_pos_tables(n_tokens):
    rows = n_tokens // GRID_W
    n_freq = D_MODEL // 4
    freq = jnp.exp(jnp.arange(n_freq, dtype=F32) * (-math.log(10000.0) / n_freq))

    def enc(p):
        ang = p[:, None] * freq[None, :]
        return jnp.concatenate([jnp.sin(ang), jnp.cos(ang)], axis=-1)

    return enc(jnp.arange(rows, dtype=F32)), enc(jnp.arange(GRID_W, dtype=F32))


def _lane_pad(v, offset):
    return jnp.zeros((1, LANES), F32).at[0, offset:offset + v.shape[0]].set(v.astype(F32))


def kernel(x_prompt, x_sample, state_delta, c, c_ctx, norm1_g, w_mod, b_mod, w_in, conv_qkv_w, A_log, dt_bias, onorm_g, dw_w, dw_b, cln_g, cln_b, w_out, norm2_g, w_group, b_group, w_expert, b_expert, w_e_gate, w_e_up, w_e_down, final_g):
    n_p, l_p, _ = x_prompt.shape
    n_s, l_s, _ = x_sample.shape
    lay = _Layout(n_p, l_p, n_s, l_s, TOK_TILE)
    lay_proj = _Layout(n_p, l_p, n_s, l_s, PROJ_TILE)
    depth = w_in.shape[0]
    assert depth == 1
    xp = x_prompt.reshape(lay.t_p, D_MODEL)
    xs = x_sample.reshape(lay.t_s, D_MODEL)
    pos = _grid_pos_tables(l_s)

    cond = jnp.concatenate([c_ctx[None, :], c], axis=0)
    cond = jnp.pad(cond, ((0, (-cond.shape[0]) % SUBLANES), (0, 0)))
    mod = _modulation(cond, w_mod[0], b_mod[0]).reshape(cond.shape[0], 6, D_MODEL)

    n_gate = 4 * N_HEADS
    wi = w_in[0]
    w_main = jnp.concatenate([wi[:, :4 * MIX_A], wi[:, 4 * MIX_A + n_gate:]], axis=1).astype(BF16)
    w_small = jnp.pad(wi[:, 4 * MIX_A:4 * MIX_A + n_gate], ((0, 0), (0, LANES - n_gate))).astype(BF16)
    qkv_raw, z, ug, ba = _inproj(lay_proj, xp, xs, pos, mod, norm1_g[0][None, :], w_main, w_small)

    cw = jnp.pad(conv_qkv_w[0], ((0, 8 - SHORT_CONV), (0, 0))).reshape(8, Q_GROUPS, LANES).transpose(1, 0, 2)
    dw = jnp.pad(dw_w[0], ((0, 32 - CONV_W), (0, 0))).reshape(32, U_GROUPS, LANES).transpose(1, 0, 2)
    alog = _lane_pad(A_log[0].reshape(-1), 2 * N_HEADS)
    dtb = _lane_pad(dt_bias[0].reshape(-1), 2 * N_HEADS)
    qkv, u_conf, gate = _conv(lay, qkv_raw, ug, ba, cw, dw, dw_b[0][None, :], cln_g[0][None, :],
                              cln_b[0][None, :], alog, dtb)

    gate_t = gate[:, :16].reshape(lay.t // CHUNK, CHUNK, 16).transpose(0, 2, 1)
    o_pf, o_pb, s_fin = _delta(qkv, gate, gate_t, n_p, l_p, 0, l_p, None)
    s0 = state_delta[:, 0].reshape(n_s, 2 * N_HEADS, HEAD_DIM, HEAD_DIM)
    o_sf, o_sb = _delta(qkv, gate, gate_t, n_s, l_s, lay.t_p, DELTA_BLK, s0)

    w_route = jnp.pad(jnp.concatenate([w_group[0], w_expert[0]], axis=1),
                      ((0, 0), (0, LANES - N_GROUPS - N_EXPERTS))).astype(BF16)
    b_route = _lane_pad(jnp.concatenate([b_group[0], b_expert[0]]), 0)
    x1, h2p, ids, wts, cnt = _outproj(lay_proj, xp, xs, pos, mod, o_pf, o_pb, o_sf, o_sb, z, u_conf,
                                      onorm_g[0][None, :], norm2_g[0][None, :], w_out[0].astype(BF16), w_route,
                                      b_route)

    counts = cnt[0, N_GROUPS:N_GROUPS + N_EXPERTS].astype(I32)
    dest0, dest1, blk_e, nused = _dispatch_plan(ids, counts)
    xb = _sc_scatter2(h2p, dest0, dest1, blk_e.shape[0] * ROUTE_BM, DISPATCH_CHUNK, "dispatch_scatter")
    yb = _experts(xb, blk_e, nused, w_e_gate[0], w_e_up[0], w_e_down[0])
    yg = _sc_gather(yb, jnp.concatenate([dest0, dest1]), COMBINE_CHUNK, "combine_gather")
    y_p, y_s = _combine(lay_proj, yg, x1, wts, mod, final_g[None, :])

    new_state = s_fin.reshape(n_p, 1, 2, N_HEADS, HEAD_DIM, HEAD_DIM)
    return (y_p.reshape(x_prompt.shape), y_s.reshape(x_sample.shape), new_state)
```

```python
import functools
import math

import jax
import jax.numpy as jnp
from jax import lax
from jax.experimental import pallas as pl
from jax.experimental.pallas import tpu as pltpu
from jax.experimental.pallas import tpu_sc as plsc

F32 = jnp.float32
BF16 = jnp.bfloat16
I32 = jnp.int32

D_MODEL = 1024
MIX_A = 512
MIX_B = 512
HEAD_DIM = 128
N_HEADS = 4
SHORT_CONV = 5
CONV_W = 31
N_GROUPS = 4
EXPERTS_PER_GROUP = 8
N_EXPERTS = 32
D_EXPERT = 512
GRID_W = 64
EPS = 1e-6

LANES = 128
SUBLANES = 8
TOK_TILE = 256
PROJ_TILE = 512
CHUNK = 128
DELTA_BLK = 512
DELTA_SEQS = 2
ROUTE_BM = 512
ROUTE_COLS = 8
COMBINE_TILE = 1024
DISPATCH_CHUNK = 64
COMBINE_CHUNK = 64
Q_HALO = 8
U_HALO = 16
VMEM_LIMIT = 56 * 1024 * 1024

HIGHEST = lax.Precision.HIGHEST


def _cparams(sem):
    return pltpu.CompilerParams(dimension_semantics=sem, vmem_limit_bytes=VMEM_LIMIT)


def _sigmoid(x):
    return 0.5 * jnp.tanh(0.5 * x) + 0.5


def _silu(x):
    return x * _sigmoid(x)


def _softplus(x):
    return jnp.maximum(x, 0.0) + jnp.log1p(jnp.exp(-jnp.abs(x)))


def _rms(x):
    return x * lax.rsqrt(jnp.mean(x * x, axis=-1, keepdims=True) + EPS)


def _pack_bf16_pairs(x):
    h = x.shape[1] // 2
    hi = pltpu.bitcast(x[:, :h].astype(F32), jnp.uint32)
    lo = pltpu.bitcast(x[:, h:].astype(F32), jnp.uint32)
    return pltpu.bitcast((hi & jnp.uint32(0xFFFF0000)) | (lo >> 16), I32)


def _unpack_bf16_pairs(p):
    u = pltpu.bitcast(p, jnp.uint32)
    return pltpu.bitcast(u & jnp.uint32(0xFFFF0000), F32), pltpu.bitcast(u << 16, F32)


def _dot(a, b):
    return jnp.dot(a, b, preferred_element_type=F32)


def _dot_nt(a, b):
    return lax.dot_general(a, b, (((1,), (1,)), ((), ())), preferred_element_type=F32)


def _dot_tn(a, b):
    return lax.dot_general(a, b, (((0,), (0,)), ((), ())), preferred_element_type=F32)


def _mod_kernel(cond_ref, w_ref, b_ref, o_ref):
    s = _silu(cond_ref[...])
    o_ref[...] = jnp.dot(s, w_ref[...], preferred_element_type=F32, precision=HIGHEST) + b_ref[...]


def _modulation(cond, w_mod, b_mod):
    n = cond.shape[0]
    tn = D_MODEL
    return pl.pallas_call(
        _mod_kernel,
        grid=(6 * D_MODEL // tn,),
        in_specs=[pl.BlockSpec((n, D_MODEL), lambda j: (0, 0)),
                  pl.BlockSpec((D_MODEL, tn), lambda j: (0, j)),
                  pl.BlockSpec((1, tn), lambda j: (0, j))],
        out_specs=pl.BlockSpec((n, tn), lambda j: (0, j)),
        out_shape=jax.ShapeDtypeStruct((n, 6 * D_MODEL), F32),
        compiler_params=_cparams(("parallel",)),
        name="mod",
    )(cond, w_mod, b_mod.reshape(1, -1))


class _Layout:
    def __init__(self, n_p, l_p, n_s, l_s, tile):
        self.n_p, self.l_p, self.n_s, self.l_s, self.tile = n_p, l_p, n_s, l_s, tile
        self.t_p = n_p * l_p
        self.t_s = n_s * l_s
        self.t = self.t_p + self.t_s
        assert self.t_p % tile == 0 and l_s % tile == 0 and l_s % DELTA_BLK == 0
        self.tiles_p = self.t_p // tile
        self.tiles_s = self.t_s // tile
        self.tiles = self.tiles_p + self.tiles_s
        self.tps_p = max(l_p // tile, 1)
        self.tps_s = l_s // tile

    def is_sample(self, i):
        return i >= self.tiles_p

    def mod_row(self, i):
        return jnp.where(i < self.tiles_p, 0, 1 + (i - self.tiles_p) // self.tps_s)

    def pos_blk(self, i):
        return jnp.where(i < self.tiles_p, 0, (i - self.tiles_p) % self.tps_s)

    def xp_blk(self, i):
        return jnp.minimum(i, self.tiles_p - 1)

    def xs_blk(self, i):
        return jnp.maximum(i - self.tiles_p, 0)

    def seq_pos(self, i):
        in_s = i >= self.tiles_p
        pos = jnp.where(in_s, (i - self.tiles_p) % self.tps_s, i % self.tps_p)
        n = jnp.where(in_s, self.tps_s, self.tps_p)
        return pos, n


def _load_x(lay, i, xp_ref, xs_ref, prow_ref, pcol_ref):
    n_grid_rows = lay.tile // GRID_W
    half = D_MODEL // 2
    prow = jnp.concatenate([jnp.broadcast_to(prow_ref[r:r + 1, :], (GRID_W, half)) for r in range(n_grid_rows)],
                           axis=0)
    pcol = jnp.concatenate([pcol_ref[...]] * n_grid_rows, axis=0)
    pos = jnp.concatenate([prow, pcol], axis=1)
    return jnp.where(lay.is_sample(i), xs_ref[...] + pos, xp_ref[...])


def _x_specs(lay):
    assert lay.tile % (SUBLANES * GRID_W) == 0
    return [pl.BlockSpec((lay.tile, D_MODEL), lambda i: (lay.xp_blk(i), 0)),
            pl.BlockSpec((lay.tile, D_MODEL), lambda i: (lay.xs_blk(i), 0)),
            pl.BlockSpec((lay.tile // GRID_W, D_MODEL // 2), lambda i: (lay.pos_blk(i), 0)),
            _full_spec((GRID_W, D_MODEL // 2))]


def _mod_spec(lay):
    return pl.BlockSpec((1, 6, D_MODEL), lambda i: (lay.mod_row(i), 0, 0))


def _row_spec(lay, width):
    return pl.BlockSpec((lay.tile, width), lambda i: (i, 0))


def _full_spec(shape):
    nd = len(shape)
    return pl.BlockSpec(shape, lambda i: (0,) * nd)


def _inproj_kernel(lay, xp_ref, xs_ref, prow_ref, pcol_ref, mod_ref, g_ref, wm_ref, ws_ref,
                   qkv_ref, z_ref, ug_ref, ba_ref):
    i = pl.program_id(0)
    x = _load_x(lay, i, xp_ref, xs_ref, prow_ref, pcol_ref)
    m = mod_ref[0]
    h = _rms(x) * g_ref[...] * (1.0 + m[1:2]) + m[0:1]
    hb = h.astype(BF16)
    qkv_ref[...] = _dot(hb, wm_ref[:, 0:3 * MIX_A])
    z_ref[...] = _dot(hb, wm_ref[:, 3 * MIX_A:4 * MIX_A])
    glu = _dot(hb, wm_ref[:, 4 * MIX_A:4 * MIX_A + 2 * MIX_B])
    ug_ref[...] = glu[:, :MIX_B] * _sigmoid(glu[:, MIX_B:])
    ba_ref[...] = _dot(hb, ws_ref[...])


def _inproj(lay, xp, xs, pos, mod, norm_g, w_main, w_small):
    t = lay.t
    return pl.pallas_call(
        functools.partial(_inproj_kernel, lay),
        grid=(lay.tiles,),
        in_specs=_x_specs(lay) + [_mod_spec(lay), _full_spec((1, D_MODEL)),
                                  _full_spec(w_main.shape), _full_spec(w_small.shape)],
        out_specs=[_row_spec(lay, 3 * MIX_A), _row_spec(lay, MIX_A), _row_spec(lay, MIX_B),
                   _row_spec(lay, LANES)],
        out_shape=[jax.ShapeDtypeStruct((t, 3 * MIX_A), F32),
                   jax.ShapeDtypeStruct((t, MIX_A), F32),
                   jax.ShapeDtypeStruct((t, MIX_B), F32),
                   jax.ShapeDtypeStruct((t, LANES), F32)],
        compiler_params=_cparams(("parallel",)),
        name="inproj",
    )(xp, xs, *pos, mod, norm_g, w_main, w_small)


CONV_PITCH = 33
CONV_OUT_ROWS = SUBLANES * CONV_PITCH
POST_ROWS = 64
Q_EXT_ROWS = 280
U_EXT_ROWS = 296
Q_GROUPS = 3 * MIX_A // LANES
U_GROUPS = MIX_B // LANES
CONV_J_BLOCK = 11


def _strided_conv(ext_scr, res_scr, w_ref, g, off, n_taps):
    for j0 in range(0, CONV_PITCH, CONV_J_BLOCK):
        js = range(j0, min(j0 + CONV_J_BLOCK, CONV_PITCH))
        v = {m: ext_scr[g, pl.ds(off + m, SUBLANES, stride=CONV_PITCH), :]
             for m in range(js[0], js[-1] + n_taps)}
        for j in js:
            acc = v[j] * w_ref[g, 0:1, :]
            for s in range(1, n_taps):
                acc = acc + v[j + s] * w_ref[g, s:s + 1, :]
            res_scr[g, pl.ds(j, SUBLANES, stride=CONV_PITCH), :] = acc


def _conv_kernel(lay, qc_ref, qp_ref, qn_ref, uc_ref, up_ref, un_ref, ba_ref, cw_ref, dw_ref,
                 dwb_ref, lng_ref, lnb_ref, alog_ref, dtb_ref,
                 qkv_ref, uo_ref, gate_ref, eq_scr, eu_scr, rq_scr, ru_scr):
    i = pl.program_id(0)
    pos, n = lay.seq_pos(i)
    has_prev = pos != 0
    has_next = pos != n - 1
    qp = jnp.where(has_prev, qp_ref[...], 0.0)
    qn = jnp.where(has_next, qn_ref[...], 0.0)
    for g in range(Q_GROUPS):
        sl = slice(g * LANES, (g + 1) * LANES)
        eq_scr[g, 0:Q_HALO, :] = qp[:, sl]
        eq_scr[g, Q_HALO:Q_HALO + TOK_TILE, :] = qc_ref[:, sl]
        eq_scr[g, Q_HALO + TOK_TILE:Q_HALO + TOK_TILE + Q_HALO, :] = qn[:, sl]
        eq_scr[g, TOK_TILE + 2 * Q_HALO:, :] = jnp.zeros((Q_EXT_ROWS - TOK_TILE - 2 * Q_HALO, LANES), F32)
    up = jnp.where(has_prev, up_ref[...], 0.0)
    un_ = jnp.where(has_next, un_ref[...], 0.0)
    for g in range(U_GROUPS):
        sl = slice(g * LANES, (g + 1) * LANES)
        eu_scr[g, 0:U_HALO, :] = up[:, sl]
        eu_scr[g, U_HALO:U_HALO + TOK_TILE, :] = uc_ref[:, sl]
        eu_scr[g, U_HALO + TOK_TILE:U_HALO + TOK_TILE + U_HALO, :] = un_[:, sl]
        eu_scr[g, TOK_TILE + 2 * U_HALO:, :] = jnp.zeros((U_EXT_ROWS - TOK_TILE - 2 * U_HALO, LANES), F32)

    def q_group(g, carry):
        _strided_conv(eq_scr, rq_scr, cw_ref, g, Q_HALO - SHORT_CONV // 2, SHORT_CONV)
        return carry

    def u_group(g, carry):
        _strided_conv(eu_scr, ru_scr, dw_ref, g, U_HALO - CONV_W // 2, CONV_W)
        return carry

    lax.fori_loop(0, Q_GROUPS, q_group, 0)
    lax.fori_loop(0, U_GROUPS, u_group, 0)

    for rc in range(TOK_TILE // POST_ROWS):
        r0 = rc * POST_ROWS
        for g in range(Q_GROUPS):
            y = _silu(rq_scr[g, r0:r0 + POST_ROWS, :])
            if g < 2 * N_HEADS:
                y = y * lax.rsqrt(jnp.sum(y * y, axis=-1, keepdims=True) + EPS)
            qkv_ref[r0:r0 + POST_ROWS, g * LANES:(g + 1) * LANES] = y
        u = jnp.concatenate([ru_scr[g, r0:r0 + POST_ROWS, :] for g in range(U_GROUPS)], axis=1) + dwb_ref[...]
        uc = u - jnp.mean(u, axis=-1, keepdims=True)
        un = uc * lax.rsqrt(jnp.mean(uc * uc, axis=-1, keepdims=True) + EPS)
        uo_ref[r0:r0 + POST_ROWS, :] = _silu(un * lng_ref[...] + lnb_ref[...]).astype(BF16)

    x = ba_ref[...]
    g = -jnp.exp(alog_ref[...]) * _softplus(x + dtb_ref[...])
    beta = _sigmoid(x)
    rows = lax.broadcasted_iota(I32, (CHUNK, CHUNK), 0)
    cols = lax.broadcasted_iota(I32, (CHUNK, CHUNK), 1)
    tri_lo = (rows >= cols).astype(F32)
    tri_up = (rows <= cols).astype(F32)
    lane = lax.broadcasted_iota(I32, (CHUNK, LANES), 1)
    for ch in range(TOK_TILE // CHUNK):
        sl = slice(ch * CHUNK, (ch + 1) * CHUNK)
        pre = jnp.dot(tri_lo, g[sl], preferred_element_type=F32, precision=HIGHEST)
        suf = jnp.dot(tri_up, g[sl], preferred_element_type=F32, precision=HIGHEST)
        gate_ref[sl, :] = jnp.where(lane < 2 * N_HEADS, beta[sl], jnp.where(lane < 3 * N_HEADS, pre, suf))


def _conv(lay, qkv_raw, ug, ba, cw, dw, dwb, lng, lnb, alog, dtb):
    t = lay.t
    qh = TOK_TILE // Q_HALO
    uh = TOK_TILE // U_HALO
    n_qh = t // Q_HALO
    n_uh = t // U_HALO
    in_specs = [
        _row_spec(lay, 3 * MIX_A),
        pl.BlockSpec((Q_HALO, 3 * MIX_A), lambda i: (jnp.maximum(i * qh - 1, 0), 0)),
        pl.BlockSpec((Q_HALO, 3 * MIX_A), lambda i: (jnp.minimum((i + 1) * qh, n_qh - 1), 0)),
        _row_spec(lay, MIX_B),
        pl.BlockSpec((U_HALO, MIX_B), lambda i: (jnp.maximum(i * uh - 1, 0), 0)),
        pl.BlockSpec((U_HALO, MIX_B), lambda i: (jnp.minimum((i + 1) * uh, n_uh - 1), 0)),
        _row_spec(lay, LANES),
        _full_spec(cw.shape), _full_spec(dw.shape), _full_spec(dwb.shape),
        _full_spec(lng.shape), _full_spec(lnb.shape), _full_spec(alog.shape), _full_spec(dtb.shape),
    ]
    return pl.pallas_call(
        functools.partial(_conv_kernel, lay),
        grid=(lay.tiles,),
        in_specs=in_specs,
        out_specs=[_row_spec(lay, 3 * MIX_A), _row_spec(lay, MIX_B), _row_spec(lay, LANES)],
        out_shape=[jax.ShapeDtypeStruct((t, 3 * MIX_A), F32),
                   jax.ShapeDtypeStruct((t, MIX_B), BF16),
                   jax.ShapeDtypeStruct((t, LANES), F32)],
        scratch_shapes=[pltpu.VMEM((Q_GROUPS, Q_EXT_ROWS, LANES), F32),
                        pltpu.VMEM((U_GROUPS, U_EXT_ROWS, LANES), F32),
                        pltpu.VMEM((Q_GROUPS, CONV_OUT_ROWS, LANES), F32),
                        pltpu.VMEM((U_GROUPS, CONV_OUT_ROWS, LANES), F32)],
        compiler_params=_cparams(("parallel",)),
        name="conv",
    )(qkv_raw, qkv_raw, qkv_raw, ug, ug, ug, ba, cw, dw, dwb, lng, lnb, alog, dtb)


INV_BASE = 8


def _b16(xs):
    return [x.astype(BF16) for x in xs]


def _tri_inverse_minus_eye(nmats, rows, cols):
    assert INV_BASE == 8
    c = nmats[0].shape[0]
    shift = int(math.log2(INV_BASE))
    same = (rows >> shift) == (cols >> shift)
    n1 = [jnp.where(same, n, 0.0) for n in nmats]
    n1b = _b16(n1)
    n2 = [_dot(x, x) for x in n1b]
    n2b = _b16(n2)
    r = [_dot(jnp.concatenate([a, b], axis=0), b) for a, b in zip(n1b, n2b)]
    q = [a + b + x[:c] for a, b, x in zip(n1, n2, r)]
    n4 = [x[c:] for x in r]
    qn4 = [_dot(a, b) for a, b in zip(_b16(q), _b16(n4))]
    q = [a + b + x for a, b, x in zip(q, n4, qn4)]
    while (1 << shift) < c:
        off = ((rows >> (shift + 1)) == (cols >> (shift + 1))) & ((rows >> shift) != (cols >> shift))
        a = [jnp.where(off, -n, 0.0) for n in nmats]
        x = [ai + _dot(ab, qb) for ai, ab, qb in zip(a, _b16(a), _b16(q))]
        qx = [_dot(qb, xb) for qb, xb in zip(_b16(q), _b16(x))]
        q = [qi - xi - qxi for qi, xi, qxi in zip(q, x, qx)]
        shift += 1
    return q


def _delta_units(units):
    c = CHUNK
    scale = HEAD_DIM ** -0.5
    rows = lax.broadcasted_iota(I32, (c, c), 0)
    cols = lax.broadcasted_iota(I32, (c, c), 1)
    pre = []
    for d, hd, qkv_ref, r0, gate, gt, load_s in units:
        lane = 2 * N_HEADS + N_HEADS * d + hd
        last = c - 1 if d == 0 else 0
        pre.append(dict(
            d=d, load_s=load_s,
            load=lambda part, qkv_ref=qkv_ref, r0=r0, hd=hd: qkv_ref[
                pl.ds(r0, c), part * MIX_A + hd * HEAD_DIM:part * MIX_A + (hd + 1) * HEAD_DIM],
            beta=gate[:, N_HEADS * d + hd:N_HEADS * d + hd + 1],
            gc=gate[:, lane:lane + 1],
            gr=gt[lane:lane + 1, :],
            gtot=gate[last:last + 1, lane:lane + 1]))

    ak = []
    for p in pre:
        k = p['load'](1)
        lhs = jnp.concatenate([k * p['beta'], p['load'](0) * scale], axis=0).astype(BF16)
        ak.append(_dot_nt(lhs, k.astype(BF16)))
    nmats, qks = [], []
    for p, a in zip(pre, ak):
        incl = (rows >= cols) if p['d'] == 0 else (rows <= cols)
        strict = (rows > cols) if p['d'] == 0 else (rows < cols)
        dec = jnp.where(incl, jnp.exp(p['gc'] - p['gr']), 0.0)
        nmats.append(jnp.where(strict, -a[:c] * dec, 0.0))
        qks.append((a[c:] * dec).astype(BF16))
    qinv = _tri_inverse_minus_eye(nmats, rows, cols)

    uw = []
    for p, qi in zip(pre, qinv):
        kb = p['load'](1) * p['beta']
        rhs = jnp.concatenate([p['load'](2) * p['beta'], kb * jnp.exp(p['gc'])], axis=1)
        uw.append(rhs + _dot(qi.astype(BF16), rhs.astype(BF16)))
    sw = []
    for p, x in zip(pre, uw):
        qdec = p['load'](0) * scale * jnp.exp(p['gc'])
        sw.append(_dot(jnp.concatenate([x[:, HEAD_DIM:], qdec], axis=0).astype(BF16), p['load_s']().astype(BF16)))
    vnb = [(x[:, :HEAD_DIM] - y[:c]).astype(BF16) for x, y in zip(uw, sw)]
    o = [y[c:] + _dot(qk, vb) for y, qk, vb in zip(sw, qks, vnb)]
    s_new = []
    for p, vb in zip(pre, vnb):
        kdec = (p['load'](1) * jnp.exp(p['gtot'] - p['gc'])).astype(BF16)
        s_new.append(p['load_s']() * jnp.exp(p['gtot']) + _dot_tn(kdec, vb))
    return list(zip(o, s_new))


def _delta_kernel(has_s0, emit_final, one_block, n_chunk, *refs):
    refs = list(refs)
    if one_block:
        seq_in = [[r for r in refs[3 * s:3 * s + 3] for _ in range(2)] for s in range(DELTA_SEQS)]
        refs = refs[3 * DELTA_SEQS:]
    else:
        seq_in = [refs[6 * s:6 * s + 6] for s in range(DELTA_SEQS)]
        refs = refs[6 * DELTA_SEQS:]
    s0_ref = refs.pop(0) if has_s0 else None
    of_ref, ob_ref = refs[:2]
    refs = refs[2:]
    sfin_ref = refs.pop(0) if emit_final else None
    s_scr = refs[0]

    j = pl.program_id(1)

    @pl.when(j == 0)
    def _():
        if has_s0:
            s_scr[...] = s0_ref[...]
        else:
            s_scr[...] = jnp.zeros(s_scr.shape, F32)

    c = CHUNK

    def body(ci, carry):
        units, where = [], []
        for s, (qkvf_ref, qkvb_ref, gatef_ref, gateb_ref, gtf_ref, gtb_ref) in enumerate(seq_in):
            for d in range(2):
                cidx = ci if d == 0 else n_chunk - 1 - ci
                r0 = pl.multiple_of(cidx * c, c)
                qkv_ref = qkvf_ref if d == 0 else qkvb_ref
                gate = (gatef_ref if d == 0 else gateb_ref)[pl.ds(r0, c), :]
                gt = (gtf_ref if d == 0 else gtb_ref)[cidx]
                for hd in range(N_HEADS):
                    units.append((d, hd, qkv_ref, r0, gate, gt,
                                  lambda s=s, idx=N_HEADS * d + hd: s_scr[s, idx]))
                    where.append((s, d, hd, r0))
        for (s, d, hd, r0), (o, s_new) in zip(where, _delta_units(units)):
            s_scr[s, N_HEADS * d + hd] = s_new
            (of_ref if d == 0 else ob_ref)[s, pl.ds(r0, c), hd * HEAD_DIM:(hd + 1) * HEAD_DIM] = o
        return carry

    lax.fori_loop(0, n_chunk, body, 0)

    if emit_final:
        @pl.when(j == pl.num_programs(1) - 1)
        def _():
            sfin_ref[...] = s_scr[...]


def _delta(qkv, gate, gate_t, n_seq, seq_len, row0, blk, s0):
    nblk = seq_len // blk
    n_chunk = blk // CHUNK
    b0 = row0 // blk
    has_s0 = s0 is not None
    emit_final = not has_s0
    n_state = 2 * N_HEADS
    one_block = nblk == 1
    assert n_seq % DELTA_SEQS == 0 and row0 % blk == 0

    in_specs, args = [], []
    for s in range(DELTA_SEQS):
        def fwd(b, j, s=s):
            return b0 + (b * DELTA_SEQS + s) * nblk + j

        def bwd(b, j, s=s):
            return b0 + (b * DELTA_SEQS + s) * nblk + (nblk - 1 - j)

        for arr, shape in ((qkv, (blk, 3 * MIX_A)), (gate, (blk, LANES)), (gate_t, (n_chunk, 16, CHUNK))):
            tail = (0,) * (len(shape) - 1)
            for f in (fwd,) if one_block else (fwd, bwd):
                in_specs.append(pl.BlockSpec(shape, lambda b, j, f=f, tail=tail: (f(b, j),) + tail))
                args.append(arr)
    state_spec = pl.BlockSpec((DELTA_SEQS, n_state, HEAD_DIM, HEAD_DIM), lambda b, j: (b, 0, 0, 0))
    if has_s0:
        in_specs.append(state_spec)
        args.append(s0)
    out_specs = [pl.BlockSpec((DELTA_SEQS, blk, MIX_A), lambda b, j: (b, j, 0)),
                 pl.BlockSpec((DELTA_SEQS, blk, MIX_A), lambda b, j: (b, nblk - 1 - j, 0))]
    out_shape = [jax.ShapeDtypeStruct((n_seq, seq_len, MIX_A), F32)] * 2
    if emit_final:
        out_specs.append(state_spec)
        out_shape.append(jax.ShapeDtypeStruct((n_seq, n_state, HEAD_DIM, HEAD_DIM), F32))
    outs = pl.pallas_call(
        functools.partial(_delta_kernel, has_s0, emit_final, one_block, n_chunk),
        grid=(n_seq // DELTA_SEQS, nblk),
        in_specs=in_specs,
        out_specs=out_specs,
        out_shape=out_shape,
        scratch_shapes=[pltpu.VMEM((DELTA_SEQS, n_state, HEAD_DIM, HEAD_DIM), F32)],
        compiler_params=_cparams(("parallel", "arbitrary")),
        name="delta_latent" if has_s0 else "delta_prompt",
    )(*args)
    return [o.reshape(n_seq * seq_len, MIX_A) for o in outs[:2]] + list(outs[2:])


def _outproj_kernel(lay, xp_ref, xs_ref, prow_ref, pcol_ref, mod_ref, opf_ref, opb_ref, osf_ref, osb_ref, z_ref,
                    u_ref, og_ref, n2_ref, wo_ref, wr_ref, br_ref, x1_ref, h2_ref, ids_ref, wts_ref, cnt_ref,
                    cnt_scr):
    i = pl.program_id(0)
    x0 = _load_x(lay, i, xp_ref, xs_ref, prow_ref, pcol_ref)
    m = mod_ref[0]
    o = jnp.where(lay.is_sample(i), osf_ref[...] + osb_ref[...], opf_ref[...] + opb_ref[...])
    z = z_ref[...]
    mix = _dot(u_ref[...], wo_ref[MIX_A:, :])
    for hd in range(N_HEADS):
        sl = slice(hd * HEAD_DIM, (hd + 1) * HEAD_DIM)
        oh = _rms(o[:, sl]) * og_ref[...] * _silu(z[:, sl])
        mix = mix + _dot(oh.astype(BF16), wo_ref[sl, :])
    x1 = x0 + m[2:3] * mix
    x1_ref[...] = x1
    h2 = _rms(x1) * n2_ref[...] * (1.0 + m[4:5]) + m[3:4]
    h2b = h2.astype(BF16)
    h2_ref[...] = _pack_bf16_pairs(h2b)

    logits = _dot(h2b, wr_ref[...]) + br_ref[...]
    lane = lax.broadcasted_iota(I32, logits.shape, 1)
    lane_f = lane.astype(F32)
    neg = jnp.float32(-jnp.inf)
    big = jnp.float32(LANES)

    def first_lane(mask):
        return jnp.min(jnp.where(mask, lane_f, big), axis=-1, keepdims=True)

    gl = jnp.where(lane < N_GROUPS, logits, neg)
    gmax = jnp.max(gl, axis=-1, keepdims=True)
    grp = first_lane(gl == gmax)
    p_grp = 1.0 / jnp.sum(jnp.where(lane < N_GROUPS, jnp.exp(gl - gmax), 0.0), axis=-1, keepdims=True)
    e_lane = lane_f - N_GROUPS
    in_grp = (e_lane >= grp * EXPERTS_PER_GROUP) & (e_lane < (grp + 1.0) * EXPERTS_PER_GROUP)
    el = jnp.where(in_grp, logits, neg)
    m1 = jnp.max(el, axis=-1, keepdims=True)
    i1f = first_lane(el == m1)
    el2 = jnp.where(lane_f == i1f, neg, el)
    m2 = jnp.max(el2, axis=-1, keepdims=True)
    i2f = first_lane(el2 == m2)
    i1 = i1f.astype(I32)
    i2 = i2f.astype(I32)
    e2 = jnp.exp(m2 - m1)
    w1 = p_grp / (1.0 + e2)
    w2 = p_grp * e2 / (1.0 + e2)
    col = lax.broadcasted_iota(I32, wts_ref.shape, 1)
    wts_ref[...] = jnp.where(col == 0, w1, jnp.where(col == 1, w2, 0.0))

    @pl.when(i == 0)
    def _():
        cnt_scr[...] = jnp.zeros(cnt_scr.shape, F32)

    oh1 = lane == i1
    oh2 = lane == i2
    oh = jnp.where(oh1, 1.0, jnp.where(oh2, 1.0, 0.0))
    tm = logits.shape[0]
    rows = lax.broadcasted_iota(I32, (tm, tm), 0)
    cols = lax.broadcasted_iota(I32, (tm, tm), 1)
    before = jnp.where(rows > cols, 1.0, 0.0).astype(BF16)
    seen = cnt_scr[...] + _dot(before, oh.astype(BF16))
    r1 = jnp.sum(jnp.where(oh1, seen, 0.0), axis=-1, keepdims=True).astype(I32)
    r2 = jnp.sum(jnp.where(oh2, seen, 0.0), axis=-1, keepdims=True).astype(I32)
    cnt_scr[...] = cnt_scr[...] + jnp.sum(oh, axis=0, keepdims=True)
    cnt_ref[...] = cnt_scr[...]
    ids_ref[...] = jnp.where(col == 0, i1 - N_GROUPS,
                             jnp.where(col == 1, i2 - N_GROUPS,
                                       jnp.where(col == 2, r1, jnp.where(col == 3, r2, 0))))


def _outproj(lay, xp, xs, pos, mod, o_pf, o_pb, o_sf, o_sb, z, u, onorm_g, norm2_g, w_out, w_route, b_route):
    t = lay.t
    p_spec = pl.BlockSpec((lay.tile, MIX_A), lambda i: (lay.xp_blk(i), 0))
    s_spec = pl.BlockSpec((lay.tile, MIX_A), lambda i: (lay.xs_blk(i), 0))
    return pl.pallas_call(
        functools.partial(_outproj_kernel, lay),
        grid=(lay.tiles,),
        in_specs=_x_specs(lay) + [_mod_spec(lay), p_spec, p_spec, s_spec, s_spec, _row_spec(lay, MIX_A),
                                  _row_spec(lay, MIX_B), _full_spec((1, HEAD_DIM)), _full_spec((1, D_MODEL)),
                                  _full_spec(w_out.shape), _full_spec(w_route.shape),
                                  _full_spec(b_route.shape)],
        out_specs=[_row_spec(lay, D_MODEL), _row_spec(lay, D_MODEL // 2), _row_spec(lay, ROUTE_COLS),
                   _row_spec(lay, ROUTE_COLS), _full_spec((1, LANES))],
        out_shape=[jax.ShapeDtypeStruct((t, D_MODEL), F32),
                   jax.ShapeDtypeStruct((t, D_MODEL // 2), I32),
                   jax.ShapeDtypeStruct((t, ROUTE_COLS), I32),
                   jax.ShapeDtypeStruct((t, ROUTE_COLS), F32),
                   jax.ShapeDtypeStruct((1, LANES), F32)],
        scratch_shapes=[pltpu.VMEM((1, LANES), F32)],
        compiler_params=_cparams(("arbitrary",)),
        name="outproj",
    )(xp, xs, *pos, mod, o_pf, o_pb, o_sf, o_sb, z, u, onorm_g, norm2_g, w_out, w_route, b_route)


def _sc_gather(table, idx, chunk, name):
    n_rows, width = idx.shape[0], table.shape[1]
    mesh = plsc.VectorSubcoreMesh(core_axis_name="c", subcore_axis_name="s")
    n_workers = mesh.num_cores * mesh.num_subcores
    per_worker = n_rows // n_workers
    n_chunks = per_worker // chunk
    assert n_rows == n_workers * n_chunks * chunk and n_chunks % 2 == 0 and chunk % 8 == 0 and chunk <= LANES

    def body(table_hbm, idx_hbm, out_hbm, idx_v, rows_v, sem):
        base = (lax.axis_index("s") * mesh.num_cores + lax.axis_index("c")) * per_worker

        def gather(slot):
            return pltpu.make_async_copy(table_hbm.at[idx_v.at[slot]], rows_v.at[slot], sem.at[slot])

        def fetch(g, slot):
            off = pl.multiple_of(base + g * chunk, 8)
            pltpu.sync_copy(idx_hbm.at[pl.ds(off, chunk)], idx_v.at[slot])
            gather(slot).start()

        for slot in range(2):
            fetch(slot, slot)

        @pl.loop(0, n_chunks, step=2)
        def _(g):
            for slot in range(2):
                off = pl.multiple_of(base + (g + slot) * chunk, 8)
                gather(slot).wait()
                pltpu.sync_copy(rows_v.at[slot], out_hbm.at[pl.ds(off, chunk)])

                @pl.when(g + slot + 2 < n_chunks)
                def _():
                    fetch(g + slot + 2, slot)

    return pl.kernel(
        body,
        out_type=jax.ShapeDtypeStruct((n_rows, width), table.dtype),
        mesh=mesh,
        scratch_types=[pltpu.VMEM((2, chunk), I32), pltpu.VMEM((2, chunk, width), table.dtype),
                       pltpu.SemaphoreType.DMA((2,))],
        name=name,
    )(table, idx)


def _sc_scatter2(src, idx0, idx1, n_out, chunk, name):
    n_rows, width = src.shape
    mesh = plsc.VectorSubcoreMesh(core_axis_name="c", subcore_axis_name="s")
    n_workers = mesh.num_cores * mesh.num_subcores
    per_worker = n_rows // n_workers
    n_chunks = per_worker // chunk
    assert n_rows == n_workers * n_chunks * chunk and n_chunks % 2 == 0 and chunk % 8 == 0 and chunk <= LANES

    def body(src_hbm, i0_hbm, i1_hbm, out_hbm, i0_v, i1_v, rows_v, sem_in, sem_out):
        base = (lax.axis_index("s") * mesh.num_cores + lax.axis_index("c")) * per_worker

        def rows_in(g, slot):
            off = pl.multiple_of(base + g * chunk, 8)
            return pltpu.make_async_copy(src_hbm.at[pl.ds(off, chunk)], rows_v.at[slot], sem_in.at[slot])

        def fetch(g, slot):
            off = pl.multiple_of(base + g * chunk, 8)
            pltpu.sync_copy(i0_hbm.at[pl.ds(off, chunk)], i0_v.at[slot])
            pltpu.sync_copy(i1_hbm.at[pl.ds(off, chunk)], i1_v.at[slot])
            rows_in(g, slot).start()

        for slot in range(2):
            fetch(slot, slot)

        @pl.loop(0, n_chunks, step=2)
        def _(g):
            for slot in range(2):
                rows_in(g + slot, slot).wait()
                puts = [pltpu.make_async_copy(rows_v.at[slot], out_hbm.at[iv.at[slot]], sem_out.at[slot])
                        for iv in (i0_v, i1_v)]
                for put in puts:
                    put.start()
                for put in puts:
                    put.wait()

                @pl.when(g + slot + 2 < n_chunks)
                def _():
                    fetch(g + slot + 2, slot)

    return pl.kernel(
        body,
        out_type=jax.ShapeDtypeStruct((n_out, width), src.dtype),
        mesh=mesh,
        scratch_types=[pltpu.VMEM((2, chunk), I32), pltpu.VMEM((2, chunk), I32),
                       pltpu.VMEM((2, chunk, width), src.dtype),
                       pltpu.SemaphoreType.DMA((2,)), pltpu.SemaphoreType.DMA((2,))],
        name=name,
    )(src, idx0, idx1)


def _expert_kernel(blk_e_ref, nused_ref, xb_ref, wg_ref, wu_ref, wd_ref, yb_ref, wg_s, wu_s, wd_s):
    i = pl.program_id(0)
    nused = nused_ref[0]

    @pl.when(i < nused)
    def _():
        changed = (i == 0) | (blk_e_ref[i] != blk_e_ref[jnp.maximum(i - 1, 0)])

        @pl.when(changed)
        def _():
            wg_s[...] = wg_ref[0].astype(BF16)
            wu_s[...] = wu_ref[0].astype(BF16)
            wd_s[...] = wd_ref[0].astype(BF16)

        half = D_MODEL // 2
        xa, xb = (v.astype(BF16) for v in _unpack_bf16_pairs(xb_ref[...]))
        g = _dot(xa, wg_s[:half, :]) + _dot(xb, wg_s[half:, :])
        u = _dot(xa, wu_s[:half, :]) + _dot(xb, wu_s[half:, :])
        hmid = (_silu(g) * u).astype(BF16)
        yb_ref[...] = _pack_bf16_pairs(_dot(hmid, wd_s[...]).astype(BF16))

    @pl.when(i >= nused)
    def _():
        yb_ref[...] = jnp.zeros(yb_ref.shape, I32)


def _experts(xb, blk_e, nused, w_gate, w_up, w_down):
    nb = blk_e.shape[0]

    def weight_spec(shape):
        return pl.BlockSpec((1,) + shape, lambda i, be, nu: (be[i], 0, 0))

    grid_spec = pltpu.PrefetchScalarGridSpec(
        num_scalar_prefetch=2,
        grid=(nb,),
        in_specs=[
            pl.BlockSpec((ROUTE_BM, D_MODEL // 2), lambda i, be, nu: (i, 0)),
            weight_spec((D_MODEL, D_EXPERT)),
            weight_spec((D_MODEL, D_EXPERT)),
            weight_spec((D_EXPERT, D_MODEL)),
        ],
        out_specs=pl.BlockSpec((ROUTE_BM, D_MODEL // 2), lambda i, be, nu: (i, 0)),
        scratch_shapes=[pltpu.VMEM((D_MODEL, D_EXPERT), BF16),
                        pltpu.VMEM((D_MODEL, D_EXPERT), BF16),
                        pltpu.VMEM((D_EXPERT, D_MODEL), BF16)],
    )
    return pl.pallas_call(
        _expert_kernel,
        grid_spec=grid_spec,
        out_shape=jax.ShapeDtypeStruct((nb * ROUTE_BM, D_MODEL // 2), I32),
        compiler_params=_cparams(("arbitrary",)),
        name="expert",
    )(blk_e, nused, xb, w_gate, w_up, w_down)


def _combine_kernel(lay, y0_ref, y1_ref, x1_ref, wts_ref, mod_ref, fg_ref, yp_ref, ys_ref):
    i = pl.program_id(0)
    m = mod_ref[0]
    w = wts_ref[...]
    a0, b0 = _unpack_bf16_pairs(y0_ref[...])
    a1, b1 = _unpack_bf16_pairs(y1_ref[...])
    ff = jnp.concatenate([a0 * w[:, 0:1] + a1 * w[:, 1:2], b0 * w[:, 0:1] + b1 * w[:, 1:2]], axis=1)
    y = _rms(x1_ref[...] + m[5:6] * ff) * fg_ref[...]

    @pl.when(i < lay.tiles_p)
    def _():
        yp_ref[...] = y

    @pl.when(i >= lay.tiles_p)
    def _():
        ys_ref[...] = y


def _combine(lay, yg, x1, wts, mod, final_g):
    return pl.pallas_call(
        functools.partial(_combine_kernel, lay),
        grid=(lay.tiles,),
        in_specs=[_row_spec(lay, D_MODEL // 2),
                  pl.BlockSpec((lay.tile, D_MODEL // 2), lambda i: (i + lay.tiles, 0)),
                  _row_spec(lay, D_MODEL), _row_spec(lay, ROUTE_COLS), _mod_spec(lay), _full_spec((1, D_MODEL))],
        out_specs=[pl.BlockSpec((lay.tile, D_MODEL), lambda i: (lay.xp_blk(i), 0)),
                   pl.BlockSpec((lay.tile, D_MODEL), lambda i: (lay.xs_blk(i), 0))],
        out_shape=[jax.ShapeDtypeStruct((lay.t_p, D_MODEL), F32),
                   jax.ShapeDtypeStruct((lay.t_s, D_MODEL), F32)],
        compiler_params=_cparams(("arbitrary",)),
        name="combine",
    )(yg, yg, x1, wts, mod, final_g)


def _dispatch_plan(ids, counts):
    n_tok = ids.shape[0]
    padded = (counts + ROUTE_BM - 1) // ROUTE_BM * ROUTE_BM
    pad_end = jnp.cumsum(padded)
    pad_start = pad_end - padded
    expert = ids[:, 0:2]
    start = jnp.sum(jnp.where(expert[:, :, None] == jnp.arange(N_EXPERTS, dtype=I32), pad_start, 0), axis=-1)
    dest = (start + ids[:, 2:4]).astype(I32)
    nb = -(-(2 * n_tok + N_EXPERTS * (ROUTE_BM - 1)) // ROUTE_BM)
    block_start = jnp.arange(nb, dtype=I32) * ROUTE_BM
    blk_e = jnp.minimum(jnp.sum(pad_end[None, :] <= block_start[:, None], axis=1), N_EXPERTS - 1).astype(I32)
    nused = (pad_end[-1:] // ROUTE_BM).astype(I32)
    return dest[:, 0], dest[:, 1], blk_e, nused


def _grid_pos_tables(n_tokens):
    rows = n_tokens // GRID_W
    n_freq = D_MODEL // 4
    freq = jnp.exp(jnp.arange(n_freq, dtype=F32) * (-math.log(10000.0) / n_freq))

    def enc(p):
        ang = p[:, None] * freq[None, :]
        return jnp.concatenate([jnp.sin(ang), jnp.cos(ang)], axis=-1)

    return enc(jnp.arange(rows, dtype=F32)), enc(jnp.arange(GRID_W, dtype=F32))


def _lane_pad(v, offset):
    return jnp.zeros((1, LANES), F32).at[0, offset:offset + v.shape[0]].set(v.astype(F32))


def kernel(x_prompt, x_sample, state_delta, c, c_ctx, norm1_g, w_mod, b_mod, w_in, conv_qkv_w, A_log, dt_bias, onorm_g, dw_w, dw_b, cln_g, cln_b, w_out, norm2_g, w_group, b_group, w_expert, b_expert, w_e_gate, w_e_up, w_e_down, final_g):
    n_p, l_p, _ = x_prompt.shape
    n_s, l_s, _ = x_sample.shape
    lay = _Layout(n_p, l_p, n_s, l_s, TOK_TILE)
    lay_proj = _Layout(n_p, l_p, n_s, l_s, PROJ_TILE)
    depth = w_in.shape[0]
    assert depth == 1
    xp = x_prompt.reshape(lay.t_p, D_MODEL)
    xs = x_sample.reshape(lay.t_s, D_MODEL)
    pos = _grid_pos_tables(l_s)

    cond = jnp.concatenate([c_ctx[None, :], c], axis=0)
    cond = jnp.pad(cond, ((0, (-cond.shape[0]) % SUBLANES), (0, 0)))
    mod = _modulation(cond, w_mod[0], b_mod[0]).reshape(cond.shape[0], 6, D_MODEL)

    n_gate = 4 * N_HEADS
    wi = w_in[0]
    w_main = jnp.concatenate([wi[:, :4 * MIX_A], wi[:, 4 * MIX_A + n_gate:]], axis=1).astype(BF16)
    w_small = jnp.pad(wi[:, 4 * MIX_A:4 * MIX_A + n_gate], ((0, 0), (0, LANES - n_gate))).astype(BF16)
    qkv_raw, z, ug, ba = _inproj(lay_proj, xp, xs, pos, mod, norm1_g[0][None, :], w_main, w_small)

    cw = jnp.pad(conv_qkv_w[0], ((0, 8 - SHORT_CONV), (0, 0))).reshape(8, Q_GROUPS, LANES).transpose(1, 0, 2)
    dw = jnp.pad(dw_w[0], ((0, 32 - CONV_W), (0, 0))).reshape(32, U_GROUPS, LANES).transpose(1, 0, 2)
    alog = _lane_pad(A_log[0].reshape(-1), 2 * N_HEADS)
    dtb = _lane_pad(dt_bias[0].reshape(-1), 2 * N_HEADS)
    qkv, u_conf, gate = _conv(lay, qkv_raw, ug, ba, cw, dw, dw_b[0][None, :], cln_g[0][None, :],
                              cln_b[0][None, :], alog, dtb)

    gate_t = gate[:, :16].reshape(lay.t // CHUNK, CHUNK, 16).transpose(0, 2, 1)
    o_pf, o_pb, s_fin = _delta(qkv, gate, gate_t, n_p, l_p, 0, l_p, None)
    s0 = state_delta[:, 0].reshape(n_s, 2 * N_HEADS, HEAD_DIM, HEAD_DIM)
    o_sf, o_sb = _delta(qkv, gate, gate_t, n_s, l_s, lay.t_p, DELTA_BLK, s0)

    w_route = jnp.pad(jnp.concatenate([w_group[0], w_expert[0]], axis=1),
                      ((0, 0), (0, LANES - N_GROUPS - N_EXPERTS))).astype(BF16)
    b_route = _lane_pad(jnp.concatenate([b_group[0], b_expert[0]]), 0)
    x1, h2p, ids, wts, cnt = _outproj(lay_proj, xp, xs, pos, mod, o_pf, o_pb, o_sf, o_sb, z, u_conf,
                                      onorm_g[0][None, :], norm2_g[0][None, :], w_out[0].astype(BF16), w_route,
                                      b_route)

    counts = cnt[0, N_GROUPS:N_GROUPS + N_EXPERTS].astype(I32)
    dest0, dest1, blk_e, nused = _dispatch_plan(ids, counts)
    xb = _sc_scatter2(h2p, dest0, dest1, blk_e.shape[0] * ROUTE_BM, DISPATCH_CHUNK, "dispatch_scatter")
    yb = _experts(xb, blk_e, nused, w_e_gate[0], w_e_up[0], w_e_down[0])
    yg = _sc_gather(yb, jnp.concatenate([dest0, dest1]), COMBINE_CHUNK, "combine_gather")
    y_p, y_s = _combine(_Layout(n_p, l_p, n_s, l_s, COMBINE_TILE), yg, x1, wts, mod, final_g[None, :])

    new_state = s_fin.reshape(n_p, 1, 2, N_HEADS, HEAD_DIM, HEAD_DIM)
    return (y_p.reshape(x_prompt.shape), y_s.reshape(x_sample.shape), new_state)
```

```python
import functools
import math

import jax
import jax.numpy as jnp
from jax import lax
from jax.experimental import pallas as pl
from jax.experimental.pallas import tpu as pltpu
from jax.experimental.pallas import tpu_sc as plsc

F32 = jnp.float32
BF16 = jnp.bfloat16
I32 = jnp.int32

D_MODEL = 1024
MIX_A = 512
MIX_B = 512
HEAD_DIM = 128
N_HEADS = 4
SHORT_CONV = 5
CONV_W = 31
N_GROUPS = 4
EXPERTS_PER_GROUP = 8
N_EXPERTS = 32
D_EXPERT = 512
GRID_W = 64
EPS = 1e-6

LANES = 128
SUBLANES = 8
TOK_TILE = 256
PROJ_TILE = 512
CHUNK = 128
DELTA_BLK = 512
DELTA_SEQS = 2
ROUTE_BM = 512
ROUTE_COLS = 8
GATE_ROWS = 16
COMBINE_TILE = 1024
DISPATCH_CHUNK = 64
COMBINE_CHUNK = 64
Q_HALO = 8
U_HALO = 16
VMEM_LIMIT = 56 * 1024 * 1024

HIGHEST = lax.Precision.HIGHEST


def _cparams(sem):
    return pltpu.CompilerParams(dimension_semantics=sem, vmem_limit_bytes=VMEM_LIMIT)


def _sigmoid(x):
    return 0.5 * jnp.tanh(0.5 * x) + 0.5


def _silu(x):
    return x * _sigmoid(x)


def _softplus(x):
    return jnp.maximum(x, 0.0) + jnp.log1p(jnp.exp(-jnp.abs(x)))


def _rms(x):
    return x * lax.rsqrt(jnp.mean(x * x, axis=-1, keepdims=True) + EPS)


def _pack_bf16_pairs(x):
    h = x.shape[1] // 2
    hi = pltpu.bitcast(x[:, :h].astype(F32), jnp.uint32)
    lo = pltpu.bitcast(x[:, h:].astype(F32), jnp.uint32)
    return pltpu.bitcast((hi & jnp.uint32(0xFFFF0000)) | (lo >> 16), I32)


def _unpack_bf16_pairs(p):
    u = pltpu.bitcast(p, jnp.uint32)
    return pltpu.bitcast(u & jnp.uint32(0xFFFF0000), F32), pltpu.bitcast(u << 16, F32)


def _dot(a, b):
    return jnp.dot(a, b, preferred_element_type=F32)


def _dot_nt(a, b):
    return lax.dot_general(a, b, (((1,), (1,)), ((), ())), preferred_element_type=F32)


def _dot_tn(a, b):
    return lax.dot_general(a, b, (((0,), (0,)), ((), ())), preferred_element_type=F32)


def _mod_kernel(cond_ref, w_ref, b_ref, o_ref):
    s = _silu(cond_ref[...])
    o_ref[...] = jnp.dot(s, w_ref[...], preferred_element_type=F32, precision=HIGHEST) + b_ref[...]


def _modulation(cond, w_mod, b_mod):
    n = cond.shape[0]
    tn = D_MODEL
    return pl.pallas_call(
        _mod_kernel,
        grid=(6 * D_MODEL // tn,),
        in_specs=[pl.BlockSpec((n, D_MODEL), lambda j: (0, 0)),
                  pl.BlockSpec((D_MODEL, tn), lambda j: (0, j)),
                  pl.BlockSpec((1, tn), lambda j: (0, j))],
        out_specs=pl.BlockSpec((n, tn), lambda j: (0, j)),
        out_shape=jax.ShapeDtypeStruct((n, 6 * D_MODEL), F32),
        compiler_params=_cparams(("parallel",)),
        name="mod",
    )(cond, w_mod, b_mod.reshape(1, -1))


class _Layout:
    def __init__(self, n_p, l_p, n_s, l_s, tile):
        self.n_p, self.l_p, self.n_s, self.l_s, self.tile = n_p, l_p, n_s, l_s, tile
        self.t_p = n_p * l_p
        self.t_s = n_s * l_s
        self.t = self.t_p + self.t_s
        assert self.t_p % tile == 0 and l_s % tile == 0 and l_s % DELTA_BLK == 0
        self.tiles_p = self.t_p // tile
        self.tiles_s = self.t_s // tile
        self.tiles = self.tiles_p + self.tiles_s
        self.tps_p = max(l_p // tile, 1)
        self.tps_s = l_s // tile

    def is_sample(self, i):
        return i >= self.tiles_p

    def mod_row(self, i):
        return jnp.where(i < self.tiles_p, 0, 1 + (i - self.tiles_p) // self.tps_s)

    def pos_blk(self, i):
        return jnp.where(i < self.tiles_p, 0, (i - self.tiles_p) % self.tps_s)

    def xp_blk(self, i):
        return jnp.minimum(i, self.tiles_p - 1)

    def xs_blk(self, i):
        return jnp.maximum(i - self.tiles_p, 0)

    def seq_pos(self, i):
        in_s = i >= self.tiles_p
        pos = jnp.where(in_s, (i - self.tiles_p) % self.tps_s, i % self.tps_p)
        n = jnp.where(in_s, self.tps_s, self.tps_p)
        return pos, n


def _load_x(lay, i, xp_ref, xs_ref, prow_ref, pcol_ref):
    n_grid_rows = lay.tile // GRID_W
    half = D_MODEL // 2
    prow = jnp.concatenate([jnp.broadcast_to(prow_ref[r:r + 1, :], (GRID_W, half)) for r in range(n_grid_rows)],
                           axis=0)
    pcol = jnp.concatenate([pcol_ref[...]] * n_grid_rows, axis=0)
    pos = jnp.concatenate([prow, pcol], axis=1)
    return jnp.where(lay.is_sample(i), xs_ref[...] + pos, xp_ref[...])


def _x_specs(lay):
    assert lay.tile % (SUBLANES * GRID_W) == 0
    return [pl.BlockSpec((lay.tile, D_MODEL), lambda i: (lay.xp_blk(i), 0)),
            pl.BlockSpec((lay.tile, D_MODEL), lambda i: (lay.xs_blk(i), 0)),
            pl.BlockSpec((lay.tile // GRID_W, D_MODEL // 2), lambda i: (lay.pos_blk(i), 0)),
            _full_spec((GRID_W, D_MODEL // 2))]


def _mod_spec(lay):
    return pl.BlockSpec((1, 6, D_MODEL), lambda i: (lay.mod_row(i), 0, 0))


def _row_spec(lay, width):
    return pl.BlockSpec((lay.tile, width), lambda i: (i, 0))


def _full_spec(shape):
    nd = len(shape)
    return pl.BlockSpec(shape, lambda i: (0,) * nd)


def _inproj_kernel(lay, xp_ref, xs_ref, prow_ref, pcol_ref, mod_ref, g_ref, wm_ref, ws_ref,
                   qkv_ref, z_ref, ug_ref, ba_ref):
    i = pl.program_id(0)
    x = _load_x(lay, i, xp_ref, xs_ref, prow_ref, pcol_ref)
    m = mod_ref[0]
    h = _rms(x) * g_ref[...] * (1.0 + m[1:2]) + m[0:1]
    hb = h.astype(BF16)
    qkv_ref[...] = _dot(hb, wm_ref[:, 0:3 * MIX_A])
    z_ref[...] = _dot(hb, wm_ref[:, 3 * MIX_A:4 * MIX_A])
    glu = _dot(hb, wm_ref[:, 4 * MIX_A:4 * MIX_A + 2 * MIX_B])
    ug_ref[...] = glu[:, :MIX_B] * _sigmoid(glu[:, MIX_B:])
    ba_ref[...] = _dot(hb, ws_ref[...])


def _inproj(lay, xp, xs, pos, mod, norm_g, w_main, w_small):
    t = lay.t
    return pl.pallas_call(
        functools.partial(_inproj_kernel, lay),
        grid=(lay.tiles,),
        in_specs=_x_specs(lay) + [_mod_spec(lay), _full_spec((1, D_MODEL)),
                                  _full_spec(w_main.shape), _full_spec(w_small.shape)],
        out_specs=[_row_spec(lay, 3 * MIX_A), _row_spec(lay, MIX_A), _row_spec(lay, MIX_B),
                   _row_spec(lay, LANES)],
        out_shape=[jax.ShapeDtypeStruct((t, 3 * MIX_A), F32),
                   jax.ShapeDtypeStruct((t, MIX_A), F32),
                   jax.ShapeDtypeStruct((t, MIX_B), F32),
                   jax.ShapeDtypeStruct((t, LANES), F32)],
        compiler_params=_cparams(("parallel",)),
        name="inproj",
    )(xp, xs, *pos, mod, norm_g, w_main, w_small)


CONV_PITCH = 33
CONV_OUT_ROWS = SUBLANES * CONV_PITCH
POST_ROWS = 64
Q_EXT_ROWS = 280
U_EXT_ROWS = 296
Q_GROUPS = 3 * MIX_A // LANES
U_GROUPS = MIX_B // LANES
CONV_J_BLOCK = 11


def _strided_conv(ext_scr, res_scr, w_ref, g, off, n_taps):
    for j0 in range(0, CONV_PITCH, CONV_J_BLOCK):
        js = range(j0, min(j0 + CONV_J_BLOCK, CONV_PITCH))
        v = {m: ext_scr[g, pl.ds(off + m, SUBLANES, stride=CONV_PITCH), :]
             for m in range(js[0], js[-1] + n_taps)}
        for j in js:
            acc = v[j] * w_ref[g, 0:1, :]
            for s in range(1, n_taps):
                acc = acc + v[j + s] * w_ref[g, s:s + 1, :]
            res_scr[g, pl.ds(j, SUBLANES, stride=CONV_PITCH), :] = acc


def _conv_kernel(lay, qc_ref, qp_ref, qn_ref, uc_ref, up_ref, un_ref, ba_ref, cw_ref, dw_ref,
                 dwb_ref, lng_ref, lnb_ref, alog_ref, dtb_ref,
                 qkv_ref, uo_ref, gate_ref, gt_ref, eq_scr, eu_scr, rq_scr, ru_scr):
    i = pl.program_id(0)
    pos, n = lay.seq_pos(i)
    has_prev = pos != 0
    has_next = pos != n - 1
    qp = jnp.where(has_prev, qp_ref[...], 0.0)
    qn = jnp.where(has_next, qn_ref[...], 0.0)
    for g in range(Q_GROUPS):
        sl = slice(g * LANES, (g + 1) * LANES)
        eq_scr[g, 0:Q_HALO, :] = qp[:, sl]
        eq_scr[g, Q_HALO:Q_HALO + TOK_TILE, :] = qc_ref[:, sl]
        eq_scr[g, Q_HALO + TOK_TILE:Q_HALO + TOK_TILE + Q_HALO, :] = qn[:, sl]
        eq_scr[g, TOK_TILE + 2 * Q_HALO:, :] = jnp.zeros((Q_EXT_ROWS - TOK_TILE - 2 * Q_HALO, LANES), F32)
    up = jnp.where(has_prev, up_ref[...], 0.0)
    un_ = jnp.where(has_next, un_ref[...], 0.0)
    for g in range(U_GROUPS):
        sl = slice(g * LANES, (g + 1) * LANES)
        eu_scr[g, 0:U_HALO, :] = up[:, sl]
        eu_scr[g, U_HALO:U_HALO + TOK_TILE, :] = uc_ref[:, sl]
        eu_scr[g, U_HALO + TOK_TILE:U_HALO + TOK_TILE + U_HALO, :] = un_[:, sl]
        eu_scr[g, TOK_TILE + 2 * U_HALO:, :] = jnp.zeros((U_EXT_ROWS - TOK_TILE - 2 * U_HALO, LANES), F32)

    def q_group(g, carry):
        _strided_conv(eq_scr, rq_scr, cw_ref, g, Q_HALO - SHORT_CONV // 2, SHORT_CONV)
        return carry

    def u_group(g, carry):
        _strided_conv(eu_scr, ru_scr, dw_ref, g, U_HALO - CONV_W // 2, CONV_W)
        return carry

    lax.fori_loop(0, Q_GROUPS, q_group, 0)
    lax.fori_loop(0, U_GROUPS, u_group, 0)

    for rc in range(TOK_TILE // POST_ROWS):
        r0 = rc * POST_ROWS
        for g in range(Q_GROUPS):
            y = _silu(rq_scr[g, r0:r0 + POST_ROWS, :])
            if g < 2 * N_HEADS:
                y = y * lax.rsqrt(jnp.sum(y * y, axis=-1, keepdims=True) + EPS)
            qkv_ref[r0:r0 + POST_ROWS, g * LANES:(g + 1) * LANES] = y
        u = jnp.concatenate([ru_scr[g, r0:r0 + POST_ROWS, :] for g in range(U_GROUPS)], axis=1) + dwb_ref[...]
        uc = u - jnp.mean(u, axis=-1, keepdims=True)
        un = uc * lax.rsqrt(jnp.mean(uc * uc, axis=-1, keepdims=True) + EPS)
        uo_ref[r0:r0 + POST_ROWS, :] = _silu(un * lng_ref[...] + lnb_ref[...]).astype(BF16)

    x = ba_ref[...]
    g = -jnp.exp(alog_ref[...]) * _softplus(x + dtb_ref[...])
    beta = _sigmoid(x)
    rows = lax.broadcasted_iota(I32, (CHUNK, CHUNK), 0)
    cols = lax.broadcasted_iota(I32, (CHUNK, CHUNK), 1)
    tri_lo = (rows >= cols).astype(F32)
    tri_up = (rows <= cols).astype(F32)
    lane = lax.broadcasted_iota(I32, (CHUNK, LANES), 1)
    for ch in range(TOK_TILE // CHUNK):
        sl = slice(ch * CHUNK, (ch + 1) * CHUNK)
        pre = jnp.dot(tri_lo, g[sl], preferred_element_type=F32, precision=HIGHEST)
        suf = jnp.dot(tri_up, g[sl], preferred_element_type=F32, precision=HIGHEST)
        gate = jnp.where(lane < 2 * N_HEADS, beta[sl], jnp.where(lane < 3 * N_HEADS, pre, suf))
        gate_ref[sl, :] = gate
        gt_ref[ch] = jnp.transpose(gate)[:GATE_ROWS, :]


def _conv(lay, qkv_raw, ug, ba, cw, dw, dwb, lng, lnb, alog, dtb):
    t = lay.t
    qh = TOK_TILE // Q_HALO
    uh = TOK_TILE // U_HALO
    n_qh = t // Q_HALO
    n_uh = t // U_HALO
    in_specs = [
        _row_spec(lay, 3 * MIX_A),
        pl.BlockSpec((Q_HALO, 3 * MIX_A), lambda i: (jnp.maximum(i * qh - 1, 0), 0)),
        pl.BlockSpec((Q_HALO, 3 * MIX_A), lambda i: (jnp.minimum((i + 1) * qh, n_qh - 1), 0)),
        _row_spec(lay, MIX_B),
        pl.BlockSpec((U_HALO, MIX_B), lambda i: (jnp.maximum(i * uh - 1, 0), 0)),
        pl.BlockSpec((U_HALO, MIX_B), lambda i: (jnp.minimum((i + 1) * uh, n_uh - 1), 0)),
        _row_spec(lay, LANES),
        _full_spec(cw.shape), _full_spec(dw.shape), _full_spec(dwb.shape),
        _full_spec(lng.shape), _full_spec(lnb.shape), _full_spec(alog.shape), _full_spec(dtb.shape),
    ]
    return pl.pallas_call(
        functools.partial(_conv_kernel, lay),
        grid=(lay.tiles,),
        in_specs=in_specs,
        out_specs=[_row_spec(lay, 3 * MIX_A), _row_spec(lay, MIX_B), _row_spec(lay, LANES),
                   pl.BlockSpec((TOK_TILE // CHUNK, GATE_ROWS, CHUNK), lambda i: (i, 0, 0))],
        out_shape=[jax.ShapeDtypeStruct((t, 3 * MIX_A), F32),
                   jax.ShapeDtypeStruct((t, MIX_B), BF16),
                   jax.ShapeDtypeStruct((t, LANES), F32),
                   jax.ShapeDtypeStruct((t // CHUNK, GATE_ROWS, CHUNK), F32)],
        scratch_shapes=[pltpu.VMEM((Q_GROUPS, Q_EXT_ROWS, LANES), F32),
                        pltpu.VMEM((U_GROUPS, U_EXT_ROWS, LANES), F32),
                        pltpu.VMEM((Q_GROUPS, CONV_OUT_ROWS, LANES), F32),
                        pltpu.VMEM((U_GROUPS, CONV_OUT_ROWS, LANES), F32)],
        compiler_params=_cparams(("parallel",)),
        name="conv",
    )(qkv_raw, qkv_raw, qkv_raw, ug, ug, ug, ba, cw, dw, dwb, lng, lnb, alog, dtb)


INV_BASE = 8


def _b16(xs):
    return [x.astype(BF16) for x in xs]


def _tri_inverse_minus_eye(nmats, rows, cols):
    assert INV_BASE == 8
    c = nmats[0].shape[0]
    shift = int(math.log2(INV_BASE))
    same = (rows >> shift) == (cols >> shift)
    n1 = [jnp.where(same, n, 0.0) for n in nmats]
    n1b = _b16(n1)
    n2 = [_dot(x, x) for x in n1b]
    n2b = _b16(n2)
    r = [_dot(jnp.concatenate([a, b], axis=0), b) for a, b in zip(n1b, n2b)]
    q = [a + b + x[:c] for a, b, x in zip(n1, n2, r)]
    n4 = [x[c:] for x in r]
    qn4 = [_dot(a, b) for a, b in zip(_b16(q), _b16(n4))]
    q = [a + b + x for a, b, x in zip(q, n4, qn4)]
    while (1 << shift) < c:
        off = ((rows >> (shift + 1)) == (cols >> (shift + 1))) & ((rows >> shift) != (cols >> shift))
        a = [jnp.where(off, -n, 0.0) for n in nmats]
        x = [ai + _dot(ab, qb) for ai, ab, qb in zip(a, _b16(a), _b16(q))]
        qx = [_dot(qb, xb) for qb, xb in zip(_b16(q), _b16(x))]
        q = [qi - xi - qxi for qi, xi, qxi in zip(q, x, qx)]
        shift += 1
    return q


def _delta_units(units):
    c = CHUNK
    scale = HEAD_DIM ** -0.5
    rows = lax.broadcasted_iota(I32, (c, c), 0)
    cols = lax.broadcasted_iota(I32, (c, c), 1)
    pre = []
    for d, hd, qkv_ref, r0, gate, gt, load_s in units:
        lane = 2 * N_HEADS + N_HEADS * d + hd
        last = c - 1 if d == 0 else 0
        pre.append(dict(
            d=d, load_s=load_s,
            load=lambda part, qkv_ref=qkv_ref, r0=r0, hd=hd: qkv_ref[
                pl.ds(r0, c), part * MIX_A + hd * HEAD_DIM:part * MIX_A + (hd + 1) * HEAD_DIM],
            beta=gate[:, N_HEADS * d + hd:N_HEADS * d + hd + 1],
            gc=gate[:, lane:lane + 1],
            gr=gt[lane:lane + 1, :],
            gtot=gate[last:last + 1, lane:lane + 1]))

    ak = []
    for p in pre:
        k = p['load'](1)
        lhs = jnp.concatenate([k * p['beta'], p['load'](0) * scale], axis=0).astype(BF16)
        ak.append(_dot_nt(lhs, k.astype(BF16)))
    nmats, qks = [], []
    for p, a in zip(pre, ak):
        incl = (rows >= cols) if p['d'] == 0 else (rows <= cols)
        strict = (rows > cols) if p['d'] == 0 else (rows < cols)
        dec = jnp.where(incl, jnp.exp(p['gc'] - p['gr']), 0.0)
        nmats.append(jnp.where(strict, -a[:c] * dec, 0.0))
        qks.append((a[c:] * dec).astype(BF16))
    qinv = _tri_inverse_minus_eye(nmats, rows, cols)

    uw = []
    for p, qi in zip(pre, qinv):
        kb = p['load'](1) * p['beta']
        rhs = jnp.concatenate([p['load'](2) * p['beta'], kb * jnp.exp(p['gc'])], axis=1)
        uw.append(rhs + _dot(qi.astype(BF16), rhs.astype(BF16)))
    sw = []
    for p, x in zip(pre, uw):
        qdec = p['load'](0) * scale * jnp.exp(p['gc'])
        sw.append(_dot(jnp.concatenate([x[:, HEAD_DIM:], qdec], axis=0).astype(BF16), p['load_s']().astype(BF16)))
    vnb = [(x[:, :HEAD_DIM] - y[:c]).astype(BF16) for x, y in zip(uw, sw)]
    o = [y[c:] + _dot(qk, vb) for y, qk, vb in zip(sw, qks, vnb)]
    s_new = []
    for p, vb in zip(pre, vnb):
        kdec = (p['load'](1) * jnp.exp(p['gtot'] - p['gc'])).astype(BF16)
        s_new.append(p['load_s']() * jnp.exp(p['gtot']) + _dot_tn(kdec, vb))
    return list(zip(o, s_new))


def _delta_kernel(has_s0, emit_final, one_block, n_chunk, *refs):
    refs = list(refs)
    if one_block:
        seq_in = [[r for r in refs[3 * s:3 * s + 3] for _ in range(2)] for s in range(DELTA_SEQS)]
        refs = refs[3 * DELTA_SEQS:]
    else:
        seq_in = [refs[6 * s:6 * s + 6] for s in range(DELTA_SEQS)]
        refs = refs[6 * DELTA_SEQS:]
    s0_ref = refs.pop(0) if has_s0 else None
    of_ref, ob_ref = refs[:2]
    refs = refs[2:]
    sfin_ref = refs.pop(0) if emit_final else None
    s_scr = refs[0]

    j = pl.program_id(1)

    @pl.when(j == 0)
    def _():
        if has_s0:
            s_scr[...] = s0_ref[...]
        else:
            s_scr[...] = jnp.zeros(s_scr.shape, F32)

    c = CHUNK

    def body(ci, carry):
        units, where = [], []
        for s, (qkvf_ref, qkvb_ref, gatef_ref, gateb_ref, gtf_ref, gtb_ref) in enumerate(seq_in):
            for d in range(2):
                cidx = ci if d == 0 else n_chunk - 1 - ci
                r0 = pl.multiple_of(cidx * c, c)
                qkv_ref = qkvf_ref if d == 0 else qkvb_ref
                gate = (gatef_ref if d == 0 else gateb_ref)[pl.ds(r0, c), :]
                gt = (gtf_ref if d == 0 else gtb_ref)[cidx]
                for hd in range(N_HEADS):
                    units.append((d, hd, qkv_ref, r0, gate, gt,
                                  lambda s=s, idx=N_HEADS * d + hd: s_scr[s, idx]))
                    where.append((s, d, hd, r0))
        for (s, d, hd, r0), (o, s_new) in zip(where, _delta_units(units)):
            s_scr[s, N_HEADS * d + hd] = s_new
            (of_ref if d == 0 else ob_ref)[s, pl.ds(r0, c), hd * HEAD_DIM:(hd + 1) * HEAD_DIM] = o
        return carry

    lax.fori_loop(0, n_chunk, body, 0)

    if emit_final:
        @pl.when(j == pl.num_programs(1) - 1)
        def _():
            sfin_ref[...] = s_scr[...]


def _delta(qkv, gate, gate_t, n_seq, seq_len, row0, blk, s0):
    nblk = seq_len // blk
    n_chunk = blk // CHUNK
    b0 = row0 // blk
    has_s0 = s0 is not None
    emit_final = not has_s0
    n_state = 2 * N_HEADS
    one_block = nblk == 1
    assert n_seq % DELTA_SEQS == 0 and row0 % blk == 0

    in_specs, args = [], []
    for s in range(DELTA_SEQS):
        def fwd(b, j, s=s):
            return b0 + (b * DELTA_SEQS + s) * nblk + j

        def bwd(b, j, s=s):
            return b0 + (b * DELTA_SEQS + s) * nblk + (nblk - 1 - j)

        for arr, shape in ((qkv, (blk, 3 * MIX_A)), (gate, (blk, LANES)), (gate_t, (n_chunk, GATE_ROWS, CHUNK))):
            tail = (0,) * (len(shape) - 1)
            for f in (fwd,) if one_block else (fwd, bwd):
                in_specs.append(pl.BlockSpec(shape, lambda b, j, f=f, tail=tail: (f(b, j),) + tail))
                args.append(arr)
    state_spec = pl.BlockSpec((DELTA_SEQS, n_state, HEAD_DIM, HEAD_DIM), lambda b, j: (b, 0, 0, 0))
    if has_s0:
        in_specs.append(state_spec)
        args.append(s0)
    out_specs = [pl.BlockSpec((DELTA_SEQS, blk, MIX_A), lambda b, j: (b, j, 0)),
                 pl.BlockSpec((DELTA_SEQS, blk, MIX_A), lambda b, j: (b, nblk - 1 - j, 0))]
    out_shape = [jax.ShapeDtypeStruct((n_seq, seq_len, MIX_A), F32)] * 2
    if emit_final:
        out_specs.append(state_spec)
        out_shape.append(jax.ShapeDtypeStruct((n_seq, n_state, HEAD_DIM, HEAD_DIM), F32))
    outs = pl.pallas_call(
        functools.partial(_delta_kernel, has_s0, emit_final, one_block, n_chunk),
        grid=(n_seq // DELTA_SEQS, nblk),
        in_specs=in_specs,
        out_specs=out_specs,
        out_shape=out_shape,
        scratch_shapes=[pltpu.VMEM((DELTA_SEQS, n_state, HEAD_DIM, HEAD_DIM), F32)],
        compiler_params=_cparams(("parallel", "arbitrary")),
        name="delta_latent" if has_s0 else "delta_prompt",
    )(*args)
    return [o.reshape(n_seq * seq_len, MIX_A) for o in outs[:2]] + list(outs[2:])


def _outproj_kernel(lay, xp_ref, xs_ref, prow_ref, pcol_ref, mod_ref, opf_ref, opb_ref, osf_ref, osb_ref, z_ref,
                    u_ref, og_ref, n2_ref, wo_ref, wr_ref, br_ref, x1_ref, h2_ref, ids_ref, wts_ref, cnt_ref,
                    cnt_scr):
    i = pl.program_id(0)
    x0 = _load_x(lay, i, xp_ref, xs_ref, prow_ref, pcol_ref)
    m = mod_ref[0]
    o = jnp.where(lay.is_sample(i), osf_ref[...] + osb_ref[...], opf_ref[...] + opb_ref[...])
    z = z_ref[...]
    mix = _dot(u_ref[...], wo_ref[MIX_A:, :])
    for hd in range(N_HEADS):
        sl = slice(hd * HEAD_DIM, (hd + 1) * HEAD_DIM)
        oh = _rms(o[:, sl]) * og_ref[...] * _silu(z[:, sl])
        mix = mix + _dot(oh.astype(BF16), wo_ref[sl, :])
    x1 = x0 + m[2:3] * mix
    x1_ref[...] = x1
    h2 = _rms(x1) * n2_ref[...] * (1.0 + m[4:5]) + m[3:4]
    h2b = h2.astype(BF16)
    h2_ref[...] = _pack_bf16_pairs(h2b)

    logits = _dot(h2b, wr_ref[...]) + br_ref[...]
    lane = lax.broadcasted_iota(I32, logits.shape, 1)
    lane_f = lane.astype(F32)
    neg = jnp.float32(-jnp.inf)
    big = jnp.float32(LANES)

    def first_lane(mask):
        return jnp.min(jnp.where(mask, lane_f, big), axis=-1, keepdims=True)

    gl = jnp.where(lane < N_GROUPS, logits, neg)
    gmax = jnp.max(gl, axis=-1, keepdims=True)
    grp = first_lane(gl == gmax)
    p_grp = 1.0 / jnp.sum(jnp.where(lane < N_GROUPS, jnp.exp(gl - gmax), 0.0), axis=-1, keepdims=True)
    e_lane = lane_f - N_GROUPS
    in_grp = (e_lane >= grp * EXPERTS_PER_GROUP) & (e_lane < (grp + 1.0) * EXPERTS_PER_GROUP)
    el = jnp.where(in_grp, logits, neg)
    m1 = jnp.max(el, axis=-1, keepdims=True)
    i1f = first_lane(el == m1)
    el2 = jnp.where(lane_f == i1f, neg, el)
    m2 = jnp.max(el2, axis=-1, keepdims=True)
    i2f = first_lane(el2 == m2)
    i1 = i1f.astype(I32)
    i2 = i2f.astype(I32)
    e2 = jnp.exp(m2 - m1)
    w1 = p_grp / (1.0 + e2)
    w2 = p_grp * e2 / (1.0 + e2)
    col = lax.broadcasted_iota(I32, wts_ref.shape, 1)
    wts_ref[...] = jnp.where(col == 0, w1, jnp.where(col == 1, w2, 0.0))

    @pl.when(i == 0)
    def _():
        cnt_scr[...] = jnp.zeros(cnt_scr.shape, F32)

    oh1 = lane == i1
    oh2 = lane == i2
    oh = jnp.where(oh1, 1.0, jnp.where(oh2, 1.0, 0.0))
    tm = logits.shape[0]
    rows = lax.broadcasted_iota(I32, (tm, tm), 0)
    cols = lax.broadcasted_iota(I32, (tm, tm), 1)
    before = jnp.where(rows > cols, 1.0, 0.0).astype(BF16)
    seen = cnt_scr[...] + _dot(before, oh.astype(BF16))
    r1 = jnp.sum(jnp.where(oh1, seen, 0.0), axis=-1, keepdims=True)
    r2 = jnp.sum(jnp.where(oh2, seen, 0.0), axis=-1, keepdims=True)
    cnt_scr[...] = cnt_scr[...] + jnp.sum(oh, axis=0, keepdims=True)
    cnt_ref[...] = cnt_scr[...]
    ids = jnp.where(lane == 0, i1f - N_GROUPS,
                    jnp.where(lane == 1, i2f - N_GROUPS, jnp.where(lane == 2, r1, jnp.where(lane == 3, r2, 0.0))))
    ids_ref[...] = jnp.transpose(ids)[:ROUTE_COLS, :].astype(I32)


def _outproj(lay, xp, xs, pos, mod, o_pf, o_pb, o_sf, o_sb, z, u, onorm_g, norm2_g, w_out, w_route, b_route):
    t = lay.t
    p_spec = pl.BlockSpec((lay.tile, MIX_A), lambda i: (lay.xp_blk(i), 0))
    s_spec = pl.BlockSpec((lay.tile, MIX_A), lambda i: (lay.xs_blk(i), 0))
    return pl.pallas_call(
        functools.partial(_outproj_kernel, lay),
        grid=(lay.tiles,),
        in_specs=_x_specs(lay) + [_mod_spec(lay), p_spec, p_spec, s_spec, s_spec, _row_spec(lay, MIX_A),
                                  _row_spec(lay, MIX_B), _full_spec((1, HEAD_DIM)), _full_spec((1, D_MODEL)),
                                  _full_spec(w_out.shape), _full_spec(w_route.shape),
                                  _full_spec(b_route.shape)],
        out_specs=[_row_spec(lay, D_MODEL), _row_spec(lay, D_MODEL // 2),
                   pl.BlockSpec((ROUTE_COLS, lay.tile), lambda i: (0, i)),
                   _row_spec(lay, ROUTE_COLS), _full_spec((1, LANES))],
        out_shape=[jax.ShapeDtypeStruct((t, D_MODEL), F32),
                   jax.ShapeDtypeStruct((t, D_MODEL // 2), I32),
                   jax.ShapeDtypeStruct((ROUTE_COLS, t), I32),
                   jax.ShapeDtypeStruct((t, ROUTE_COLS), F32),
                   jax.ShapeDtypeStruct((1, LANES), F32)],
        scratch_shapes=[pltpu.VMEM((1, LANES), F32)],
        compiler_params=_cparams(("arbitrary",)),
        name="outproj",
    )(xp, xs, *pos, mod, o_pf, o_pb, o_sf, o_sb, z, u, onorm_g, norm2_g, w_out, w_route, b_route)


def _sc_gather(table, idx, chunk, name):
    n_rows, width = idx.shape[0], table.shape[1]
    mesh = plsc.VectorSubcoreMesh(core_axis_name="c", subcore_axis_name="s")
    n_workers = mesh.num_cores * mesh.num_subcores
    per_worker = n_rows // n_workers
    n_chunks = per_worker // chunk
    assert n_rows == n_workers * n_chunks * chunk and n_chunks % 2 == 0 and chunk % 8 == 0 and chunk <= LANES

    def body(table_hbm, idx_hbm, out_hbm, idx_v, rows_v, sem):
        base = (lax.axis_index("s") * mesh.num_cores + lax.axis_index("c")) * per_worker

        def gather(slot):
            return pltpu.make_async_copy(table_hbm.at[idx_v.at[slot]], rows_v.at[slot], sem.at[slot])

        def fetch(g, slot):
            off = pl.multiple_of(base + g * chunk, 8)
            pltpu.sync_copy(idx_hbm.at[pl.ds(off, chunk)], idx_v.at[slot])
            gather(slot).start()

        for slot in range(2):
            fetch(slot, slot)

        @pl.loop(0, n_chunks, step=2)
        def _(g):
            for slot in range(2):
                off = pl.multiple_of(base + (g + slot) * chunk, 8)
                gather(slot).wait()
                pltpu.sync_copy(rows_v.at[slot], out_hbm.at[pl.ds(off, chunk)])

                @pl.when(g + slot + 2 < n_chunks)
                def _():
                    fetch(g + slot + 2, slot)

    return pl.kernel(
        body,
        out_type=jax.ShapeDtypeStruct((n_rows, width), table.dtype),
        mesh=mesh,
        scratch_types=[pltpu.VMEM((2, chunk), I32), pltpu.VMEM((2, chunk, width), table.dtype),
                       pltpu.SemaphoreType.DMA((2,))],
        name=name,
    )(table, idx)


def _sc_scatter2(src, idx0, idx1, n_out, chunk, name):
    n_rows, width = src.shape
    mesh = plsc.VectorSubcoreMesh(core_axis_name="c", subcore_axis_name="s")
    n_workers = mesh.num_cores * mesh.num_subcores
    per_worker = n_rows // n_workers
    n_chunks = per_worker // chunk
    assert n_rows == n_workers * n_chunks * chunk and n_chunks % 2 == 0 and chunk % 8 == 0 and chunk <= LANES

    def body(src_hbm, i0_hbm, i1_hbm, out_hbm, i0_v, i1_v, rows_v, sem_in, sem_out):
        base = (lax.axis_index("s") * mesh.num_cores + lax.axis_index("c")) * per_worker

        def rows_in(g, slot):
            off = pl.multiple_of(base + g * chunk, 8)
            return pltpu.make_async_copy(src_hbm.at[pl.ds(off, chunk)], rows_v.at[slot], sem_in.at[slot])

        def fetch(g, slot):
            off = pl.multiple_of(base + g * chunk, 8)
            pltpu.sync_copy(i0_hbm.at[pl.ds(off, chunk)], i0_v.at[slot])
            pltpu.sync_copy(i1_hbm.at[pl.ds(off, chunk)], i1_v.at[slot])
            rows_in(g, slot).start()

        for slot in range(2):
            fetch(slot, slot)

        @pl.loop(0, n_chunks, step=2)
        def _(g):
            for slot in range(2):
                rows_in(g + slot, slot).wait()
                puts = [pltpu.make_async_copy(rows_v.at[slot], out_hbm.at[iv.at[slot]], sem_out.at[slot])
                        for iv in (i0_v, i1_v)]
                for put in puts:
                    put.start()
                for put in puts:
                    put.wait()

                @pl.when(g + slot + 2 < n_chunks)
                def _():
                    fetch(g + slot + 2, slot)

    return pl.kernel(
        body,
        out_type=jax.ShapeDtypeStruct((n_out, width), src.dtype),
        mesh=mesh,
        scratch_types=[pltpu.VMEM((2, chunk), I32), pltpu.VMEM((2, chunk), I32),
                       pltpu.VMEM((2, chunk, width), src.dtype),
                       pltpu.SemaphoreType.DMA((2,)), pltpu.SemaphoreType.DMA((2,))],
        name=name,
    )(src, idx0, idx1)


def _expert_kernel(blk_e_ref, nused_ref, xb_ref, wg_ref, wu_ref, wd_ref, yb_ref, wg_s, wu_s, wd_s):
    i = pl.program_id(0)
    nused = nused_ref[0]

    @pl.when(i < nused)
    def _():
        changed = (i == 0) | (blk_e_ref[i] != blk_e_ref[jnp.maximum(i - 1, 0)])

        @pl.when(changed)
        def _():
            wg_s[...] = wg_ref[0].astype(BF16)
            wu_s[...] = wu_ref[0].astype(BF16)
            wd_s[...] = wd_ref[0].astype(BF16)

        half = D_MODEL // 2
        xa, xb = (v.astype(BF16) for v in _unpack_bf16_pairs(xb_ref[...]))
        g = _dot(xa, wg_s[:half, :]) + _dot(xb, wg_s[half:, :])
        u = _dot(xa, wu_s[:half, :]) + _dot(xb, wu_s[half:, :])
        hmid = (_silu(g) * u).astype(BF16)
        yb_ref[...] = _pack_bf16_pairs(_dot(hmid, wd_s[...]).astype(BF16))

    @pl.when(i >= nused)
    def _():
        yb_ref[...] = jnp.zeros(yb_ref.shape, I32)


def _experts(xb, blk_e, nused, w_gate, w_up, w_down):
    nb = blk_e.shape[0]

    def weight_spec(shape):
        return pl.BlockSpec((1,) + shape, lambda i, be, nu: (be[i], 0, 0))

    grid_spec = pltpu.PrefetchScalarGridSpec(
        num_scalar_prefetch=2,
        grid=(nb,),
        in_specs=[
            pl.BlockSpec((ROUTE_BM, D_MODEL // 2), lambda i, be, nu: (i, 0)),
            weight_spec((D_MODEL, D_EXPERT)),
            weight_spec((D_MODEL, D_EXPERT)),
            weight_spec((D_EXPERT, D_MODEL)),
        ],
        out_specs=pl.BlockSpec((ROUTE_BM, D_MODEL // 2), lambda i, be, nu: (i, 0)),
        scratch_shapes=[pltpu.VMEM((D_MODEL, D_EXPERT), BF16),
                        pltpu.VMEM((D_MODEL, D_EXPERT), BF16),
                        pltpu.VMEM((D_EXPERT, D_MODEL), BF16)],
    )
    return pl.pallas_call(
        _expert_kernel,
        grid_spec=grid_spec,
        out_shape=jax.ShapeDtypeStruct((nb * ROUTE_BM, D_MODEL // 2), I32),
        compiler_params=_cparams(("arbitrary",)),
        name="expert",
    )(blk_e, nused, xb, w_gate, w_up, w_down)


def _combine_kernel(lay, y0_ref, y1_ref, x1_ref, wts_ref, mod_ref, fg_ref, yp_ref, ys_ref):
    i = pl.program_id(0)
    m = mod_ref[0]
    w = wts_ref[...]
    a0, b0 = _unpack_bf16_pairs(y0_ref[...])
    a1, b1 = _unpack_bf16_pairs(y1_ref[...])
    ff = jnp.concatenate([a0 * w[:, 0:1] + a1 * w[:, 1:2], b0 * w[:, 0:1] + b1 * w[:, 1:2]], axis=1)
    y = _rms(x1_ref[...] + m[5:6] * ff) * fg_ref[...]

    @pl.when(i < lay.tiles_p)
    def _():
        yp_ref[...] = y

    @pl.when(i >= lay.tiles_p)
    def _():
        ys_ref[...] = y


def _combine(lay, yg, x1, wts, mod, final_g):
    return pl.pallas_call(
        functools.partial(_combine_kernel, lay),
        grid=(lay.tiles,),
        in_specs=[_row_spec(lay, D_MODEL // 2),
                  pl.BlockSpec((lay.tile, D_MODEL // 2), lambda i: (i + lay.tiles, 0)),
                  _row_spec(lay, D_MODEL), _row_spec(lay, ROUTE_COLS), _mod_spec(lay), _full_spec((1, D_MODEL))],
        out_specs=[pl.BlockSpec((lay.tile, D_MODEL), lambda i: (lay.xp_blk(i), 0)),
                   pl.BlockSpec((lay.tile, D_MODEL), lambda i: (lay.xs_blk(i), 0))],
        out_shape=[jax.ShapeDtypeStruct((lay.t_p, D_MODEL), F32),
                   jax.ShapeDtypeStruct((lay.t_s, D_MODEL), F32)],
        compiler_params=_cparams(("arbitrary",)),
        name="combine",
    )(yg, yg, x1, wts, mod, final_g)


def _dispatch_plan(ids, counts):
    n_tok = ids.shape[1]
    padded = (counts + ROUTE_BM - 1) // ROUTE_BM * ROUTE_BM
    pad_end = jnp.cumsum(padded)
    pad_start = pad_end - padded
    experts = jnp.arange(N_EXPERTS, dtype=I32)[:, None]
    dest = [(jnp.sum(jnp.where(ids[k][None, :] == experts, pad_start[:, None], 0), axis=0) + ids[2 + k]).astype(I32)
            for k in range(2)]
    nb = -(-(2 * n_tok + N_EXPERTS * (ROUTE_BM - 1)) // ROUTE_BM)
    block_start = jnp.arange(nb, dtype=I32) * ROUTE_BM
    blk_e = jnp.minimum(jnp.sum(pad_end[None, :] <= block_start[:, None], axis=1), N_EXPERTS - 1).astype(I32)
    nused = (pad_end[-1:] // ROUTE_BM).astype(I32)
    return dest[0], dest[1], blk_e, nused


def _grid_pos_tables(n_tokens):
    rows = n_tokens // GRID_W
    n_freq = D_MODEL // 4
    freq = jnp.exp(jnp.arange(n_freq, dtype=F32) * (-math.log(10000.0) / n_freq))

    def enc(p):
        ang = p[:, None] * freq[None, :]
        return jnp.concatenate([jnp.sin(ang), jnp.cos(ang)], axis=-1)

    return enc(jnp.arange(rows, dtype=F32)), enc(jnp.arange(GRID_W, dtype=F32))


def _lane_pad(v, offset):
    return jnp.zeros((1, LANES), F32).at[0, offset:offset + v.shape[0]].set(v.astype(F32))


def kernel(x_prompt, x_sample, state_delta, c, c_ctx, norm1_g, w_mod, b_mod, w_in, conv_qkv_w, A_log, dt_bias, onorm_g, dw_w, dw_b, cln_g, cln_b, w_out, norm2_g, w_group, b_group, w_expert, b_expert, w_e_gate, w_e_up, w_e_down, final_g):
    n_p, l_p, _ = x_prompt.shape
    n_s, l_s, _ = x_sample.shape
    lay = _Layout(n_p, l_p, n_s, l_s, TOK_TILE)
    lay_proj = _Layout(n_p, l_p, n_s, l_s, PROJ_TILE)
    depth = w_in.shape[0]
    assert depth == 1
    xp = x_prompt.reshape(lay.t_p, D_MODEL)
    xs = x_sample.reshape(lay.t_s, D_MODEL)
    pos = _grid_pos_tables(l_s)

    cond = jnp.concatenate([c_ctx[None, :], c], axis=0)
    cond = jnp.pad(cond, ((0, (-cond.shape[0]) % SUBLANES), (0, 0)))
    mod = _modulation(cond, w_mod[0], b_mod[0]).reshape(cond.shape[0], 6, D_MODEL)

    n_gate = 4 * N_HEADS
    wi = w_in[0]
    w_main = jnp.concatenate([wi[:, :4 * MIX_A], wi[:, 4 * MIX_A + n_gate:]], axis=1).astype(BF16)
    w_small = jnp.pad(wi[:, 4 * MIX_A:4 * MIX_A + n_gate], ((0, 0), (0, LANES - n_gate))).astype(BF16)
    qkv_raw, z, ug, ba = _inproj(lay_proj, xp, xs, pos, mod, norm1_g[0][None, :], w_main, w_small)

    cw = jnp.pad(conv_qkv_w[0], ((0, 8 - SHORT_CONV), (0, 0))).reshape(8, Q_GROUPS, LANES).transpose(1, 0, 2)
    dw = jnp.pad(dw_w[0], ((0, 32 - CONV_W), (0, 0))).reshape(32, U_GROUPS, LANES).transpose(1, 0, 2)
    alog = _lane_pad(A_log[0].reshape(-1), 2 * N_HEADS)
    dtb = _lane_pad(dt_bias[0].reshape(-1), 2 * N_HEADS)
    qkv, u_conf, gate, gate_t = _conv(lay, qkv_raw, ug, ba, cw, dw, dw_b[0][None, :], cln_g[0][None, :],
                                      cln_b[0][None, :], alog, dtb)

    o_pf, o_pb, s_fin = _delta(qkv, gate, gate_t, n_p, l_p, 0, l_p, None)
    s0 = state_delta[:, 0].reshape(n_s, 2 * N_HEADS, HEAD_DIM, HEAD_DIM)
    o_sf, o_sb = _delta(qkv, gate, gate_t, n_s, l_s, lay.t_p, DELTA_BLK, s0)

    w_route = jnp.pad(jnp.concatenate([w_group[0], w_expert[0]], axis=1),
                      ((0, 0), (0, LANES - N_GROUPS - N_EXPERTS))).astype(BF16)
    b_route = _lane_pad(jnp.concatenate([b_group[0], b_expert[0]]), 0)
    x1, h2p, ids, wts, cnt = _outproj(lay_proj, xp, xs, pos, mod, o_pf, o_pb, o_sf, o_sb, z, u_conf,
                                      onorm_g[0][None, :], norm2_g[0][None, :], w_out[0].astype(BF16), w_route,
                                      b_route)

    counts = cnt[0, N_GROUPS:N_GROUPS + N_EXPERTS].astype(I32)
    dest0, dest1, blk_e, nused = _dispatch_plan(ids, counts)
    xb = _sc_scatter2(h2p, dest0, dest1, blk_e.shape[0] * ROUTE_BM, DISPATCH_CHUNK, "dispatch_scatter")
    yb = _experts(xb, blk_e, nused, w_e_gate[0], w_e_up[0], w_e_down[0])
    yg = _sc_gather(yb, jnp.concatenate([dest0, dest1]), COMBINE_CHUNK, "combine_gather")
    y_p, y_s = _combine(_Layout(n_p, l_p, n_s, l_s, COMBINE_TILE), yg, x1, wts, mod, final_g[None, :])

    new_state = s_fin.reshape(n_p, 1, 2, N_HEADS, HEAD_DIM, HEAD_DIM)
    return (y_p.reshape(x_prompt.shape), y_s.reshape(x_sample.shape), new_state)
```

```python
import functools
import math

import jax
import jax.numpy as jnp
from jax import lax
from jax.experimental import pallas as pl
from jax.experimental.pallas import tpu as pltpu
from jax.experimental.pallas import tpu_sc as plsc

F32 = jnp.float32
BF16 = jnp.bfloat16
I32 = jnp.int32

D_MODEL = 1024
MIX_A = 512
MIX_B = 512
HEAD_DIM = 128
N_HEADS = 4
SHORT_CONV = 5
CONV_W = 31
N_GROUPS = 4
EXPERTS_PER_GROUP = 8
N_EXPERTS = 32
D_EXPERT = 512
GRID_W = 64
EPS = 1e-6

LANES = 128
SUBLANES = 8
TOK_TILE = 256
PROJ_TILE = 512
PROJ_ROWS = 256
CHUNK = 128
DELTA_BLK = 512
DELTA_SEQS = 2
ROUTE_BM = 512
ROUTE_COLS = 8
GATE_ROWS = 16
COMBINE_TILE = 1024
DISPATCH_CHUNK = 64
COMBINE_CHUNK = 64
Q_HALO = 8
U_HALO = 16
VMEM_LIMIT = 56 * 1024 * 1024

HIGHEST = lax.Precision.HIGHEST


def _cparams(sem):
    return pltpu.CompilerParams(dimension_semantics=sem, vmem_limit_bytes=VMEM_LIMIT)


def _sigmoid(x):
    return 0.5 * jnp.tanh(0.5 * x) + 0.5


def _silu(x):
    return x * _sigmoid(x)


def _softplus(x):
    return jnp.maximum(x, 0.0) + jnp.log1p(jnp.exp(-jnp.abs(x)))


def _rms(x):
    return x * lax.rsqrt(jnp.mean(x * x, axis=-1, keepdims=True) + EPS)


def _pack_bf16_pairs(x):
    h = x.shape[1] // 2
    hi = pltpu.bitcast(x[:, :h].astype(F32), jnp.uint32)
    lo = pltpu.bitcast(x[:, h:].astype(F32), jnp.uint32)
    return pltpu.bitcast((hi & jnp.uint32(0xFFFF0000)) | (lo >> 16), I32)


def _unpack_bf16_pairs(p):
    u = pltpu.bitcast(p, jnp.uint32)
    return pltpu.bitcast(u & jnp.uint32(0xFFFF0000), F32), pltpu.bitcast(u << 16, F32)


def _dot(a, b):
    return jnp.dot(a, b, preferred_element_type=F32)


def _dot_nt(a, b):
    return lax.dot_general(a, b, (((1,), (1,)), ((), ())), preferred_element_type=F32)


def _dot_tn(a, b):
    return lax.dot_general(a, b, (((0,), (0,)), ((), ())), preferred_element_type=F32)


def _mod_kernel(cond_ref, w_ref, b_ref, o_ref):
    s = _silu(cond_ref[...])
    o_ref[...] = jnp.dot(s, w_ref[...], preferred_element_type=F32, precision=HIGHEST) + b_ref[...]


def _modulation(cond, w_mod, b_mod):
    n = cond.shape[0]
    tn = D_MODEL
    return pl.pallas_call(
        _mod_kernel,
        grid=(6 * D_MODEL // tn,),
        in_specs=[pl.BlockSpec((n, D_MODEL), lambda j: (0, 0)),
                  pl.BlockSpec((D_MODEL, tn), lambda j: (0, j)),
                  pl.BlockSpec((1, tn), lambda j: (0, j))],
        out_specs=pl.BlockSpec((n, tn), lambda j: (0, j)),
        out_shape=jax.ShapeDtypeStruct((n, 6 * D_MODEL), F32),
        compiler_params=_cparams(("parallel",)),
        name="mod",
    )(cond, w_mod, b_mod.reshape(1, -1))


class _Layout:
    def __init__(self, n_p, l_p, n_s, l_s, tile):
        self.n_p, self.l_p, self.n_s, self.l_s, self.tile = n_p, l_p, n_s, l_s, tile
        self.t_p = n_p * l_p
        self.t_s = n_s * l_s
        self.t = self.t_p + self.t_s
        assert self.t_p % tile == 0 and l_s % tile == 0 and l_s % DELTA_BLK == 0
        self.tiles_p = self.t_p // tile
        self.tiles_s = self.t_s // tile
        self.tiles = self.tiles_p + self.tiles_s
        self.tps_p = max(l_p // tile, 1)
        self.tps_s = l_s // tile

    def is_sample(self, i):
        return i >= self.tiles_p

    def mod_row(self, i):
        return jnp.where(i < self.tiles_p, 0, 1 + (i - self.tiles_p) // self.tps_s)

    def pos_blk(self, i):
        return jnp.where(i < self.tiles_p, 0, (i - self.tiles_p) % self.tps_s)

    def xp_blk(self, i):
        return jnp.minimum(i, self.tiles_p - 1)

    def xs_blk(self, i):
        return jnp.maximum(i - self.tiles_p, 0)

    def seq_pos(self, i):
        in_s = i >= self.tiles_p
        pos = jnp.where(in_s, (i - self.tiles_p) % self.tps_s, i % self.tps_p)
        n = jnp.where(in_s, self.tps_s, self.tps_p)
        return pos, n


def _load_x(lay, i, xp_ref, xs_ref, prow_ref, pcol_ref, rows):
    g0, g1 = rows.start // GRID_W, rows.stop // GRID_W
    half = D_MODEL // 2
    prow = jnp.concatenate([jnp.broadcast_to(prow_ref[r:r + 1, :], (GRID_W, half)) for r in range(g0, g1)],
                           axis=0)
    pcol = jnp.concatenate([pcol_ref[...]] * (g1 - g0), axis=0)
    pos = jnp.concatenate([prow, pcol], axis=1)
    return jnp.where(lay.is_sample(i), xs_ref[rows, :] + pos, xp_ref[rows, :])


def _x_specs(lay):
    assert lay.tile % (SUBLANES * GRID_W) == 0
    return [pl.BlockSpec((lay.tile, D_MODEL), lambda i: (lay.xp_blk(i), 0)),
            pl.BlockSpec((lay.tile, D_MODEL), lambda i: (lay.xs_blk(i), 0)),
            pl.BlockSpec((lay.tile // GRID_W, D_MODEL // 2), lambda i: (lay.pos_blk(i), 0)),
            _full_spec((GRID_W, D_MODEL // 2))]


def _mod_spec(lay):
    return pl.BlockSpec((1, 6, D_MODEL), lambda i: (lay.mod_row(i), 0, 0))


def _row_spec(lay, width):
    return pl.BlockSpec((lay.tile, width), lambda i: (i, 0))


def _full_spec(shape):
    nd = len(shape)
    return pl.BlockSpec(shape, lambda i: (0,) * nd)


def _inproj_kernel(lay, xp_ref, xs_ref, prow_ref, pcol_ref, mod_ref, g_ref, wm_ref, ws_ref,
                   qkv_ref, z_ref, ug_ref, ba_ref):
    i = pl.program_id(0)
    m = mod_ref[0]
    for r0 in range(0, lay.tile, PROJ_ROWS):
        rows = slice(r0, r0 + PROJ_ROWS)
        x = _load_x(lay, i, xp_ref, xs_ref, prow_ref, pcol_ref, rows)
        h = _rms(x) * g_ref[...] * (1.0 + m[1:2]) + m[0:1]
        hb = h.astype(BF16)
        qkv_ref[rows, :] = _dot(hb, wm_ref[:, 0:3 * MIX_A])
        z_ref[rows, :] = _dot(hb, wm_ref[:, 3 * MIX_A:4 * MIX_A])
        glu = _dot(hb, wm_ref[:, 4 * MIX_A:4 * MIX_A + 2 * MIX_B])
        ug_ref[rows, :] = glu[:, :MIX_B] * _sigmoid(glu[:, MIX_B:])
        ba_ref[rows, :] = _dot(hb, ws_ref[...])


def _inproj(lay, xp, xs, pos, mod, norm_g, w_main, w_small):
    t = lay.t
    return pl.pallas_call(
        functools.partial(_inproj_kernel, lay),
        grid=(lay.tiles,),
        in_specs=_x_specs(lay) + [_mod_spec(lay), _full_spec((1, D_MODEL)),
                                  _full_spec(w_main.shape), _full_spec(w_small.shape)],
        out_specs=[_row_spec(lay, 3 * MIX_A), _row_spec(lay, MIX_A), _row_spec(lay, MIX_B),
                   _row_spec(lay, LANES)],
        out_shape=[jax.ShapeDtypeStruct((t, 3 * MIX_A), F32),
                   jax.ShapeDtypeStruct((t, MIX_A), F32),
                   jax.ShapeDtypeStruct((t, MIX_B), F32),
                   jax.ShapeDtypeStruct((t, LANES), F32)],
        compiler_params=_cparams(("parallel",)),
        name="inproj",
    )(xp, xs, *pos, mod, norm_g, w_main, w_small)


CONV_PITCH = 33
CONV_OUT_ROWS = SUBLANES * CONV_PITCH
POST_ROWS = 64
Q_EXT_ROWS = 280
U_EXT_ROWS = 296
Q_GROUPS = 3 * MIX_A // LANES
U_GROUPS = MIX_B // LANES
CONV_J_BLOCK = 11


def _strided_conv(ext_scr, res_scr, w_ref, g, off, n_taps):
    for j0 in range(0, CONV_PITCH, CONV_J_BLOCK):
        js = range(j0, min(j0 + CONV_J_BLOCK, CONV_PITCH))
        v = {m: ext_scr[g, pl.ds(off + m, SUBLANES, stride=CONV_PITCH), :]
             for m in range(js[0], js[-1] + n_taps)}
        for j in js:
            acc = v[j] * w_ref[g, 0:1, :]
            for s in range(1, n_taps):
                acc = acc + v[j + s] * w_ref[g, s:s + 1, :]
            res_scr[g, pl.ds(j, SUBLANES, stride=CONV_PITCH), :] = acc


def _conv_kernel(lay, qc_ref, qp_ref, qn_ref, uc_ref, up_ref, un_ref, ba_ref, cw_ref, dw_ref,
                 dwb_ref, lng_ref, lnb_ref, alog_ref, dtb_ref,
                 qkv_ref, uo_ref, gate_ref, gt_ref, eq_scr, eu_scr, rq_scr, ru_scr):
    i = pl.program_id(0)
    pos, n = lay.seq_pos(i)
    has_prev = pos != 0
    has_next = pos != n - 1
    qp = jnp.where(has_prev, qp_ref[...], 0.0)
    qn = jnp.where(has_next, qn_ref[...], 0.0)
    for g in range(Q_GROUPS):
        sl = slice(g * LANES, (g + 1) * LANES)
        eq_scr[g, 0:Q_HALO, :] = qp[:, sl]
        eq_scr[g, Q_HALO:Q_HALO + TOK_TILE, :] = qc_ref[:, sl]
        eq_scr[g, Q_HALO + TOK_TILE:Q_HALO + TOK_TILE + Q_HALO, :] = qn[:, sl]
        eq_scr[g, TOK_TILE + 2 * Q_HALO:, :] = jnp.zeros((Q_EXT_ROWS - TOK_TILE - 2 * Q_HALO, LANES), F32)
    up = jnp.where(has_prev, up_ref[...], 0.0)
    un_ = jnp.where(has_next, un_ref[...], 0.0)
    for g in range(U_GROUPS):
        sl = slice(g * LANES, (g + 1) * LANES)
        eu_scr[g, 0:U_HALO, :] = up[:, sl]
        eu_scr[g, U_HALO:U_HALO + TOK_TILE, :] = uc_ref[:, sl]
        eu_scr[g, U_HALO + TOK_TILE:U_HALO + TOK_TILE + U_HALO, :] = un_[:, sl]
        eu_scr[g, TOK_TILE + 2 * U_HALO:, :] = jnp.zeros((U_EXT_ROWS - TOK_TILE - 2 * U_HALO, LANES), F32)

    def q_group(g, carry):
        _strided_conv(eq_scr, rq_scr, cw_ref, g, Q_HALO - SHORT_CONV // 2, SHORT_CONV)
        return carry

    def u_group(g, carry):
        _strided_conv(eu_scr, ru_scr, dw_ref, g, U_HALO - CONV_W // 2, CONV_W)
        return carry

    lax.fori_loop(0, Q_GROUPS, q_group, 0)
    lax.fori_loop(0, U_GROUPS, u_group, 0)

    for rc in range(TOK_TILE // POST_ROWS):
        r0 = rc * POST_ROWS
        for g in range(Q_GROUPS):
            y = _silu(rq_scr[g, r0:r0 + POST_ROWS, :])
            if g < 2 * N_HEADS:
                y = y * lax.rsqrt(jnp.sum(y * y, axis=-1, keepdims=True) + EPS)
            qkv_ref[r0:r0 + POST_ROWS, g * LANES:(g + 1) * LANES] = y
        u = jnp.concatenate([ru_scr[g, r0:r0 + POST_ROWS, :] for g in range(U_GROUPS)], axis=1) + dwb_ref[...]
        uc = u - jnp.mean(u, axis=-1, keepdims=True)
        un = uc * lax.rsqrt(jnp.mean(uc * uc, axis=-1, keepdims=True) + EPS)
        uo_ref[r0:r0 + POST_ROWS, :] = _silu(un * lng_ref[...] + lnb_ref[...]).astype(BF16)

    x = ba_ref[...]
    g = -jnp.exp(alog_ref[...]) * _softplus(x + dtb_ref[...])
    beta = _sigmoid(x)
    rows = lax.broadcasted_iota(I32, (CHUNK, CHUNK), 0)
    cols = lax.broadcasted_iota(I32, (CHUNK, CHUNK), 1)
    tri_lo = (rows >= cols).astype(F32)
    tri_up = (rows <= cols).astype(F32)
    lane = lax.broadcasted_iota(I32, (CHUNK, LANES), 1)
    for ch in range(TOK_TILE // CHUNK):
        sl = slice(ch * CHUNK, (ch + 1) * CHUNK)
        pre = jnp.dot(tri_lo, g[sl], preferred_element_type=F32, precision=HIGHEST)
        suf = jnp.dot(tri_up, g[sl], preferred_element_type=F32, precision=HIGHEST)
        gate = jnp.where(lane < 2 * N_HEADS, beta[sl], jnp.where(lane < 3 * N_HEADS, pre, suf))
        gate_ref[sl, :] = gate
        gt_ref[ch] = jnp.transpose(gate)[:GATE_ROWS, :]


def _conv(lay, qkv_raw, ug, ba, cw, dw, dwb, lng, lnb, alog, dtb):
    t = lay.t
    qh = TOK_TILE // Q_HALO
    uh = TOK_TILE // U_HALO
    n_qh = t // Q_HALO
    n_uh = t // U_HALO
    in_specs = [
        _row_spec(lay, 3 * MIX_A),
        pl.BlockSpec((Q_HALO, 3 * MIX_A), lambda i: (jnp.maximum(i * qh - 1, 0), 0)),
        pl.BlockSpec((Q_HALO, 3 * MIX_A), lambda i: (jnp.minimum((i + 1) * qh, n_qh - 1), 0)),
        _row_spec(lay, MIX_B),
        pl.BlockSpec((U_HALO, MIX_B), lambda i: (jnp.maximum(i * uh - 1, 0), 0)),
        pl.BlockSpec((U_HALO, MIX_B), lambda i: (jnp.minimum((i + 1) * uh, n_uh - 1), 0)),
        _row_spec(lay, LANES),
        _full_spec(cw.shape), _full_spec(dw.shape), _full_spec(dwb.shape),
        _full_spec(lng.shape), _full_spec(lnb.shape), _full_spec(alog.shape), _full_spec(dtb.shape),
    ]
    return pl.pallas_call(
        functools.partial(_conv_kernel, lay),
        grid=(lay.tiles,),
        in_specs=in_specs,
        out_specs=[_row_spec(lay, 3 * MIX_A), _row_spec(lay, MIX_B), _row_spec(lay, LANES),
                   pl.BlockSpec((TOK_TILE // CHUNK, GATE_ROWS, CHUNK), lambda i: (i, 0, 0))],
        out_shape=[jax.ShapeDtypeStruct((t, 3 * MIX_A), F32),
                   jax.ShapeDtypeStruct((t, MIX_B), BF16),
                   jax.ShapeDtypeStruct((t, LANES), F32),
                   jax.ShapeDtypeStruct((t // CHUNK, GATE_ROWS, CHUNK), F32)],
        scratch_shapes=[pltpu.VMEM((Q_GROUPS, Q_EXT_ROWS, LANES), F32),
                        pltpu.VMEM((U_GROUPS, U_EXT_ROWS, LANES), F32),
                        pltpu.VMEM((Q_GROUPS, CONV_OUT_ROWS, LANES), F32),
                        pltpu.VMEM((U_GROUPS, CONV_OUT_ROWS, LANES), F32)],
        compiler_params=_cparams(("parallel",)),
        name="conv",
    )(qkv_raw, qkv_raw, qkv_raw, ug, ug, ug, ba, cw, dw, dwb, lng, lnb, alog, dtb)


INV_BASE = 8


def _b16(xs):
    return [x.astype(BF16) for x in xs]


def _tri_inverse_minus_eye(nmats, rows, cols):
    assert INV_BASE == 8
    c = nmats[0].shape[0]
    shift = int(math.log2(INV_BASE))
    same = (rows >> shift) == (cols >> shift)
    n1 = [jnp.where(same, n, 0.0) for n in nmats]
    n1b = _b16(n1)
    n2 = [_dot(x, x) for x in n1b]
    n2b = _b16(n2)
    r = [_dot(jnp.concatenate([a, b], axis=0), b) for a, b in zip(n1b, n2b)]
    q = [a + b + x[:c] for a, b, x in zip(n1, n2, r)]
    n4 = [x[c:] for x in r]
    qn4 = [_dot(a, b) for a, b in zip(_b16(q), _b16(n4))]
    q = [a + b + x for a, b, x in zip(q, n4, qn4)]
    while (1 << shift) < c:
        off = ((rows >> (shift + 1)) == (cols >> (shift + 1))) & ((rows >> shift) != (cols >> shift))
        a = [jnp.where(off, -n, 0.0) for n in nmats]
        x = [ai + _dot(ab, qb) for ai, ab, qb in zip(a, _b16(a), _b16(q))]
        qx = [_dot(qb, xb) for qb, xb in zip(_b16(q), _b16(x))]
        q = [qi - xi - qxi for qi, xi, qxi in zip(q, x, qx)]
        shift += 1
    return q


def _delta_units(units):
    c = CHUNK
    scale = HEAD_DIM ** -0.5
    rows = lax.broadcasted_iota(I32, (c, c), 0)
    cols = lax.broadcasted_iota(I32, (c, c), 1)
    pre = []
    for d, hd, qkv_ref, r0, gate, gt, load_s in units:
        lane = 2 * N_HEADS + N_HEADS * d + hd
        last = c - 1 if d == 0 else 0
        pre.append(dict(
            d=d, load_s=load_s,
            load=lambda part, qkv_ref=qkv_ref, r0=r0, hd=hd: qkv_ref[
                pl.ds(r0, c), part * MIX_A + hd * HEAD_DIM:part * MIX_A + (hd + 1) * HEAD_DIM],
            beta=gate[:, N_HEADS * d + hd:N_HEADS * d + hd + 1],
            gc=gate[:, lane:lane + 1],
            gr=gt[lane:lane + 1, :],
            gtot=gate[last:last + 1, lane:lane + 1]))

    ak = []
    for p in pre:
        k = p['load'](1)
        lhs = jnp.concatenate([k * p['beta'], p['load'](0) * scale], axis=0).astype(BF16)
        ak.append(_dot_nt(lhs, k.astype(BF16)))
    nmats, qks = [], []
    for p, a in zip(pre, ak):
        incl = (rows >= cols) if p['d'] == 0 else (rows <= cols)
        strict = (rows > cols) if p['d'] == 0 else (rows < cols)
        dec = jnp.where(incl, jnp.exp(p['gc'] - p['gr']), 0.0)
        nmats.append(jnp.where(strict, -a[:c] * dec, 0.0))
        qks.append((a[c:] * dec).astype(BF16))
    qinv = _tri_inverse_minus_eye(nmats, rows, cols)

    uw = []
    for p, qi in zip(pre, qinv):
        kb = p['load'](1) * p['beta']
        rhs = jnp.concatenate([p['load'](2) * p['beta'], kb * jnp.exp(p['gc'])], axis=1)
        uw.append(rhs + _dot(qi.astype(BF16), rhs.astype(BF16)))
    sw = []
    for p, x in zip(pre, uw):
        qdec = p['load'](0) * scale * jnp.exp(p['gc'])
        sw.append(_dot(jnp.concatenate([x[:, HEAD_DIM:], qdec], axis=0).astype(BF16), p['load_s']().astype(BF16)))
    vnb = [(x[:, :HEAD_DIM] - y[:c]).astype(BF16) for x, y in zip(uw, sw)]
    o = [y[c:] + _dot(qk, vb) for y, qk, vb in zip(sw, qks, vnb)]
    s_new = []
    for p, vb in zip(pre, vnb):
        kdec = (p['load'](1) * jnp.exp(p['gtot'] - p['gc'])).astype(BF16)
        s_new.append(p['load_s']() * jnp.exp(p['gtot']) + _dot_tn(kdec, vb))
    return list(zip(o, s_new))


def _delta_kernel(has_s0, emit_final, one_block, n_chunk, *refs):
    refs = list(refs)
    if one_block:
        seq_in = [[r for r in refs[3 * s:3 * s + 3] for _ in range(2)] for s in range(DELTA_SEQS)]
        refs = refs[3 * DELTA_SEQS:]
    else:
        seq_in = [refs[6 * s:6 * s + 6] for s in range(DELTA_SEQS)]
        refs = refs[6 * DELTA_SEQS:]
    s0_ref = refs.pop(0) if has_s0 else None
    of_ref, ob_ref = refs[:2]
    refs = refs[2:]
    sfin_ref = refs.pop(0) if emit_final else None
    s_scr = refs[0]

    j = pl.program_id(1)

    @pl.when(j == 0)
    def _():
        if has_s0:
            s_scr[...] = s0_ref[...]
        else:
            s_scr[...] = jnp.zeros(s_scr.shape, F32)

    c = CHUNK

    def body(ci, carry):
        units, where = [], []
        for s, (qkvf_ref, qkvb_ref, gatef_ref, gateb_ref, gtf_ref, gtb_ref) in enumerate(seq_in):
            for d in range(2):
                cidx = ci if d == 0 else n_chunk - 1 - ci
                r0 = pl.multiple_of(cidx * c, c)
                qkv_ref = qkvf_ref if d == 0 else qkvb_ref
                gate = (gatef_ref if d == 0 else gateb_ref)[pl.ds(r0, c), :]
                gt = (gtf_ref if d == 0 else gtb_ref)[cidx]
                for hd in range(N_HEADS):
                    units.append((d, hd, qkv_ref, r0, gate, gt,
                                  lambda s=s, idx=N_HEADS * d + hd: s_scr[s, idx]))
                    where.append((s, d, hd, r0))
        for (s, d, hd, r0), (o, s_new) in zip(where, _delta_units(units)):
            s_scr[s, N_HEADS * d + hd] = s_new
            (of_ref if d == 0 else ob_ref)[s, pl.ds(r0, c), hd * HEAD_DIM:(hd + 1) * HEAD_DIM] = o
        return carry

    lax.fori_loop(0, n_chunk, body, 0)

    if emit_final:
        @pl.when(j == pl.num_programs(1) - 1)
        def _():
            sfin_ref[...] = s_scr[...]


def _delta(qkv, gate, gate_t, n_seq, seq_len, row0, blk, s0):
    nblk = seq_len // blk
    n_chunk = blk // CHUNK
    b0 = row0 // blk
    has_s0 = s0 is not None
    emit_final = not has_s0
    n_state = 2 * N_HEADS
    one_block = nblk == 1
    assert n_seq % DELTA_SEQS == 0 and row0 % blk == 0

    in_specs, args = [], []
    for s in range(DELTA_SEQS):
        def fwd(b, j, s=s):
            return b0 + (b * DELTA_SEQS + s) * nblk + j

        def bwd(b, j, s=s):
            return b0 + (b * DELTA_SEQS + s) * nblk + (nblk - 1 - j)

        for arr, shape in ((qkv, (blk, 3 * MIX_A)), (gate, (blk, LANES)), (gate_t, (n_chunk, GATE_ROWS, CHUNK))):
            tail = (0,) * (len(shape) - 1)
            for f in (fwd,) if one_block else (fwd, bwd):
                in_specs.append(pl.BlockSpec(shape, lambda b, j, f=f, tail=tail: (f(b, j),) + tail))
                args.append(arr)
    state_spec = pl.BlockSpec((DELTA_SEQS, n_state, HEAD_DIM, HEAD_DIM), lambda b, j: (b, 0, 0, 0))
    if has_s0:
        in_specs.append(state_spec)
        args.append(s0)
    out_specs = [pl.BlockSpec((DELTA_SEQS, blk, MIX_A), lambda b, j: (b, j, 0)),
                 pl.BlockSpec((DELTA_SEQS, blk, MIX_A), lambda b, j: (b, nblk - 1 - j, 0))]
    out_shape = [jax.ShapeDtypeStruct((n_seq, seq_len, MIX_A), F32)] * 2
    if emit_final:
        out_specs.append(state_spec)
        out_shape.append(jax.ShapeDtypeStruct((n_seq, n_state, HEAD_DIM, HEAD_DIM), F32))
    outs = pl.pallas_call(
        functools.partial(_delta_kernel, has_s0, emit_final, one_block, n_chunk),
        grid=(n_seq // DELTA_SEQS, nblk),
        in_specs=in_specs,
        out_specs=out_specs,
        out_shape=out_shape,
        scratch_shapes=[pltpu.VMEM((DELTA_SEQS, n_state, HEAD_DIM, HEAD_DIM), F32)],
        compiler_params=_cparams(("parallel", "arbitrary")),
        name="delta_latent" if has_s0 else "delta_prompt",
    )(*args)
    return [o.reshape(n_seq * seq_len, MIX_A) for o in outs[:2]] + list(outs[2:])


def _outproj_kernel(lay, xp_ref, xs_ref, prow_ref, pcol_ref, mod_ref, opf_ref, opb_ref, osf_ref, osb_ref, z_ref,
                    u_ref, og_ref, n2_ref, wo_ref, wr_ref, br_ref, x1_ref, h2_ref, ids_ref, wts_ref, cnt_ref,
                    cnt_scr):
    i = pl.program_id(0)
    m = mod_ref[0]

    @pl.when(i == 0)
    def _():
        cnt_scr[...] = jnp.zeros(cnt_scr.shape, F32)

    halves = [slice(r0, r0 + PROJ_ROWS) for r0 in range(0, lay.tile, PROJ_ROWS)]
    n = PROJ_ROWS
    lane = lax.broadcasted_iota(I32, (n, LANES), 1)
    lane_f = lane.astype(F32)
    neg = jnp.float32(-jnp.inf)
    big = jnp.float32(LANES)

    def first_lane(mask):
        return jnp.min(jnp.where(mask, lane_f, big), axis=-1, keepdims=True)

    gated = []
    for rows in halves:
        o = jnp.where(lay.is_sample(i), osf_ref[rows, :] + osb_ref[rows, :], opf_ref[rows, :] + opb_ref[rows, :])
        z = z_ref[rows, :]
        gated.append([(_rms(o[:, sl]) * og_ref[...] * _silu(z[:, sl])).astype(BF16)
                      for sl in (slice(hd * HEAD_DIM, (hd + 1) * HEAD_DIM) for hd in range(N_HEADS))])
    mixes = []
    for rows, heads in zip(halves, gated):
        mix = _dot(u_ref[rows, :], wo_ref[MIX_A:, :])
        for hd, oh in enumerate(heads):
            mix = mix + _dot(oh, wo_ref[hd * HEAD_DIM:(hd + 1) * HEAD_DIM, :])
        mixes.append(mix)
    h2bs = []
    for rows, mix in zip(halves, mixes):
        x1 = _load_x(lay, i, xp_ref, xs_ref, prow_ref, pcol_ref, rows) + m[2:3] * mix
        x1_ref[rows, :] = x1
        h2b = (_rms(x1) * n2_ref[...] * (1.0 + m[4:5]) + m[3:4]).astype(BF16)
        h2_ref[rows, :] = _pack_bf16_pairs(h2b)
        h2bs.append(h2b)
    logits_all = [_dot(h2b, wr_ref[...]) + br_ref[...] for h2b in h2bs]

    picks = []
    for rows, logits in zip(halves, logits_all):
        gl = jnp.where(lane < N_GROUPS, logits, neg)
        gmax = jnp.max(gl, axis=-1, keepdims=True)
        grp = first_lane(gl == gmax)
        p_grp = 1.0 / jnp.sum(jnp.where(lane < N_GROUPS, jnp.exp(gl - gmax), 0.0), axis=-1, keepdims=True)
        e_lane = lane_f - N_GROUPS
        in_grp = (e_lane >= grp * EXPERTS_PER_GROUP) & (e_lane < (grp + 1.0) * EXPERTS_PER_GROUP)
        el = jnp.where(in_grp, logits, neg)
        m1 = jnp.max(el, axis=-1, keepdims=True)
        i1f = first_lane(el == m1)
        el2 = jnp.where(lane_f == i1f, neg, el)
        m2 = jnp.max(el2, axis=-1, keepdims=True)
        i2f = first_lane(el2 == m2)
        e2 = jnp.exp(m2 - m1)
        w1 = p_grp / (1.0 + e2)
        w2 = p_grp * e2 / (1.0 + e2)
        col = lax.broadcasted_iota(I32, (n, ROUTE_COLS), 1)
        wts_ref[rows, :] = jnp.where(col == 0, w1, jnp.where(col == 1, w2, 0.0))
        picks.append((i1f, i2f))

    r_i = lax.broadcasted_iota(I32, (n, n), 0)
    c_i = lax.broadcasted_iota(I32, (n, n), 1)
    before = jnp.where(r_i > c_i, 1.0, 0.0).astype(BF16)
    onehots = [(lane_f == i1f, lane_f == i2f) for i1f, i2f in picks]
    chosen = [jnp.where(oh1, 1.0, jnp.where(oh2, 1.0, 0.0)) for oh1, oh2 in onehots]
    prefix = [_dot(before, oh.astype(BF16)) for oh in chosen]
    counts = cnt_scr[...]
    for rows, (i1f, i2f), (oh1, oh2), oh, pre in zip(halves, picks, onehots, chosen, prefix):
        seen = counts + pre
        r1 = jnp.sum(jnp.where(oh1, seen, 0.0), axis=-1, keepdims=True)
        r2 = jnp.sum(jnp.where(oh2, seen, 0.0), axis=-1, keepdims=True)
        ids = jnp.where(lane == 0, i1f - N_GROUPS,
                        jnp.where(lane == 1, i2f - N_GROUPS,
                                  jnp.where(lane == 2, r1, jnp.where(lane == 3, r2, 0.0))))
        ids_ref[:, rows] = jnp.transpose(ids)[:ROUTE_COLS, :].astype(I32)
        counts = counts + jnp.sum(oh, axis=0, keepdims=True)
    cnt_scr[...] = counts
    cnt_ref[...] = counts


def _outproj(lay, xp, xs, pos, mod, o_pf, o_pb, o_sf, o_sb, z, u, onorm_g, norm2_g, w_out, w_route, b_route):
    t = lay.t
    p_spec = pl.BlockSpec((lay.tile, MIX_A), lambda i: (lay.xp_blk(i), 0))
    s_spec = pl.BlockSpec((lay.tile, MIX_A), lambda i: (lay.xs_blk(i), 0))
    return pl.pallas_call(
        functools.partial(_outproj_kernel, lay),
        grid=(lay.tiles,),
        in_specs=_x_specs(lay) + [_mod_spec(lay), p_spec, p_spec, s_spec, s_spec, _row_spec(lay, MIX_A),
                                  _row_spec(lay, MIX_B), _full_spec((1, HEAD_DIM)), _full_spec((1, D_MODEL)),
                                  _full_spec(w_out.shape), _full_spec(w_route.shape),
                                  _full_spec(b_route.shape)],
        out_specs=[_row_spec(lay, D_MODEL), _row_spec(lay, D_MODEL // 2),
                   pl.BlockSpec((ROUTE_COLS, lay.tile), lambda i: (0, i)),
                   _row_spec(lay, ROUTE_COLS), _full_spec((1, LANES))],
        out_shape=[jax.ShapeDtypeStruct((t, D_MODEL), F32),
                   jax.ShapeDtypeStruct((t, D_MODEL // 2), I32),
                   jax.ShapeDtypeStruct((ROUTE_COLS, t), I32),
                   jax.ShapeDtypeStruct((t, ROUTE_COLS), F32),
                   jax.ShapeDtypeStruct((1, LANES), F32)],
        scratch_shapes=[pltpu.VMEM((1, LANES), F32)],
        compiler_params=_cparams(("arbitrary",)),
        name="outproj",
    )(xp, xs, *pos, mod, o_pf, o_pb, o_sf, o_sb, z, u, onorm_g, norm2_g, w_out, w_route, b_route)


def _sc_gather(table, idx, chunk, name):
    n_rows, width = idx.shape[0], table.shape[1]
    mesh = plsc.VectorSubcoreMesh(core_axis_name="c", subcore_axis_name="s")
    n_workers = mesh.num_cores * mesh.num_subcores
    per_worker = n_rows // n_workers
    n_chunks = per_worker // chunk
    assert n_rows == n_workers * n_chunks * chunk and n_chunks % 2 == 0 and chunk % 8 == 0 and chunk <= LANES

    def body(table_hbm, idx_hbm, out_hbm, idx_v, rows_v, sem):
        base = (lax.axis_index("s") * mesh.num_cores + lax.axis_index("c")) * per_worker

        def gather(slot):
            return pltpu.make_async_copy(table_hbm.at[idx_v.at[slot]], rows_v.at[slot], sem.at[slot])

        def fetch(g, slot):
            off = pl.multiple_of(base + g * chunk, 8)
            pltpu.sync_copy(idx_hbm.at[pl.ds(off, chunk)], idx_v.at[slot])
            gather(slot).start()

        for slot in range(2):
            fetch(slot, slot)

        @pl.loop(0, n_chunks, step=2)
        def _(g):
            for slot in range(2):
                off = pl.multiple_of(base + (g + slot) * chunk, 8)
                gather(slot).wait()
                pltpu.sync_copy(rows_v.at[slot], out_hbm.at[pl.ds(off, chunk)])

                @pl.when(g + slot + 2 < n_chunks)
                def _():
                    fetch(g + slot + 2, slot)

    return pl.kernel(
        body,
        out_type=jax.ShapeDtypeStruct((n_rows, width), table.dtype),
        mesh=mesh,
        scratch_types=[pltpu.VMEM((2, chunk), I32), pltpu.VMEM((2, chunk, width), table.dtype),
                       pltpu.SemaphoreType.DMA((2,))],
        name=name,
    )(table, idx)


def _sc_scatter2(src, idx0, idx1, n_out, chunk, name):
    n_rows, width = src.shape
    mesh = plsc.VectorSubcoreMesh(core_axis_name="c", subcore_axis_name="s")
    n_workers = mesh.num_cores * mesh.num_subcores
    per_worker = n_rows // n_workers
    n_chunks = per_worker // chunk
    assert n_rows == n_workers * n_chunks * chunk and n_chunks % 2 == 0 and chunk % 8 == 0 and chunk <= LANES

    def body(src_hbm, i0_hbm, i1_hbm, out_hbm, i0_v, i1_v, rows_v, sem_in, sem_out):
        base = (lax.axis_index("s") * mesh.num_cores + lax.axis_index("c")) * per_worker

        def rows_in(g, slot):
            off = pl.multiple_of(base + g * chunk, 8)
            return pltpu.make_async_copy(src_hbm.at[pl.ds(off, chunk)], rows_v.at[slot], sem_in.at[slot])

        def fetch(g, slot):
            off = pl.multiple_of(base + g * chunk, 8)
            pltpu.sync_copy(i0_hbm.at[pl.ds(off, chunk)], i0_v.at[slot])
            pltpu.sync_copy(i1_hbm.at[pl.ds(off, chunk)], i1_v.at[slot])
            rows_in(g, slot).start()

        for slot in range(2):
            fetch(slot, slot)

        @pl.loop(0, n_chunks, step=2)
        def _(g):
            for slot in range(2):
                rows_in(g + slot, slot).wait()
                puts = [pltpu.make_async_copy(rows_v.at[slot], out_hbm.at[iv.at[slot]], sem_out.at[slot])
                        for iv in (i0_v, i1_v)]
                for put in puts:
                    put.start()
                for put in puts:
                    put.wait()

                @pl.when(g + slot + 2 < n_chunks)
                def _():
                    fetch(g + slot + 2, slot)

    return pl.kernel(
        body,
        out_type=jax.ShapeDtypeStruct((n_out, width), src.dtype),
        mesh=mesh,
        scratch_types=[pltpu.VMEM((2, chunk), I32), pltpu.VMEM((2, chunk), I32),
                       pltpu.VMEM((2, chunk, width), src.dtype),
                       pltpu.SemaphoreType.DMA((2,)), pltpu.SemaphoreType.DMA((2,))],
        name=name,
    )(src, idx0, idx1)


def _expert_kernel(blk_e_ref, nused_ref, xb_ref, wg_ref, wu_ref, wd_ref, yb_ref, wg_s, wu_s, wd_s):
    i = pl.program_id(0)
    nused = nused_ref[0]

    @pl.when(i < nused)
    def _():
        changed = (i == 0) | (blk_e_ref[i] != blk_e_ref[jnp.maximum(i - 1, 0)])

        @pl.when(changed)
        def _():
            wg_s[...] = wg_ref[0].astype(BF16)
            wu_s[...] = wu_ref[0].astype(BF16)
            wd_s[...] = wd_ref[0].astype(BF16)

        half = D_MODEL // 2
        xa, xb = (v.astype(BF16) for v in _unpack_bf16_pairs(xb_ref[...]))
        g = _dot(xa, wg_s[:half, :]) + _dot(xb, wg_s[half:, :])
        u = _dot(xa, wu_s[:half, :]) + _dot(xb, wu_s[half:, :])
        hmid = (_silu(g) * u).astype(BF16)
        yb_ref[...] = _pack_bf16_pairs(_dot(hmid, wd_s[...]).astype(BF16))

    @pl.when(i >= nused)
    def _():
        yb_ref[...] = jnp.zeros(yb_ref.shape, I32)


def _experts(xb, blk_e, nused, w_gate, w_up, w_down):
    nb = blk_e.shape[0]

    def weight_spec(shape):
        return pl.BlockSpec((1,) + shape, lambda i, be, nu: (be[i], 0, 0))

    grid_spec = pltpu.PrefetchScalarGridSpec(
        num_scalar_prefetch=2,
        grid=(nb,),
        in_specs=[
            pl.BlockSpec((ROUTE_BM, D_MODEL // 2), lambda i, be, nu: (i, 0)),
            weight_spec((D_MODEL, D_EXPERT)),
            weight_spec((D_MODEL, D_EXPERT)),
            weight_spec((D_EXPERT, D_MODEL)),
        ],
        out_specs=pl.BlockSpec((ROUTE_BM, D_MODEL // 2), lambda i, be, nu: (i, 0)),
        scratch_shapes=[pltpu.VMEM((D_MODEL, D_EXPERT), BF16),
                        pltpu.VMEM((D_MODEL, D_EXPERT), BF16),
                        pltpu.VMEM((D_EXPERT, D_MODEL), BF16)],
    )
    return pl.pallas_call(
        _expert_kernel,
        grid_spec=grid_spec,
        out_shape=jax.ShapeDtypeStruct((nb * ROUTE_BM, D_MODEL // 2), I32),
        compiler_params=_cparams(("arbitrary",)),
        name="expert",
    )(blk_e, nused, xb, w_gate, w_up, w_down)


def _combine_kernel(lay, y0_ref, y1_ref, x1_ref, wts_ref, mod_ref, fg_ref, yp_ref, ys_ref):
    i = pl.program_id(0)
    m = mod_ref[0]
    w = wts_ref[...]
    a0, b0 = _unpack_bf16_pairs(y0_ref[...])
    a1, b1 = _unpack_bf16_pairs(y1_ref[...])
    ff = jnp.concatenate([a0 * w[:, 0:1] + a1 * w[:, 1:2], b0 * w[:, 0:1] + b1 * w[:, 1:2]], axis=1)
    y = _rms(x1_ref[...] + m[5:6] * ff) * fg_ref[...]

    @pl.when(i < lay.tiles_p)
    def _():
        yp_ref[...] = y

    @pl.when(i >= lay.tiles_p)
    def _():
        ys_ref[...] = y


def _combine(lay, yg, x1, wts, mod, final_g):
    return pl.pallas_call(
        functools.partial(_combine_kernel, lay),
        grid=(lay.tiles,),
        in_specs=[_row_spec(lay, D_MODEL // 2),
                  pl.BlockSpec((lay.tile, D_MODEL // 2), lambda i: (i + lay.tiles, 0)),
                  _row_spec(lay, D_MODEL), _row_spec(lay, ROUTE_COLS), _mod_spec(lay), _full_spec((1, D_MODEL))],
        out_specs=[pl.BlockSpec((lay.tile, D_MODEL), lambda i: (lay.xp_blk(i), 0)),
                   pl.BlockSpec((lay.tile, D_MODEL), lambda i: (lay.xs_blk(i), 0))],
        out_shape=[jax.ShapeDtypeStruct((lay.t_p, D_MODEL), F32),
                   jax.ShapeDtypeStruct((lay.t_s, D_MODEL), F32)],
        compiler_params=_cparams(("arbitrary",)),
        name="combine",
    )(yg, yg, x1, wts, mod, final_g)


def _dispatch_plan(ids, counts):
    n_tok = ids.shape[1]
    padded = (counts + ROUTE_BM - 1) // ROUTE_BM * ROUTE_BM
    pad_end = jnp.cumsum(padded)
    pad_start = pad_end - padded
    experts = jnp.arange(N_EXPERTS, dtype=I32)[:, None]
    dest = [(jnp.sum(jnp.where(ids[k][None, :] == experts, pad_start[:, None], 0), axis=0) + ids[2 + k]).astype(I32)
            for k in range(2)]
    nb = -(-(2 * n_tok + N_EXPERTS * (ROUTE_BM - 1)) // ROUTE_BM)
    block_start = jnp.arange(nb, dtype=I32) * ROUTE_BM
    blk_e = jnp.minimum(jnp.sum(pad_end[None, :] <= block_start[:, None], axis=1), N_EXPERTS - 1).astype(I32)
    nused = (pad_end[-1:] // ROUTE_BM).astype(I32)
    return dest[0], dest[1], blk_e, nused


def _grid_pos_tables(n_tokens):
    rows = n_tokens // GRID_W
    n_freq = D_MODEL // 4
    freq = jnp.exp(jnp.arange(n_freq, dtype=F32) * (-math.log(10000.0) / n_freq))

    def enc(p):
        ang = p[:, None] * freq[None, :]
        return jnp.concatenate([jnp.sin(ang), jnp.cos(ang)], axis=-1)

    return enc(jnp.arange(rows, dtype=F32)), enc(jnp.arange(GRID_W, dtype=F32))


def _lane_pad(v, offset):
    return jnp.zeros((1, LANES), F32).at[0, offset:offset + v.shape[0]].set(v.astype(F32))


def kernel(x_prompt, x_sample, state_delta, c, c_ctx, norm1_g, w_mod, b_mod, w_in, conv_qkv_w, A_log, dt_bias, onorm_g, dw_w, dw_b, cln_g, cln_b, w_out, norm2_g, w_group, b_group, w_expert, b_expert, w_e_gate, w_e_up, w_e_down, final_g):
    n_p, l_p, _ = x_prompt.shape
    n_s, l_s, _ = x_sample.shape
    lay = _Layout(n_p, l_p, n_s, l_s, TOK_TILE)
    lay_proj = _Layout(n_p, l_p, n_s, l_s, PROJ_TILE)
    depth = w_in.shape[0]
    assert depth == 1
    xp = x_prompt.reshape(lay.t_p, D_MODEL)
    xs = x_sample.reshape(lay.t_s, D_MODEL)
    pos = _grid_pos_tables(l_s)

    cond = jnp.concatenate([c_ctx[None, :], c], axis=0)
    cond = jnp.pad(cond, ((0, (-cond.shape[0]) % SUBLANES), (0, 0)))
    mod = _modulation(cond, w_mod[0], b_mod[0]).reshape(cond.shape[0], 6, D_MODEL)

    n_gate = 4 * N_HEADS
    wi = w_in[0]
    w_main = jnp.concatenate([wi[:, :4 * MIX_A], wi[:, 4 * MIX_A + n_gate:]], axis=1).astype(BF16)
    w_small = jnp.pad(wi[:, 4 * MIX_A:4 * MIX_A + n_gate], ((0, 0), (0, LANES - n_gate))).astype(BF16)
    qkv_raw, z, ug, ba = _inproj(lay_proj, xp, xs, pos, mod, norm1_g[0][None, :], w_main, w_small)

    cw = jnp.pad(conv_qkv_w[0], ((0, 8 - SHORT_CONV), (0, 0))).reshape(8, Q_GROUPS, LANES).transpose(1, 0, 2)
    dw = jnp.pad(dw_w[0], ((0, 32 - CONV_W), (0, 0))).reshape(32, U_GROUPS, LANES).transpose(1, 0, 2)
    alog = _lane_pad(A_log[0].reshape(-1), 2 * N_HEADS)
    dtb = _lane_pad(dt_bias[0].reshape(-1), 2 * N_HEADS)
    qkv, u_conf, gate, gate_t = _conv(lay, qkv_raw, ug, ba, cw, dw, dw_b[0][None, :], cln_g[0][None, :],
                                      cln_b[0][None, :], alog, dtb)

    o_pf, o_pb, s_fin = _delta(qkv, gate, gate_t, n_p, l_p, 0, l_p, None)
    s0 = state_delta[:, 0].reshape(n_s, 2 * N_HEADS, HEAD_DIM, HEAD_DIM)
    o_sf, o_sb = _delta(qkv, gate, gate_t, n_s, l_s, lay.t_p, DELTA_BLK, s0)

    w_route = jnp.pad(jnp.concatenate([w_group[0], w_expert[0]], axis=1),
                      ((0, 0), (0, LANES - N_GROUPS - N_EXPERTS))).astype(BF16)
    b_route = _lane_pad(jnp.concatenate([b_group[0], b_expert[0]]), 0)
    x1, h2p, ids, wts, cnt = _outproj(lay_proj, xp, xs, pos, mod, o_pf, o_pb, o_sf, o_sb, z, u_conf,
                                      onorm_g[0][None, :], norm2_g[0][None, :], w_out[0].astype(BF16), w_route,
                                      b_route)

    counts = cnt[0, N_GROUPS:N_GROUPS + N_EXPERTS].astype(I32)
    dest0, dest1, blk_e, nused = _dispatch_plan(ids, counts)
    xb = _sc_scatter2(h2p, dest0, dest1, blk_e.shape[0] * ROUTE_BM, DISPATCH_CHUNK, "dispatch_scatter")
    yb = _experts(xb, blk_e, nused, w_e_gate[0], w_e_up[0], w_e_down[0])
    yg = _sc_gather(yb, jnp.concatenate([dest0, dest1]), COMBINE_CHUNK, "combine_gather")
    y_p, y_s = _combine(_Layout(n_p, l_p, n_s, l_s, COMBINE_TILE), yg, x1, wts, mod, final_g[None, :])

    new_state = s_fin.reshape(n_p, 1, 2, N_HEADS, HEAD_DIM, HEAD_DIM)
    return (y_p.reshape(x_prompt.shape), y_s.reshape(x_sample.shape), new_state)
```

```python
import functools
import math

import jax
import jax.numpy as jnp
from jax import lax
from jax.experimental import pallas as pl
from jax.experimental.pallas import tpu as pltpu
from jax.experimental.pallas import tpu_sc as plsc

F32 = jnp.float32
BF16 = jnp.bfloat16
I32 = jnp.int32

D_MODEL = 1024
MIX_A = 512
MIX_B = 512
HEAD_DIM = 128
N_HEADS = 4
SHORT_CONV = 5
CONV_W = 31
N_GROUPS = 4
EXPERTS_PER_GROUP = 8
N_EXPERTS = 32
D_EXPERT = 512
GRID_W = 64
EPS = 1e-6

LANES = 128
SUBLANES = 8
TOK_TILE = 256
PROJ_TILE = 512
PROJ_ROWS = 256
CHUNK = 128
DELTA_BLK = 512
DELTA_SEQS = 2
ROUTE_BM = 512
ROUTE_COLS = 8
GATE_ROWS = 16
COMBINE_TILE = 1024
DISPATCH_CHUNK = 64
COMBINE_CHUNK = 64
Q_HALO = 8
U_HALO = 16
VMEM_LIMIT = 56 * 1024 * 1024

HIGHEST = lax.Precision.HIGHEST


def _cparams(sem):
    return pltpu.CompilerParams(dimension_semantics=sem, vmem_limit_bytes=VMEM_LIMIT)


def _sigmoid(x):
    return 0.5 * jnp.tanh(0.5 * x) + 0.5


def _silu(x):
    return x * _sigmoid(x)


def _softplus(x):
    return jnp.maximum(x, 0.0) + jnp.log1p(jnp.exp(-jnp.abs(x)))


def _rms(x):
    return x * lax.rsqrt(jnp.mean(x * x, axis=-1, keepdims=True) + EPS)


def _pack_bf16_pairs(x):
    h = x.shape[1] // 2
    hi = pltpu.bitcast(x[:, :h].astype(F32), jnp.uint32)
    lo = pltpu.bitcast(x[:, h:].astype(F32), jnp.uint32)
    return pltpu.bitcast((hi & jnp.uint32(0xFFFF0000)) | (lo >> 16), I32)


def _unpack_bf16_pairs(p):
    u = pltpu.bitcast(p, jnp.uint32)
    return pltpu.bitcast(u & jnp.uint32(0xFFFF0000), F32), pltpu.bitcast(u << 16, F32)


def _dot(a, b):
    return jnp.dot(a, b, preferred_element_type=F32)


def _dot_nt(a, b):
    return lax.dot_general(a, b, (((1,), (1,)), ((), ())), preferred_element_type=F32)


def _dot_tn(a, b):
    return lax.dot_general(a, b, (((0,), (0,)), ((), ())), preferred_element_type=F32)


def _mod_kernel(cond_ref, w_ref, b_ref, o_ref):
    s = _silu(cond_ref[...])
    o_ref[...] = jnp.dot(s, w_ref[...], preferred_element_type=F32, precision=HIGHEST) + b_ref[...]


def _modulation(cond, w_mod, b_mod):
    n = cond.shape[0]
    tn = D_MODEL
    return pl.pallas_call(
        _mod_kernel,
        grid=(6 * D_MODEL // tn,),
        in_specs=[pl.BlockSpec((n, D_MODEL), lambda j: (0, 0)),
                  pl.BlockSpec((D_MODEL, tn), lambda j: (0, j)),
                  pl.BlockSpec((1, tn), lambda j: (0, j))],
        out_specs=pl.BlockSpec((n, tn), lambda j: (0, j)),
        out_shape=jax.ShapeDtypeStruct((n, 6 * D_MODEL), F32),
        compiler_params=_cparams(("parallel",)),
        name="mod",
    )(cond, w_mod, b_mod.reshape(1, -1))


class _Layout:
    def __init__(self, n_p, l_p, n_s, l_s, tile):
        self.n_p, self.l_p, self.n_s, self.l_s, self.tile = n_p, l_p, n_s, l_s, tile
        self.t_p = n_p * l_p
        self.t_s = n_s * l_s
        self.t = self.t_p + self.t_s
        assert self.t_p % tile == 0 and l_s % tile == 0 and l_s % DELTA_BLK == 0
        self.tiles_p = self.t_p // tile
        self.tiles_s = self.t_s // tile
        self.tiles = self.tiles_p + self.tiles_s
        self.tps_p = max(l_p // tile, 1)
        self.tps_s = l_s // tile

    def is_sample(self, i):
        return i >= self.tiles_p

    def mod_row(self, i):
        return jnp.where(i < self.tiles_p, 0, 1 + (i - self.tiles_p) // self.tps_s)

    def pos_blk(self, i):
        return jnp.where(i < self.tiles_p, 0, (i - self.tiles_p) % self.tps_s)

    def xp_blk(self, i):
        return jnp.minimum(i, self.tiles_p - 1)

    def xs_blk(self, i):
        return jnp.maximum(i - self.tiles_p, 0)

    def seq_pos(self, i):
        in_s = i >= self.tiles_p
        pos = jnp.where(in_s, (i - self.tiles_p) % self.tps_s, i % self.tps_p)
        n = jnp.where(in_s, self.tps_s, self.tps_p)
        return pos, n


def _load_x(lay, i, xp_ref, xs_ref, prow_ref, pcol_ref, rows):
    g0, g1 = rows.start // GRID_W, rows.stop // GRID_W
    half = D_MODEL // 2
    prow = jnp.concatenate([jnp.broadcast_to(prow_ref[r:r + 1, :], (GRID_W, half)) for r in range(g0, g1)],
                           axis=0)
    pcol = jnp.concatenate([pcol_ref[...]] * (g1 - g0), axis=0)
    pos = jnp.concatenate([prow, pcol], axis=1)
    return jnp.where(lay.is_sample(i), xs_ref[rows, :] + pos, xp_ref[rows, :])


def _x_specs(lay):
    assert lay.tile % (SUBLANES * GRID_W) == 0
    return [pl.BlockSpec((lay.tile, D_MODEL), lambda i: (lay.xp_blk(i), 0)),
            pl.BlockSpec((lay.tile, D_MODEL), lambda i: (lay.xs_blk(i), 0)),
            pl.BlockSpec((lay.tile // GRID_W, D_MODEL // 2), lambda i: (lay.pos_blk(i), 0)),
            _full_spec((GRID_W, D_MODEL // 2))]


def _mod_spec(lay):
    return pl.BlockSpec((1, 6, D_MODEL), lambda i: (lay.mod_row(i), 0, 0))


def _row_spec(lay, width):
    return pl.BlockSpec((lay.tile, width), lambda i: (i, 0))


def _full_spec(shape):
    nd = len(shape)
    return pl.BlockSpec(shape, lambda i: (0,) * nd)


def _inproj_kernel(lay, xp_ref, xs_ref, prow_ref, pcol_ref, mod_ref, g_ref, wm_ref, ws_ref,
                   qkv_ref, z_ref, ug_ref, ba_ref):
    i = pl.program_id(0)
    m = mod_ref[0]
    x = _load_x(lay, i, xp_ref, xs_ref, prow_ref, pcol_ref, slice(0, lay.tile))
    h = _rms(x) * g_ref[...] * (1.0 + m[1:2]) + m[0:1]
    hb = h.astype(BF16)
    qkv_ref[...] = _dot(hb, wm_ref[:, 0:3 * MIX_A])
    z_ref[...] = _dot(hb, wm_ref[:, 3 * MIX_A:4 * MIX_A])
    glu = _dot(hb, wm_ref[:, 4 * MIX_A:4 * MIX_A + 2 * MIX_B])
    ug_ref[...] = glu[:, :MIX_B] * _sigmoid(glu[:, MIX_B:])
    ba_ref[...] = _dot(hb, ws_ref[...])


def _inproj(lay, xp, xs, pos, mod, norm_g, w_main, w_small):
    t = lay.t
    return pl.pallas_call(
        functools.partial(_inproj_kernel, lay),
        grid=(lay.tiles,),
        in_specs=_x_specs(lay) + [_mod_spec(lay), _full_spec((1, D_MODEL)),
                                  _full_spec(w_main.shape), _full_spec(w_small.shape)],
        out_specs=[_row_spec(lay, 3 * MIX_A), _row_spec(lay, MIX_A), _row_spec(lay, MIX_B),
                   _row_spec(lay, LANES)],
        out_shape=[jax.ShapeDtypeStruct((t, 3 * MIX_A), F32),
                   jax.ShapeDtypeStruct((t, MIX_A), F32),
                   jax.ShapeDtypeStruct((t, MIX_B), F32),
                   jax.ShapeDtypeStruct((t, LANES), F32)],
        compiler_params=_cparams(("parallel",)),
        name="inproj",
    )(xp, xs, *pos, mod, norm_g, w_main, w_small)


CONV_PITCH = 33
CONV_OUT_ROWS = SUBLANES * CONV_PITCH
POST_ROWS = 64
Q_EXT_ROWS = 280
U_EXT_ROWS = 296
Q_GROUPS = 3 * MIX_A // LANES
U_GROUPS = MIX_B // LANES
CONV_J_BLOCK = 11


def _strided_conv(ext_scr, res_scr, w_ref, g, off, n_taps):
    for j0 in range(0, CONV_PITCH, CONV_J_BLOCK):
        js = range(j0, min(j0 + CONV_J_BLOCK, CONV_PITCH))
        v = {m: ext_scr[g, pl.ds(off + m, SUBLANES, stride=CONV_PITCH), :]
             for m in range(js[0], js[-1] + n_taps)}
        for j in js:
            acc = v[j] * w_ref[g, 0:1, :]
            for s in range(1, n_taps):
                acc = acc + v[j + s] * w_ref[g, s:s + 1, :]
            res_scr[g, pl.ds(j, SUBLANES, stride=CONV_PITCH), :] = acc


def _conv_kernel(lay, qc_ref, qp_ref, qn_ref, uc_ref, up_ref, un_ref, ba_ref, cw_ref, dw_ref,
                 dwb_ref, lng_ref, lnb_ref, alog_ref, dtb_ref,
                 qkv_ref, uo_ref, gate_ref, gt_ref, eq_scr, eu_scr, rq_scr, ru_scr):
    i = pl.program_id(0)
    pos, n = lay.seq_pos(i)
    has_prev = pos != 0
    has_next = pos != n - 1
    qp = jnp.where(has_prev, qp_ref[...], 0.0)
    qn = jnp.where(has_next, qn_ref[...], 0.0)
    for g in range(Q_GROUPS):
        sl = slice(g * LANES, (g + 1) * LANES)
        eq_scr[g, 0:Q_HALO, :] = qp[:, sl]
        eq_scr[g, Q_HALO:Q_HALO + TOK_TILE, :] = qc_ref[:, sl]
        eq_scr[g, Q_HALO + TOK_TILE:Q_HALO + TOK_TILE + Q_HALO, :] = qn[:, sl]
        eq_scr[g, TOK_TILE + 2 * Q_HALO:, :] = jnp.zeros((Q_EXT_ROWS - TOK_TILE - 2 * Q_HALO, LANES), F32)
    up = jnp.where(has_prev, up_ref[...], 0.0)
    un_ = jnp.where(has_next, un_ref[...], 0.0)
    for g in range(U_GROUPS):
        sl = slice(g * LANES, (g + 1) * LANES)
        eu_scr[g, 0:U_HALO, :] = up[:, sl]
        eu_scr[g, U_HALO:U_HALO + TOK_TILE, :] = uc_ref[:, sl]
        eu_scr[g, U_HALO + TOK_TILE:U_HALO + TOK_TILE + U_HALO, :] = un_[:, sl]
        eu_scr[g, TOK_TILE + 2 * U_HALO:, :] = jnp.zeros((U_EXT_ROWS - TOK_TILE - 2 * U_HALO, LANES), F32)

    def q_group(g, carry):
        _strided_conv(eq_scr, rq_scr, cw_ref, g, Q_HALO - SHORT_CONV // 2, SHORT_CONV)
        return carry

    def u_group(g, carry):
        _strided_conv(eu_scr, ru_scr, dw_ref, g, U_HALO - CONV_W // 2, CONV_W)
        return carry

    lax.fori_loop(0, Q_GROUPS, q_group, 0)
    lax.fori_loop(0, U_GROUPS, u_group, 0)

    for rc in range(TOK_TILE // POST_ROWS):
        r0 = rc * POST_ROWS
        for g in range(Q_GROUPS):
            y = _silu(rq_scr[g, r0:r0 + POST_ROWS, :])
            if g < 2 * N_HEADS:
                y = y * lax.rsqrt(jnp.sum(y * y, axis=-1, keepdims=True) + EPS)
            qkv_ref[r0:r0 + POST_ROWS, g * LANES:(g + 1) * LANES] = y
        u = jnp.concatenate([ru_scr[g, r0:r0 + POST_ROWS, :] for g in range(U_GROUPS)], axis=1) + dwb_ref[...]
        uc = u - jnp.mean(u, axis=-1, keepdims=True)
        un = uc * lax.rsqrt(jnp.mean(uc * uc, axis=-1, keepdims=True) + EPS)
        uo_ref[r0:r0 + POST_ROWS, :] = _silu(un * lng_ref[...] + lnb_ref[...]).astype(BF16)

    x = ba_ref[...]
    g = -jnp.exp(alog_ref[...]) * _softplus(x + dtb_ref[...])
    beta = _sigmoid(x)
    rows = lax.broadcasted_iota(I32, (CHUNK, CHUNK), 0)
    cols = lax.broadcasted_iota(I32, (CHUNK, CHUNK), 1)
    tri_lo = (rows >= cols).astype(F32)
    tri_up = (rows <= cols).astype(F32)
    lane = lax.broadcasted_iota(I32, (CHUNK, LANES), 1)
    for ch in range(TOK_TILE // CHUNK):
        sl = slice(ch * CHUNK, (ch + 1) * CHUNK)
        pre = jnp.dot(tri_lo, g[sl], preferred_element_type=F32, precision=HIGHEST)
        suf = jnp.dot(tri_up, g[sl], preferred_element_type=F32, precision=HIGHEST)
        gate = jnp.where(lane < 2 * N_HEADS, beta[sl], jnp.where(lane < 3 * N_HEADS, pre, suf))
        gate_ref[sl, :] = gate
        gt_ref[ch] = jnp.transpose(gate)[:GATE_ROWS, :]


def _conv(lay, qkv_raw, ug, ba, cw, dw, dwb, lng, lnb, alog, dtb):
    t = lay.t
    qh = TOK_TILE // Q_HALO
    uh = TOK_TILE // U_HALO
    n_qh = t // Q_HALO
    n_uh = t // U_HALO
    in_specs = [
        _row_spec(lay, 3 * MIX_A),
        pl.BlockSpec((Q_HALO, 3 * MIX_A), lambda i: (jnp.maximum(i * qh - 1, 0), 0)),
        pl.BlockSpec((Q_HALO, 3 * MIX_A), lambda i: (jnp.minimum((i + 1) * qh, n_qh - 1), 0)),
        _row_spec(lay, MIX_B),
        pl.BlockSpec((U_HALO, MIX_B), lambda i: (jnp.maximum(i * uh - 1, 0), 0)),
        pl.BlockSpec((U_HALO, MIX_B), lambda i: (jnp.minimum((i + 1) * uh, n_uh - 1), 0)),
        _row_spec(lay, LANES),
        _full_spec(cw.shape), _full_spec(dw.shape), _full_spec(dwb.shape),
        _full_spec(lng.shape), _full_spec(lnb.shape), _full_spec(alog.shape), _full_spec(dtb.shape),
    ]
    return pl.pallas_call(
        functools.partial(_conv_kernel, lay),
        grid=(lay.tiles,),
        in_specs=in_specs,
        out_specs=[_row_spec(lay, 3 * MIX_A), _row_spec(lay, MIX_B), _row_spec(lay, LANES),
                   pl.BlockSpec((TOK_TILE // CHUNK, GATE_ROWS, CHUNK), lambda i: (i, 0, 0))],
        out_shape=[jax.ShapeDtypeStruct((t, 3 * MIX_A), F32),
                   jax.ShapeDtypeStruct((t, MIX_B), BF16),
                   jax.ShapeDtypeStruct((t, LANES), F32),
                   jax.ShapeDtypeStruct((t // CHUNK, GATE_ROWS, CHUNK), F32)],
        scratch_shapes=[pltpu.VMEM((Q_GROUPS, Q_EXT_ROWS, LANES), F32),
                        pltpu.VMEM((U_GROUPS, U_EXT_ROWS, LANES), F32),
                        pltpu.VMEM((Q_GROUPS, CONV_OUT_ROWS, LANES), F32),
                        pltpu.VMEM((U_GROUPS, CONV_OUT_ROWS, LANES), F32)],
        compiler_params=_cparams(("parallel",)),
        name="conv",
    )(qkv_raw, qkv_raw, qkv_raw, ug, ug, ug, ba, cw, dw, dwb, lng, lnb, alog, dtb)


INV_BASE = 8


def _b16(xs):
    return [x.astype(BF16) for x in xs]


def _tri_inverse_minus_eye(nmats, rows, cols):
    assert INV_BASE == 8
    c = nmats[0].shape[0]
    shift = int(math.log2(INV_BASE))
    same = (rows >> shift) == (cols >> shift)
    n1 = [jnp.where(same, n, 0.0) for n in nmats]
    n1b = _b16(n1)
    n2 = [_dot(x, x) for x in n1b]
    n2b = _b16(n2)
    r = [_dot(jnp.concatenate([a, b], axis=0), b) for a, b in zip(n1b, n2b)]
    q = [a + b + x[:c] for a, b, x in zip(n1, n2, r)]
    n4 = [x[c:] for x in r]
    qn4 = [_dot(a, b) for a, b in zip(_b16(q), _b16(n4))]
    q = [a + b + x for a, b, x in zip(q, n4, qn4)]
    while (1 << shift) < c:
        off = ((rows >> (shift + 1)) == (cols >> (shift + 1))) & ((rows >> shift) != (cols >> shift))
        a = [jnp.where(off, -n, 0.0) for n in nmats]
        x = [ai + _dot(ab, qb) for ai, ab, qb in zip(a, _b16(a), _b16(q))]
        qx = [_dot(qb, xb) for qb, xb in zip(_b16(q), _b16(x))]
        q = [qi - xi - qxi for qi, xi, qxi in zip(q, x, qx)]
        shift += 1
    return q


def _delta_units(units):
    c = CHUNK
    scale = HEAD_DIM ** -0.5
    rows = lax.broadcasted_iota(I32, (c, c), 0)
    cols = lax.broadcasted_iota(I32, (c, c), 1)
    pre = []
    for d, hd, qkv_ref, r0, gate, gt, load_s in units:
        lane = 2 * N_HEADS + N_HEADS * d + hd
        last = c - 1 if d == 0 else 0
        pre.append(dict(
            d=d, load_s=load_s,
            load=lambda part, qkv_ref=qkv_ref, r0=r0, hd=hd: qkv_ref[
                pl.ds(r0, c), part * MIX_A + hd * HEAD_DIM:part * MIX_A + (hd + 1) * HEAD_DIM],
            beta=gate[:, N_HEADS * d + hd:N_HEADS * d + hd + 1],
            gc=gate[:, lane:lane + 1],
            gr=gt[lane:lane + 1, :],
            gtot=gate[last:last + 1, lane:lane + 1]))

    ak = []
    for p in pre:
        k = p['load'](1)
        lhs = jnp.concatenate([k * p['beta'], p['load'](0) * scale], axis=0).astype(BF16)
        ak.append(_dot_nt(lhs, k.astype(BF16)))
    nmats, qks = [], []
    for p, a in zip(pre, ak):
        incl = (rows >= cols) if p['d'] == 0 else (rows <= cols)
        strict = (rows > cols) if p['d'] == 0 else (rows < cols)
        dec = jnp.where(incl, jnp.exp(p['gc'] - p['gr']), 0.0)
        nmats.append(jnp.where(strict, -a[:c] * dec, 0.0))
        qks.append((a[c:] * dec).astype(BF16))
    qinv = _tri_inverse_minus_eye(nmats, rows, cols)

    uw = []
    for p, qi in zip(pre, qinv):
        kb = p['load'](1) * p['beta']
        rhs = jnp.concatenate([p['load'](2) * p['beta'], kb * jnp.exp(p['gc'])], axis=1)
        uw.append(rhs + _dot(qi.astype(BF16), rhs.astype(BF16)))
    sw = []
    for p, x in zip(pre, uw):
        qdec = p['load'](0) * scale * jnp.exp(p['gc'])
        sw.append(_dot(jnp.concatenate([x[:, HEAD_DIM:], qdec], axis=0).astype(BF16), p['load_s']().astype(BF16)))
    vnb = [(x[:, :HEAD_DIM] - y[:c]).astype(BF16) for x, y in zip(uw, sw)]
    o = [y[c:] + _dot(qk, vb) for y, qk, vb in zip(sw, qks, vnb)]
    s_new = []
    for p, vb in zip(pre, vnb):
        kdec = (p['load'](1) * jnp.exp(p['gtot'] - p['gc'])).astype(BF16)
        s_new.append(p['load_s']() * jnp.exp(p['gtot']) + _dot_tn(kdec, vb))
    return list(zip(o, s_new))


def _delta_kernel(has_s0, emit_final, one_block, n_chunk, *refs):
    refs = list(refs)
    if one_block:
        seq_in = [[r for r in refs[3 * s:3 * s + 3] for _ in range(2)] for s in range(DELTA_SEQS)]
        refs = refs[3 * DELTA_SEQS:]
    else:
        seq_in = [refs[6 * s:6 * s + 6] for s in range(DELTA_SEQS)]
        refs = refs[6 * DELTA_SEQS:]
    s0_ref = refs.pop(0) if has_s0 else None
    of_ref, ob_ref = refs[:2]
    refs = refs[2:]
    sfin_ref = refs.pop(0) if emit_final else None
    s_scr = refs[0]

    j = pl.program_id(1)

    @pl.when(j == 0)
    def _():
        if has_s0:
            s_scr[...] = s0_ref[...]
        else:
            s_scr[...] = jnp.zeros(s_scr.shape, F32)

    c = CHUNK

    def body(ci, carry):
        units, where = [], []
        for s, (qkvf_ref, qkvb_ref, gatef_ref, gateb_ref, gtf_ref, gtb_ref) in enumerate(seq_in):
            for d in range(2):
                cidx = ci if d == 0 else n_chunk - 1 - ci
                r0 = pl.multiple_of(cidx * c, c)
                qkv_ref = qkvf_ref if d == 0 else qkvb_ref
                gate = (gatef_ref if d == 0 else gateb_ref)[pl.ds(r0, c), :]
                gt = (gtf_ref if d == 0 else gtb_ref)[cidx]
                for hd in range(N_HEADS):
                    units.append((d, hd, qkv_ref, r0, gate, gt,
                                  lambda s=s, idx=N_HEADS * d + hd: s_scr[s, idx]))
                    where.append((s, d, hd, r0))
        for (s, d, hd, r0), (o, s_new) in zip(where, _delta_units(units)):
            s_scr[s, N_HEADS * d + hd] = s_new
            (of_ref if d == 0 else ob_ref)[s, pl.ds(r0, c), hd * HEAD_DIM:(hd + 1) * HEAD_DIM] = o
        return carry

    lax.fori_loop(0, n_chunk, body, 0)

    if emit_final:
        @pl.when(j == pl.num_programs(1) - 1)
        def _():
            sfin_ref[...] = s_scr[...]


def _delta(qkv, gate, gate_t, n_seq, seq_len, row0, blk, s0):
    nblk = seq_len // blk
    n_chunk = blk // CHUNK
    b0 = row0 // blk
    has_s0 = s0 is not None
    emit_final = not has_s0
    n_state = 2 * N_HEADS
    one_block = nblk == 1
    assert n_seq % DELTA_SEQS == 0 and row0 % blk == 0

    in_specs, args = [], []
    for s in range(DELTA_SEQS):
        def fwd(b, j, s=s):
            return b0 + (b * DELTA_SEQS + s) * nblk + j

        def bwd(b, j, s=s):
            return b0 + (b * DELTA_SEQS + s) * nblk + (nblk - 1 - j)

        for arr, shape in ((qkv, (blk, 3 * MIX_A)), (gate, (blk, LANES)), (gate_t, (n_chunk, GATE_ROWS, CHUNK))):
            tail = (0,) * (len(shape) - 1)
            for f in (fwd,) if one_block else (fwd, bwd):
                in_specs.append(pl.BlockSpec(shape, lambda b, j, f=f, tail=tail: (f(b, j),) + tail))
                args.append(arr)
    state_spec = pl.BlockSpec((DELTA_SEQS, n_state, HEAD_DIM, HEAD_DIM), lambda b, j: (b, 0, 0, 0))
    if has_s0:
        in_specs.append(state_spec)
        args.append(s0)
    out_specs = [pl.BlockSpec((DELTA_SEQS, blk, MIX_A), lambda b, j: (b, j, 0)),
                 pl.BlockSpec((DELTA_SEQS, blk, MIX_A), lambda b, j: (b, nblk - 1 - j, 0))]
    out_shape = [jax.ShapeDtypeStruct((n_seq, seq_len, MIX_A), F32)] * 2
    if emit_final:
        out_specs.append(state_spec)
        out_shape.append(jax.ShapeDtypeStruct((n_seq, n_state, HEAD_DIM, HEAD_DIM), F32))
    outs = pl.pallas_call(
        functools.partial(_delta_kernel, has_s0, emit_final, one_block, n_chunk),
        grid=(n_seq // DELTA_SEQS, nblk),
        in_specs=in_specs,
        out_specs=out_specs,
        out_shape=out_shape,
        scratch_shapes=[pltpu.VMEM((DELTA_SEQS, n_state, HEAD_DIM, HEAD_DIM), F32)],
        compiler_params=_cparams(("parallel", "arbitrary")),
        name="delta_latent" if has_s0 else "delta_prompt",
    )(*args)
    return [o.reshape(n_seq * seq_len, MIX_A) for o in outs[:2]] + list(outs[2:])


def _outproj_kernel(lay, xp_ref, xs_ref, prow_ref, pcol_ref, mod_ref, opf_ref, opb_ref, osf_ref, osb_ref, z_ref,
                    u_ref, og_ref, n2_ref, wo_ref, wr_ref, br_ref, x1_ref, h2_ref, ids_ref, wts_ref, cnt_ref,
                    cnt_scr):
    i = pl.program_id(0)
    m = mod_ref[0]

    @pl.when(i == 0)
    def _():
        cnt_scr[...] = jnp.zeros(cnt_scr.shape, F32)

    halves = [slice(r0, r0 + PROJ_ROWS) for r0 in range(0, lay.tile, PROJ_ROWS)]
    n = PROJ_ROWS
    lane = lax.broadcasted_iota(I32, (n, LANES), 1)
    lane_f = lane.astype(F32)
    neg = jnp.float32(-jnp.inf)
    big = jnp.float32(LANES)

    def first_lane(mask):
        return jnp.min(jnp.where(mask, lane_f, big), axis=-1, keepdims=True)

    gated = []
    for rows in halves:
        o = jnp.where(lay.is_sample(i), osf_ref[rows, :] + osb_ref[rows, :], opf_ref[rows, :] + opb_ref[rows, :])
        z = z_ref[rows, :]
        gated.append([(_rms(o[:, sl]) * og_ref[...] * _silu(z[:, sl])).astype(BF16)
                      for sl in (slice(hd * HEAD_DIM, (hd + 1) * HEAD_DIM) for hd in range(N_HEADS))])
    mixes = []
    for rows, heads in zip(halves, gated):
        mix = _dot(u_ref[rows, :], wo_ref[MIX_A:, :])
        for hd, oh in enumerate(heads):
            mix = mix + _dot(oh, wo_ref[hd * HEAD_DIM:(hd + 1) * HEAD_DIM, :])
        mixes.append(mix)
    h2bs = []
    for rows, mix in zip(halves, mixes):
        x1 = _load_x(lay, i, xp_ref, xs_ref, prow_ref, pcol_ref, rows) + m[2:3] * mix
        x1_ref[rows, :] = x1
        h2b = (_rms(x1) * n2_ref[...] * (1.0 + m[4:5]) + m[3:4]).astype(BF16)
        h2_ref[rows, :] = _pack_bf16_pairs(h2b)
        h2bs.append(h2b)
    logits_all = [_dot(h2b, wr_ref[...]) + br_ref[...] for h2b in h2bs]

    picks = []
    for rows, logits in zip(halves, logits_all):
        gl = jnp.where(lane < N_GROUPS, logits, neg)
        gmax = jnp.max(gl, axis=-1, keepdims=True)
        grp = first_lane(gl == gmax)
        p_grp = 1.0 / jnp.sum(jnp.where(lane < N_GROUPS, jnp.exp(gl - gmax), 0.0), axis=-1, keepdims=True)
        e_lane = lane_f - N_GROUPS
        in_grp = (e_lane >= grp * EXPERTS_PER_GROUP) & (e_lane < (grp + 1.0) * EXPERTS_PER_GROUP)
        el = jnp.where(in_grp, logits, neg)
        m1 = jnp.max(el, axis=-1, keepdims=True)
        i1f = first_lane(el == m1)
        el2 = jnp.where(lane_f == i1f, neg, el)
        m2 = jnp.max(el2, axis=-1, keepdims=True)
        i2f = first_lane(el2 == m2)
        e2 = jnp.exp(m2 - m1)
        w1 = p_grp / (1.0 + e2)
        w2 = p_grp * e2 / (1.0 + e2)
        col = lax.broadcasted_iota(I32, (n, ROUTE_COLS), 1)
        wts_ref[rows, :] = jnp.where(col == 0, w1, jnp.where(col == 1, w2, 0.0))
        picks.append((i1f, i2f))

    r_i = lax.broadcasted_iota(I32, (n, n), 0)
    c_i = lax.broadcasted_iota(I32, (n, n), 1)
    before = jnp.where(r_i > c_i, 1.0, 0.0).astype(BF16)
    onehots = [(lane_f == i1f, lane_f == i2f) for i1f, i2f in picks]
    chosen = [jnp.where(oh1, 1.0, jnp.where(oh2, 1.0, 0.0)) for oh1, oh2 in onehots]
    prefix = [_dot(before, oh.astype(BF16)) for oh in chosen]
    counts = cnt_scr[...]
    for rows, (i1f, i2f), (oh1, oh2), oh, pre in zip(halves, picks, onehots, chosen, prefix):
        seen = counts + pre
        r1 = jnp.sum(jnp.where(oh1, seen, 0.0), axis=-1, keepdims=True)
        r2 = jnp.sum(jnp.where(oh2, seen, 0.0), axis=-1, keepdims=True)
        ids = jnp.where(lane == 0, i1f - N_GROUPS,
                        jnp.where(lane == 1, i2f - N_GROUPS,
                                  jnp.where(lane == 2, r1, jnp.where(lane == 3, r2, 0.0))))
        ids_ref[:, rows] = jnp.transpose(ids)[:ROUTE_COLS, :].astype(I32)
        counts = counts + jnp.sum(oh, axis=0, keepdims=True)
    cnt_scr[...] = counts
    cnt_ref[...] = counts


def _outproj(lay, xp, xs, pos, mod, o_pf, o_pb, o_sf, o_sb, z, u, onorm_g, norm2_g, w_out, w_route, b_route):
    t = lay.t
    p_spec = pl.BlockSpec((lay.tile, MIX_A), lambda i: (lay.xp_blk(i), 0))
    s_spec = pl.BlockSpec((lay.tile, MIX_A), lambda i: (lay.xs_blk(i), 0))
    return pl.pallas_call(
        functools.partial(_outproj_kernel, lay),
        grid=(lay.tiles,),
        in_specs=_x_specs(lay) + [_mod_spec(lay), p_spec, p_spec, s_spec, s_spec, _row_spec(lay, MIX_A),
                                  _row_spec(lay, MIX_B), _full_spec((1, HEAD_DIM)), _full_spec((1, D_MODEL)),
                                  _full_spec(w_out.shape), _full_spec(w_route.shape),
                                  _full_spec(b_route.shape)],
        out_specs=[_row_spec(lay, D_MODEL), _row_spec(lay, D_MODEL // 2),
                   pl.BlockSpec((ROUTE_COLS, lay.tile), lambda i: (0, i)),
                   _row_spec(lay, ROUTE_COLS), _full_spec((1, LANES))],
        out_shape=[jax.ShapeDtypeStruct((t, D_MODEL), F32),
                   jax.ShapeDtypeStruct((t, D_MODEL // 2), I32),
                   jax.ShapeDtypeStruct((ROUTE_COLS, t), I32),
                   jax.ShapeDtypeStruct((t, ROUTE_COLS), F32),
                   jax.ShapeDtypeStruct((1, LANES), F32)],
        scratch_shapes=[pltpu.VMEM((1, LANES), F32)],
        compiler_params=_cparams(("arbitrary",)),
        name="outproj",
    )(xp, xs, *pos, mod, o_pf, o_pb, o_sf, o_sb, z, u, onorm_g, norm2_g, w_out, w_route, b_route)


def _sc_gather(table, idx, chunk, name):
    n_rows, width = idx.shape[0], table.shape[1]
    mesh = plsc.VectorSubcoreMesh(core_axis_name="c", subcore_axis_name="s")
    n_workers = mesh.num_cores * mesh.num_subcores
    per_worker = n_rows // n_workers
    n_chunks = per_worker // chunk
    assert n_rows == n_workers * n_chunks * chunk and n_chunks % 2 == 0 and chunk % 8 == 0 and chunk <= LANES

    def body(table_hbm, idx_hbm, out_hbm, idx_v, rows_v, sem):
        base = (lax.axis_index("s") * mesh.num_cores + lax.axis_index("c")) * per_worker

        def gather(slot):
            return pltpu.make_async_copy(table_hbm.at[idx_v.at[slot]], rows_v.at[slot], sem.at[slot])

        def fetch(g, slot):
            off = pl.multiple_of(base + g * chunk, 8)
            pltpu.sync_copy(idx_hbm.at[pl.ds(off, chunk)], idx_v.at[slot])
            gather(slot).start()

        for slot in range(2):
            fetch(slot, slot)

        @pl.loop(0, n_chunks, step=2)
        def _(g):
            for slot in range(2):
                off = pl.multiple_of(base + (g + slot) * chunk, 8)
                gather(slot).wait()
                pltpu.sync_copy(rows_v.at[slot], out_hbm.at[pl.ds(off, chunk)])

                @pl.when(g + slot + 2 < n_chunks)
                def _():
                    fetch(g + slot + 2, slot)

    return pl.kernel(
        body,
        out_type=jax.ShapeDtypeStruct((n_rows, width), table.dtype),
        mesh=mesh,
        scratch_types=[pltpu.VMEM((2, chunk), I32), pltpu.VMEM((2, chunk, width), table.dtype),
                       pltpu.SemaphoreType.DMA((2,))],
        name=name,
    )(table, idx)


def _sc_scatter2(src, idx0, idx1, n_out, chunk, name):
    n_rows, width = src.shape
    mesh = plsc.VectorSubcoreMesh(core_axis_name="c", subcore_axis_name="s")
    n_workers = mesh.num_cores * mesh.num_subcores
    per_worker = n_rows // n_workers
    n_chunks = per_worker // chunk
    assert n_rows == n_workers * n_chunks * chunk and n_chunks % 2 == 0 and chunk % 8 == 0 and chunk <= LANES

    def body(src_hbm, i0_hbm, i1_hbm, out_hbm, i0_v, i1_v, rows_v, sem_in, sem_out):
        base = (lax.axis_index("s") * mesh.num_cores + lax.axis_index("c")) * per_worker

        def rows_in(g, slot):
            off = pl.multiple_of(base + g * chunk, 8)
            return pltpu.make_async_copy(src_hbm.at[pl.ds(off, chunk)], rows_v.at[slot], sem_in.at[slot])

        def fetch(g, slot):
            off = pl.multiple_of(base + g * chunk, 8)
            pltpu.sync_copy(i0_hbm.at[pl.ds(off, chunk)], i0_v.at[slot])
            pltpu.sync_copy(i1_hbm.at[pl.ds(off, chunk)], i1_v.at[slot])
            rows_in(g, slot).start()

        for slot in range(2):
            fetch(slot, slot)

        @pl.loop(0, n_chunks, step=2)
        def _(g):
            for slot in range(2):
                rows_in(g + slot, slot).wait()
                puts = [pltpu.make_async_copy(rows_v.at[slot], out_hbm.at[iv.at[slot]], sem_out.at[slot])
                        for iv in (i0_v, i1_v)]
                for put in puts:
                    put.start()
                for put in puts:
                    put.wait()

                @pl.when(g + slot + 2 < n_chunks)
                def _():
                    fetch(g + slot + 2, slot)

    return pl.kernel(
        body,
        out_type=jax.ShapeDtypeStruct((n_out, width), src.dtype),
        mesh=mesh,
        scratch_types=[pltpu.VMEM((2, chunk), I32), pltpu.VMEM((2, chunk), I32),
                       pltpu.VMEM((2, chunk, width), src.dtype),
                       pltpu.SemaphoreType.DMA((2,)), pltpu.SemaphoreType.DMA((2,))],
        name=name,
    )(src, idx0, idx1)


def _expert_kernel(blk_e_ref, nused_ref, xb_ref, wg_ref, wu_ref, wd_ref, yb_ref, wg_s, wu_s, wd_s):
    i = pl.program_id(0)
    nused = nused_ref[0]

    @pl.when(i < nused)
    def _():
        changed = (i == 0) | (blk_e_ref[i] != blk_e_ref[jnp.maximum(i - 1, 0)])

        @pl.when(changed)
        def _():
            wg_s[...] = wg_ref[0].astype(BF16)
            wu_s[...] = wu_ref[0].astype(BF16)
            wd_s[...] = wd_ref[0].astype(BF16)

        half = D_MODEL // 2
        xa, xb = (v.astype(BF16) for v in _unpack_bf16_pairs(xb_ref[...]))
        g = _dot(xa, wg_s[:half, :]) + _dot(xb, wg_s[half:, :])
        u = _dot(xa, wu_s[:half, :]) + _dot(xb, wu_s[half:, :])
        hmid = (_silu(g) * u).astype(BF16)
        yb_ref[...] = _pack_bf16_pairs(_dot(hmid, wd_s[...]).astype(BF16))

    @pl.when(i >= nused)
    def _():
        yb_ref[...] = jnp.zeros(yb_ref.shape, I32)


def _experts(xb, blk_e, nused, w_gate, w_up, w_down):
    nb = blk_e.shape[0]

    def weight_spec(shape):
        return pl.BlockSpec((1,) + shape, lambda i, be, nu: (be[i], 0, 0))

    grid_spec = pltpu.PrefetchScalarGridSpec(
        num_scalar_prefetch=2,
        grid=(nb,),
        in_specs=[
            pl.BlockSpec((ROUTE_BM, D_MODEL // 2), lambda i, be, nu: (i, 0)),
            weight_spec((D_MODEL, D_EXPERT)),
            weight_spec((D_MODEL, D_EXPERT)),
            weight_spec((D_EXPERT, D_MODEL)),
        ],
        out_specs=pl.BlockSpec((ROUTE_BM, D_MODEL // 2), lambda i, be, nu: (i, 0)),
        scratch_shapes=[pltpu.VMEM((D_MODEL, D_EXPERT), BF16),
                        pltpu.VMEM((D_MODEL, D_EXPERT), BF16),
                        pltpu.VMEM((D_EXPERT, D_MODEL), BF16)],
    )
    return pl.pallas_call(
        _expert_kernel,
        grid_spec=grid_spec,
        out_shape=jax.ShapeDtypeStruct((nb * ROUTE_BM, D_MODEL // 2), I32),
        compiler_params=_cparams(("arbitrary",)),
        name="expert",
    )(blk_e, nused, xb, w_gate, w_up, w_down)


def _combine_kernel(lay, y0_ref, y1_ref, x1_ref, wts_ref, mod_ref, fg_ref, yp_ref, ys_ref):
    i = pl.program_id(0)
    m = mod_ref[0]
    w = wts_ref[...]
    a0, b0 = _unpack_bf16_pairs(y0_ref[...])
    a1, b1 = _unpack_bf16_pairs(y1_ref[...])
    ff = jnp.concatenate([a0 * w[:, 0:1] + a1 * w[:, 1:2], b0 * w[:, 0:1] + b1 * w[:, 1:2]], axis=1)
    y = _rms(x1_ref[...] + m[5:6] * ff) * fg_ref[...]

    @pl.when(i < lay.tiles_p)
    def _():
        yp_ref[...] = y

    @pl.when(i >= lay.tiles_p)
    def _():
        ys_ref[...] = y


def _combine(lay, yg, x1, wts, mod, final_g):
    return pl.pallas_call(
        functools.partial(_combine_kernel, lay),
        grid=(lay.tiles,),
        in_specs=[_row_spec(lay, D_MODEL // 2),
                  pl.BlockSpec((lay.tile, D_MODEL // 2), lambda i: (i + lay.tiles, 0)),
                  _row_spec(lay, D_MODEL), _row_spec(lay, ROUTE_COLS), _mod_spec(lay), _full_spec((1, D_MODEL))],
        out_specs=[pl.BlockSpec((lay.tile, D_MODEL), lambda i: (lay.xp_blk(i), 0)),
                   pl.BlockSpec((lay.tile, D_MODEL), lambda i: (lay.xs_blk(i), 0))],
        out_shape=[jax.ShapeDtypeStruct((lay.t_p, D_MODEL), F32),
                   jax.ShapeDtypeStruct((lay.t_s, D_MODEL), F32)],
        compiler_params=_cparams(("arbitrary",)),
        name="combine",
    )(yg, yg, x1, wts, mod, final_g)


def _dispatch_plan(ids, counts):
    n_tok = ids.shape[1]
    padded = (counts + ROUTE_BM - 1) // ROUTE_BM * ROUTE_BM
    pad_end = jnp.cumsum(padded)
    pad_start = pad_end - padded
    experts = jnp.arange(N_EXPERTS, dtype=I32)[:, None]
    dest = [(jnp.sum(jnp.where(ids[k][None, :] == experts, pad_start[:, None], 0), axis=0) + ids[2 + k]).astype(I32)
            for k in range(2)]
    nb = -(-(2 * n_tok + N_EXPERTS * (ROUTE_BM - 1)) // ROUTE_BM)
    block_start = jnp.arange(nb, dtype=I32) * ROUTE_BM
    blk_e = jnp.minimum(jnp.sum(pad_end[None, :] <= block_start[:, None], axis=1), N_EXPERTS - 1).astype(I32)
    nused = (pad_end[-1:] // ROUTE_BM).astype(I32)
    return dest[0], dest[1], blk_e, nused


def _grid_pos_tables(n_tokens):
    rows = n_tokens // GRID_W
    n_freq = D_MODEL // 4
    freq = jnp.exp(jnp.arange(n_freq, dtype=F32) * (-math.log(10000.0) / n_freq))

    def enc(p):
        ang = p[:, None] * freq[None, :]
        return jnp.concatenate([jnp.sin(ang), jnp.cos(ang)], axis=-1)

    return enc(jnp.arange(rows, dtype=F32)), enc(jnp.arange(GRID_W, dtype=F32))


def _lane_pad(v, offset):
    return jnp.zeros((1, LANES), F32).at[0, offset:offset + v.shape[0]].set(v.astype(F32))


def kernel(x_prompt, x_sample, state_delta, c, c_ctx, norm1_g, w_mod, b_mod, w_in, conv_qkv_w, A_log, dt_bias, onorm_g, dw_w, dw_b, cln_g, cln_b, w_out, norm2_g, w_group, b_group, w_expert, b_expert, w_e_gate, w_e_up, w_e_down, final_g):
    n_p, l_p, _ = x_prompt.shape
    n_s, l_s, _ = x_sample.shape
    lay = _Layout(n_p, l_p, n_s, l_s, TOK_TILE)
    lay_proj = _Layout(n_p, l_p, n_s, l_s, PROJ_TILE)
    depth = w_in.shape[0]
    assert depth == 1
    xp = x_prompt.reshape(lay.t_p, D_MODEL)
    xs = x_sample.reshape(lay.t_s, D_MODEL)
    pos = _grid_pos_tables(l_s)

    cond = jnp.concatenate([c_ctx[None, :], c], axis=0)
    cond = jnp.pad(cond, ((0, (-cond.shape[0]) % SUBLANES), (0, 0)))
    mod = _modulation(cond, w_mod[0], b_mod[0]).reshape(cond.shape[0], 6, D_MODEL)

    n_gate = 4 * N_HEADS
    wi = w_in[0]
    w_main = jnp.concatenate([wi[:, :4 * MIX_A], wi[:, 4 * MIX_A + n_gate:]], axis=1).astype(BF16)
    w_small = jnp.pad(wi[:, 4 * MIX_A:4 * MIX_A + n_gate], ((0, 0), (0, LANES - n_gate))).astype(BF16)
    qkv_raw, z, ug, ba = _inproj(lay_proj, xp, xs, pos, mod, norm1_g[0][None, :], w_main, w_small)

    cw = jnp.pad(conv_qkv_w[0], ((0, 8 - SHORT_CONV), (0, 0))).reshape(8, Q_GROUPS, LANES).transpose(1, 0, 2)
    dw = jnp.pad(dw_w[0], ((0, 32 - CONV_W), (0, 0))).reshape(32, U_GROUPS, LANES).transpose(1, 0, 2)
    alog = _lane_pad(A_log[0].reshape(-1), 2 * N_HEADS)
    dtb = _lane_pad(dt_bias[0].reshape(-1), 2 * N_HEADS)
    qkv, u_conf, gate, gate_t = _conv(lay, qkv_raw, ug, ba, cw, dw, dw_b[0][None, :], cln_g[0][None, :],
                                      cln_b[0][None, :], alog, dtb)

    o_pf, o_pb, s_fin = _delta(qkv, gate, gate_t, n_p, l_p, 0, l_p, None)
    s0 = state_delta[:, 0].reshape(n_s, 2 * N_HEADS, HEAD_DIM, HEAD_DIM)
    o_sf, o_sb = _delta(qkv, gate, gate_t, n_s, l_s, lay.t_p, DELTA_BLK, s0)

    w_route = jnp.pad(jnp.concatenate([w_group[0], w_expert[0]], axis=1),
                      ((0, 0), (0, LANES - N_GROUPS - N_EXPERTS))).astype(BF16)
    b_route = _lane_pad(jnp.concatenate([b_group[0], b_expert[0]]), 0)
    x1, h2p, ids, wts, cnt = _outproj(lay_proj, xp, xs, pos, mod, o_pf, o_pb, o_sf, o_sb, z, u_conf,
                                      onorm_g[0][None, :], norm2_g[0][None, :], w_out[0].astype(BF16), w_route,
                                      b_route)

    counts = cnt[0, N_GROUPS:N_GROUPS + N_EXPERTS].astype(I32)
    dest0, dest1, blk_e, nused = _dispatch_plan(ids, counts)
    xb = _sc_scatter2(h2p, dest0, dest1, blk_e.shape[0] * ROUTE_BM, DISPATCH_CHUNK, "dispatch_scatter")
    yb = _experts(xb, blk_e, nused, w_e_gate[0], w_e_up[0], w_e_down[0])
    yg = _sc_gather(yb, jnp.concatenate([dest0, dest1]), COMBINE_CHUNK, "combine_gather")
    y_p, y_s = _combine(_Layout(n_p, l_p, n_s, l_s, COMBINE_TILE), yg, x1, wts, mod, final_g[None, :])

    new_state = s_fin.reshape(n_p, 1, 2, N_HEADS, HEAD_DIM, HEAD_DIM)
    return (y_p.reshape(x_prompt.shape), y_s.reshape(x_sample.shape), new_state)
```

```python
import functools
import math

import jax
import jax.numpy as jnp
from jax import lax
from jax.experimental import pallas as pl
from jax.experimental.pallas import tpu as pltpu
from jax.experimental.pallas import tpu_sc as plsc

F32 = jnp.float32
BF16 = jnp.bfloat16
I32 = jnp.int32

D_MODEL = 1024
MIX_A = 512
MIX_B = 512
HEAD_DIM = 128
N_HEADS = 4
SHORT_CONV = 5
CONV_W = 31
N_GROUPS = 4
EXPERTS_PER_GROUP = 8
N_EXPERTS = 32
D_EXPERT = 512
GRID_W = 64
EPS = 1e-6

LANES = 128
SUBLANES = 8
TOK_TILE = 256
PROJ_TILE = 512
PROJ_ROWS = 128
CHUNK = 128
DELTA_BLK = 512
DELTA_SEQS = 2
ROUTE_BM = 512
ROUTE_COLS = 8
GATE_ROWS = 16
COMBINE_TILE = 1024
DISPATCH_CHUNK = 64
COMBINE_CHUNK = 64
Q_HALO = 8
U_HALO = 16
VMEM_LIMIT = 56 * 1024 * 1024

HIGHEST = lax.Precision.HIGHEST


def _cparams(sem):
    return pltpu.CompilerParams(dimension_semantics=sem, vmem_limit_bytes=VMEM_LIMIT)


def _sigmoid(x):
    return 0.5 * jnp.tanh(0.5 * x) + 0.5


def _silu(x):
    return x * _sigmoid(x)


def _softplus(x):
    return jnp.maximum(x, 0.0) + jnp.log1p(jnp.exp(-jnp.abs(x)))


def _rms(x):
    return x * lax.rsqrt(jnp.mean(x * x, axis=-1, keepdims=True) + EPS)


def _pack_bf16_pairs(x):
    h = x.shape[1] // 2
    hi = pltpu.bitcast(x[:, :h].astype(F32), jnp.uint32)
    lo = pltpu.bitcast(x[:, h:].astype(F32), jnp.uint32)
    return pltpu.bitcast((hi & jnp.uint32(0xFFFF0000)) | (lo >> 16), I32)


def _unpack_bf16_pairs(p):
    u = pltpu.bitcast(p, jnp.uint32)
    return pltpu.bitcast(u & jnp.uint32(0xFFFF0000), F32), pltpu.bitcast(u << 16, F32)


def _dot(a, b):
    return jnp.dot(a, b, preferred_element_type=F32)


def _dot_nt(a, b):
    return lax.dot_general(a, b, (((1,), (1,)), ((), ())), preferred_element_type=F32)


def _dot_tn(a, b):
    return lax.dot_general(a, b, (((0,), (0,)), ((), ())), preferred_element_type=F32)


def _mod_kernel(cond_ref, w_ref, b_ref, o_ref):
    s = _silu(cond_ref[...])
    o_ref[...] = jnp.dot(s, w_ref[...], preferred_element_type=F32, precision=HIGHEST) + b_ref[...]


def _modulation(cond, w_mod, b_mod):
    n = cond.shape[0]
    tn = D_MODEL
    return pl.pallas_call(
        _mod_kernel,
        grid=(6 * D_MODEL // tn,),
        in_specs=[pl.BlockSpec((n, D_MODEL), lambda j: (0, 0)),
                  pl.BlockSpec((D_MODEL, tn), lambda j: (0, j)),
                  pl.BlockSpec((1, tn), lambda j: (0, j))],
        out_specs=pl.BlockSpec((n, tn), lambda j: (0, j)),
        out_shape=jax.ShapeDtypeStruct((n, 6 * D_MODEL), F32),
        compiler_params=_cparams(("parallel",)),
        name="mod",
    )(cond, w_mod, b_mod.reshape(1, -1))


class _Layout:
    def __init__(self, n_p, l_p, n_s, l_s, tile):
        self.n_p, self.l_p, self.n_s, self.l_s, self.tile = n_p, l_p, n_s, l_s, tile
        self.t_p = n_p * l_p
        self.t_s = n_s * l_s
        self.t = self.t_p + self.t_s
        assert self.t_p % tile == 0 and l_s % tile == 0 and l_s % DELTA_BLK == 0
        self.tiles_p = self.t_p // tile
        self.tiles_s = self.t_s // tile
        self.tiles = self.tiles_p + self.tiles_s
        self.tps_p = max(l_p // tile, 1)
        self.tps_s = l_s // tile

    def is_sample(self, i):
        return i >= self.tiles_p

    def mod_row(self, i):
        return jnp.where(i < self.tiles_p, 0, 1 + (i - self.tiles_p) // self.tps_s)

    def pos_blk(self, i):
        return jnp.where(i < self.tiles_p, 0, (i - self.tiles_p) % self.tps_s)

    def xp_blk(self, i):
        return jnp.minimum(i, self.tiles_p - 1)

    def xs_blk(self, i):
        return jnp.maximum(i - self.tiles_p, 0)

    def seq_pos(self, i):
        in_s = i >= self.tiles_p
        pos = jnp.where(in_s, (i - self.tiles_p) % self.tps_s, i % self.tps_p)
        n = jnp.where(in_s, self.tps_s, self.tps_p)
        return pos, n


def _load_x(lay, i, xp_ref, xs_ref, prow_ref, pcol_ref, rows):
    g0, g1 = rows.start // GRID_W, rows.stop // GRID_W
    half = D_MODEL // 2
    prow = jnp.concatenate([jnp.broadcast_to(prow_ref[r:r + 1, :], (GRID_W, half)) for r in range(g0, g1)],
                           axis=0)
    pcol = jnp.concatenate([pcol_ref[...]] * (g1 - g0), axis=0)
    pos = jnp.concatenate([prow, pcol], axis=1)
    return jnp.where(lay.is_sample(i), xs_ref[rows, :] + pos, xp_ref[rows, :])


def _x_specs(lay):
    assert lay.tile % (SUBLANES * GRID_W) == 0
    return [pl.BlockSpec((lay.tile, D_MODEL), lambda i: (lay.xp_blk(i), 0)),
            pl.BlockSpec((lay.tile, D_MODEL), lambda i: (lay.xs_blk(i), 0)),
            pl.BlockSpec((lay.tile // GRID_W, D_MODEL // 2), lambda i: (lay.pos_blk(i), 0)),
            _full_spec((GRID_W, D_MODEL // 2))]


def _mod_spec(lay):
    return pl.BlockSpec((1, 6, D_MODEL), lambda i: (lay.mod_row(i), 0, 0))


def _row_spec(lay, width):
    return pl.BlockSpec((lay.tile, width), lambda i: (i, 0))


def _full_spec(shape):
    nd = len(shape)
    return pl.BlockSpec(shape, lambda i: (0,) * nd)


def _inproj_kernel(lay, xp_ref, xs_ref, prow_ref, pcol_ref, mod_ref, g_ref, wm_ref, ws_ref,
                   qkv_ref, z_ref, ug_ref, ba_ref):
    i = pl.program_id(0)
    m = mod_ref[0]
    x = _load_x(lay, i, xp_ref, xs_ref, prow_ref, pcol_ref, slice(0, lay.tile))
    h = _rms(x) * g_ref[...] * (1.0 + m[1:2]) + m[0:1]
    hb = h.astype(BF16)
    qkv_ref[...] = _dot(hb, wm_ref[:, 0:3 * MIX_A])
    z_ref[...] = _dot(hb, wm_ref[:, 3 * MIX_A:4 * MIX_A])
    glu = _dot(hb, wm_ref[:, 4 * MIX_A:4 * MIX_A + 2 * MIX_B])
    ug_ref[...] = glu[:, :MIX_B] * _sigmoid(glu[:, MIX_B:])
    ba_ref[...] = _dot(hb, ws_ref[...])


def _inproj(lay, xp, xs, pos, mod, norm_g, w_main, w_small):
    t = lay.t
    return pl.pallas_call(
        functools.partial(_inproj_kernel, lay),
        grid=(lay.tiles,),
        in_specs=_x_specs(lay) + [_mod_spec(lay), _full_spec((1, D_MODEL)),
                                  _full_spec(w_main.shape), _full_spec(w_small.shape)],
        out_specs=[_row_spec(lay, 3 * MIX_A), _row_spec(lay, MIX_A), _row_spec(lay, MIX_B),
                   _row_spec(lay, LANES)],
        out_shape=[jax.ShapeDtypeStruct((t, 3 * MIX_A), F32),
                   jax.ShapeDtypeStruct((t, MIX_A), F32),
                   jax.ShapeDtypeStruct((t, MIX_B), F32),
                   jax.ShapeDtypeStruct((t, LANES), F32)],
        compiler_params=_cparams(("parallel",)),
        name="inproj",
    )(xp, xs, *pos, mod, norm_g, w_main, w_small)


CONV_PITCH = 33
CONV_OUT_ROWS = SUBLANES * CONV_PITCH
POST_ROWS = 64
Q_EXT_ROWS = 280
U_EXT_ROWS = 296
Q_GROUPS = 3 * MIX_A // LANES
U_GROUPS = MIX_B // LANES
CONV_J_BLOCK = 11


def _strided_conv(ext_scr, res_scr, w_ref, g, off, n_taps):
    for j0 in range(0, CONV_PITCH, CONV_J_BLOCK):
        js = range(j0, min(j0 + CONV_J_BLOCK, CONV_PITCH))
        v = {m: ext_scr[g, pl.ds(off + m, SUBLANES, stride=CONV_PITCH), :]
             for m in range(js[0], js[-1] + n_taps)}
        for j in js:
            acc = v[j] * w_ref[g, 0:1, :]
            for s in range(1, n_taps):
                acc = acc + v[j + s] * w_ref[g, s:s + 1, :]
            res_scr[g, pl.ds(j, SUBLANES, stride=CONV_PITCH), :] = acc


def _conv_kernel(lay, qc_ref, qp_ref, qn_ref, uc_ref, up_ref, un_ref, ba_ref, cw_ref, dw_ref,
                 dwb_ref, lng_ref, lnb_ref, alog_ref, dtb_ref,
                 qkv_ref, uo_ref, gate_ref, gt_ref, eq_scr, eu_scr, rq_scr, ru_scr):
    i = pl.program_id(0)
    pos, n = lay.seq_pos(i)
    has_prev = pos != 0
    has_next = pos != n - 1
    qp = jnp.where(has_prev, qp_ref[...], 0.0)
    qn = jnp.where(has_next, qn_ref[...], 0.0)
    for g in range(Q_GROUPS):
        sl = slice(g * LANES, (g + 1) * LANES)
        eq_scr[g, 0:Q_HALO, :] = qp[:, sl]
        eq_scr[g, Q_HALO:Q_HALO + TOK_TILE, :] = qc_ref[:, sl]
        eq_scr[g, Q_HALO + TOK_TILE:Q_HALO + TOK_TILE + Q_HALO, :] = qn[:, sl]
        eq_scr[g, TOK_TILE + 2 * Q_HALO:, :] = jnp.zeros((Q_EXT_ROWS - TOK_TILE - 2 * Q_HALO, LANES), F32)
    up = jnp.where(has_prev, up_ref[...], 0.0)
    un_ = jnp.where(has_next, un_ref[...], 0.0)
    for g in range(U_GROUPS):
        sl = slice(g * LANES, (g + 1) * LANES)
        eu_scr[g, 0:U_HALO, :] = up[:, sl]
        eu_scr[g, U_HALO:U_HALO + TOK_TILE, :] = uc_ref[:, sl]
        eu_scr[g, U_HALO + TOK_TILE:U_HALO + TOK_TILE + U_HALO, :] = un_[:, sl]
        eu_scr[g, TOK_TILE + 2 * U_HALO:, :] = jnp.zeros((U_EXT_ROWS - TOK_TILE - 2 * U_HALO, LANES), F32)

    def q_group(g, carry):
        _strided_conv(eq_scr, rq_scr, cw_ref, g, Q_HALO - SHORT_CONV // 2, SHORT_CONV)
        return carry

    def u_group(g, carry):
        _strided_conv(eu_scr, ru_scr, dw_ref, g, U_HALO - CONV_W // 2, CONV_W)
        return carry

    lax.fori_loop(0, Q_GROUPS, q_group, 0)
    lax.fori_loop(0, U_GROUPS, u_group, 0)

    for rc in range(TOK_TILE // POST_ROWS):
        r0 = rc * POST_ROWS
        for g in range(Q_GROUPS):
            y = _silu(rq_scr[g, r0:r0 + POST_ROWS, :])
            if g < 2 * N_HEADS:
                y = y * lax.rsqrt(jnp.sum(y * y, axis=-1, keepdims=True) + EPS)
            qkv_ref[r0:r0 + POST_ROWS, g * LANES:(g + 1) * LANES] = y
        u = jnp.concatenate([ru_scr[g, r0:r0 + POST_ROWS, :] for g in range(U_GROUPS)], axis=1) + dwb_ref[...]
        uc = u - jnp.mean(u, axis=-1, keepdims=True)
        un = uc * lax.rsqrt(jnp.mean(uc * uc, axis=-1, keepdims=True) + EPS)
        uo_ref[r0:r0 + POST_ROWS, :] = _silu(un * lng_ref[...] + lnb_ref[...]).astype(BF16)

    x = ba_ref[...]
    g = -jnp.exp(alog_ref[...]) * _softplus(x + dtb_ref[...])
    beta = _sigmoid(x)
    rows = lax.broadcasted_iota(I32, (CHUNK, CHUNK), 0)
    cols = lax.broadcasted_iota(I32, (CHUNK, CHUNK), 1)
    tri_lo = (rows >= cols).astype(F32)
    tri_up = (rows <= cols).astype(F32)
    lane = lax.broadcasted_iota(I32, (CHUNK, LANES), 1)
    for ch in range(TOK_TILE // CHUNK):
        sl = slice(ch * CHUNK, (ch + 1) * CHUNK)
        pre = jnp.dot(tri_lo, g[sl], preferred_element_type=F32, precision=HIGHEST)
        suf = jnp.dot(tri_up, g[sl], preferred_element_type=F32, precision=HIGHEST)
        gate = jnp.where(lane < 2 * N_HEADS, beta[sl], jnp.where(lane < 3 * N_HEADS, pre, suf))
        gate_ref[sl, :] = gate
        gt_ref[ch] = jnp.transpose(gate)[:GATE_ROWS, :]


def _conv(lay, qkv_raw, ug, ba, cw, dw, dwb, lng, lnb, alog, dtb):
    t = lay.t
    qh = TOK_TILE // Q_HALO
    uh = TOK_TILE // U_HALO
    n_qh = t // Q_HALO
    n_uh = t // U_HALO
    in_specs = [
        _row_spec(lay, 3 * MIX_A),
        pl.BlockSpec((Q_HALO, 3 * MIX_A), lambda i: (jnp.maximum(i * qh - 1, 0), 0)),
        pl.BlockSpec((Q_HALO, 3 * MIX_A), lambda i: (jnp.minimum((i + 1) * qh, n_qh - 1), 0)),
        _row_spec(lay, MIX_B),
        pl.BlockSpec((U_HALO, MIX_B), lambda i: (jnp.maximum(i * uh - 1, 0), 0)),
        pl.BlockSpec((U_HALO, MIX_B), lambda i: (jnp.minimum((i + 1) * uh, n_uh - 1), 0)),
        _row_spec(lay, LANES),
        _full_spec(cw.shape), _full_spec(dw.shape), _full_spec(dwb.shape),
        _full_spec(lng.shape), _full_spec(lnb.shape), _full_spec(alog.shape), _full_spec(dtb.shape),
    ]
    return pl.pallas_call(
        functools.partial(_conv_kernel, lay),
        grid=(lay.tiles,),
        in_specs=in_specs,
        out_specs=[_row_spec(lay, 3 * MIX_A), _row_spec(lay, MIX_B), _row_spec(lay, LANES),
                   pl.BlockSpec((TOK_TILE // CHUNK, GATE_ROWS, CHUNK), lambda i: (i, 0, 0))],
        out_shape=[jax.ShapeDtypeStruct((t, 3 * MIX_A), F32),
                   jax.ShapeDtypeStruct((t, MIX_B), BF16),
                   jax.ShapeDtypeStruct((t, LANES), F32),
                   jax.ShapeDtypeStruct((t // CHUNK, GATE_ROWS, CHUNK), F32)],
        scratch_shapes=[pltpu.VMEM((Q_GROUPS, Q_EXT_ROWS, LANES), F32),
                        pltpu.VMEM((U_GROUPS, U_EXT_ROWS, LANES), F32),
                        pltpu.VMEM((Q_GROUPS, CONV_OUT_ROWS, LANES), F32),
                        pltpu.VMEM((U_GROUPS, CONV_OUT_ROWS, LANES), F32)],
        compiler_params=_cparams(("parallel",)),
        name="conv",
    )(qkv_raw, qkv_raw, qkv_raw, ug, ug, ug, ba, cw, dw, dwb, lng, lnb, alog, dtb)


INV_BASE = 8


def _b16(xs):
    return [x.astype(BF16) for x in xs]


def _tri_inverse_minus_eye(nmats, rows, cols):
    assert INV_BASE == 8
    c = nmats[0].shape[0]
    shift = int(math.log2(INV_BASE))
    same = (rows >> shift) == (cols >> shift)
    n1 = [jnp.where(same, n, 0.0) for n in nmats]
    n1b = _b16(n1)
    n2 = [_dot(x, x) for x in n1b]
    n2b = _b16(n2)
    r = [_dot(jnp.concatenate([a, b], axis=0), b) for a, b in zip(n1b, n2b)]
    q = [a + b + x[:c] for a, b, x in zip(n1, n2, r)]
    n4 = [x[c:] for x in r]
    qn4 = [_dot(a, b) for a, b in zip(_b16(q), _b16(n4))]
    q = [a + b + x for a, b, x in zip(q, n4, qn4)]
    while (1 << shift) < c:
        off = ((rows >> (shift + 1)) == (cols >> (shift + 1))) & ((rows >> shift) != (cols >> shift))
        a = [jnp.where(off, -n, 0.0) for n in nmats]
        x = [ai + _dot(ab, qb) for ai, ab, qb in zip(a, _b16(a), _b16(q))]
        qx = [_dot(qb, xb) for qb, xb in zip(_b16(q), _b16(x))]
        q = [qi - xi - qxi for qi, xi, qxi in zip(q, x, qx)]
        shift += 1
    return q


def _delta_units(units):
    c = CHUNK
    scale = HEAD_DIM ** -0.5
    rows = lax.broadcasted_iota(I32, (c, c), 0)
    cols = lax.broadcasted_iota(I32, (c, c), 1)
    pre = []
    for d, hd, qkv_ref, r0, gate, gt, load_s in units:
        lane = 2 * N_HEADS + N_HEADS * d + hd
        last = c - 1 if d == 0 else 0
        pre.append(dict(
            d=d, load_s=load_s,
            load=lambda part, qkv_ref=qkv_ref, r0=r0, hd=hd: qkv_ref[
                pl.ds(r0, c), part * MIX_A + hd * HEAD_DIM:part * MIX_A + (hd + 1) * HEAD_DIM],
            beta=gate[:, N_HEADS * d + hd:N_HEADS * d + hd + 1],
            gc=gate[:, lane:lane + 1],
            gr=gt[lane:lane + 1, :],
            gtot=gate[last:last + 1, lane:lane + 1]))

    ak = []
    for p in pre:
        k = p['load'](1)
        lhs = jnp.concatenate([k * p['beta'], p['load'](0) * scale], axis=0).astype(BF16)
        ak.append(_dot_nt(lhs, k.astype(BF16)))
    nmats, qks = [], []
    for p, a in zip(pre, ak):
        incl = (rows >= cols) if p['d'] == 0 else (rows <= cols)
        strict = (rows > cols) if p['d'] == 0 else (rows < cols)
        dec = jnp.where(incl, jnp.exp(p['gc'] - p['gr']), 0.0)
        nmats.append(jnp.where(strict, -a[:c] * dec, 0.0))
        qks.append((a[c:] * dec).astype(BF16))
    qinv = _tri_inverse_minus_eye(nmats, rows, cols)

    uw = []
    for p, qi in zip(pre, qinv):
        kb = p['load'](1) * p['beta']
        rhs = jnp.concatenate([p['load'](2) * p['beta'], kb * jnp.exp(p['gc'])], axis=1)
        uw.append(rhs + _dot(qi.astype(BF16), rhs.astype(BF16)))
    sw = []
    for p, x in zip(pre, uw):
        qdec = p['load'](0) * scale * jnp.exp(p['gc'])
        sw.append(_dot(jnp.concatenate([x[:, HEAD_DIM:], qdec], axis=0).astype(BF16), p['load_s']().astype(BF16)))
    vnb = [(x[:, :HEAD_DIM] - y[:c]).astype(BF16) for x, y in zip(uw, sw)]
    o = [y[c:] + _dot(qk, vb) for y, qk, vb in zip(sw, qks, vnb)]
    s_new = []
    for p, vb in zip(pre, vnb):
        kdec = (p['load'](1) * jnp.exp(p['gtot'] - p['gc'])).astype(BF16)
        s_new.append(p['load_s']() * jnp.exp(p['gtot']) + _dot_tn(kdec, vb))
    return list(zip(o, s_new))


def _delta_kernel(has_s0, emit_final, one_block, n_chunk, *refs):
    refs = list(refs)
    if one_block:
        seq_in = [[r for r in refs[3 * s:3 * s + 3] for _ in range(2)] for s in range(DELTA_SEQS)]
        refs = refs[3 * DELTA_SEQS:]
    else:
        seq_in = [refs[6 * s:6 * s + 6] for s in range(DELTA_SEQS)]
        refs = refs[6 * DELTA_SEQS:]
    s0_ref = refs.pop(0) if has_s0 else None
    of_ref, ob_ref = refs[:2]
    refs = refs[2:]
    sfin_ref = refs.pop(0) if emit_final else None
    s_scr = refs[0]

    j = pl.program_id(1)

    @pl.when(j == 0)
    def _():
        if has_s0:
            s_scr[...] = s0_ref[...]
        else:
            s_scr[...] = jnp.zeros(s_scr.shape, F32)

    c = CHUNK

    def body(ci, carry):
        units, where = [], []
        for s, (qkvf_ref, qkvb_ref, gatef_ref, gateb_ref, gtf_ref, gtb_ref) in enumerate(seq_in):
            for d in range(2):
                cidx = ci if d == 0 else n_chunk - 1 - ci
                r0 = pl.multiple_of(cidx * c, c)
                qkv_ref = qkvf_ref if d == 0 else qkvb_ref
                gate = (gatef_ref if d == 0 else gateb_ref)[pl.ds(r0, c), :]
                gt = (gtf_ref if d == 0 else gtb_ref)[cidx]
                for hd in range(N_HEADS):
                    units.append((d, hd, qkv_ref, r0, gate, gt,
                                  lambda s=s, idx=N_HEADS * d + hd: s_scr[s, idx]))
                    where.append((s, d, hd, r0))
        for (s, d, hd, r0), (o, s_new) in zip(where, _delta_units(units)):
            s_scr[s, N_HEADS * d + hd] = s_new
            (of_ref if d == 0 else ob_ref)[s, pl.ds(r0, c), hd * HEAD_DIM:(hd + 1) * HEAD_DIM] = o
        return carry

    lax.fori_loop(0, n_chunk, body, 0)

    if emit_final:
        @pl.when(j == pl.num_programs(1) - 1)
        def _():
            sfin_ref[...] = s_scr[...]


def _delta(qkv, gate, gate_t, n_seq, seq_len, row0, blk, s0):
    nblk = seq_len // blk
    n_chunk = blk // CHUNK
    b0 = row0 // blk
    has_s0 = s0 is not None
    emit_final = not has_s0
    n_state = 2 * N_HEADS
    one_block = nblk == 1
    assert n_seq % DELTA_SEQS == 0 and row0 % blk == 0

    in_specs, args = [], []
    for s in range(DELTA_SEQS):
        def fwd(b, j, s=s):
            return b0 + (b * DELTA_SEQS + s) * nblk + j

        def bwd(b, j, s=s):
            return b0 + (b * DELTA_SEQS + s) * nblk + (nblk - 1 - j)

        for arr, shape in ((qkv, (blk, 3 * MIX_A)), (gate, (blk, LANES)), (gate_t, (n_chunk, GATE_ROWS, CHUNK))):
            tail = (0,) * (len(shape) - 1)
            for f in (fwd,) if one_block else (fwd, bwd):
                in_specs.append(pl.BlockSpec(shape, lambda b, j, f=f, tail=tail: (f(b, j),) + tail))
                args.append(arr)
    state_spec = pl.BlockSpec((DELTA_SEQS, n_state, HEAD_DIM, HEAD_DIM), lambda b, j: (b, 0, 0, 0))
    if has_s0:
        in_specs.append(state_spec)
        args.append(s0)
    out_specs = [pl.BlockSpec((DELTA_SEQS, blk, MIX_A), lambda b, j: (b, j, 0)),
                 pl.BlockSpec((DELTA_SEQS, blk, MIX_A), lambda b, j: (b, nblk - 1 - j, 0))]
    out_shape = [jax.ShapeDtypeStruct((n_seq, seq_len, MIX_A), F32)] * 2
    if emit_final:
        out_specs.append(state_spec)
        out_shape.append(jax.ShapeDtypeStruct((n_seq, n_state, HEAD_DIM, HEAD_DIM), F32))
    outs = pl.pallas_call(
        functools.partial(_delta_kernel, has_s0, emit_final, one_block, n_chunk),
        grid=(n_seq // DELTA_SEQS, nblk),
        in_specs=in_specs,
        out_specs=out_specs,
        out_shape=out_shape,
        scratch_shapes=[pltpu.VMEM((DELTA_SEQS, n_state, HEAD_DIM, HEAD_DIM), F32)],
        compiler_params=_cparams(("parallel", "arbitrary")),
        name="delta_latent" if has_s0 else "delta_prompt",
    )(*args)
    return [o.reshape(n_seq * seq_len, MIX_A) for o in outs[:2]] + list(outs[2:])


def _outproj_kernel(lay, xp_ref, xs_ref, prow_ref, pcol_ref, mod_ref, opf_ref, opb_ref, osf_ref, osb_ref, z_ref,
                    u_ref, og_ref, n2_ref, wo_ref, wr_ref, br_ref, x1_ref, h2_ref, ids_ref, wts_ref, cnt_ref,
                    cnt_scr):
    i = pl.program_id(0)
    m = mod_ref[0]

    @pl.when(i == 0)
    def _():
        cnt_scr[...] = jnp.zeros(cnt_scr.shape, F32)

    halves = [slice(r0, r0 + PROJ_ROWS) for r0 in range(0, lay.tile, PROJ_ROWS)]
    n = PROJ_ROWS
    lane = lax.broadcasted_iota(I32, (n, LANES), 1)
    lane_f = lane.astype(F32)
    neg = jnp.float32(-jnp.inf)
    big = jnp.float32(LANES)

    def first_lane(mask):
        return jnp.min(jnp.where(mask, lane_f, big), axis=-1, keepdims=True)

    gated = []
    for rows in halves:
        o = jnp.where(lay.is_sample(i), osf_ref[rows, :] + osb_ref[rows, :], opf_ref[rows, :] + opb_ref[rows, :])
        z = z_ref[rows, :]
        gated.append([(_rms(o[:, sl]) * og_ref[...] * _silu(z[:, sl])).astype(BF16)
                      for sl in (slice(hd * HEAD_DIM, (hd + 1) * HEAD_DIM) for hd in range(N_HEADS))])
    mixes = []
    for rows, heads in zip(halves, gated):
        mix = _dot(u_ref[rows, :], wo_ref[MIX_A:, :])
        for hd, oh in enumerate(heads):
            mix = mix + _dot(oh, wo_ref[hd * HEAD_DIM:(hd + 1) * HEAD_DIM, :])
        mixes.append(mix)
    h2bs = []
    for rows, mix in zip(halves, mixes):
        x1 = _load_x(lay, i, xp_ref, xs_ref, prow_ref, pcol_ref, rows) + m[2:3] * mix
        x1_ref[rows, :] = x1
        h2b = (_rms(x1) * n2_ref[...] * (1.0 + m[4:5]) + m[3:4]).astype(BF16)
        h2_ref[rows, :] = _pack_bf16_pairs(h2b)
        h2bs.append(h2b)
    logits_all = [_dot(h2b, wr_ref[...]) + br_ref[...] for h2b in h2bs]

    picks = []
    for rows, logits in zip(halves, logits_all):
        gl = jnp.where(lane < N_GROUPS, logits, neg)
        gmax = jnp.max(gl, axis=-1, keepdims=True)
        grp = first_lane(gl == gmax)
        p_grp = 1.0 / jnp.sum(jnp.where(lane < N_GROUPS, jnp.exp(gl - gmax), 0.0), axis=-1, keepdims=True)
        e_lane = lane_f - N_GROUPS
        in_grp = (e_lane >= grp * EXPERTS_PER_GROUP) & (e_lane < (grp + 1.0) * EXPERTS_PER_GROUP)
        el = jnp.where(in_grp, logits, neg)
        m1 = jnp.max(el, axis=-1, keepdims=True)
        i1f = first_lane(el == m1)
        el2 = jnp.where(lane_f == i1f, neg, el)
        m2 = jnp.max(el2, axis=-1, keepdims=True)
        i2f = first_lane(el2 == m2)
        e2 = jnp.exp(m2 - m1)
        w1 = p_grp / (1.0 + e2)
        w2 = p_grp * e2 / (1.0 + e2)
        col = lax.broadcasted_iota(I32, (n, ROUTE_COLS), 1)
        wts_ref[rows, :] = jnp.where(col == 0, w1, jnp.where(col == 1, w2, 0.0))
        picks.append((i1f, i2f))

    r_i = lax.broadcasted_iota(I32, (n, n), 0)
    c_i = lax.broadcasted_iota(I32, (n, n), 1)
    before = jnp.where(r_i > c_i, 1.0, 0.0).astype(BF16)
    onehots = [(lane_f == i1f, lane_f == i2f) for i1f, i2f in picks]
    chosen = [jnp.where(oh1, 1.0, jnp.where(oh2, 1.0, 0.0)) for oh1, oh2 in onehots]
    prefix = [_dot(before, oh.astype(BF16)) for oh in chosen]
    counts = cnt_scr[...]
    for rows, (i1f, i2f), (oh1, oh2), oh, pre in zip(halves, picks, onehots, chosen, prefix):
        seen = counts + pre
        r1 = jnp.sum(jnp.where(oh1, seen, 0.0), axis=-1, keepdims=True)
        r2 = jnp.sum(jnp.where(oh2, seen, 0.0), axis=-1, keepdims=True)
        ids = jnp.where(lane == 0, i1f - N_GROUPS,
                        jnp.where(lane == 1, i2f - N_GROUPS,
                                  jnp.where(lane == 2, r1, jnp.where(lane == 3, r2, 0.0))))
        ids_ref[:, rows] = jnp.transpose(ids)[:ROUTE_COLS, :].astype(I32)
        counts = counts + jnp.sum(oh, axis=0, keepdims=True)
    cnt_scr[...] = counts
    cnt_ref[...] = counts


def _outproj(lay, xp, xs, pos, mod, o_pf, o_pb, o_sf, o_sb, z, u, onorm_g, norm2_g, w_out, w_route, b_route):
    t = lay.t
    p_spec = pl.BlockSpec((lay.tile, MIX_A), lambda i: (lay.xp_blk(i), 0))
    s_spec = pl.BlockSpec((lay.tile, MIX_A), lambda i: (lay.xs_blk(i), 0))
    return pl.pallas_call(
        functools.partial(_outproj_kernel, lay),
        grid=(lay.tiles,),
        in_specs=_x_specs(lay) + [_mod_spec(lay), p_spec, p_spec, s_spec, s_spec, _row_spec(lay, MIX_A),
                                  _row_spec(lay, MIX_B), _full_spec((1, HEAD_DIM)), _full_spec((1, D_MODEL)),
                                  _full_spec(w_out.shape), _full_spec(w_route.shape),
                                  _full_spec(b_route.shape)],
        out_specs=[_row_spec(lay, D_MODEL), _row_spec(lay, D_MODEL // 2),
                   pl.BlockSpec((ROUTE_COLS, lay.tile), lambda i: (0, i)),
                   _row_spec(lay, ROUTE_COLS), _full_spec((1, LANES))],
        out_shape=[jax.ShapeDtypeStruct((t, D_MODEL), F32),
                   jax.ShapeDtypeStruct((t, D_MODEL // 2), I32),
                   jax.ShapeDtypeStruct((ROUTE_COLS, t), I32),
                   jax.ShapeDtypeStruct((t, ROUTE_COLS), F32),
                   jax.ShapeDtypeStruct((1, LANES), F32)],
        scratch_shapes=[pltpu.VMEM((1, LANES), F32)],
        compiler_params=_cparams(("arbitrary",)),
        name="outproj",
    )(xp, xs, *pos, mod, o_pf, o_pb, o_sf, o_sb, z, u, onorm_g, norm2_g, w_out, w_route, b_route)


def _sc_gather(table, idx, chunk, name):
    n_rows, width = idx.shape[0], table.shape[1]
    mesh = plsc.VectorSubcoreMesh(core_axis_name="c", subcore_axis_name="s")
    n_workers = mesh.num_cores * mesh.num_subcores
    per_worker = n_rows // n_workers
    n_chunks = per_worker // chunk
    assert n_rows == n_workers * n_chunks * chunk and n_chunks % 2 == 0 and chunk % 8 == 0 and chunk <= LANES

    def body(table_hbm, idx_hbm, out_hbm, idx_v, rows_v, sem):
        base = (lax.axis_index("s") * mesh.num_cores + lax.axis_index("c")) * per_worker

        def gather(slot):
            return pltpu.make_async_copy(table_hbm.at[idx_v.at[slot]], rows_v.at[slot], sem.at[slot])

        def fetch(g, slot):
            off = pl.multiple_of(base + g * chunk, 8)
            pltpu.sync_copy(idx_hbm.at[pl.ds(off, chunk)], idx_v.at[slot])
            gather(slot).start()

        for slot in range(2):
            fetch(slot, slot)

        @pl.loop(0, n_chunks, step=2)
        def _(g):
            for slot in range(2):
                off = pl.multiple_of(base + (g + slot) * chunk, 8)
                gather(slot).wait()
                pltpu.sync_copy(rows_v.at[slot], out_hbm.at[pl.ds(off, chunk)])

                @pl.when(g + slot + 2 < n_chunks)
                def _():
                    fetch(g + slot + 2, slot)

    return pl.kernel(
        body,
        out_type=jax.ShapeDtypeStruct((n_rows, width), table.dtype),
        mesh=mesh,
        scratch_types=[pltpu.VMEM((2, chunk), I32), pltpu.VMEM((2, chunk, width), table.dtype),
                       pltpu.SemaphoreType.DMA((2,))],
        name=name,
    )(table, idx)


def _sc_scatter2(src, idx0, idx1, n_out, chunk, name):
    n_rows, width = src.shape
    mesh = plsc.VectorSubcoreMesh(core_axis_name="c", subcore_axis_name="s")
    n_workers = mesh.num_cores * mesh.num_subcores
    per_worker = n_rows // n_workers
    n_chunks = per_worker // chunk
    assert n_rows == n_workers * n_chunks * chunk and n_chunks % 2 == 0 and chunk % 8 == 0 and chunk <= LANES

    def body(src_hbm, i0_hbm, i1_hbm, out_hbm, i0_v, i1_v, rows_v, sem_in, sem_out):
        base = (lax.axis_index("s") * mesh.num_cores + lax.axis_index("c")) * per_worker

        def rows_in(g, slot):
            off = pl.multiple_of(base + g * chunk, 8)
            return pltpu.make_async_copy(src_hbm.at[pl.ds(off, chunk)], rows_v.at[slot], sem_in.at[slot])

        def fetch(g, slot):
            off = pl.multiple_of(base + g * chunk, 8)
            pltpu.sync_copy(i0_hbm.at[pl.ds(off, chunk)], i0_v.at[slot])
            pltpu.sync_copy(i1_hbm.at[pl.ds(off, chunk)], i1_v.at[slot])
            rows_in(g, slot).start()

        for slot in range(2):
            fetch(slot, slot)

        @pl.loop(0, n_chunks, step=2)
        def _(g):
            for slot in range(2):
                rows_in(g + slot, slot).wait()
                puts = [pltpu.make_async_copy(rows_v.at[slot], out_hbm.at[iv.at[slot]], sem_out.at[slot])
                        for iv in (i0_v, i1_v)]
                for put in puts:
                    put.start()
                for put in puts:
                    put.wait()

                @pl.when(g + slot + 2 < n_chunks)
                def _():
                    fetch(g + slot + 2, slot)

    return pl.kernel(
        body,
        out_type=jax.ShapeDtypeStruct((n_out, width), src.dtype),
        mesh=mesh,
        scratch_types=[pltpu.VMEM((2, chunk), I32), pltpu.VMEM((2, chunk), I32),
                       pltpu.VMEM((2, chunk, width), src.dtype),
                       pltpu.SemaphoreType.DMA((2,)), pltpu.SemaphoreType.DMA((2,))],
        name=name,
    )(src, idx0, idx1)


def _expert_kernel(blk_e_ref, nused_ref, xb_ref, wg_ref, wu_ref, wd_ref, yb_ref, wg_s, wu_s, wd_s):
    i = pl.program_id(0)
    nused = nused_ref[0]

    @pl.when(i < nused)
    def _():
        changed = (i == 0) | (blk_e_ref[i] != blk_e_ref[jnp.maximum(i - 1, 0)])

        @pl.when(changed)
        def _():
            wg_s[...] = wg_ref[0].astype(BF16)
            wu_s[...] = wu_ref[0].astype(BF16)
            wd_s[...] = wd_ref[0].astype(BF16)

        half = D_MODEL // 2
        xa, xb = (v.astype(BF16) for v in _unpack_bf16_pairs(xb_ref[...]))
        g = _dot(xa, wg_s[:half, :]) + _dot(xb, wg_s[half:, :])
        u = _dot(xa, wu_s[:half, :]) + _dot(xb, wu_s[half:, :])
        hmid = (_silu(g) * u).astype(BF16)
        yb_ref[...] = _pack_bf16_pairs(_dot(hmid, wd_s[...]).astype(BF16))

    @pl.when(i >= nused)
    def _():
        yb_ref[...] = jnp.zeros(yb_ref.shape, I32)


def _experts(xb, blk_e, nused, w_gate, w_up, w_down):
    nb = blk_e.shape[0]

    def weight_spec(shape):
        return pl.BlockSpec((1,) + shape, lambda i, be, nu: (be[i], 0, 0))

    grid_spec = pltpu.PrefetchScalarGridSpec(
        num_scalar_prefetch=2,
        grid=(nb,),
        in_specs=[
            pl.BlockSpec((ROUTE_BM, D_MODEL // 2), lambda i, be, nu: (i, 0)),
            weight_spec((D_MODEL, D_EXPERT)),
            weight_spec((D_MODEL, D_EXPERT)),
            weight_spec((D_EXPERT, D_MODEL)),
        ],
        out_specs=pl.BlockSpec((ROUTE_BM, D_MODEL // 2), lambda i, be, nu: (i, 0)),
        scratch_shapes=[pltpu.VMEM((D_MODEL, D_EXPERT), BF16),
                        pltpu.VMEM((D_MODEL, D_EXPERT), BF16),
                        pltpu.VMEM((D_EXPERT, D_MODEL), BF16)],
    )
    return pl.pallas_call(
        _expert_kernel,
        grid_spec=grid_spec,
        out_shape=jax.ShapeDtypeStruct((nb * ROUTE_BM, D_MODEL // 2), I32),
        compiler_params=_cparams(("arbitrary",)),
        name="expert",
    )(blk_e, nused, xb, w_gate, w_up, w_down)


def _combine_kernel(lay, y0_ref, y1_ref, x1_ref, wts_ref, mod_ref, fg_ref, yp_ref, ys_ref):
    i = pl.program_id(0)
    m = mod_ref[0]
    w = wts_ref[...]
    a0, b0 = _unpack_bf16_pairs(y0_ref[...])
    a1, b1 = _unpack_bf16_pairs(y1_ref[...])
    ff = jnp.concatenate([a0 * w[:, 0:1] + a1 * w[:, 1:2], b0 * w[:, 0:1] + b1 * w[:, 1:2]], axis=1)
    y = _rms(x1_ref[...] + m[5:6] * ff) * fg_ref[...]

    @pl.when(i < lay.tiles_p)
    def _():
        yp_ref[...] = y

    @pl.when(i >= lay.tiles_p)
    def _():
        ys_ref[...] = y


def _combine(lay, yg, x1, wts, mod, final_g):
    return pl.pallas_call(
        functools.partial(_combine_kernel, lay),
        grid=(lay.tiles,),
        in_specs=[_row_spec(lay, D_MODEL // 2),
                  pl.BlockSpec((lay.tile, D_MODEL // 2), lambda i: (i + lay.tiles, 0)),
                  _row_spec(lay, D_MODEL), _row_spec(lay, ROUTE_COLS), _mod_spec(lay), _full_spec((1, D_MODEL))],
        out_specs=[pl.BlockSpec((lay.tile, D_MODEL), lambda i: (lay.xp_blk(i), 0)),
                   pl.BlockSpec((lay.tile, D_MODEL), lambda i: (lay.xs_blk(i), 0))],
        out_shape=[jax.ShapeDtypeStruct((lay.t_p, D_MODEL), F32),
                   jax.ShapeDtypeStruct((lay.t_s, D_MODEL), F32)],
        compiler_params=_cparams(("arbitrary",)),
        name="combine",
    )(yg, yg, x1, wts, mod, final_g)


def _dispatch_plan(ids, counts):
    n_tok = ids.shape[1]
    padded = (counts + ROUTE_BM - 1) // ROUTE_BM * ROUTE_BM
    pad_end = jnp.cumsum(padded)
    pad_start = pad_end - padded
    experts = jnp.arange(N_EXPERTS, dtype=I32)[:, None]
    dest = [(jnp.sum(jnp.where(ids[k][None, :] == experts, pad_start[:, None], 0), axis=0) + ids[2 + k]).astype(I32)
            for k in range(2)]
    nb = -(-(2 * n_tok + N_EXPERTS * (ROUTE_BM - 1)) // ROUTE_BM)
    block_start = jnp.arange(nb, dtype=I32) * ROUTE_BM
    blk_e = jnp.minimum(jnp.sum(pad_end[None, :] <= block_start[:, None], axis=1), N_EXPERTS - 1).astype(I32)
    nused = (pad_end[-1:] // ROUTE_BM).astype(I32)
    return dest[0], dest[1], blk_e, nused


def _grid_pos_tables(n_tokens):
    rows = n_tokens // GRID_W
    n_freq = D_MODEL // 4
    freq = jnp.exp(jnp.arange(n_freq, dtype=F32) * (-math.log(10000.0) / n_freq))

    def enc(p):
        ang = p[:, None] * freq[None, :]
        return jnp.concatenate([jnp.sin(ang), jnp.cos(ang)], axis=-1)

    return enc(jnp.arange(rows, dtype=F32)), enc(jnp.arange(GRID_W, dtype=F32))


def _lane_pad(v, offset):
    return jnp.zeros((1, LANES), F32).at[0, offset:offset + v.shape[0]].set(v.astype(F32))


def kernel(x_prompt, x_sample, state_delta, c, c_ctx, norm1_g, w_mod, b_mod, w_in, conv_qkv_w, A_log, dt_bias, onorm_g, dw_w, dw_b, cln_g, cln_b, w_out, norm2_g, w_group, b_group, w_expert, b_expert, w_e_gate, w_e_up, w_e_down, final_g):
    n_p, l_p, _ = x_prompt.shape
    n_s, l_s, _ = x_sample.shape
    lay = _Layout(n_p, l_p, n_s, l_s, TOK_TILE)
    lay_proj = _Layout(n_p, l_p, n_s, l_s, PROJ_TILE)
    depth = w_in.shape[0]
    assert depth == 1
    xp = x_prompt.reshape(lay.t_p, D_MODEL)
    xs = x_sample.reshape(lay.t_s, D_MODEL)
    pos = _grid_pos_tables(l_s)

    cond = jnp.concatenate([c_ctx[None, :], c], axis=0)
    cond = jnp.pad(cond, ((0, (-cond.shape[0]) % SUBLANES), (0, 0)))
    mod = _modulation(cond, w_mod[0], b_mod[0]).reshape(cond.shape[0], 6, D_MODEL)

    n_gate = 4 * N_HEADS
    wi = w_in[0]
    w_main = jnp.concatenate([wi[:, :4 * MIX_A], wi[:, 4 * MIX_A + n_gate:]], axis=1).astype(BF16)
    w_small = jnp.pad(wi[:, 4 * MIX_A:4 * MIX_A + n_gate], ((0, 0), (0, LANES - n_gate))).astype(BF16)
    qkv_raw, z, ug, ba = _inproj(lay_proj, xp, xs, pos, mod, norm1_g[0][None, :], w_main, w_small)

    cw = jnp.pad(conv_qkv_w[0], ((0, 8 - SHORT_CONV), (0, 0))).reshape(8, Q_GROUPS, LANES).transpose(1, 0, 2)
    dw = jnp.pad(dw_w[0], ((0, 32 - CONV_W), (0, 0))).reshape(32, U_GROUPS, LANES).transpose(1, 0, 2)
    alog = _lane_pad(A_log[0].reshape(-1), 2 * N_HEADS)
    dtb = _lane_pad(dt_bias[0].reshape(-1), 2 * N_HEADS)
    qkv, u_conf, gate, gate_t = _conv(lay, qkv_raw, ug, ba, cw, dw, dw_b[0][None, :], cln_g[0][None, :],
                                      cln_b[0][None, :], alog, dtb)

    o_pf, o_pb, s_fin = _delta(qkv, gate, gate_t, n_p, l_p, 0, l_p, None)
    s0 = state_delta[:, 0].reshape(n_s, 2 * N_HEADS, HEAD_DIM, HEAD_DIM)
    o_sf, o_sb = _delta(qkv, gate, gate_t, n_s, l_s, lay.t_p, DELTA_BLK, s0)

    w_route = jnp.pad(jnp.concatenate([w_group[0], w_expert[0]], axis=1),
                      ((0, 0), (0, LANES - N_GROUPS - N_EXPERTS))).astype(BF16)
    b_route = _lane_pad(jnp.concatenate([b_group[0], b_expert[0]]), 0)
    x1, h2p, ids, wts, cnt = _outproj(lay_proj, xp, xs, pos, mod, o_pf, o_pb, o_sf, o_sb, z, u_conf,
                                      onorm_g[0][None, :], norm2_g[0][None, :], w_out[0].astype(BF16), w_route,
                                      b_route)

    counts = cnt[0, N_GROUPS:N_GROUPS + N_EXPERTS].astype(I32)
    dest0, dest1, blk_e, nused = _dispatch_plan(ids, counts)
    xb = _sc_scatter2(h2p, dest0, dest1, blk_e.shape[0] * ROUTE_BM, DISPATCH_CHUNK, "dispatch_scatter")
    yb = _experts(xb, blk_e, nused, w_e_gate[0], w_e_up[0], w_e_down[0])
    yg = _sc_gather(yb, jnp.concatenate([dest0, dest1]), COMBINE_CHUNK, "combine_gather")
    y_p, y_s = _combine(_Layout(n_p, l_p, n_s, l_s, COMBINE_TILE), yg, x1, wts, mod, final_g[None, :])

    new_state = s_fin.reshape(n_p, 1, 2, N_HEADS, HEAD_DIM, HEAD_DIM)
    return (y_p.reshape(x_prompt.shape), y_s.reshape(x_sample.shape), new_state)
```

```python
import functools
import math

import jax
import jax.numpy as jnp
from jax import lax
from jax.experimental import pallas as pl
from jax.experimental.pallas import tpu as pltpu
from jax.experimental.pallas import tpu_sc as plsc

F32 = jnp.float32
BF16 = jnp.bfloat16
I32 = jnp.int32

D_MODEL = 1024
MIX_A = 512
MIX_B = 512
HEAD_DIM = 128
N_HEADS = 4
SHORT_CONV = 5
CONV_W = 31
N_GROUPS = 4
EXPERTS_PER_GROUP = 8
N_EXPERTS = 32
D_EXPERT = 512
GRID_W = 64
EPS = 1e-6

LANES = 128
SUBLANES = 8
TOK_TILE = 256
PROJ_TILE = 512
PROJ_ROWS = 256
CHUNK = 128
DELTA_BLK = 512
DELTA_SEQS = 2
ROUTE_BM = 512
ROUTE_COLS = 8
GATE_ROWS = 16
COMBINE_TILE = 1024
DISPATCH_CHUNK = 64
COMBINE_CHUNK = 64
Q_HALO = 8
U_HALO = 16
VMEM_LIMIT = 56 * 1024 * 1024

HIGHEST = lax.Precision.HIGHEST


def _cparams(sem):
    return pltpu.CompilerParams(dimension_semantics=sem, vmem_limit_bytes=VMEM_LIMIT)


def _sigmoid(x):
    return 0.5 * jnp.tanh(0.5 * x) + 0.5


def _silu(x):
    return x * _sigmoid(x)


def _softplus(x):
    return jnp.maximum(x, 0.0) + jnp.log1p(jnp.exp(-jnp.abs(x)))


def _rms(x):
    return x * lax.rsqrt(jnp.mean(x * x, axis=-1, keepdims=True) + EPS)


def _pack_bf16_pairs(x):
    h = x.shape[1] // 2
    hi = pltpu.bitcast(x[:, :h].astype(F32), jnp.uint32)
    lo = pltpu.bitcast(x[:, h:].astype(F32), jnp.uint32)
    return pltpu.bitcast((hi & jnp.uint32(0xFFFF0000)) | (lo >> 16), I32)


def _unpack_bf16_pairs(p):
    u = pltpu.bitcast(p, jnp.uint32)
    return pltpu.bitcast(u & jnp.uint32(0xFFFF0000), F32), pltpu.bitcast(u << 16, F32)


def _dot(a, b):
    return jnp.dot(a, b, preferred_element_type=F32)


def _dot_nt(a, b):
    return lax.dot_general(a, b, (((1,), (1,)), ((), ())), preferred_element_type=F32)


def _dot_tn(a, b):
    return lax.dot_general(a, b, (((0,), (0,)), ((), ())), preferred_element_type=F32)


def _mod_kernel(cond_ref, w_ref, b_ref, o_ref):
    s = _silu(cond_ref[...])
    o_ref[...] = jnp.dot(s, w_ref[...], preferred_element_type=F32, precision=HIGHEST) + b_ref[...]


def _modulation(cond, w_mod, b_mod):
    n = cond.shape[0]
    tn = D_MODEL
    return pl.pallas_call(
        _mod_kernel,
        grid=(6 * D_MODEL // tn,),
        in_specs=[pl.BlockSpec((n, D_MODEL), lambda j: (0, 0)),
                  pl.BlockSpec((D_MODEL, tn), lambda j: (0, j)),
                  pl.BlockSpec((1, tn), lambda j: (0, j))],
        out_specs=pl.BlockSpec((n, tn), lambda j: (0, j)),
        out_shape=jax.ShapeDtypeStruct((n, 6 * D_MODEL), F32),
        compiler_params=_cparams(("parallel",)),
        name="mod",
    )(cond, w_mod, b_mod.reshape(1, -1))


class _Layout:
    def __init__(self, n_p, l_p, n_s, l_s, tile):
        self.n_p, self.l_p, self.n_s, self.l_s, self.tile = n_p, l_p, n_s, l_s, tile
        self.t_p = n_p * l_p
        self.t_s = n_s * l_s
        self.t = self.t_p + self.t_s
        assert self.t_p % tile == 0 and l_s % tile == 0 and l_s % DELTA_BLK == 0
        self.tiles_p = self.t_p // tile
        self.tiles_s = self.t_s // tile
        self.tiles = self.tiles_p + self.tiles_s
        self.tps_p = max(l_p // tile, 1)
        self.tps_s = l_s // tile

    def is_sample(self, i):
        return i >= self.tiles_p

    def mod_row(self, i):
        return jnp.where(i < self.tiles_p, 0, 1 + (i - self.tiles_p) // self.tps_s)

    def pos_blk(self, i):
        return jnp.where(i < self.tiles_p, 0, (i - self.tiles_p) % self.tps_s)

    def xp_blk(self, i):
        return jnp.minimum(i, self.tiles_p - 1)

    def xs_blk(self, i):
        return jnp.maximum(i - self.tiles_p, 0)

    def seq_pos(self, i):
        in_s = i >= self.tiles_p
        pos = jnp.where(in_s, (i - self.tiles_p) % self.tps_s, i % self.tps_p)
        n = jnp.where(in_s, self.tps_s, self.tps_p)
        return pos, n


def _load_x(lay, i, xp_ref, xs_ref, prow_ref, pcol_ref, rows):
    g0, g1 = rows.start // GRID_W, rows.stop // GRID_W
    half = D_MODEL // 2
    prow = jnp.concatenate([jnp.broadcast_to(prow_ref[r:r + 1, :], (GRID_W, half)) for r in range(g0, g1)],
                           axis=0)
    pcol = jnp.concatenate([pcol_ref[...]] * (g1 - g0), axis=0)
    pos = jnp.concatenate([prow, pcol], axis=1)
    return jnp.where(lay.is_sample(i), xs_ref[rows, :] + pos, xp_ref[rows, :])


def _x_specs(lay):
    assert lay.tile % (SUBLANES * GRID_W) == 0
    return [pl.BlockSpec((lay.tile, D_MODEL), lambda i: (lay.xp_blk(i), 0)),
            pl.BlockSpec((lay.tile, D_MODEL), lambda i: (lay.xs_blk(i), 0)),
            pl.BlockSpec((lay.tile // GRID_W, D_MODEL // 2), lambda i: (lay.pos_blk(i), 0)),
            _full_spec((GRID_W, D_MODEL // 2))]


def _mod_spec(lay):
    return pl.BlockSpec((1, 6, D_MODEL), lambda i: (lay.mod_row(i), 0, 0))


def _row_spec(lay, width):
    return pl.BlockSpec((lay.tile, width), lambda i: (i, 0))


def _full_spec(shape):
    nd = len(shape)
    return pl.BlockSpec(shape, lambda i: (0,) * nd)


def _inproj_kernel(lay, xp_ref, xs_ref, prow_ref, pcol_ref, mod_ref, g_ref, wm_ref, ws_ref,
                   qkv_ref, z_ref, ug_ref, ba_ref):
    i = pl.program_id(0)
    m = mod_ref[0]
    x = _load_x(lay, i, xp_ref, xs_ref, prow_ref, pcol_ref, slice(0, lay.tile))
    h = _rms(x) * g_ref[...] * (1.0 + m[1:2]) + m[0:1]
    hb = h.astype(BF16)
    qkv_ref[...] = _dot(hb, wm_ref[:, 0:3 * MIX_A])
    z_ref[...] = _dot(hb, wm_ref[:, 3 * MIX_A:4 * MIX_A])
    glu = _dot(hb, wm_ref[:, 4 * MIX_A:4 * MIX_A + 2 * MIX_B])
    ug_ref[...] = glu[:, :MIX_B] * _sigmoid(glu[:, MIX_B:])
    ba_ref[...] = _dot(hb, ws_ref[...])


def _inproj(lay, xp, xs, pos, mod, norm_g, w_main, w_small):
    t = lay.t
    return pl.pallas_call(
        functools.partial(_inproj_kernel, lay),
        grid=(lay.tiles,),
        in_specs=_x_specs(lay) + [_mod_spec(lay), _full_spec((1, D_MODEL)),
                                  _full_spec(w_main.shape), _full_spec(w_small.shape)],
        out_specs=[_row_spec(lay, 3 * MIX_A), _row_spec(lay, MIX_A), _row_spec(lay, MIX_B),
                   _row_spec(lay, LANES)],
        out_shape=[jax.ShapeDtypeStruct((t, 3 * MIX_A), F32),
                   jax.ShapeDtypeStruct((t, MIX_A), F32),
                   jax.ShapeDtypeStruct((t, MIX_B), F32),
                   jax.ShapeDtypeStruct((t, LANES), F32)],
        compiler_params=_cparams(("parallel",)),
        name="inproj",
    )(xp, xs, *pos, mod, norm_g, w_main, w_small)


CONV_PITCH = 33
CONV_OUT_ROWS = SUBLANES * CONV_PITCH
POST_ROWS = 64
Q_EXT_ROWS = 280
U_EXT_ROWS = 296
Q_GROUPS = 3 * MIX_A // LANES
U_GROUPS = MIX_B // LANES
CONV_J_BLOCK = 11


def _strided_conv(ext_scr, res_scr, w_ref, g, off, n_taps):
    for j0 in range(0, CONV_PITCH, CONV_J_BLOCK):
        js = range(j0, min(j0 + CONV_J_BLOCK, CONV_PITCH))
        v = {m: ext_scr[g, pl.ds(off + m, SUBLANES, stride=CONV_PITCH), :]
             for m in range(js[0], js[-1] + n_taps)}
        for j in js:
            acc = v[j] * w_ref[g, 0:1, :]
            for s in range(1, n_taps):
                acc = acc + v[j + s] * w_ref[g, s:s + 1, :]
            res_scr[g, pl.ds(j, SUBLANES, stride=CONV_PITCH), :] = acc


def _conv_kernel(lay, qc_ref, qp_ref, qn_ref, uc_ref, up_ref, un_ref, ba_ref, cw_ref, dw_ref,
                 dwb_ref, lng_ref, lnb_ref, alog_ref, dtb_ref,
                 qkv_ref, uo_ref, gate_ref, gt_ref, eq_scr, eu_scr, rq_scr, ru_scr):
    i = pl.program_id(0)
    pos, n = lay.seq_pos(i)
    has_prev = pos != 0
    has_next = pos != n - 1
    qp = jnp.where(has_prev, qp_ref[...], 0.0)
    qn = jnp.where(has_next, qn_ref[...], 0.0)
    for g in range(Q_GROUPS):
        sl = slice(g * LANES, (g + 1) * LANES)
        eq_scr[g, 0:Q_HALO, :] = qp[:, sl]
        eq_scr[g, Q_HALO:Q_HALO + TOK_TILE, :] = qc_ref[:, sl]
        eq_scr[g, Q_HALO + TOK_TILE:Q_HALO + TOK_TILE + Q_HALO, :] = qn[:, sl]
        eq_scr[g, TOK_TILE + 2 * Q_HALO:, :] = jnp.zeros((Q_EXT_ROWS - TOK_TILE - 2 * Q_HALO, LANES), F32)
    up = jnp.where(has_prev, up_ref[...], 0.0)
    un_ = jnp.where(has_next, un_ref[...], 0.0)
    for g in range(U_GROUPS):
        sl = slice(g * LANES, (g + 1) * LANES)
        eu_scr[g, 0:U_HALO, :] = up[:, sl]
        eu_scr[g, U_HALO:U_HALO + TOK_TILE, :] = uc_ref[:, sl]
        eu_scr[g, U_HALO + TOK_TILE:U_HALO + TOK_TILE + U_HALO, :] = un_[:, sl]
        eu_scr[g, TOK_TILE + 2 * U_HALO:, :] = jnp.zeros((U_EXT_ROWS - TOK_TILE - 2 * U_HALO, LANES), F32)

    def q_group(g, carry):
        _strided_conv(eq_scr, rq_scr, cw_ref, g, Q_HALO - SHORT_CONV // 2, SHORT_CONV)
        return carry

    def u_group(g, carry):
        _strided_conv(eu_scr, ru_scr, dw_ref, g, U_HALO - CONV_W // 2, CONV_W)
        return carry

    lax.fori_loop(0, Q_GROUPS, q_group, 0)
    lax.fori_loop(0, U_GROUPS, u_group, 0)

    for rc in range(TOK_TILE // POST_ROWS):
        r0 = rc * POST_ROWS
        for g in range(Q_GROUPS):
            y = _silu(rq_scr[g, r0:r0 + POST_ROWS, :])
            if g < 2 * N_HEADS:
                y = y * lax.rsqrt(jnp.sum(y * y, axis=-1, keepdims=True) + EPS)
            qkv_ref[r0:r0 + POST_ROWS, g * LANES:(g + 1) * LANES] = y
        u = jnp.concatenate([ru_scr[g, r0:r0 + POST_ROWS, :] for g in range(U_GROUPS)], axis=1) + dwb_ref[...]
        uc = u - jnp.mean(u, axis=-1, keepdims=True)
        un = uc * lax.rsqrt(jnp.mean(uc * uc, axis=-1, keepdims=True) + EPS)
        uo_ref[r0:r0 + POST_ROWS, :] = _silu(un * lng_ref[...] + lnb_ref[...]).astype(BF16)

    x = ba_ref[...]
    g = -jnp.exp(alog_ref[...]) * _softplus(x + dtb_ref[...])
    beta = _sigmoid(x)
    rows = lax.broadcasted_iota(I32, (CHUNK, CHUNK), 0)
    cols = lax.broadcasted_iota(I32, (CHUNK, CHUNK), 1)
    tri_lo = (rows >= cols).astype(F32)
    tri_up = (rows <= cols).astype(F32)
    lane = lax.broadcasted_iota(I32, (CHUNK, LANES), 1)
    for ch in range(TOK_TILE // CHUNK):
        sl = slice(ch * CHUNK, (ch + 1) * CHUNK)
        pre = jnp.dot(tri_lo, g[sl], preferred_element_type=F32, precision=HIGHEST)
        suf = jnp.dot(tri_up, g[sl], preferred_element_type=F32, precision=HIGHEST)
        gate = jnp.where(lane < 2 * N_HEADS, beta[sl], jnp.where(lane < 3 * N_HEADS, pre, suf))
        gate_ref[sl, :] = gate
        gt_ref[ch] = jnp.transpose(gate)[:GATE_ROWS, :]


def _conv(lay, qkv_raw, ug, ba, cw, dw, dwb, lng, lnb, alog, dtb):
    t = lay.t
    qh = TOK_TILE // Q_HALO
    uh = TOK_TILE // U_HALO
    n_qh = t // Q_HALO
    n_uh = t // U_HALO
    in_specs = [
        _row_spec(lay, 3 * MIX_A),
        pl.BlockSpec((Q_HALO, 3 * MIX_A), lambda i: (jnp.maximum(i * qh - 1, 0), 0)),
        pl.BlockSpec((Q_HALO, 3 * MIX_A), lambda i: (jnp.minimum((i + 1) * qh, n_qh - 1), 0)),
        _row_spec(lay, MIX_B),
        pl.BlockSpec((U_HALO, MIX_B), lambda i: (jnp.maximum(i * uh - 1, 0), 0)),
        pl.BlockSpec((U_HALO, MIX_B), lambda i: (jnp.minimum((i + 1) * uh, n_uh - 1), 0)),
        _row_spec(lay, LANES),
        _full_spec(cw.shape), _full_spec(dw.shape), _full_spec(dwb.shape),
        _full_spec(lng.shape), _full_spec(lnb.shape), _full_spec(alog.shape), _full_spec(dtb.shape),
    ]
    return pl.pallas_call(
        functools.partial(_conv_kernel, lay),
        grid=(lay.tiles,),
        in_specs=in_specs,
        out_specs=[_row_spec(lay, 3 * MIX_A), _row_spec(lay, MIX_B), _row_spec(lay, LANES),
                   pl.BlockSpec((TOK_TILE // CHUNK, GATE_ROWS, CHUNK), lambda i: (i, 0, 0))],
        out_shape=[jax.ShapeDtypeStruct((t, 3 * MIX_A), F32),
                   jax.ShapeDtypeStruct((t, MIX_B), BF16),
                   jax.ShapeDtypeStruct((t, LANES), F32),
                   jax.ShapeDtypeStruct((t // CHUNK, GATE_ROWS, CHUNK), F32)],
        scratch_shapes=[pltpu.VMEM((Q_GROUPS, Q_EXT_ROWS, LANES), F32),
                        pltpu.VMEM((U_GROUPS, U_EXT_ROWS, LANES), F32),
                        pltpu.VMEM((Q_GROUPS, CONV_OUT_ROWS, LANES), F32),
                        pltpu.VMEM((U_GROUPS, CONV_OUT_ROWS, LANES), F32)],
        compiler_params=_cparams(("parallel",)),
        name="conv",
    )(qkv_raw, qkv_raw, qkv_raw, ug, ug, ug, ba, cw, dw, dwb, lng, lnb, alog, dtb)


INV_BASE = 8


def _b16(xs):
    return [x.astype(BF16) for x in xs]


def _tri_inverse_minus_eye(nmats, rows, cols):
    assert INV_BASE == 8
    c = nmats[0].shape[0]
    shift = int(math.log2(INV_BASE))
    same = (rows >> shift) == (cols >> shift)
    n1 = [jnp.where(same, n, 0.0) for n in nmats]
    n1b = _b16(n1)
    n2 = [_dot(x, x) for x in n1b]
    n2b = _b16(n2)
    r = [_dot(jnp.concatenate([a, b], axis=0), b) for a, b in zip(n1b, n2b)]
    q = [a + b + x[:c] for a, b, x in zip(n1, n2, r)]
    n4 = [x[c:] for x in r]
    qn4 = [_dot(a, b) for a, b in zip(_b16(q), _b16(n4))]
    q = [a + b + x for a, b, x in zip(q, n4, qn4)]
    while (1 << shift) < c:
        off = ((rows >> (shift + 1)) == (cols >> (shift + 1))) & ((rows >> shift) != (cols >> shift))
        a = [jnp.where(off, -n, 0.0) for n in nmats]
        x = [ai + _dot(ab, qb) for ai, ab, qb in zip(a, _b16(a), _b16(q))]
        qx = [_dot(qb, xb) for qb, xb in zip(_b16(q), _b16(x))]
        q = [qi - xi - qxi for qi, xi, qxi in zip(q, x, qx)]
        shift += 1
    return q


def _delta_units(units):
    c = CHUNK
    scale = HEAD_DIM ** -0.5
    rows = lax.broadcasted_iota(I32, (c, c), 0)
    cols = lax.broadcasted_iota(I32, (c, c), 1)
    pre = []
    for d, hd, qkv_ref, r0, gate, gt, load_s in units:
        lane = 2 * N_HEADS + N_HEADS * d + hd
        last = c - 1 if d == 0 else 0
        pre.append(dict(
            d=d, load_s=load_s,
            load=lambda part, qkv_ref=qkv_ref, r0=r0, hd=hd: qkv_ref[
                pl.ds(r0, c), part * MIX_A + hd * HEAD_DIM:part * MIX_A + (hd + 1) * HEAD_DIM],
            beta=gate[:, N_HEADS * d + hd:N_HEADS * d + hd + 1],
            gc=gate[:, lane:lane + 1],
            gr=gt[lane:lane + 1, :],
            gtot=gate[last:last + 1, lane:lane + 1]))

    ak = []
    for p in pre:
        k = p['load'](1)
        lhs = jnp.concatenate([k * p['beta'], p['load'](0) * scale], axis=0).astype(BF16)
        ak.append(_dot_nt(lhs, k.astype(BF16)))
    nmats, qks = [], []
    for p, a in zip(pre, ak):
        incl = (rows >= cols) if p['d'] == 0 else (rows <= cols)
        strict = (rows > cols) if p['d'] == 0 else (rows < cols)
        dec = jnp.where(incl, jnp.exp(p['gc'] - p['gr']), 0.0)
        nmats.append(jnp.where(strict, -a[:c] * dec, 0.0))
        qks.append((a[c:] * dec).astype(BF16))
    qinv = _tri_inverse_minus_eye(nmats, rows, cols)

    uw = []
    for p, qi in zip(pre, qinv):
        kb = p['load'](1) * p['beta']
        rhs = jnp.concatenate([p['load'](2) * p['beta'], kb * jnp.exp(p['gc'])], axis=1)
        uw.append(rhs + _dot(qi.astype(BF16), rhs.astype(BF16)))
    sw = []
    for p, x in zip(pre, uw):
        qdec = p['load'](0) * scale * jnp.exp(p['gc'])
        sw.append(_dot(jnp.concatenate([x[:, HEAD_DIM:], qdec], axis=0).astype(BF16), p['load_s']().astype(BF16)))
    vnb = [(x[:, :HEAD_DIM] - y[:c]).astype(BF16) for x, y in zip(uw, sw)]
    o = [y[c:] + _dot(qk, vb) for y, qk, vb in zip(sw, qks, vnb)]
    s_new = []
    for p, vb in zip(pre, vnb):
        kdec = (p['load'](1) * jnp.exp(p['gtot'] - p['gc'])).astype(BF16)
        s_new.append(p['load_s']() * jnp.exp(p['gtot']) + _dot_tn(kdec, vb))
    return list(zip(o, s_new))


def _delta_kernel(has_s0, emit_final, one_block, n_chunk, *refs):
    refs = list(refs)
    if one_block:
        seq_in = [[r for r in refs[3 * s:3 * s + 3] for _ in range(2)] for s in range(DELTA_SEQS)]
        refs = refs[3 * DELTA_SEQS:]
    else:
        seq_in = [refs[6 * s:6 * s + 6] for s in range(DELTA_SEQS)]
        refs = refs[6 * DELTA_SEQS:]
    s0_ref = refs.pop(0) if has_s0 else None
    of_ref, ob_ref = refs[:2]
    refs = refs[2:]
    sfin_ref = refs.pop(0) if emit_final else None
    s_scr = refs[0]

    j = pl.program_id(1)

    @pl.when(j == 0)
    def _():
        if has_s0:
            s_scr[...] = s0_ref[...]
        else:
            s_scr[...] = jnp.zeros(s_scr.shape, F32)

    c = CHUNK

    def body(ci, carry):
        units, where = [], []
        for s, (qkvf_ref, qkvb_ref, gatef_ref, gateb_ref, gtf_ref, gtb_ref) in enumerate(seq_in):
            for d in range(2):
                cidx = ci if d == 0 else n_chunk - 1 - ci
                r0 = pl.multiple_of(cidx * c, c)
                qkv_ref = qkvf_ref if d == 0 else qkvb_ref
                gate = (gatef_ref if d == 0 else gateb_ref)[pl.ds(r0, c), :]
                gt = (gtf_ref if d == 0 else gtb_ref)[cidx]
                for hd in range(N_HEADS):
                    units.append((d, hd, qkv_ref, r0, gate, gt,
                                  lambda s=s, idx=N_HEADS * d + hd: s_scr[s, idx]))
                    where.append((s, d, hd, r0))
        for (s, d, hd, r0), (o, s_new) in zip(where, _delta_units(units)):
            s_scr[s, N_HEADS * d + hd] = s_new
            (of_ref if d == 0 else ob_ref)[s, pl.ds(r0, c), hd * HEAD_DIM:(hd + 1) * HEAD_DIM] = o
        return carry

    lax.fori_loop(0, n_chunk, body, 0)

    if emit_final:
        @pl.when(j == pl.num_programs(1) - 1)
        def _():
            sfin_ref[...] = s_scr[...]


def _delta(qkv, gate, gate_t, n_seq, seq_len, row0, blk, s0):
    nblk = seq_len // blk
    n_chunk = blk // CHUNK
    b0 = row0 // blk
    has_s0 = s0 is not None
    emit_final = not has_s0
    n_state = 2 * N_HEADS
    one_block = nblk == 1
    assert n_seq % DELTA_SEQS == 0 and row0 % blk == 0

    in_specs, args = [], []
    for s in range(DELTA_SEQS):
        def fwd(b, j, s=s):
            return b0 + (b * DELTA_SEQS + s) * nblk + j

        def bwd(b, j, s=s):
            return b0 + (b * DELTA_SEQS + s) * nblk + (nblk - 1 - j)

        for arr, shape in ((qkv, (blk, 3 * MIX_A)), (gate, (blk, LANES)), (gate_t, (n_chunk, GATE_ROWS, CHUNK))):
            tail = (0,) * (len(shape) - 1)
            for f in (fwd,) if one_block else (fwd, bwd):
                in_specs.append(pl.BlockSpec(shape, lambda b, j, f=f, tail=tail: (f(b, j),) + tail))
                args.append(arr)
    state_spec = pl.BlockSpec((DELTA_SEQS, n_state, HEAD_DIM, HEAD_DIM), lambda b, j: (b, 0, 0, 0))
    if has_s0:
        in_specs.append(state_spec)
        args.append(s0)
    out_specs = [pl.BlockSpec((DELTA_SEQS, blk, MIX_A), lambda b, j: (b, j, 0)),
                 pl.BlockSpec((DELTA_SEQS, blk, MIX_A), lambda b, j: (b, nblk - 1 - j, 0))]
    out_shape = [jax.ShapeDtypeStruct((n_seq, seq_len, MIX_A), F32)] * 2
    if emit_final:
        out_specs.append(state_spec)
        out_shape.append(jax.ShapeDtypeStruct((n_seq, n_state, HEAD_DIM, HEAD_DIM), F32))
    outs = pl.pallas_call(
        functools.partial(_delta_kernel, has_s0, emit_final, one_block, n_chunk),
        grid=(n_seq // DELTA_SEQS, nblk),
        in_specs=in_specs,
        out_specs=out_specs,
        out_shape=out_shape,
        scratch_shapes=[pltpu.VMEM((DELTA_SEQS, n_state, HEAD_DIM, HEAD_DIM), F32)],
        compiler_params=_cparams(("parallel", "arbitrary")),
        name="delta_latent" if has_s0 else "delta_prompt",
    )(*args)
    return [o.reshape(n_seq * seq_len, MIX_A) for o in outs[:2]] + list(outs[2:])


def _outproj_kernel(lay, xp_ref, xs_ref, prow_ref, pcol_ref, mod_ref, opf_ref, opb_ref, osf_ref, osb_ref, z_ref,
                    u_ref, og_ref, n2_ref, wo_ref, wr_ref, br_ref, x1_ref, h2_ref, ids_ref, wts_ref, cnt_ref,
                    cnt_scr):
    i = pl.program_id(0)
    m = mod_ref[0]

    @pl.when(i == 0)
    def _():
        cnt_scr[...] = jnp.zeros(cnt_scr.shape, F32)

    halves = [slice(r0, r0 + PROJ_ROWS) for r0 in range(0, lay.tile, PROJ_ROWS)]
    n = PROJ_ROWS
    lane = lax.broadcasted_iota(I32, (n, LANES), 1)
    lane_f = lane.astype(F32)
    neg = jnp.float32(-jnp.inf)
    big = jnp.float32(LANES)

    def first_lane(mask):
        return jnp.min(jnp.where(mask, lane_f, big), axis=-1, keepdims=True)

    gated = []
    for rows in halves:
        o = jnp.where(lay.is_sample(i), osf_ref[rows, :] + osb_ref[rows, :], opf_ref[rows, :] + opb_ref[rows, :])
        z = z_ref[rows, :]
        gated.append([(_rms(o[:, sl]) * og_ref[...] * _silu(z[:, sl])).astype(BF16)
                      for sl in (slice(hd * HEAD_DIM, (hd + 1) * HEAD_DIM) for hd in range(N_HEADS))])
    mixes = []
    for rows, heads in zip(halves, gated):
        mix = _dot(u_ref[rows, :], wo_ref[MIX_A:, :])
        for hd, oh in enumerate(heads):
            mix = mix + _dot(oh, wo_ref[hd * HEAD_DIM:(hd + 1) * HEAD_DIM, :])
        mixes.append(mix)
    h2bs = []
    for rows, mix in zip(halves, mixes):
        x1 = _load_x(lay, i, xp_ref, xs_ref, prow_ref, pcol_ref, rows) + m[2:3] * mix
        x1_ref[rows, :] = x1
        h2b = (_rms(x1) * n2_ref[...] * (1.0 + m[4:5]) + m[3:4]).astype(BF16)
        h2_ref[rows, :] = _pack_bf16_pairs(h2b)
        h2bs.append(h2b)
    logits_all = [_dot(h2b, wr_ref[...]) + br_ref[...] for h2b in h2bs]

    picks = []
    for rows, logits in zip(halves, logits_all):
        gl = jnp.where(lane < N_GROUPS, logits, neg)
        gmax = jnp.max(gl, axis=-1, keepdims=True)
        grp = first_lane(gl == gmax)
        p_grp = 1.0 / jnp.sum(jnp.where(lane < N_GROUPS, jnp.exp(gl - gmax), 0.0), axis=-1, keepdims=True)
        e_lane = lane_f - N_GROUPS
        in_grp = (e_lane >= grp * EXPERTS_PER_GROUP) & (e_lane < (grp + 1.0) * EXPERTS_PER_GROUP)
        el = jnp.where(in_grp, logits, neg)
        m1 = jnp.max(el, axis=-1, keepdims=True)
        i1f = first_lane(el == m1)
        el2 = jnp.where(lane_f == i1f, neg, el)
        m2 = jnp.max(el2, axis=-1, keepdims=True)
        i2f = first_lane(el2 == m2)
        e2 = jnp.exp(m2 - m1)
        w1 = p_grp / (1.0 + e2)
        w2 = p_grp * e2 / (1.0 + e2)
        col = lax.broadcasted_iota(I32, (n, ROUTE_COLS), 1)
        wts_ref[rows, :] = jnp.where(col == 0, w1, jnp.where(col == 1, w2, 0.0))
        picks.append((i1f, i2f))

    r_i = lax.broadcasted_iota(I32, (n, n), 0)
    c_i = lax.broadcasted_iota(I32, (n, n), 1)
    before = jnp.where(r_i > c_i, 1.0, 0.0).astype(BF16)
    onehots = [(lane_f == i1f, lane_f == i2f) for i1f, i2f in picks]
    chosen = [jnp.where(oh1, 1.0, jnp.where(oh2, 1.0, 0.0)) for oh1, oh2 in onehots]
    prefix = [_dot(before, oh.astype(BF16)) for oh in chosen]
    counts = cnt_scr[...]
    for rows, (i1f, i2f), (oh1, oh2), oh, pre in zip(halves, picks, onehots, chosen, prefix):
        seen = counts + pre
        r1 = jnp.sum(jnp.where(oh1, seen, 0.0), axis=-1, keepdims=True)
        r2 = jnp.sum(jnp.where(oh2, seen, 0.0), axis=-1, keepdims=True)
        ids = jnp.where(lane == 0, i1f - N_GROUPS,
                        jnp.where(lane == 1, i2f - N_GROUPS,
                                  jnp.where(lane == 2, r1, jnp.where(lane == 3, r2, 0.0))))
        ids_ref[:, rows] = jnp.transpose(ids)[:ROUTE_COLS, :].astype(I32)
        counts = counts + jnp.sum(oh, axis=0, keepdims=True)
    cnt_scr[...] = counts
    cnt_ref[...] = counts


def _outproj(lay, xp, xs, pos, mod, o_pf, o_pb, o_sf, o_sb, z, u, onorm_g, norm2_g, w_out, w_route, b_route):
    t = lay.t
    p_spec = pl.BlockSpec((lay.tile, MIX_A), lambda i: (lay.xp_blk(i), 0))
    s_spec = pl.BlockSpec((lay.tile, MIX_A), lambda i: (lay.xs_blk(i), 0))
    return pl.pallas_call(
        functools.partial(_outproj_kernel, lay),
        grid=(lay.tiles,),
        in_specs=_x_specs(lay) + [_mod_spec(lay), p_spec, p_spec, s_spec, s_spec, _row_spec(lay, MIX_A),
                                  _row_spec(lay, MIX_B), _full_spec((1, HEAD_DIM)), _full_spec((1, D_MODEL)),
                                  _full_spec(w_out.shape), _full_spec(w_route.shape),
                                  _full_spec(b_route.shape)],
        out_specs=[_row_spec(lay, D_MODEL), _row_spec(lay, D_MODEL // 2),
                   pl.BlockSpec((ROUTE_COLS, lay.tile), lambda i: (0, i)),
                   _row_spec(lay, ROUTE_COLS), _full_spec((1, LANES))],
        out_shape=[jax.ShapeDtypeStruct((t, D_MODEL), F32),
                   jax.ShapeDtypeStruct((t, D_MODEL // 2), I32),
                   jax.ShapeDtypeStruct((ROUTE_COLS, t), I32),
                   jax.ShapeDtypeStruct((t, ROUTE_COLS), F32),
                   jax.ShapeDtypeStruct((1, LANES), F32)],
        scratch_shapes=[pltpu.VMEM((1, LANES), F32)],
        compiler_params=_cparams(("arbitrary",)),
        name="outproj",
    )(xp, xs, *pos, mod, o_pf, o_pb, o_sf, o_sb, z, u, onorm_g, norm2_g, w_out, w_route, b_route)


def _sc_gather(table, idx, chunk, name):
    n_rows, width = idx.shape[0], table.shape[1]
    mesh = plsc.VectorSubcoreMesh(core_axis_name="c", subcore_axis_name="s")
    n_workers = mesh.num_cores * mesh.num_subcores
    per_worker = n_rows // n_workers
    n_chunks = per_worker // chunk
    assert n_rows == n_workers * n_chunks * chunk and n_chunks % 2 == 0 and chunk % 8 == 0 and chunk <= LANES

    def body(table_hbm, idx_hbm, out_hbm, idx_v, rows_v, sem):
        base = (lax.axis_index("s") * mesh.num_cores + lax.axis_index("c")) * per_worker

        def gather(slot):
            return pltpu.make_async_copy(table_hbm.at[idx_v.at[slot]], rows_v.at[slot], sem.at[slot])

        def fetch(g, slot):
            off = pl.multiple_of(base + g * chunk, 8)
            pltpu.sync_copy(idx_hbm.at[pl.ds(off, chunk)], idx_v.at[slot])
            gather(slot).start()

        for slot in range(2):
            fetch(slot, slot)

        @pl.loop(0, n_chunks, step=2)
        def _(g):
            for slot in range(2):
                off = pl.multiple_of(base + (g + slot) * chunk, 8)
                gather(slot).wait()
                pltpu.sync_copy(rows_v.at[slot], out_hbm.at[pl.ds(off, chunk)])

                @pl.when(g + slot + 2 < n_chunks)
                def _():
                    fetch(g + slot + 2, slot)

    return pl.kernel(
        body,
        out_type=jax.ShapeDtypeStruct((n_rows, width), table.dtype),
        mesh=mesh,
        scratch_types=[pltpu.VMEM((2, chunk), I32), pltpu.VMEM((2, chunk, width), table.dtype),
                       pltpu.SemaphoreType.DMA((2,))],
        name=name,
    )(table, idx)


def _sc_scatter2(src, idx0, idx1, n_out, chunk, name):
    n_rows, width = src.shape
    mesh = plsc.VectorSubcoreMesh(core_axis_name="c", subcore_axis_name="s")
    n_workers = mesh.num_cores * mesh.num_subcores
    per_worker = n_rows // n_workers
    n_chunks = per_worker // chunk
    assert n_rows == n_workers * n_chunks * chunk and n_chunks % 2 == 0 and chunk % 8 == 0 and chunk <= LANES

    def body(src_hbm, i0_hbm, i1_hbm, out_hbm, i0_v, i1_v, rows_v, sem_in, sem_out):
        base = (lax.axis_index("s") * mesh.num_cores + lax.axis_index("c")) * per_worker

        def rows_in(g, slot):
            off = pl.multiple_of(base + g * chunk, 8)
            return pltpu.make_async_copy(src_hbm.at[pl.ds(off, chunk)], rows_v.at[slot], sem_in.at[slot])

        def fetch(g, slot):
            off = pl.multiple_of(base + g * chunk, 8)
            pltpu.sync_copy(i0_hbm.at[pl.ds(off, chunk)], i0_v.at[slot])
            pltpu.sync_copy(i1_hbm.at[pl.ds(off, chunk)], i1_v.at[slot])
            rows_in(g, slot).start()

        for slot in range(2):
            fetch(slot, slot)

        @pl.loop(0, n_chunks, step=2)
        def _(g):
            for slot in range(2):
                rows_in(g + slot, slot).wait()
                puts = [pltpu.make_async_copy(rows_v.at[slot], out_hbm.at[iv.at[slot]], sem_out.at[slot])
                        for iv in (i0_v, i1_v)]
                for put in puts:
                    put.start()
                for put in puts:
                    put.wait()

                @pl.when(g + slot + 2 < n_chunks)
                def _():
                    fetch(g + slot + 2, slot)

    return pl.kernel(
        body,
        out_type=jax.ShapeDtypeStruct((n_out, width), src.dtype),
        mesh=mesh,
        scratch_types=[pltpu.VMEM((2, chunk), I32), pltpu.VMEM((2, chunk), I32),
                       pltpu.VMEM((2, chunk, width), src.dtype),
                       pltpu.SemaphoreType.DMA((2,)), pltpu.SemaphoreType.DMA((2,))],
        name=name,
    )(src, idx0, idx1)


def _expert_kernel(blk_e_ref, nused_ref, valid_ref, xb_ref, wg_ref, wu_ref, wd_ref, yb_ref, wg_s, wu_s, wd_s):
    i = pl.program_id(0)
    nused = nused_ref[0]
    small = ROUTE_BM // 2

    def mlp(rows):
        half = D_MODEL // 2
        xa, xb = (v.astype(BF16) for v in _unpack_bf16_pairs(xb_ref[rows, :]))
        g = _dot(xa, wg_s[:half, :]) + _dot(xb, wg_s[half:, :])
        u = _dot(xa, wu_s[:half, :]) + _dot(xb, wu_s[half:, :])
        hmid = (_silu(g) * u).astype(BF16)
        yb_ref[rows, :] = _pack_bf16_pairs(_dot(hmid, wd_s[...]).astype(BF16))

    @pl.when(i < nused)
    def _():
        changed = (i == 0) | (blk_e_ref[i] != blk_e_ref[jnp.maximum(i - 1, 0)])

        @pl.when(changed)
        def _():
            wg_s[...] = wg_ref[0].astype(BF16)
            wu_s[...] = wu_ref[0].astype(BF16)
            wd_s[...] = wd_ref[0].astype(BF16)

        @pl.when(valid_ref[i] > small)
        def _():
            mlp(slice(0, ROUTE_BM))

        @pl.when(valid_ref[i] <= small)
        def _():
            mlp(slice(0, small))
            yb_ref[small:, :] = jnp.zeros((ROUTE_BM - small, D_MODEL // 2), I32)

    @pl.when(i >= nused)
    def _():
        yb_ref[...] = jnp.zeros(yb_ref.shape, I32)


def _experts(xb, blk_e, nused, valid, w_gate, w_up, w_down):
    nb = blk_e.shape[0]

    def weight_spec(shape):
        return pl.BlockSpec((1,) + shape, lambda i, be, nu, va: (be[i], 0, 0))

    grid_spec = pltpu.PrefetchScalarGridSpec(
        num_scalar_prefetch=3,
        grid=(nb,),
        in_specs=[
            pl.BlockSpec((ROUTE_BM, D_MODEL // 2), lambda i, be, nu, va: (i, 0)),
            weight_spec((D_MODEL, D_EXPERT)),
            weight_spec((D_MODEL, D_EXPERT)),
            weight_spec((D_EXPERT, D_MODEL)),
        ],
        out_specs=pl.BlockSpec((ROUTE_BM, D_MODEL // 2), lambda i, be, nu, va: (i, 0)),
        scratch_shapes=[pltpu.VMEM((D_MODEL, D_EXPERT), BF16),
                        pltpu.VMEM((D_MODEL, D_EXPERT), BF16),
                        pltpu.VMEM((D_EXPERT, D_MODEL), BF16)],
    )
    return pl.pallas_call(
        _expert_kernel,
        grid_spec=grid_spec,
        out_shape=jax.ShapeDtypeStruct((nb * ROUTE_BM, D_MODEL // 2), I32),
        compiler_params=_cparams(("arbitrary",)),
        name="expert",
    )(blk_e, nused, valid, xb, w_gate, w_up, w_down)


def _combine_kernel(lay, y0_ref, y1_ref, x1_ref, wts_ref, mod_ref, fg_ref, yp_ref, ys_ref):
    i = pl.program_id(0)
    m = mod_ref[0]
    w = wts_ref[...]
    a0, b0 = _unpack_bf16_pairs(y0_ref[...])
    a1, b1 = _unpack_bf16_pairs(y1_ref[...])
    ff = jnp.concatenate([a0 * w[:, 0:1] + a1 * w[:, 1:2], b0 * w[:, 0:1] + b1 * w[:, 1:2]], axis=1)
    y = _rms(x1_ref[...] + m[5:6] * ff) * fg_ref[...]

    @pl.when(i < lay.tiles_p)
    def _():
        yp_ref[...] = y

    @pl.when(i >= lay.tiles_p)
    def _():
        ys_ref[...] = y


def _combine(lay, yg, x1, wts, mod, final_g):
    return pl.pallas_call(
        functools.partial(_combine_kernel, lay),
        grid=(lay.tiles,),
        in_specs=[_row_spec(lay, D_MODEL // 2),
                  pl.BlockSpec((lay.tile, D_MODEL // 2), lambda i: (i + lay.tiles, 0)),
                  _row_spec(lay, D_MODEL), _row_spec(lay, ROUTE_COLS), _mod_spec(lay), _full_spec((1, D_MODEL))],
        out_specs=[pl.BlockSpec((lay.tile, D_MODEL), lambda i: (lay.xp_blk(i), 0)),
                   pl.BlockSpec((lay.tile, D_MODEL), lambda i: (lay.xs_blk(i), 0))],
        out_shape=[jax.ShapeDtypeStruct((lay.t_p, D_MODEL), F32),
                   jax.ShapeDtypeStruct((lay.t_s, D_MODEL), F32)],
        compiler_params=_cparams(("arbitrary",)),
        name="combine",
    )(yg, yg, x1, wts, mod, final_g)


def _dispatch_plan(ids, counts):
    n_tok = ids.shape[1]
    padded = (counts + ROUTE_BM - 1) // ROUTE_BM * ROUTE_BM
    pad_end = jnp.cumsum(padded)
    pad_start = pad_end - padded
    experts = jnp.arange(N_EXPERTS, dtype=I32)[:, None]
    dest = [(jnp.sum(jnp.where(ids[k][None, :] == experts, pad_start[:, None], 0), axis=0) + ids[2 + k]).astype(I32)
            for k in range(2)]
    nb = -(-(2 * n_tok + N_EXPERTS * (ROUTE_BM - 1)) // ROUTE_BM)
    block_start = jnp.arange(nb, dtype=I32) * ROUTE_BM
    blk_e = jnp.minimum(jnp.sum(pad_end[None, :] <= block_start[:, None], axis=1), N_EXPERTS - 1).astype(I32)
    nused = (pad_end[-1:] // ROUTE_BM).astype(I32)
    valid = jnp.clip(counts[blk_e] - (block_start - pad_start[blk_e]), 0, ROUTE_BM).astype(I32)
    return dest[0], dest[1], blk_e, nused, valid


def _grid_pos_tables(n_tokens):
    rows = n_tokens // GRID_W
    n_freq = D_MODEL // 4
    freq = jnp.exp(jnp.arange(n_freq, dtype=F32) * (-math.log(10000.0) / n_freq))

    def enc(p):
        ang = p[:, None] * freq[None, :]
        return jnp.concatenate([jnp.sin(ang), jnp.cos(ang)], axis=-1)

    return enc(jnp.arange(rows, dtype=F32)), enc(jnp.arange(GRID_W, dtype=F32))


def _lane_pad(v, offset):
    return jnp.zeros((1, LANES), F32).at[0, offset:offset + v.shape[0]].set(v.astype(F32))


def kernel(x_prompt, x_sample, state_delta, c, c_ctx, norm1_g, w_mod, b_mod, w_in, conv_qkv_w, A_log, dt_bias, onorm_g, dw_w, dw_b, cln_g, cln_b, w_out, norm2_g, w_group, b_group, w_expert, b_expert, w_e_gate, w_e_up, w_e_down, final_g):
    n_p, l_p, _ = x_prompt.shape
    n_s, l_s, _ = x_sample.shape
    lay = _Layout(n_p, l_p, n_s, l_s, TOK_TILE)
    lay_proj = _Layout(n_p, l_p, n_s, l_s, PROJ_TILE)
    depth = w_in.shape[0]
    assert depth == 1
    xp = x_prompt.reshape(lay.t_p, D_MODEL)
    xs = x_sample.reshape(lay.t_s, D_MODEL)
    pos = _grid_pos_tables(l_s)

    cond = jnp.concatenate([c_ctx[None, :], c], axis=0)
    cond = jnp.pad(cond, ((0, (-cond.shape[0]) % SUBLANES), (0, 0)))
    mod = _modulation(cond, w_mod[0], b_mod[0]).reshape(cond.shape[0], 6, D_MODEL)

    n_gate = 4 * N_HEADS
    wi = w_in[0]
    w_main = jnp.concatenate([wi[:, :4 * MIX_A], wi[:, 4 * MIX_A + n_gate:]], axis=1).astype(BF16)
    w_small = jnp.pad(wi[:, 4 * MIX_A:4 * MIX_A + n_gate], ((0, 0), (0, LANES - n_gate))).astype(BF16)
    qkv_raw, z, ug, ba = _inproj(lay_proj, xp, xs, pos, mod, norm1_g[0][None, :], w_main, w_small)

    cw = jnp.pad(conv_qkv_w[0], ((0, 8 - SHORT_CONV), (0, 0))).reshape(8, Q_GROUPS, LANES).transpose(1, 0, 2)
    dw = jnp.pad(dw_w[0], ((0, 32 - CONV_W), (0, 0))).reshape(32, U_GROUPS, LANES).transpose(1, 0, 2)
    alog = _lane_pad(A_log[0].reshape(-1), 2 * N_HEADS)
    dtb = _lane_pad(dt_bias[0].reshape(-1), 2 * N_HEADS)
    qkv, u_conf, gate, gate_t = _conv(lay, qkv_raw, ug, ba, cw, dw, dw_b[0][None, :], cln_g[0][None, :],
                                      cln_b[0][None, :], alog, dtb)

    o_pf, o_pb, s_fin = _delta(qkv, gate, gate_t, n_p, l_p, 0, l_p, None)
    s0 = state_delta[:, 0].reshape(n_s, 2 * N_HEADS, HEAD_DIM, HEAD_DIM)
    o_sf, o_sb = _delta(qkv, gate, gate_t, n_s, l_s, lay.t_p, DELTA_BLK, s0)

    w_route = jnp.pad(jnp.concatenate([w_group[0], w_expert[0]], axis=1),
                      ((0, 0), (0, LANES - N_GROUPS - N_EXPERTS))).astype(BF16)
    b_route = _lane_pad(jnp.concatenate([b_group[0], b_expert[0]]), 0)
    x1, h2p, ids, wts, cnt = _outproj(lay_proj, xp, xs, pos, mod, o_pf, o_pb, o_sf, o_sb, z, u_conf,
                                      onorm_g[0][None, :], norm2_g[0][None, :], w_out[0].astype(BF16), w_route,
                                      b_route)

    counts = cnt[0, N_GROUPS:N_GROUPS + N_EXPERTS].astype(I32)
    dest0, dest1, blk_e, nused, valid = _dispatch_plan(ids, counts)
    xb = _sc_scatter2(h2p, dest0, dest1, blk_e.shape[0] * ROUTE_BM, DISPATCH_CHUNK, "dispatch_scatter")
    yb = _experts(xb, blk_e, nused, valid, w_e_gate[0], w_e_up[0], w_e_down[0])
    yg = _sc_gather(yb, jnp.concatenate([dest0, dest1]), COMBINE_CHUNK, "combine_gather")
    y_p, y_s = _combine(_Layout(n_p, l_p, n_s, l_s, COMBINE_TILE), yg, x1, wts, mod, final_g[None, :])

    new_state = s_fin.reshape(n_p, 1, 2, N_HEADS, HEAD_DIM, HEAD_DIM)
    return (y_p.reshape(x_prompt.shape), y_s.reshape(x_sample.shape), new_state)
```
